```python
import math
import jax, jax.numpy as jnp
from jax import lax
import numpy as np

D_MODEL = 2048
BATCH = 4
SEQ = 2048
DEPTH = 1
DEC_BATCH = 128
DEC_SEQ = 1
PAST_LEN = 16384
PAGE_SIZE = 128

S5_WIDTH = 1024
S5_GROUP = 16
S5_GROUPS = S5_WIDTH // S5_GROUP
S5_STATE = 64
DT_MIN = 1e-3
DT_MAX = 1e-1
HG_WIDTH = 1024
HG_HEADS = 8
HG_DK = HG_WIDTH // HG_HEADS
HG_DV = HG_WIDTH // HG_HEADS
HG_CHUNK = 64
IN_WIDTH = S5_WIDTH + 4 * HG_WIDTH + 2 * D_MODEL
D_FF = 5632
EPS = 1e-6

kernel_name = "hybrid_s5_hgrn2_macaron_decode_step"


def rmsnorm(x, g):
    xf = x.astype(jnp.float32)
    y = xf * lax.rsqrt(jnp.mean(xf * xf, axis=-1, keepdims=True) + EPS)
    return y * g.astype(jnp.float32)


def swiglu(h, w_gate, w_up, w_down):
    return (jax.nn.silu(h @ w_gate) * (h @ w_up)) @ w_down


def s5_mixer(u, h0_re, h0_im, lam_re, lam_im, log_dt, b_re, b_im, c_re, c_im, d_skip, w_glu, b_glu):
    f32 = jnp.float32
    bsz, L, _ = u.shape
    ug = u.reshape(bsz, L, S5_GROUPS, S5_GROUP).astype(jnp.complex64)
    lam = lax.complex(lam_re.astype(f32), lam_im.astype(f32))
    dt = jnp.exp(log_dt.astype(f32))[:, None]
    a_bar = jnp.exp(lam * dt)
    b = lax.complex(b_re.astype(f32), b_im.astype(f32))
    b_bar = ((a_bar - 1.0) / lam)[..., None] * b
    bu = jnp.einsum('gpn,blgn->blgp', b_bar, ug)
    h0 = lax.complex(h0_re.astype(f32), h0_im.astype(f32))
    bu = bu.at[:, 0].add(a_bar * h0)
    a = jnp.broadcast_to(a_bar, bu.shape)

    def combine(e1, e2):
        a1, b1 = e1
        a2, b2 = e2
        return a1 * a2, a2 * b1 + b2

    _, h = lax.associative_scan(combine, (a, bu), axis=1)
    c = lax.complex(c_re.astype(f32), c_im.astype(f32))
    y = jnp.real(jnp.einsum('gnp,blgp->blgn', c, h)).reshape(bsz, L, S5_WIDTH)
    y = y + d_skip.astype(f32) * u
    z = jax.nn.gelu(y)
    out = z * jax.nn.sigmoid(z @ w_glu.astype(f32) + b_glu.astype(f32))
    h_last = h[:, -1]
    return out, jnp.real(h_last), jnp.imag(h_last)


def hgrn2_mixer(q, f_logit, inp, gate, lb, S0, g_norm):
    bsz, L, _ = q.shape
    f = lb + (1.0 - lb) * jax.nn.sigmoid(f_logit)
    log_f = jnp.log(f)
    k = 1.0 - f
    q = jax.nn.silu(q)
    chunk = min(HG_CHUNK, L)
    n_chunks = -(-L // chunk)
    pad = n_chunks * chunk - L

    def to_chunks(t, d):
        t = jnp.pad(t, ((0, 0), (0, pad), (0, 0)))
        return t.reshape(bsz, n_chunks, chunk, HG_HEADS, d).transpose(1, 0, 3, 2, 4)

    qc, kc, gc = to_chunks(q, HG_DK), to_chunks(k, HG_DK), to_chunks(log_f, HG_DK)
    ic = to_chunks(inp, HG_DV)
    mask = jnp.tril(jnp.ones((chunk, chunk), dtype=bool))[:, :, None]

    def step(S, xs):
        qb, kb, gb, ib = xs
        G = jnp.cumsum(gb, axis=2)
        o_inter = jnp.einsum('bhtk,bhkv->bhtv', qb * jnp.exp(G), S)
        diff = G[:, :, :, None, :] - G[:, :, None, :, :]
        decay = jnp.exp(jnp.where(mask, diff, -jnp.inf))
        att = jnp.einsum('bhtk,bhtsk,bhsk->bhts', qb, decay, kb)
        o = o_inter + jnp.einsum('bhts,bhsv->bhtv', att, ib)
        G_last = G[:, :, -1]
        S_new = jnp.exp(G_last)[..., None] * S + jnp.einsum(
            'bhsk,bhsv->bhkv', kb * jnp.exp(G_last[:, :, None] - G), ib)
        return S_new, o

    S_fin, oc = lax.scan(step, S0, (qc, kc, gc, ic))
    o = oc.transpose(1, 0, 3, 2, 4).reshape(bsz, n_chunks * chunk, HG_HEADS, HG_DV)[:, :L]
    o = rmsnorm(o, g_norm.reshape(HG_HEADS, HG_DV)).reshape(bsz, L, HG_WIDTH)
    return o * jax.nn.silu(gate), S_fin


def decoder_layer(x, s5_re, s5_im, S0, lb, p):
    dt = x.dtype
    f32 = jnp.float32
    h = rmsnorm(x, p['ffn1_pre_norm']).astype(dt)
    x = x + (0.5 * rmsnorm(swiglu(h, p['ffn1_w_gate'], p['ffn1_w_up'], p['ffn1_w_down']), p['ffn1_post_norm'])).astype(dt)
    h = rmsnorm(x, p['mix_pre_norm']).astype(dt)
    proj = (h @ p['w_in']).astype(f32)
    cuts = [S5_WIDTH, S5_WIDTH + HG_WIDTH, S5_WIDTH + 2 * HG_WIDTH, S5_WIDTH + 3 * HG_WIDTH,
            S5_WIDTH + 4 * HG_WIDTH, S5_WIDTH + 4 * HG_WIDTH + D_MODEL]
    u, q, f_logit, inp, og, g_s5, g_hg = jnp.split(proj, cuts, axis=-1)
    s5_out, s5_re_new, s5_im_new = s5_mixer(
        u, s5_re, s5_im, p['s5_lambda_re'], p['s5_lambda_im'], p['s5_log_dt'], p['s5_b_re'], p['s5_b_im'],
        p['s5_c_re'], p['s5_c_im'], p['s5_d'], p['s5_w_glu'], p['s5_b_glu'])
    hg_out, S_new = hgrn2_mixer(q, f_logit, inp, og, lb, S0.astype(f32), p['hgrn_out_norm'])
    merged = (jax.nn.sigmoid(g_s5) * (s5_out @ p['w_branch_s5'].astype(f32))
              + jax.nn.sigmoid(g_hg) * (hg_out @ p['w_branch_hgrn'].astype(f32)))
    mix = merged.astype(dt) @ p['w_out']
    x = x + rmsnorm(mix, p['mix_post_norm']).astype(dt)
    h = rmsnorm(x, p['ffn2_pre_norm']).astype(dt)
    x = x + (0.5 * rmsnorm(swiglu(h, p['ffn2_w_gate'], p['ffn2_w_up'], p['ffn2_w_down']), p['ffn2_post_norm'])).astype(dt)
    return x, s5_re_new, s5_im_new, S_new


def setup_inputs(seed: int = 0) -> dict:
    key = jax.random.key(seed)
    ks = iter(jax.random.split(key, 48))
    f32 = jnp.float32

    def nrm(shape, scale):
        return scale * jax.random.normal(next(ks), shape, f32)

    def gain(shape):
        return 1.0 + 0.02 * jax.random.normal(next(ks), shape, f32)

    L_, G, P, N = DEPTH, S5_GROUPS, S5_STATE, S5_GROUP
    lam_im_base = jnp.broadcast_to(math.pi * jnp.arange(P, dtype=f32), (L_, G, P))
    return {
        'x_prompt': nrm((BATCH, SEQ, D_MODEL), 1.0),
        'x_sample': nrm((DEC_BATCH, DEC_SEQ, D_MODEL), 1.0),
        'state_s5_re': nrm((L_, DEC_BATCH, G, P), 0.3),
        'state_s5_im': nrm((L_, DEC_BATCH, G, P), 0.3),
        'state_hgrn': nrm((L_, DEC_BATCH, HG_HEADS, HG_DK, HG_DV), 0.3),
        'ffn1_pre_norm': gain((L_, D_MODEL)),
        'ffn1_w_gate': nrm((L_, D_MODEL, D_FF), D_MODEL ** -0.5),
        'ffn1_w_up': nrm((L_, D_MODEL, D_FF), D_MODEL ** -0.5),
        'ffn1_w_down': nrm((L_, D_FF, D_MODEL), D_FF ** -0.5),
        'ffn1_post_norm': gain((L_, D_MODEL)),
        'mix_pre_norm': gain((L_, D_MODEL)),
        'w_in': nrm((L_, D_MODEL, IN_WIDTH), D_MODEL ** -0.5),
        's5_lambda_re': -0.5 + nrm((L_, G, P), 0.01),
        's5_lambda_im': lam_im_base + nrm((L_, G, P), 0.01),
        's5_log_dt': jax.random.uniform(next(ks), (L_, G), f32, math.log(DT_MIN), math.log(DT_MAX)),
        's5_b_re': nrm((L_, G, P, N), (2 * N) ** -0.5),
        's5_b_im': nrm((L_, G, P, N), (2 * N) ** -0.5),
        's5_c_re': nrm((L_, G, N, P), (2 * P) ** -0.5),
        's5_c_im': nrm((L_, G, N, P), (2 * P) ** -0.5),
        's5_d': nrm((L_, S5_WIDTH), 1.0),
        's5_w_glu': nrm((L_, S5_WIDTH, S5_WIDTH), S5_WIDTH ** -0.5),
        's5_b_glu': nrm((L_, S5_WIDTH), 0.02),
        'hgrn_lb_logits': nrm((L_ + 1, HG_WIDTH), 0.5),
        'hgrn_out_norm': gain((L_, HG_WIDTH)),
        'w_branch_s5': nrm((L_, S5_WIDTH, D_MODEL), S5_WIDTH ** -0.5),
        'w_branch_hgrn': nrm((L_, HG_WIDTH, D_MODEL), HG_WIDTH ** -0.5),
        'w_out': nrm((L_, D_MODEL, D_MODEL), D_MODEL ** -0.5),
        'mix_post_norm': gain((L_, D_MODEL)),
        'ffn2_pre_norm': gain((L_, D_MODEL)),
        'ffn2_w_gate': nrm((L_, D_MODEL, D_FF), D_MODEL ** -0.5),
        'ffn2_w_up': nrm((L_, D_MODEL, D_FF), D_MODEL ** -0.5),
        'ffn2_w_down': nrm((L_, D_FF, D_MODEL), D_FF ** -0.5),
        'ffn2_post_norm': gain((L_, D_MODEL)),
    }


def reference(x_prompt, x_sample, state_s5_re, state_s5_im, state_hgrn,
              ffn1_pre_norm, ffn1_w_gate, ffn1_w_up, ffn1_w_down, ffn1_post_norm,
              mix_pre_norm, w_in, s5_lambda_re, s5_lambda_im, s5_log_dt, s5_b_re, s5_b_im,
              s5_c_re, s5_c_im, s5_d, s5_w_glu, s5_b_glu, hgrn_lb_logits, hgrn_out_norm,
              w_branch_s5, w_branch_hgrn, w_out, mix_post_norm,
              ffn2_pre_norm, ffn2_w_gate, ffn2_w_up, ffn2_w_down, ffn2_post_norm):
    f32 = jnp.float32
    lb_all = jnp.cumsum(jax.nn.softmax(hgrn_lb_logits.astype(f32), axis=0), axis=0)
    bp = x_prompt.shape[0]
    yp, ys = x_prompt, x_sample
    p_re, p_im, p_hg, s_re, s_im, s_hg = [], [], [], [], [], []
    for l in range(DEPTH):
        p = {
            'ffn1_pre_norm': ffn1_pre_norm[l], 'ffn1_w_gate': ffn1_w_gate[l], 'ffn1_w_up': ffn1_w_up[l],
            'ffn1_w_down': ffn1_w_down[l], 'ffn1_post_norm': ffn1_post_norm[l],
            'mix_pre_norm': mix_pre_norm[l], 'w_in': w_in[l],
            's5_lambda_re': s5_lambda_re[l], 's5_lambda_im': s5_lambda_im[l], 's5_log_dt': s5_log_dt[l],
            's5_b_re': s5_b_re[l], 's5_b_im': s5_b_im[l], 's5_c_re': s5_c_re[l], 's5_c_im': s5_c_im[l],
            's5_d': s5_d[l], 's5_w_glu': s5_w_glu[l], 's5_b_glu': s5_b_glu[l],
            'hgrn_out_norm': hgrn_out_norm[l], 'w_branch_s5': w_branch_s5[l],
            'w_branch_hgrn': w_branch_hgrn[l], 'w_out': w_out[l], 'mix_post_norm': mix_post_norm[l],
            'ffn2_pre_norm': ffn2_pre_norm[l], 'ffn2_w_gate': ffn2_w_gate[l], 'ffn2_w_up': ffn2_w_up[l],
            'ffn2_w_down': ffn2_w_down[l], 'ffn2_post_norm': ffn2_post_norm[l],
        }
        lb = lb_all[l]
        z_s5 = jnp.zeros((bp, S5_GROUPS, S5_STATE), f32)
        z_hg = jnp.zeros((bp, HG_HEADS, HG_DK, HG_DV), f32)
        yp, a_re, a_im, a_hg = decoder_layer(yp, z_s5, z_s5, z_hg, lb, p)
        ys, b_re, b_im, b_hg = decoder_layer(ys, state_s5_re[l], state_s5_im[l], state_hgrn[l], lb, p)
        p_re.append(a_re); p_im.append(a_im); p_hg.append(a_hg)
        s_re.append(b_re); s_im.append(b_im); s_hg.append(b_hg)
    new_s5_re_p = jnp.stack(p_re)
    new_s5_im_p = jnp.stack(p_im)
    new_hgrn_p = jnp.stack(p_hg)
    new_s5_re_s = jnp.stack(s_re)
    new_s5_im_s = jnp.stack(s_im)
    new_hgrn_s = jnp.stack(s_hg)
    return (yp, ys, new_s5_re_p, new_s5_im_p, new_hgrn_p, new_s5_re_s, new_s5_im_s, new_hgrn_s)
```

```python
import functools

import jax
import jax.numpy as jnp
from jax import lax
from jax.experimental import pallas as pl
from jax.experimental.pallas import tpu as pltpu

F32 = jnp.float32
BF16 = jnp.bfloat16
EPS = 1e-6
HIGHEST = lax.Precision.HIGHEST

LANES = 128
SUBLANES = 8
VMEM_LIMIT = 56 * 1024 * 1024

ROW_TILE = 640
FF_TILE = 512
IN_TILE = 1024
HG_CHUNK = 64
SAMPLE_TOKENS_PER_STEP = 8


def _cparams(*sem):
    return pltpu.CompilerParams(dimension_semantics=sem, vmem_limit_bytes=VMEM_LIMIT)


def _rms(x, g):
    return x * lax.rsqrt(jnp.mean(x * x, axis=-1, keepdims=True) + EPS) * g


def _silu(x):
    return x * jax.nn.sigmoid(x)


def _dot(a, b):
    return jnp.dot(a, b, preferred_element_type=F32)


def _dot_hi(a, b):
    return jnp.dot(a, b, preferred_element_type=F32, precision=HIGHEST)


def _dot_nt(a, b):
    return lax.dot_general(a, b, (((1,), (1,)), ((), ())), preferred_element_type=F32)


def _dot_tn(a, b):
    return lax.dot_general(a, b, (((0,), (0,)), ((), ())), preferred_element_type=F32)


def _ffn_body(x_ref, pre_ref, wg_ref, wu_ref, wd_ref, post_ref, o_ref, h_ref, acc_ref):
    k = pl.program_id(1)

    @pl.when(k == 0)
    def _():
        h_ref[...] = _rms(x_ref[...], pre_ref[...]).astype(BF16)
        acc_ref[...] = jnp.zeros_like(acc_ref)

    h = h_ref[...]
    g = _dot(h, wg_ref[...])
    u = _dot(h, wu_ref[...])
    acc_ref[...] += _dot((_silu(g) * u).astype(BF16), wd_ref[...])

    @pl.when(k == pl.num_programs(1) - 1)
    def _():
        o_ref[...] = x_ref[...] + 0.5 * _rms(acc_ref[...], post_ref[...])


def _ffn(x, pre, wg, wu, wd, post):
    m, d = x.shape
    dff = wg.shape[1]
    tm, tf = ROW_TILE, FF_TILE
    return pl.pallas_call(
        _ffn_body,
        out_shape=jax.ShapeDtypeStruct((m, d), F32),
        grid=(m // tm, dff // tf),
        in_specs=[
            pl.BlockSpec((tm, d), lambda i, k: (i, 0)),
            pl.BlockSpec((1, d), lambda i, k: (0, 0)),
            pl.BlockSpec((d, tf), lambda i, k: (0, k)),
            pl.BlockSpec((d, tf), lambda i, k: (0, k)),
            pl.BlockSpec((tf, d), lambda i, k: (k, 0)),
            pl.BlockSpec((1, d), lambda i, k: (0, 0)),
        ],
        out_specs=pl.BlockSpec((tm, d), lambda i, k: (i, 0)),
        scratch_shapes=[pltpu.VMEM((tm, d), BF16), pltpu.VMEM((tm, d), F32)],
        compiler_params=_cparams("parallel", "arbitrary"),
        name="ffn",
    )(x, pre, wg, wu, wd, post)


def _inproj_body(x_ref, g_ref, w_ref, o_ref, h_ref):
    @pl.when(pl.program_id(1) == 0)
    def _():
        h_ref[...] = _rms(x_ref[...], g_ref[...]).astype(BF16)

    o_ref[...] = _dot(h_ref[...], w_ref[...])


def _inproj(x, g, w):
    m, d = x.shape
    n = w.shape[1]
    tm, tn = ROW_TILE, IN_TILE
    return pl.pallas_call(
        _inproj_body,
        out_shape=jax.ShapeDtypeStruct((m, n), F32),
        grid=(m // tm, n // tn),
        in_specs=[
            pl.BlockSpec((tm, d), lambda i, j: (i, 0)),
            pl.BlockSpec((1, d), lambda i, j: (0, 0)),
            pl.BlockSpec((d, tn), lambda i, j: (0, j)),
        ],
        out_specs=pl.BlockSpec((tm, tn), lambda i, j: (i, j)),
        scratch_shapes=[pltpu.VMEM((tm, d), BF16)],
        compiler_params=_cparams("parallel", "arbitrary"),
        name="inproj",
    )(x, g, w)


def _s5_tables(lam_re, lam_im, log_dt, b_re, b_im, c_re, c_im):
    g, p = lam_re.shape
    n = b_re.shape[-1]
    gpt = LANES // n
    nv = g // gpt
    dt = jnp.exp(log_dt)[:, None]
    er = jnp.exp(lam_re * dt)
    th = lam_im * dt
    a_re, a_im = er * jnp.cos(th), er * jnp.sin(th)
    den = lam_re * lam_re + lam_im * lam_im
    k_re = ((a_re - 1.0) * lam_re + a_im * lam_im) / den
    k_im = (a_im * lam_re - (a_re - 1.0) * lam_im) / den
    bb_re = k_re[..., None] * b_re - k_im[..., None] * b_im
    bb_im = k_re[..., None] * b_im + k_im[..., None] * b_re
    eye = jnp.eye(gpt, dtype=F32)

    def in_blk(bb):
        t = bb.reshape(nv, gpt, p, n)
        return jnp.einsum("ab,vapn->vanbp", eye, t).reshape(nv, gpt * n, gpt * p)

    def out_blk(cc):
        t = cc.reshape(nv, gpt, n, p)
        return jnp.einsum("ab,vanp->vapbn", eye, t).reshape(nv, gpt * p, gpt * n)

    w_in = jnp.concatenate([in_blk(bb_re), in_blk(bb_im)], axis=-1)
    w_out = jnp.concatenate([out_blk(c_re), -out_blk(c_im)], axis=1)
    return w_in, w_out, a_re.reshape(nv, 1, gpt * p), a_im.reshape(nv, 1, gpt * p)


def _cmul(ar, ai, br, bi):
    return ar * br - ai * bi, ar * bi + ai * br


def _s5_prompt_body(u_ref, w_ref, cm_ref, ar_ref, ai_ref, d_ref, z_ref, hl_ref, buh_ref,
                    *, seg, pitch):
    nseg = SUBLANES
    nc = w_ref.shape[2] // LANES
    hc = nc // 2

    for j in range(nseg):
        bu = _dot_hi(u_ref[j * seg:(j + 1) * seg, :], w_ref[0])
        for c in range(nc):
            buh_ref[c, j * pitch:j * pitch + seg, :] = bu[:, c * LANES:(c + 1) * LANES]

    ar = [jnp.broadcast_to(ar_ref[0, :, c * LANES:(c + 1) * LANES], (nseg, LANES)) for c in range(hc)]
    ai = [jnp.broadcast_to(ai_ref[0, :, c * LANES:(c + 1) * LANES], (nseg, LANES)) for c in range(hc)]

    def rows(t):
        return pl.ds(t, nseg, stride=pitch)

    def advance(t, hs):
        out = []
        for c in range(hc):
            pr, pi = _cmul(ar[c], ai[c], hs[2 * c], hs[2 * c + 1])
            out += [pr + buh_ref[c, rows(t), :], pi + buh_ref[hc + c, rows(t), :]]
        return tuple(out)

    zeros = tuple(jnp.zeros((nseg, LANES), F32) for _ in range(2 * hc))
    ends = lax.fori_loop(0, seg, advance, zeros, unroll=4)

    pw = [(ar[c], ai[c]) for c in range(hc)]
    for _ in range(seg.bit_length() - 1):
        pw = [_cmul(r, i, r, i) for r, i in pw]

    row = lax.broadcasted_iota(jnp.int32, (nseg, LANES), 0)
    init = list(zeros)
    for j in range(1, nseg):
        for c in range(hc):
            pr, pi = _cmul(pw[c][0], pw[c][1], init[2 * c], init[2 * c + 1])
            nr = pltpu.roll(pr + ends[2 * c], 1, 0)
            ni = pltpu.roll(pi + ends[2 * c + 1], 1, 0)
            init[2 * c] = jnp.where(row == j, nr, init[2 * c])
            init[2 * c + 1] = jnp.where(row == j, ni, init[2 * c + 1])

    def advance_store(t, hs):
        new = advance(t, hs)
        for c in range(hc):
            buh_ref[c, rows(t), :] = new[2 * c]
            buh_ref[hc + c, rows(t), :] = new[2 * c + 1]
        return new

    last = lax.fori_loop(0, seg, advance_store, tuple(init), unroll=4)
    for c in range(hc):
        hl_ref[0, 0, :, c * LANES:(c + 1) * LANES] = last[2 * c]
        hl_ref[0, 0, :, (hc + c) * LANES:(hc + c + 1) * LANES] = last[2 * c + 1]

    for j in range(nseg):
        y = d_ref[...] * u_ref[j * seg:(j + 1) * seg, :]
        for c in range(nc):
            y = y + _dot_hi(buh_ref[c, j * pitch:j * pitch + seg, :],
                            cm_ref[0, c * LANES:(c + 1) * LANES, :])
        z_ref[j * seg:(j + 1) * seg, :] = jax.nn.gelu(y)


def _s5_prompt(proj, w_in, w_out, a_re, a_im, d_skip, *, batch, seq, rows_total):
    nv = w_in.shape[0]
    sw = w_in.shape[2]
    seg = seq // SUBLANES
    assert seg * SUBLANES == seq and seg & (seg - 1) == 0
    pitch = seg + SUBLANES
    body = functools.partial(_s5_prompt_body, seg=seg, pitch=pitch)
    return pl.pallas_call(
        body,
        out_shape=(jax.ShapeDtypeStruct((rows_total, nv * LANES), F32),
                   jax.ShapeDtypeStruct((batch, nv, SUBLANES, sw), F32)),
        grid=(batch, nv),
        in_specs=[
            pl.BlockSpec((seq, LANES), lambda b, v: (b, v)),
            pl.BlockSpec((1, LANES, sw), lambda b, v: (v, 0, 0)),
            pl.BlockSpec((1, sw, LANES), lambda b, v: (v, 0, 0)),
            pl.BlockSpec((1, 1, sw // 2), lambda b, v: (v, 0, 0)),
            pl.BlockSpec((1, 1, sw // 2), lambda b, v: (v, 0, 0)),
            pl.BlockSpec((1, LANES), lambda b, v: (0, v)),
        ],
        out_specs=(pl.BlockSpec((seq, LANES), lambda b, v: (b, v)),
                   pl.BlockSpec((1, 1, SUBLANES, sw), lambda b, v: (b, v, 0, 0))),
        scratch_shapes=[pltpu.VMEM((sw // LANES, SUBLANES * pitch, LANES), F32)],
        compiler_params=_cparams("parallel", "parallel"),
        name="s5_prompt",
    )(proj, w_in, w_out, a_re, a_im, d_skip)


def _s5_sample_body(u_ref, hre_ref, him_ref, w_ref, cm_ref, ar_ref, ai_ref, d_ref, zin_ref,
                    z_ref, ore_ref, oim_ref):
    del zin_ref
    half = ar_ref.shape[2]
    u = u_ref[...]
    bu = _dot_hi(u, w_ref[0])
    pr, pi = _cmul(ar_ref[0], ai_ref[0], hre_ref[...], him_ref[...])
    hr = pr + bu[:, :half]
    hi = pi + bu[:, half:]
    ore_ref[...] = hr
    oim_ref[...] = hi
    y = d_ref[...] * u + _dot_hi(hr, cm_ref[0, :half, :]) + _dot_hi(hi, cm_ref[0, half:, :])
    z_ref[...] = jax.nn.gelu(y)


def _s5_sample(proj, h_re, h_im, w_in, w_out, a_re, a_im, d_skip, z_all, *, row0):
    nv = w_in.shape[0]
    sw = w_in.shape[2]
    ns = h_re.shape[0]
    rb = row0 // ns
    assert rb * ns == row0
    st = pl.BlockSpec((ns, sw // 2), lambda v: (0, v))
    return pl.pallas_call(
        _s5_sample_body,
        out_shape=(jax.ShapeDtypeStruct(z_all.shape, F32),
                   jax.ShapeDtypeStruct(h_re.shape, F32),
                   jax.ShapeDtypeStruct(h_im.shape, F32)),
        grid=(nv,),
        in_specs=[
            pl.BlockSpec((ns, LANES), lambda v: (rb, v)),
            st, st,
            pl.BlockSpec((1, LANES, sw), lambda v: (v, 0, 0)),
            pl.BlockSpec((1, sw, LANES), lambda v: (v, 0, 0)),
            pl.BlockSpec((1, 1, sw // 2), lambda v: (v, 0, 0)),
            pl.BlockSpec((1, 1, sw // 2), lambda v: (v, 0, 0)),
            pl.BlockSpec((1, LANES), lambda v: (0, v)),
            pl.BlockSpec(memory_space=pl.ANY),
        ],
        out_specs=(pl.BlockSpec((ns, LANES), lambda v: (rb, v)), st, st),
        input_output_aliases={8: 0},
        compiler_params=_cparams("parallel"),
        name="s5_sample",
    )(proj, h_re, h_im, w_in, w_out, a_re, a_im, d_skip, z_all)


def _hg_prompt_body(q_ref, f_ref, i_ref, og_ref, lb_ref, gn_ref, o_ref, sfin_ref, st_ref, *, dk):
    c = pl.program_id(1)
    nh = st_ref.shape[0]
    ch = q_ref.shape[0]

    @pl.when(c == 0)
    def _():
        st_ref[...] = jnp.zeros_like(st_ref)

    lb = lb_ref[...]
    f = lb + (1.0 - lb) * jax.nn.sigmoid(f_ref[...])
    logf = jnp.log(f)
    kk = 1.0 - f
    qs = _silu(q_ref[...])
    r = lax.broadcasted_iota(jnp.int32, (ch, ch), 0)
    s = lax.broadcasted_iota(jnp.int32, (ch, ch), 1)
    causal = s <= r
    gcum = _dot_hi(causal.astype(F32), logf)
    glast = gcum[ch - 1:ch, :]
    qg = (qs * jnp.exp(gcum)).astype(BF16)
    kn = (kk * jnp.exp(-gcum)).astype(BF16)
    kh = (kk * jnp.exp(glast - gcum)).astype(BF16)
    dec = jnp.exp(glast)
    ib = i_ref[...].astype(BF16)
    for h in range(nh):
        sl = slice(h * dk, (h + 1) * dk)
        st = st_ref[h]
        att = jnp.where(causal, _dot_nt(qg[:, sl], kn[:, sl]), 0.0)
        o = _dot_nt(qg[:, sl], st.astype(BF16)) + _dot(att.astype(BF16), ib[:, sl])
        st_ref[h] = st * dec[:, sl] + _dot_tn(ib[:, sl], kh[:, sl])
        og = og_ref[:, sl]
        o_ref[:, sl] = _rms(o, gn_ref[:, sl]) * _silu(og)

    @pl.when(c == pl.num_programs(1) - 1)
    def _():
        sfin_ref[0] = st_ref[...]


def _hg_prompt(proj, lb, gn, *, batch, seq, rows_total, nh, col0):
    width = lb.shape[1]
    dk = width // nh
    ch = HG_CHUNK
    nchunk = seq // ch
    cb = col0 // width
    assert cb * width == col0 and nchunk * ch == seq

    def tok(k):
        return pl.BlockSpec((ch, width), lambda b, c: (b * nchunk + c, cb + k))

    vec = pl.BlockSpec((1, width), lambda b, c: (0, 0))
    return pl.pallas_call(
        functools.partial(_hg_prompt_body, dk=dk),
        out_shape=(jax.ShapeDtypeStruct((rows_total, width), F32),
                   jax.ShapeDtypeStruct((batch, nh, dk, dk), F32)),
        grid=(batch, nchunk),
        in_specs=[tok(0), tok(1), tok(2), tok(3), vec, vec],
        out_specs=(pl.BlockSpec((ch, width), lambda b, c: (b * nchunk + c, 0)),
                   pl.BlockSpec((1, nh, dk, dk), lambda b, c: (b, 0, 0, 0))),
        scratch_shapes=[pltpu.VMEM((nh, dk, dk), F32)],
        compiler_params=_cparams("parallel", "arbitrary"),
        name="hgrn_prompt",
    )(proj, proj, proj, proj, lb, gn)


def _split3(x):
    p1 = x.astype(BF16)
    r1 = x - p1.astype(F32)
    p2 = r1.astype(BF16)
    p3 = (r1 - p2.astype(F32)).astype(BF16)
    return p1, p2, p3


def _hg_sample_body(q_ref, f_ref, i_ref, og_ref, lb_ref, gn_ref, s0_ref, hgin_ref,
                    hg_ref, s_ref, fq_ref, *, dk):
    del hgin_ref
    step = pl.program_id(0)
    nt = q_ref.shape[0]
    tb, nh = s0_ref.shape[0], s0_ref.shape[1]

    @pl.when(step == 0)
    def _():
        lb = lb_ref[...]
        f = lb + (1.0 - lb) * jax.nn.sigmoid(f_ref[...])
        qs = _silu(q_ref[...])
        for src, base in ((f, 0), (qs, nh)):
            for h in range(nh):
                t = src[:, h * dk:(h + 1) * dk].T
                for p, piece in enumerate(_split3(t)):
                    fq_ref[(base + h) * dk:(base + h + 1) * dk, p * nt:(p + 1) * nt] = piece

    tok = lax.broadcasted_iota(jnp.int32, (3 * nt, dk), 0)
    for j in range(tb):
        n = step * tb + j
        onehot = ((tok == n) | (tok == n + nt) | (tok == n + 2 * nt)).astype(BF16)
        fq = _dot(fq_ref[...], onehot)
        irow = i_ref[pl.ds(n, 1), :]
        ogrow = og_ref[pl.ds(n, 1), :]
        for h in range(nh):
            sl = slice(h * dk, (h + 1) * dk)
            fb = fq[h * dk:(h + 1) * dk, :]
            qb = fq[(nh + h) * dk:(nh + h + 1) * dk, :]
            s1 = fb * s0_ref[j, h] + (1.0 - fb) * irow[:, sl]
            s_ref[j, h] = s1
            o = jnp.sum(qb * s1, axis=0, keepdims=True)
            hg_ref[j:j + 1, sl] = _rms(o, gn_ref[:, sl]) * _silu(ogrow[:, sl])


def _hg_sample(proj, lb, gn, s0, hg_all, *, row0, nh, col0):
    width = lb.shape[1]
    dk = width // nh
    ns = s0.shape[0]
    tb = SAMPLE_TOKENS_PER_STEP
    rb, cb = row0 // ns, col0 // width
    assert rb * ns == row0 and cb * width == col0 and ns % tb == 0

    def tok(k):
        return pl.BlockSpec((ns, width), lambda t: (rb, cb + k))

    vec = pl.BlockSpec((1, width), lambda t: (0, 0))
    sspec = pl.BlockSpec((tb, nh, dk, dk), lambda t: (t, 0, 0, 0))
    return pl.pallas_call(
        functools.partial(_hg_sample_body, dk=dk),
        out_shape=(jax.ShapeDtypeStruct(hg_all.shape, F32),
                   jax.ShapeDtypeStruct(s0.shape, F32)),
        grid=(ns // tb,),
        in_specs=[tok(0), tok(1), tok(2), tok(3), vec, vec, sspec,
                  pl.BlockSpec(memory_space=pl.ANY)],
        out_specs=(pl.BlockSpec((tb, width), lambda t: (row0 // tb + t, 0)), sspec),
        scratch_shapes=[pltpu.VMEM((2 * nh * dk, 3 * ns), BF16)],
        input_output_aliases={7: 0},
        compiler_params=_cparams("arbitrary"),
        name="hgrn_sample",
    )(proj, proj, proj, proj, lb, gn, s0, hg_all)


def _merge_body(z_ref, hg_ref, gs0_ref, gs1_ref, gh0_ref, gh1_ref, x_ref,
                wglu_ref, bglu_ref, wbs_ref, wbh_ref, wout_ref, post_ref, o_ref):
    z = z_ref[...]
    s5o = z * jax.nn.sigmoid(_dot(z.astype(BF16), wglu_ref[...]) + bglu_ref[...])
    a = _dot(s5o.astype(BF16), wbs_ref[...])
    b = _dot(hg_ref[...].astype(BF16), wbh_ref[...])
    half = gs0_ref.shape[1]
    m0 = jax.nn.sigmoid(gs0_ref[...]) * a[:, :half] + jax.nn.sigmoid(gh0_ref[...]) * b[:, :half]
    m1 = jax.nn.sigmoid(gs1_ref[...]) * a[:, half:] + jax.nn.sigmoid(gh1_ref[...]) * b[:, half:]
    mix = (_dot(m0.astype(BF16), wout_ref[:half, :]) + _dot(m1.astype(BF16), wout_ref[half:, :]))
    o_ref[...] = x_ref[...] + _rms(mix, post_ref[...])


def _merge(z, hg, proj, x, wglu, bglu, wbs, wbh, wout, post, *, col0, tm):
    m, d = x.shape
    w = z.shape[1]
    cb = col0 // w
    assert cb * w == col0 and d == 2 * w

    def gate(k):
        return pl.BlockSpec((tm, w), lambda i: (i, cb + k))

    def const(shape):
        return pl.BlockSpec(shape, lambda i: (0, 0), pipeline_mode=pl.Buffered(1))

    return pl.pallas_call(
        _merge_body,
        out_shape=jax.ShapeDtypeStruct((m, d), F32),
        grid=(m // tm,),
        in_specs=[
            pl.BlockSpec((tm, w), lambda i: (i, 0)),
            pl.BlockSpec((tm, w), lambda i: (i, 0)),
            gate(0), gate(1), gate(2), gate(3),
            pl.BlockSpec((tm, d), lambda i: (i, 0)),
            const((w, w)), const((1, w)), const((w, d)), const((w, d)), const((d, d)), const((1, d)),
        ],
        out_specs=pl.BlockSpec((tm, d), lambda i: (i, 0)),
        compiler_params=_cparams("parallel"),
        name="merge",
    )(z, hg, proj, proj, proj, proj, x, wglu, bglu, wbs, wbh, wout, post)


MERGE_ROW_TILE = 320


def kernel(x_prompt, x_sample, state_s5_re, state_s5_im, state_hgrn, ffn1_pre_norm, ffn1_w_gate, ffn1_w_up, ffn1_w_down, ffn1_post_norm, mix_pre_norm, w_in, s5_lambda_re, s5_lambda_im, s5_log_dt, s5_b_re, s5_b_im, s5_c_re, s5_c_im, s5_d, s5_w_glu, s5_b_glu, hgrn_lb_logits, hgrn_out_norm, w_branch_s5, w_branch_hgrn, w_out, mix_post_norm, ffn2_pre_norm, ffn2_w_gate, ffn2_w_up, ffn2_w_down, ffn2_post_norm):
    depth = ffn1_w_gate.shape[0]
    assert depth == 1
    batch, seq, d = x_prompt.shape
    ns = x_sample.shape[0]
    assert x_sample.shape[1] == 1
    g, p = s5_lambda_re.shape[1:]
    nh, dk = state_hgrn.shape[2], state_hgrn.shape[3]
    s5w = s5_d.shape[1]
    hgw = nh * dk
    mp = batch * seq
    m = mp + ns

    bf = lambda a: a[0].astype(BF16)
    row = lambda a: a[0].reshape(1, -1).astype(F32)

    lb_all = jnp.cumsum(jax.nn.softmax(hgrn_lb_logits.astype(F32), axis=0), axis=0)
    lb = lb_all[0].reshape(1, hgw)

    x = jnp.concatenate([x_prompt.reshape(mp, d), x_sample.reshape(ns, d)], axis=0)
    x = _ffn(x, row(ffn1_pre_norm), bf(ffn1_w_gate), bf(ffn1_w_up), bf(ffn1_w_down),
             row(ffn1_post_norm))
    proj = _inproj(x, row(mix_pre_norm), bf(w_in))

    tw_in, tw_out, a_re, a_im = _s5_tables(
        s5_lambda_re[0], s5_lambda_im[0], s5_log_dt[0], s5_b_re[0], s5_b_im[0],
        s5_c_re[0], s5_c_im[0])
    d_skip = row(s5_d)
    z, hlast = _s5_prompt(proj, tw_in, tw_out, a_re, a_im, d_skip,
                          batch=batch, seq=seq, rows_total=m)
    z, s_re, s_im = _s5_sample(proj, state_s5_re[0].reshape(ns, g * p),
                               state_s5_im[0].reshape(ns, g * p),
                               tw_in, tw_out, a_re, a_im, d_skip, z, row0=mp)
    half = hlast.shape[-1] // 2
    p_re = hlast[:, :, SUBLANES - 1, :half].reshape(1, batch, g, p)
    p_im = hlast[:, :, SUBLANES - 1, half:].reshape(1, batch, g, p)

    gn = row(hgrn_out_norm)
    hg, st_p = _hg_prompt(proj, lb, gn, batch=batch, seq=seq, rows_total=m, nh=nh, col0=s5w)
    hg, st_s = _hg_sample(proj, lb, gn, state_hgrn[0], hg, row0=mp, nh=nh, col0=s5w)

    x = _merge(z, hg, proj, x, bf(s5_w_glu), row(s5_b_glu), bf(w_branch_s5), bf(w_branch_hgrn),
               bf(w_out), row(mix_post_norm), col0=s5w + 4 * hgw, tm=MERGE_ROW_TILE)
    x = _ffn(x, row(ffn2_pre_norm), bf(ffn2_w_gate), bf(ffn2_w_up), bf(ffn2_w_down),
             row(ffn2_post_norm))

    return (x[:mp].reshape(batch, seq, d), x[mp:].reshape(ns, 1, d),
            p_re, p_im, jnp.swapaxes(st_p, -1, -2)[None],
            s_re.reshape(1, ns, g, p), s_im.reshape(1, ns, g, p), st_s[None])
```

```python
import functools

import jax
import jax.numpy as jnp
import numpy as np
from jax import lax
from jax.experimental import pallas as pl
from jax.experimental.pallas import tpu as pltpu

F32 = jnp.float32
BF16 = jnp.bfloat16
EPS = 1e-6
HIGHEST = lax.Precision.HIGHEST

LANES = 128
SUBLANES = 8
VMEM_LIMIT = 56 * 1024 * 1024

ROW_TILE = 640
FF_TILE = 512
IN_TILE = 1024
IN_ROW_TILE = 1040
HG_CHUNK = 64
SAMPLE_TOKENS_PER_STEP = 8


def _cparams(*sem):
    return pltpu.CompilerParams(dimension_semantics=sem, vmem_limit_bytes=VMEM_LIMIT)


def _rms(x, g):
    return x * lax.rsqrt(jnp.mean(x * x, axis=-1, keepdims=True) + EPS) * g


def _silu(x):
    return x * jax.nn.sigmoid(x)


def _dot(a, b):
    return jnp.dot(a, b, preferred_element_type=F32)


def _dot_hi(a, b):
    return jnp.dot(a, b, preferred_element_type=F32, precision=HIGHEST)


def _dot_nt(a, b):
    return lax.dot_general(a, b, (((1,), (1,)), ((), ())), preferred_element_type=F32)


def _dot_tn(a, b):
    return lax.dot_general(a, b, (((0,), (0,)), ((), ())), preferred_element_type=F32)


def _ffn_body(*refs, ns, split_in, split_out):
    refs = list(refs)
    x_ref = refs.pop(0)
    xs_ref = refs.pop(0) if split_in else None
    pre_ref, wg_ref, wu_ref, wd_ref, post_ref, o_ref = refs[:6]
    os_ref = refs[6] if split_out else None
    h_ref, acc_ref = refs[-2:]
    i, k = pl.program_id(0), pl.program_id(1)
    last_i = pl.num_programs(0) - 1
    cut = h_ref.shape[0] - ns

    @pl.when(k == 0)
    def _():
        acc_ref[...] = jnp.zeros_like(acc_ref)
        if split_in:
            @pl.when(i < last_i)
            def _():
                h_ref[...] = _rms(x_ref[...], pre_ref[...]).astype(BF16)

            @pl.when(i == last_i)
            def _():
                h_ref[:cut, :] = _rms(x_ref[:cut, :], pre_ref[...]).astype(BF16)
                h_ref[cut:, :] = _rms(xs_ref[...], pre_ref[...]).astype(BF16)
        else:
            h_ref[...] = _rms(x_ref[...], pre_ref[...]).astype(BF16)

    h = h_ref[...]
    g = _dot(h, wg_ref[...])
    u = _dot(h, wu_ref[...])
    acc_ref[...] += _dot((_silu(g) * u).astype(BF16), wd_ref[...])

    @pl.when(k == pl.num_programs(1) - 1)
    def _():
        @pl.when(i < last_i)
        def _():
            o_ref[...] = x_ref[...] + 0.5 * _rms(acc_ref[...], post_ref[...])

        @pl.when(i == last_i)
        def _():
            o_ref[:cut, :] = x_ref[:cut, :] + 0.5 * _rms(acc_ref[:cut, :], post_ref[...])
            xt = xs_ref[...] if split_in else x_ref[cut:, :]
            tail = xt + 0.5 * _rms(acc_ref[cut:, :], post_ref[...])
            if split_out:
                os_ref[...] = tail
            else:
                o_ref[cut:, :] = tail


def _ffn(x, xs, pre, wg, wu, wd, post, *, ns, split_out):
    split_in = xs is not None
    d = x.shape[1]
    m = x.shape[0] + (ns if split_in else 0)
    dff = wg.shape[1]
    tm, tf = ROW_TILE, FF_TILE
    assert m % tm == 0 and 0 < ns < tm and ns % SUBLANES == 0
    tok = pl.BlockSpec((tm, d), lambda i, k: (i, 0))
    smp = pl.BlockSpec((ns, d), lambda i, k: (0, 0))
    vec = pl.BlockSpec((1, d), lambda i, k: (0, 0))
    in_specs = [tok] + ([smp] if split_in else []) + [
        vec,
        pl.BlockSpec((d, tf), lambda i, k: (0, k)),
        pl.BlockSpec((d, tf), lambda i, k: (0, k)),
        pl.BlockSpec((tf, d), lambda i, k: (k, 0)),
        vec,
    ]
    if split_out:
        out_shape = (jax.ShapeDtypeStruct((m - ns, d), F32), jax.ShapeDtypeStruct((ns, d), F32))
        out_specs = (tok, smp)
    else:
        out_shape = jax.ShapeDtypeStruct((m, d), F32)
        out_specs = tok
    args = [x] + ([xs] if split_in else []) + [pre, wg, wu, wd, post]
    return pl.pallas_call(
        functools.partial(_ffn_body, ns=ns, split_in=split_in, split_out=split_out),
        out_shape=out_shape,
        grid=(m // tm, dff // tf),
        in_specs=in_specs,
        out_specs=out_specs,
        scratch_shapes=[pltpu.VMEM((tm, d), BF16), pltpu.VMEM((tm, d), F32)],
        compiler_params=_cparams("arbitrary" if split_out else "parallel", "arbitrary"),
        name="ffn",
    )(*args)


def _inproj_body(x_ref, g_ref, w_ref, o_ref, h_ref):
    @pl.when(pl.program_id(1) == 0)
    def _():
        h_ref[...] = _rms(x_ref[...], g_ref[...]).astype(BF16)

    o_ref[...] = _dot(h_ref[...], w_ref[...])


def _inproj(x, g, w):
    m, d = x.shape
    n = w.shape[1]
    tm, tn = IN_ROW_TILE, IN_TILE
    assert m % tm == 0
    return pl.pallas_call(
        _inproj_body,
        out_shape=jax.ShapeDtypeStruct((m, n), F32),
        grid=(m // tm, n // tn),
        in_specs=[
            pl.BlockSpec((tm, d), lambda i, j: (i, 0)),
            pl.BlockSpec((1, d), lambda i, j: (0, 0)),
            pl.BlockSpec((d, tn), lambda i, j: (0, j)),
        ],
        out_specs=pl.BlockSpec((tm, tn), lambda i, j: (i, j)),
        scratch_shapes=[pltpu.VMEM((tm, d), BF16)],
        compiler_params=_cparams("parallel", "arbitrary"),
        name="inproj",
    )(x, g, w)


def _s5_tables(lam_re, lam_im, log_dt, b_re, b_im, c_re, c_im):
    g, p = lam_re.shape
    n = b_re.shape[-1]
    gpt = LANES // n
    nv = g // gpt
    dt = jnp.exp(log_dt)[:, None]
    er = jnp.exp(lam_re * dt)
    th = lam_im * dt
    a_re, a_im = er * jnp.cos(th), er * jnp.sin(th)
    den = lam_re * lam_re + lam_im * lam_im
    k_re = ((a_re - 1.0) * lam_re + a_im * lam_im) / den
    k_im = (a_im * lam_re - (a_re - 1.0) * lam_im) / den
    bb_re = k_re[..., None] * b_re - k_im[..., None] * b_im
    bb_im = k_re[..., None] * b_im + k_im[..., None] * b_re
    eye = jnp.eye(gpt, dtype=F32)

    def in_blk(bb):
        t = bb.reshape(nv, gpt, p, n)
        return jnp.einsum("ab,vapn->vanbp", eye, t).reshape(nv, gpt * n, gpt * p)

    def out_blk(cc):
        t = cc.reshape(nv, gpt, n, p)
        return jnp.einsum("ab,vanp->vapbn", eye, t).reshape(nv, gpt * p, gpt * n)

    w_in = jnp.concatenate([in_blk(bb_re), in_blk(bb_im)], axis=-1)
    w_out = jnp.concatenate([out_blk(c_re), -out_blk(c_im)], axis=1)
    return w_in, w_out, a_re.reshape(nv, 1, gpt * p), a_im.reshape(nv, 1, gpt * p)


def _cmul(ar, ai, br, bi):
    return ar * br - ai * bi, ar * bi + ai * br


def _s5_prompt_body(u_ref, w_ref, cm_ref, ar_ref, ai_ref, d_ref, z_ref, hl_ref, buh_ref,
                    *, seg, pitch):
    nseg = SUBLANES
    nc = w_ref.shape[2] // LANES
    hc = nc // 2

    for j in range(nseg):
        bu = _dot(u_ref[j * seg:(j + 1) * seg, :].astype(BF16), w_ref[0])
        for c in range(nc):
            buh_ref[c, j * pitch:j * pitch + seg, :] = bu[:, c * LANES:(c + 1) * LANES]

    ar = [jnp.broadcast_to(ar_ref[0, :, c * LANES:(c + 1) * LANES], (nseg, LANES)) for c in range(hc)]
    ai = [jnp.broadcast_to(ai_ref[0, :, c * LANES:(c + 1) * LANES], (nseg, LANES)) for c in range(hc)]

    def rows(t):
        return pl.ds(t, nseg, stride=pitch)

    def advance(t, hs):
        out = []
        for c in range(hc):
            pr, pi = _cmul(ar[c], ai[c], hs[2 * c], hs[2 * c + 1])
            out += [pr + buh_ref[c, rows(t), :], pi + buh_ref[hc + c, rows(t), :]]
        return tuple(out)

    zeros = tuple(jnp.zeros((nseg, LANES), F32) for _ in range(2 * hc))
    ends = lax.fori_loop(0, seg, advance, zeros, unroll=4)

    pw = [(ar[c], ai[c]) for c in range(hc)]
    for _ in range(seg.bit_length() - 1):
        pw = [_cmul(r, i, r, i) for r, i in pw]

    row = lax.broadcasted_iota(jnp.int32, (nseg, LANES), 0)
    init = list(zeros)
    for j in range(1, nseg):
        for c in range(hc):
            pr, pi = _cmul(pw[c][0], pw[c][1], init[2 * c], init[2 * c + 1])
            nr = pltpu.roll(pr + ends[2 * c], 1, 0)
            ni = pltpu.roll(pi + ends[2 * c + 1], 1, 0)
            init[2 * c] = jnp.where(row == j, nr, init[2 * c])
            init[2 * c + 1] = jnp.where(row == j, ni, init[2 * c + 1])

    def advance_store(t, hs):
        new = advance(t, hs)
        for c in range(hc):
            buh_ref[c, rows(t), :] = new[2 * c]
            buh_ref[hc + c, rows(t), :] = new[2 * c + 1]
        return new

    last = lax.fori_loop(0, seg, advance_store, tuple(init), unroll=4)
    for c in range(hc):
        hl_ref[0, 0, :, c * LANES:(c + 1) * LANES] = last[2 * c]
        hl_ref[0, 0, :, (hc + c) * LANES:(hc + c + 1) * LANES] = last[2 * c + 1]

    for j in range(nseg):
        y = d_ref[...] * u_ref[j * seg:(j + 1) * seg, :]
        for c in range(nc):
            y = y + _dot(buh_ref[c, j * pitch:j * pitch + seg, :].astype(BF16),
                         cm_ref[0, c * LANES:(c + 1) * LANES, :])
        z_ref[j * seg:(j + 1) * seg, :] = jax.nn.gelu(y)


def _s5_prompt(proj, w_in, w_out, a_re, a_im, d_skip, *, batch, seq, rows_total):
    nv = w_in.shape[0]
    sw = w_in.shape[2]
    seg = seq // SUBLANES
    assert seg * SUBLANES == seq and seg & (seg - 1) == 0
    pitch = seg + SUBLANES
    body = functools.partial(_s5_prompt_body, seg=seg, pitch=pitch)
    return pl.pallas_call(
        body,
        out_shape=(jax.ShapeDtypeStruct((rows_total, nv * LANES), F32),
                   jax.ShapeDtypeStruct((batch, nv, SUBLANES, sw), F32)),
        grid=(batch, nv),
        in_specs=[
            pl.BlockSpec((seq, LANES), lambda b, v: (b, v)),
            pl.BlockSpec((1, LANES, sw), lambda b, v: (v, 0, 0)),
            pl.BlockSpec((1, sw, LANES), lambda b, v: (v, 0, 0)),
            pl.BlockSpec((1, 1, sw // 2), lambda b, v: (v, 0, 0)),
            pl.BlockSpec((1, 1, sw // 2), lambda b, v: (v, 0, 0)),
            pl.BlockSpec((1, LANES), lambda b, v: (0, v)),
        ],
        out_specs=(pl.BlockSpec((seq, LANES), lambda b, v: (b, v)),
                   pl.BlockSpec((1, 1, SUBLANES, sw), lambda b, v: (b, v, 0, 0))),
        scratch_shapes=[pltpu.VMEM((sw // LANES, SUBLANES * pitch, LANES), F32)],
        compiler_params=_cparams("parallel", "parallel"),
        name="s5_prompt",
    )(proj, w_in, w_out, a_re, a_im, d_skip)


def _s5_sample_body(u_ref, hre_ref, him_ref, w_ref, cm_ref, ar_ref, ai_ref, d_ref, zin_ref,
                    z_ref, ore_ref, oim_ref):
    del zin_ref
    half = ar_ref.shape[2]
    u = u_ref[...]
    bu = _dot_hi(u, w_ref[0])
    pr, pi = _cmul(ar_ref[0], ai_ref[0], hre_ref[...], him_ref[...])
    hr = pr + bu[:, :half]
    hi = pi + bu[:, half:]
    ore_ref[...] = hr
    oim_ref[...] = hi
    y = d_ref[...] * u + _dot_hi(hr, cm_ref[0, :half, :]) + _dot_hi(hi, cm_ref[0, half:, :])
    z_ref[...] = jax.nn.gelu(y)


def _s5_sample(proj, h_re, h_im, w_in, w_out, a_re, a_im, d_skip, z_all, *, row0):
    nv = w_in.shape[0]
    sw = w_in.shape[2]
    ns = h_re.shape[0]
    rb = row0 // ns
    assert rb * ns == row0
    st = pl.BlockSpec((ns, sw // 2), lambda v: (0, v))
    return pl.pallas_call(
        _s5_sample_body,
        out_shape=(jax.ShapeDtypeStruct(z_all.shape, F32),
                   jax.ShapeDtypeStruct(h_re.shape, F32),
                   jax.ShapeDtypeStruct(h_im.shape, F32)),
        grid=(nv,),
        in_specs=[
            pl.BlockSpec((ns, LANES), lambda v: (rb, v)),
            st, st,
            pl.BlockSpec((1, LANES, sw), lambda v: (v, 0, 0)),
            pl.BlockSpec((1, sw, LANES), lambda v: (v, 0, 0)),
            pl.BlockSpec((1, 1, sw // 2), lambda v: (v, 0, 0)),
            pl.BlockSpec((1, 1, sw // 2), lambda v: (v, 0, 0)),
            pl.BlockSpec((1, LANES), lambda v: (0, v)),
            pl.BlockSpec(memory_space=pl.ANY),
        ],
        out_specs=(pl.BlockSpec((ns, LANES), lambda v: (rb, v)), st, st),
        input_output_aliases={8: 0},
        compiler_params=_cparams("parallel"),
        name="s5_sample",
    )(proj, h_re, h_im, w_in, w_out, a_re, a_im, d_skip, z_all)


def _hg_span_matrices(ch):
    t = np.arange(ch)[:, None]
    r = np.arange(ch)[None, :]
    mats = [r <= t, r > t]
    for v in range(ch.bit_length() - 1):
        base = (t >> v) << v
        upper = ((t >> v) & 1) == 1
        mats.append(np.where(upper, (r >= base) & (r <= t), (r > t) & (r < base + (1 << v))))
    mm = np.stack(mats).astype(np.float32)
    return jnp.asarray(np.concatenate([mm, mm, mm], axis=-1), BF16)


def _hg_prompt_body(q_ref, f_ref, i_ref, og_ref, lb_ref, gn_ref, mm_ref, o_ref, sfin_ref, st_ref,
                    *, dk):
    c = pl.program_id(1)
    nh = st_ref.shape[0]
    ch, width = q_ref.shape
    nlev = ch.bit_length() - 1

    @pl.when(c == 0)
    def _():
        st_ref[...] = jnp.zeros_like(st_ref)

    lb = lb_ref[...]
    f = lb + (1.0 - lb) * jax.nn.sigmoid(f_ref[...])
    kk = 1.0 - f
    qs = _silu(q_ref[...])
    logf3 = jnp.concatenate(_split3(jnp.log(f)), axis=0)

    def decay(idx):
        return jnp.exp(_dot(mm_ref[idx], logf3))

    eg = decay(0)
    qg = (qs * eg).astype(BF16)
    dec = eg[ch - 1:ch, :]
    kh = (kk * decay(1)).astype(BF16)
    trow = lax.broadcasted_iota(jnp.int32, (ch, width), 0)
    lev = [(decay(2 + v) * jnp.where(((trow >> v) & 1) == 1, qs, kk)).astype(BF16)
           for v in range(nlev)]
    qb, kb, ib = qs.astype(BF16), kk.astype(BF16), i_ref[...].astype(BF16)
    r = lax.broadcasted_iota(jnp.int32, (ch, ch), 0)
    s = lax.broadcasted_iota(jnp.int32, (ch, ch), 1)
    xr = jnp.where(r > s, r ^ s, 0)
    heads = [slice(h * dk, (h + 1) * dk) for h in range(nh)]
    atts = []
    for sl in heads:
        att = jnp.where(r == s, _dot_nt(qb[:, sl], kb[:, sl]), 0.0)
        for v in range(nlev):
            att = jnp.where((xr >> v) == 1, _dot_nt(lev[v][:, sl], lev[v][:, sl]), att)
        atts.append(att.astype(BF16))
    for h, sl in enumerate(heads):
        o = _dot_nt(qg[:, sl], st_ref[h].astype(BF16)) + _dot(atts[h], ib[:, sl])
        og = og_ref[:, sl]
        o_ref[:, sl] = _rms(o, gn_ref[:, sl]) * _silu(og)
    for h, sl in enumerate(heads):
        st_ref[h] = st_ref[h] * dec[:, sl] + _dot_tn(ib[:, sl], kh[:, sl])

    @pl.when(c == pl.num_programs(1) - 1)
    def _():
        sfin_ref[0] = st_ref[...]


def _hg_prompt(proj, lb, gn, *, batch, seq, rows_total, nh, col0):
    width = lb.shape[1]
    dk = width // nh
    ch = HG_CHUNK
    nchunk = seq // ch
    cb = col0 // width
    assert cb * width == col0 and nchunk * ch == seq

    def tok(k):
        return pl.BlockSpec((ch, width), lambda b, c: (b * nchunk + c, cb + k))

    vec = pl.BlockSpec((1, width), lambda b, c: (0, 0))
    mm = _hg_span_matrices(ch)
    return pl.pallas_call(
        functools.partial(_hg_prompt_body, dk=dk),
        out_shape=(jax.ShapeDtypeStruct((rows_total, width), F32),
                   jax.ShapeDtypeStruct((batch, nh, dk, dk), F32)),
        grid=(batch, nchunk),
        in_specs=[tok(0), tok(1), tok(2), tok(3), vec, vec,
                  pl.BlockSpec(mm.shape, lambda b, c: (0, 0, 0))],
        out_specs=(pl.BlockSpec((ch, width), lambda b, c: (b * nchunk + c, 0)),
                   pl.BlockSpec((1, nh, dk, dk), lambda b, c: (b, 0, 0, 0))),
        scratch_shapes=[pltpu.VMEM((nh, dk, dk), F32)],
        compiler_params=_cparams("parallel", "arbitrary"),
        name="hgrn_prompt",
    )(proj, proj, proj, proj, lb, gn, mm)


def _split3(x):
    p1 = x.astype(BF16)
    r1 = x - p1.astype(F32)
    p2 = r1.astype(BF16)
    p3 = (r1 - p2.astype(F32)).astype(BF16)
    return p1, p2, p3


def _hg_sample_body(q_ref, f_ref, i_ref, og_ref, lb_ref, gn_ref, s0_ref, hgin_ref,
                    hg_ref, s_ref, fq_ref, *, dk):
    del hgin_ref
    step = pl.program_id(0)
    nt = q_ref.shape[0]
    tb, nh = s0_ref.shape[0], s0_ref.shape[1]

    @pl.when(step == 0)
    def _():
        lb = lb_ref[...]
        f = lb + (1.0 - lb) * jax.nn.sigmoid(f_ref[...])
        qs = _silu(q_ref[...])
        for src, base in ((f, 0), (qs, nh)):
            for h in range(nh):
                t = src[:, h * dk:(h + 1) * dk].T
                for p, piece in enumerate(_split3(t)):
                    fq_ref[(base + h) * dk:(base + h + 1) * dk, p * nt:(p + 1) * nt] = piece

    tok = lax.broadcasted_iota(jnp.int32, (3 * nt, dk), 0)
    for j in range(tb):
        n = step * tb + j
        onehot = ((tok == n) | (tok == n + nt) | (tok == n + 2 * nt)).astype(BF16)
        fq = _dot(fq_ref[...], onehot)
        irow = i_ref[pl.ds(n, 1), :]
        ogrow = og_ref[pl.ds(n, 1), :]
        for h in range(nh):
            sl = slice(h * dk, (h + 1) * dk)
            fb = fq[h * dk:(h + 1) * dk, :]
            qb = fq[(nh + h) * dk:(nh + h + 1) * dk, :]
            s1 = fb * s0_ref[j, h] + (1.0 - fb) * irow[:, sl]
            s_ref[j, h] = s1
            o = jnp.sum(qb * s1, axis=0, keepdims=True)
            hg_ref[j:j + 1, sl] = _rms(o, gn_ref[:, sl]) * _silu(ogrow[:, sl])


def _hg_sample(proj, lb, gn, s0, hg_all, *, row0, nh, col0):
    width = lb.shape[1]
    dk = width // nh
    ns = s0.shape[0]
    tb = SAMPLE_TOKENS_PER_STEP
    rb, cb = row0 // ns, col0 // width
    assert rb * ns == row0 and cb * width == col0 and ns % tb == 0

    def tok(k):
        return pl.BlockSpec((ns, width), lambda t: (rb, cb + k))

    vec = pl.BlockSpec((1, width), lambda t: (0, 0))
    sspec = pl.BlockSpec((tb, nh, dk, dk), lambda t: (t, 0, 0, 0))
    return pl.pallas_call(
        functools.partial(_hg_sample_body, dk=dk),
        out_shape=(jax.ShapeDtypeStruct(hg_all.shape, F32),
                   jax.ShapeDtypeStruct(s0.shape, F32)),
        grid=(ns // tb,),
        in_specs=[tok(0), tok(1), tok(2), tok(3), vec, vec, sspec,
                  pl.BlockSpec(memory_space=pl.ANY)],
        out_specs=(pl.BlockSpec((tb, width), lambda t: (row0 // tb + t, 0)), sspec),
        scratch_shapes=[pltpu.VMEM((2 * nh * dk, 3 * ns), BF16)],
        input_output_aliases={7: 0},
        compiler_params=_cparams("arbitrary"),
        name="hgrn_sample",
    )(proj, proj, proj, proj, lb, gn, s0, hg_all)


def _merge_body(z_ref, hg_ref, gs0_ref, gs1_ref, gh0_ref, gh1_ref, x_ref,
                wglu_ref, bglu_ref, wbs_ref, wbh_ref, wout_ref, post_ref, o_ref):
    z = z_ref[...]
    s5o = z * jax.nn.sigmoid(_dot(z.astype(BF16), wglu_ref[...]) + bglu_ref[...])
    a = _dot(s5o.astype(BF16), wbs_ref[...])
    b = _dot(hg_ref[...].astype(BF16), wbh_ref[...])
    half = gs0_ref.shape[1]
    m0 = jax.nn.sigmoid(gs0_ref[...]) * a[:, :half] + jax.nn.sigmoid(gh0_ref[...]) * b[:, :half]
    m1 = jax.nn.sigmoid(gs1_ref[...]) * a[:, half:] + jax.nn.sigmoid(gh1_ref[...]) * b[:, half:]
    mix = (_dot(m0.astype(BF16), wout_ref[:half, :]) + _dot(m1.astype(BF16), wout_ref[half:, :]))
    o_ref[...] = x_ref[...] + _rms(mix, post_ref[...])


def _merge(z, hg, proj, x, wglu, bglu, wbs, wbh, wout, post, *, col0, tm):
    m, d = x.shape
    w = z.shape[1]
    cb = col0 // w
    assert cb * w == col0 and d == 2 * w

    def gate(k):
        return pl.BlockSpec((tm, w), lambda i: (i, cb + k))

    def const(shape):
        return pl.BlockSpec(shape, lambda i: (0, 0), pipeline_mode=pl.Buffered(1))

    return pl.pallas_call(
        _merge_body,
        out_shape=jax.ShapeDtypeStruct((m, d), F32),
        grid=(m // tm,),
        in_specs=[
            pl.BlockSpec((tm, w), lambda i: (i, 0)),
            pl.BlockSpec((tm, w), lambda i: (i, 0)),
            gate(0), gate(1), gate(2), gate(3),
            pl.BlockSpec((tm, d), lambda i: (i, 0)),
            const((w, w)), const((1, w)), const((w, d)), const((w, d)), const((d, d)), const((1, d)),
        ],
        out_specs=pl.BlockSpec((tm, d), lambda i: (i, 0)),
        compiler_params=_cparams("parallel"),
        name="merge",
    )(z, hg, proj, proj, proj, proj, x, wglu, bglu, wbs, wbh, wout, post)


MERGE_ROW_TILE = 320


def kernel(x_prompt, x_sample, state_s5_re, state_s5_im, state_hgrn, ffn1_pre_norm, ffn1_w_gate, ffn1_w_up, ffn1_w_down, ffn1_post_norm, mix_pre_norm, w_in, s5_lambda_re, s5_lambda_im, s5_log_dt, s5_b_re, s5_b_im, s5_c_re, s5_c_im, s5_d, s5_w_glu, s5_b_glu, hgrn_lb_logits, hgrn_out_norm, w_branch_s5, w_branch_hgrn, w_out, mix_post_norm, ffn2_pre_norm, ffn2_w_gate, ffn2_w_up, ffn2_w_down, ffn2_post_norm):
    depth = ffn1_w_gate.shape[0]
    assert depth == 1
    batch, seq, d = x_prompt.shape
    ns = x_sample.shape[0]
    assert x_sample.shape[1] == 1
    g, p = s5_lambda_re.shape[1:]
    nh, dk = state_hgrn.shape[2], state_hgrn.shape[3]
    s5w = s5_d.shape[1]
    hgw = nh * dk
    mp = batch * seq
    m = mp + ns

    bf = lambda a: a[0].astype(BF16)
    row = lambda a: a[0].reshape(1, -1).astype(F32)

    lb_all = jnp.cumsum(jax.nn.softmax(hgrn_lb_logits.astype(F32), axis=0), axis=0)
    lb = lb_all[0].reshape(1, hgw)

    x = _ffn(x_prompt.reshape(mp, d), x_sample.reshape(ns, d), row(ffn1_pre_norm),
             bf(ffn1_w_gate), bf(ffn1_w_up), bf(ffn1_w_down), row(ffn1_post_norm),
             ns=ns, split_out=False)
    proj = _inproj(x, row(mix_pre_norm), bf(w_in))

    tw_in, tw_out, a_re, a_im = _s5_tables(
        s5_lambda_re[0], s5_lambda_im[0], s5_log_dt[0], s5_b_re[0], s5_b_im[0],
        s5_c_re[0], s5_c_im[0])
    d_skip = row(s5_d)
    z, hlast = _s5_prompt(proj, tw_in.astype(BF16), tw_out.astype(BF16), a_re, a_im, d_skip,
                          batch=batch, seq=seq, rows_total=m)
    z, s_re, s_im = _s5_sample(proj, state_s5_re[0].reshape(ns, g * p),
                               state_s5_im[0].reshape(ns, g * p),
                               tw_in, tw_out, a_re, a_im, d_skip, z, row0=mp)
    half = hlast.shape[-1] // 2
    p_re = hlast[:, :, SUBLANES - 1, :half].reshape(1, batch, g, p)
    p_im = hlast[:, :, SUBLANES - 1, half:].reshape(1, batch, g, p)

    gn = row(hgrn_out_norm)
    hg, st_p = _hg_prompt(proj, lb, gn, batch=batch, seq=seq, rows_total=m, nh=nh, col0=s5w)
    hg, st_s = _hg_sample(proj, lb, gn, state_hgrn[0], hg, row0=mp, nh=nh, col0=s5w)

    x = _merge(z, hg, proj, x, bf(s5_w_glu), row(s5_b_glu), bf(w_branch_s5), bf(w_branch_hgrn),
               bf(w_out), row(mix_post_norm), col0=s5w + 4 * hgw, tm=MERGE_ROW_TILE)
    yp, ys = _ffn(x, None, row(ffn2_pre_norm), bf(ffn2_w_gate), bf(ffn2_w_up), bf(ffn2_w_down),
                  row(ffn2_post_norm), ns=ns, split_out=True)

    return (yp.reshape(batch, seq, d), ys.reshape(ns, 1, d),
            p_re, p_im, jnp.swapaxes(st_p, -1, -2)[None],
            s_re.reshape(1, ns, g, p), s_im.reshape(1, ns, g, p), st_s[None])
```

```python
import functools

import jax
import jax.numpy as jnp
import numpy as np
from jax import lax
from jax.experimental import pallas as pl
from jax.experimental.pallas import tpu as pltpu

F32 = jnp.float32
BF16 = jnp.bfloat16
EPS = 1e-6
HIGHEST = lax.Precision.HIGHEST

LANES = 128
SUBLANES = 8
VMEM_LIMIT = 56 * 1024 * 1024

ROW_TILE = 640
FF_TILE = 512
IN_TILE = 1024
IN_ROW_TILE = 1040
HG_CHUNK = 64
SAMPLE_TOKENS_PER_STEP = 8


def _cparams(*sem):
    return pltpu.CompilerParams(dimension_semantics=sem, vmem_limit_bytes=VMEM_LIMIT)


def _rms(x, g):
    return x * lax.rsqrt(jnp.mean(x * x, axis=-1, keepdims=True) + EPS) * g


def _silu(x):
    return x * jax.nn.sigmoid(x)


def _dot(a, b):
    return jnp.dot(a, b, preferred_element_type=F32)


def _dot_hi(a, b):
    return jnp.dot(a, b, preferred_element_type=F32, precision=HIGHEST)


def _dot_nt(a, b):
    return lax.dot_general(a, b, (((1,), (1,)), ((), ())), preferred_element_type=F32)


def _dot_tn(a, b):
    return lax.dot_general(a, b, (((0,), (0,)), ((), ())), preferred_element_type=F32)


def _ffn_body(*refs, ns, split_in, split_out):
    refs = list(refs)
    x_ref = refs.pop(0)
    xs_ref = refs.pop(0) if split_in else None
    pre_ref, wg_ref, wu_ref, wd_ref, post_ref, o_ref = refs[:6]
    os_ref = refs[6] if split_out else None
    h_ref, acc_ref = refs[-2:]
    i, k = pl.program_id(0), pl.program_id(1)
    last_i = pl.num_programs(0) - 1
    last_k = pl.num_programs(1) - 1
    cut = h_ref.shape[0] - ns

    def swiglu_down(h):
        g = _dot(h, wg_ref[...])
        u = _dot(h, wu_ref[...])
        return _dot((_silu(g) * u).astype(BF16), wd_ref[...])

    @pl.when((i < last_i) & (k == 0))
    def _():
        h = _rms(x_ref[...], pre_ref[...]).astype(BF16)
        h_ref[...] = h
        acc_ref[...] = swiglu_down(h)

    @pl.when((i < last_i) & (k == last_k))
    def _():
        acc = acc_ref[...] + swiglu_down(h_ref[...])
        o_ref[...] = x_ref[...] + 0.5 * _rms(acc, post_ref[...])

    @pl.when((i == last_i) & (k == 0))
    def _():
        h_ref[:cut, :] = _rms(x_ref[:cut, :], pre_ref[...]).astype(BF16)
        xt = xs_ref[...] if split_in else x_ref[cut:, :]
        h_ref[cut:, :] = _rms(xt, pre_ref[...]).astype(BF16)
        acc_ref[...] = jnp.zeros_like(acc_ref)

    @pl.when(((k > 0) & (k < last_k)) | (i == last_i))
    def _():
        acc_ref[...] += swiglu_down(h_ref[...])

    @pl.when((i == last_i) & (k == last_k))
    def _():
        o_ref[:cut, :] = x_ref[:cut, :] + 0.5 * _rms(acc_ref[:cut, :], post_ref[...])
        xt = xs_ref[...] if split_in else x_ref[cut:, :]
        tail = xt + 0.5 * _rms(acc_ref[cut:, :], post_ref[...])
        if split_out:
            os_ref[...] = tail
        else:
            o_ref[cut:, :] = tail


def _ffn(x, xs, pre, wg, wu, wd, post, *, ns, split_out):
    split_in = xs is not None
    d = x.shape[1]
    m = x.shape[0] + (ns if split_in else 0)
    dff = wg.shape[1]
    tm, tf = ROW_TILE, FF_TILE
    assert m % tm == 0 and 0 < ns < tm and ns % SUBLANES == 0
    tok = pl.BlockSpec((tm, d), lambda i, k: (i, 0))
    smp = pl.BlockSpec((ns, d), lambda i, k: (0, 0))
    vec = pl.BlockSpec((1, d), lambda i, k: (0, 0))
    in_specs = [tok] + ([smp] if split_in else []) + [
        vec,
        pl.BlockSpec((d, tf), lambda i, k: (0, k)),
        pl.BlockSpec((d, tf), lambda i, k: (0, k)),
        pl.BlockSpec((tf, d), lambda i, k: (k, 0)),
        vec,
    ]
    if split_out:
        out_shape = (jax.ShapeDtypeStruct((m - ns, d), F32), jax.ShapeDtypeStruct((ns, d), F32))
        out_specs = (tok, smp)
    else:
        out_shape = jax.ShapeDtypeStruct((m, d), F32)
        out_specs = tok
    args = [x] + ([xs] if split_in else []) + [pre, wg, wu, wd, post]
    return pl.pallas_call(
        functools.partial(_ffn_body, ns=ns, split_in=split_in, split_out=split_out),
        out_shape=out_shape,
        grid=(m // tm, dff // tf),
        in_specs=in_specs,
        out_specs=out_specs,
        scratch_shapes=[pltpu.VMEM((tm, d), BF16), pltpu.VMEM((tm, d), F32)],
        compiler_params=_cparams("arbitrary" if split_out else "parallel", "arbitrary"),
        name="ffn",
    )(*args)


def _inproj_body(x_ref, g_ref, w_ref, o_ref, h_ref):
    j = pl.program_id(1)

    @pl.when(j == 0)
    def _():
        h = _rms(x_ref[...], g_ref[...]).astype(BF16)
        h_ref[...] = h
        o_ref[...] = _dot(h, w_ref[...])

    @pl.when(j > 0)
    def _():
        o_ref[...] = _dot(h_ref[...], w_ref[...])


def _inproj(x, g, w):
    m, d = x.shape
    n = w.shape[1]
    tm, tn = IN_ROW_TILE, IN_TILE
    assert m % tm == 0
    return pl.pallas_call(
        _inproj_body,
        out_shape=jax.ShapeDtypeStruct((m, n), F32),
        grid=(m // tm, n // tn),
        in_specs=[
            pl.BlockSpec((tm, d), lambda i, j: (i, 0)),
            pl.BlockSpec((1, d), lambda i, j: (0, 0)),
            pl.BlockSpec((d, tn), lambda i, j: (0, j)),
        ],
        out_specs=pl.BlockSpec((tm, tn), lambda i, j: (i, j)),
        scratch_shapes=[pltpu.VMEM((tm, d), BF16)],
        compiler_params=_cparams("parallel", "arbitrary"),
        name="inproj",
    )(x, g, w)


def _s5_tables(lam_re, lam_im, log_dt, b_re, b_im, c_re, c_im):
    g, p = lam_re.shape
    n = b_re.shape[-1]
    gpt = LANES // n
    nv = g // gpt
    dt = jnp.exp(log_dt)[:, None]
    er = jnp.exp(lam_re * dt)
    th = lam_im * dt
    a_re, a_im = er * jnp.cos(th), er * jnp.sin(th)
    den = lam_re * lam_re + lam_im * lam_im
    k_re = ((a_re - 1.0) * lam_re + a_im * lam_im) / den
    k_im = (a_im * lam_re - (a_re - 1.0) * lam_im) / den
    bb_re = k_re[..., None] * b_re - k_im[..., None] * b_im
    bb_im = k_re[..., None] * b_im + k_im[..., None] * b_re
    eye = jnp.eye(gpt, dtype=F32)

    def in_blk(bb):
        t = bb.reshape(nv, gpt, p, n)
        return jnp.einsum("ab,vapn->vanbp", eye, t).reshape(nv, gpt * n, gpt * p)

    def out_blk(cc):
        t = cc.reshape(nv, gpt, n, p)
        return jnp.einsum("ab,vanp->vapbn", eye, t).reshape(nv, gpt * p, gpt * n)

    w_in = jnp.concatenate([in_blk(bb_re), in_blk(bb_im)], axis=-1)
    w_out = jnp.concatenate([out_blk(c_re), -out_blk(c_im)], axis=1)
    return w_in, w_out, a_re.reshape(nv, 1, gpt * p), a_im.reshape(nv, 1, gpt * p)


def _cmul(ar, ai, br, bi):
    return ar * br - ai * bi, ar * bi + ai * br


def _s5_prompt_body(u_ref, w_ref, cm_ref, ar_ref, ai_ref, d_ref, z_ref, hl_ref, buh_ref,
                    *, seg, pitch):
    nseg = SUBLANES
    nc = w_ref.shape[2] // LANES
    hc = nc // 2

    for j in range(nseg):
        bu = _dot(u_ref[j * seg:(j + 1) * seg, :].astype(BF16), w_ref[0])
        for c in range(nc):
            buh_ref[c, j * pitch:j * pitch + seg, :] = bu[:, c * LANES:(c + 1) * LANES]

    ar = [jnp.broadcast_to(ar_ref[0, :, c * LANES:(c + 1) * LANES], (nseg, LANES)) for c in range(hc)]
    ai = [jnp.broadcast_to(ai_ref[0, :, c * LANES:(c + 1) * LANES], (nseg, LANES)) for c in range(hc)]

    def rows(t):
        return pl.ds(t, nseg, stride=pitch)

    def advance(t, hs):
        out = []
        for c in range(hc):
            pr, pi = _cmul(ar[c], ai[c], hs[2 * c], hs[2 * c + 1])
            out += [pr + buh_ref[c, rows(t), :], pi + buh_ref[hc + c, rows(t), :]]
        return tuple(out)

    zeros = tuple(jnp.zeros((nseg, LANES), F32) for _ in range(2 * hc))
    ends = lax.fori_loop(0, seg, advance, zeros, unroll=4)

    pw = [(ar[c], ai[c]) for c in range(hc)]
    for _ in range(seg.bit_length() - 1):
        pw = [_cmul(r, i, r, i) for r, i in pw]

    row = lax.broadcasted_iota(jnp.int32, (nseg, LANES), 0)
    init = list(zeros)
    for j in range(1, nseg):
        for c in range(hc):
            pr, pi = _cmul(pw[c][0], pw[c][1], init[2 * c], init[2 * c + 1])
            nr = pltpu.roll(pr + ends[2 * c], 1, 0)
            ni = pltpu.roll(pi + ends[2 * c + 1], 1, 0)
            init[2 * c] = jnp.where(row == j, nr, init[2 * c])
            init[2 * c + 1] = jnp.where(row == j, ni, init[2 * c + 1])

    def advance_store(t, hs):
        new = advance(t, hs)
        for c in range(hc):
            buh_ref[c, rows(t), :] = new[2 * c]
            buh_ref[hc + c, rows(t), :] = new[2 * c + 1]
        return new

    last = lax.fori_loop(0, seg, advance_store, tuple(init), unroll=4)
    for c in range(hc):
        hl_ref[0, 0, :, c * LANES:(c + 1) * LANES] = last[2 * c]
        hl_ref[0, 0, :, (hc + c) * LANES:(hc + c + 1) * LANES] = last[2 * c + 1]

    for j in range(nseg):
        y = d_ref[...] * u_ref[j * seg:(j + 1) * seg, :]
        for c in range(nc):
            y = y + _dot(buh_ref[c, j * pitch:j * pitch + seg, :].astype(BF16),
                         cm_ref[0, c * LANES:(c + 1) * LANES, :])
        z_ref[j * seg:(j + 1) * seg, :] = jax.nn.gelu(y)


def _s5_prompt(proj, w_in, w_out, a_re, a_im, d_skip, *, batch, seq, rows_total):
    nv = w_in.shape[0]
    sw = w_in.shape[2]
    seg = seq // SUBLANES
    assert seg * SUBLANES == seq and seg & (seg - 1) == 0
    pitch = seg + SUBLANES
    body = functools.partial(_s5_prompt_body, seg=seg, pitch=pitch)
    return pl.pallas_call(
        body,
        out_shape=(jax.ShapeDtypeStruct((rows_total, nv * LANES), F32),
                   jax.ShapeDtypeStruct((batch, nv, SUBLANES, sw), F32)),
        grid=(batch, nv),
        in_specs=[
            pl.BlockSpec((seq, LANES), lambda b, v: (b, v)),
            pl.BlockSpec((1, LANES, sw), lambda b, v: (v, 0, 0)),
            pl.BlockSpec((1, sw, LANES), lambda b, v: (v, 0, 0)),
            pl.BlockSpec((1, 1, sw // 2), lambda b, v: (v, 0, 0)),
            pl.BlockSpec((1, 1, sw // 2), lambda b, v: (v, 0, 0)),
            pl.BlockSpec((1, LANES), lambda b, v: (0, v)),
        ],
        out_specs=(pl.BlockSpec((seq, LANES), lambda b, v: (b, v)),
                   pl.BlockSpec((1, 1, SUBLANES, sw), lambda b, v: (b, v, 0, 0))),
        scratch_shapes=[pltpu.VMEM((sw // LANES, SUBLANES * pitch, LANES), F32)],
        compiler_params=_cparams("parallel", "parallel"),
        name="s5_prompt",
    )(proj, w_in, w_out, a_re, a_im, d_skip)


def _s5_sample_body(u_ref, hre_ref, him_ref, w_ref, cm_ref, ar_ref, ai_ref, d_ref, zin_ref,
                    z_ref, ore_ref, oim_ref):
    del zin_ref
    half = ar_ref.shape[2]
    u = u_ref[...]
    bu = _dot_hi(u, w_ref[0])
    pr, pi = _cmul(ar_ref[0], ai_ref[0], hre_ref[...], him_ref[...])
    hr = pr + bu[:, :half]
    hi = pi + bu[:, half:]
    ore_ref[...] = hr
    oim_ref[...] = hi
    y = d_ref[...] * u + _dot_hi(hr, cm_ref[0, :half, :]) + _dot_hi(hi, cm_ref[0, half:, :])
    z_ref[...] = jax.nn.gelu(y)


def _s5_sample(proj, h_re, h_im, w_in, w_out, a_re, a_im, d_skip, z_all, *, row0):
    nv = w_in.shape[0]
    sw = w_in.shape[2]
    ns = h_re.shape[0]
    rb = row0 // ns
    assert rb * ns == row0
    st = pl.BlockSpec((ns, sw // 2), lambda v: (0, v))
    return pl.pallas_call(
        _s5_sample_body,
        out_shape=(jax.ShapeDtypeStruct(z_all.shape, F32),
                   jax.ShapeDtypeStruct(h_re.shape, F32),
                   jax.ShapeDtypeStruct(h_im.shape, F32)),
        grid=(nv,),
        in_specs=[
            pl.BlockSpec((ns, LANES), lambda v: (rb, v)),
            st, st,
            pl.BlockSpec((1, LANES, sw), lambda v: (v, 0, 0)),
            pl.BlockSpec((1, sw, LANES), lambda v: (v, 0, 0)),
            pl.BlockSpec((1, 1, sw // 2), lambda v: (v, 0, 0)),
            pl.BlockSpec((1, 1, sw // 2), lambda v: (v, 0, 0)),
            pl.BlockSpec((1, LANES), lambda v: (0, v)),
            pl.BlockSpec(memory_space=pl.ANY),
        ],
        out_specs=(pl.BlockSpec((ns, LANES), lambda v: (rb, v)), st, st),
        input_output_aliases={8: 0},
        compiler_params=_cparams("parallel"),
        name="s5_sample",
    )(proj, h_re, h_im, w_in, w_out, a_re, a_im, d_skip, z_all)


def _hg_span_matrices(ch):
    t = np.arange(ch)[:, None]
    r = np.arange(ch)[None, :]
    mats = [r <= t, r > t]
    for v in range(ch.bit_length() - 1):
        base = (t >> v) << v
        upper = ((t >> v) & 1) == 1
        mats.append(np.where(upper, (r >= base) & (r <= t), (r > t) & (r < base + (1 << v))))
    mm = np.stack(mats).astype(np.float32)
    return jnp.asarray(np.concatenate([mm, mm, mm], axis=-1), BF16)


def _hg_prompt_body(q_ref, f_ref, i_ref, og_ref, lb_ref, gn_ref, mm_ref, o_ref, sfin_ref, st_ref,
                    *, dk):
    c = pl.program_id(1)
    nh = st_ref.shape[0]
    ch, width = q_ref.shape
    nlev = ch.bit_length() - 1

    @pl.when(c == 0)
    def _():
        st_ref[...] = jnp.zeros_like(st_ref)

    lb = lb_ref[...]
    f = lb + (1.0 - lb) * jax.nn.sigmoid(f_ref[...])
    kk = 1.0 - f
    qs = _silu(q_ref[...])
    logf3 = jnp.concatenate(_split3(jnp.log(f)), axis=0)

    def decay(idx):
        return jnp.exp(_dot(mm_ref[idx], logf3))

    eg = decay(0)
    qg = (qs * eg).astype(BF16)
    dec = eg[ch - 1:ch, :]
    kh = (kk * decay(1)).astype(BF16)
    trow = lax.broadcasted_iota(jnp.int32, (ch, width), 0)
    lev = [(decay(2 + v) * jnp.where(((trow >> v) & 1) == 1, qs, kk)).astype(BF16)
           for v in range(nlev)]
    qb, kb, ib = qs.astype(BF16), kk.astype(BF16), i_ref[...].astype(BF16)
    r = lax.broadcasted_iota(jnp.int32, (ch, ch), 0)
    s = lax.broadcasted_iota(jnp.int32, (ch, ch), 1)
    xr = jnp.where(r > s, r ^ s, 0)
    heads = [slice(h * dk, (h + 1) * dk) for h in range(nh)]
    atts = []
    for sl in heads:
        att = jnp.where(r == s, _dot_nt(qb[:, sl], kb[:, sl]), 0.0)
        for v in range(nlev):
            att = jnp.where((xr >> v) == 1, _dot_nt(lev[v][:, sl], lev[v][:, sl]), att)
        atts.append(att.astype(BF16))
    for h, sl in enumerate(heads):
        o = _dot_nt(qg[:, sl], st_ref[h].astype(BF16)) + _dot(atts[h], ib[:, sl])
        og = og_ref[:, sl]
        o_ref[:, sl] = _rms(o, gn_ref[:, sl]) * _silu(og)
    for h, sl in enumerate(heads):
        st_ref[h] = st_ref[h] * dec[:, sl] + _dot_tn(ib[:, sl], kh[:, sl])

    @pl.when(c == pl.num_programs(1) - 1)
    def _():
        sfin_ref[0] = st_ref[...]


def _hg_prompt(proj, lb, gn, *, batch, seq, rows_total, nh, col0):
    width = lb.shape[1]
    dk = width // nh
    ch = HG_CHUNK
    nchunk = seq // ch
    cb = col0 // width
    assert cb * width == col0 and nchunk * ch == seq

    def tok(k):
        return pl.BlockSpec((ch, width), lambda b, c: (b * nchunk + c, cb + k))

    vec = pl.BlockSpec((1, width), lambda b, c: (0, 0))
    mm = _hg_span_matrices(ch)
    return pl.pallas_call(
        functools.partial(_hg_prompt_body, dk=dk),
        out_shape=(jax.ShapeDtypeStruct((rows_total, width), F32),
                   jax.ShapeDtypeStruct((batch, nh, dk, dk), F32)),
        grid=(batch, nchunk),
        in_specs=[tok(0), tok(1), tok(2), tok(3), vec, vec,
                  pl.BlockSpec(mm.shape, lambda b, c: (0, 0, 0))],
        out_specs=(pl.BlockSpec((ch, width), lambda b, c: (b * nchunk + c, 0)),
                   pl.BlockSpec((1, nh, dk, dk), lambda b, c: (b, 0, 0, 0))),
        scratch_shapes=[pltpu.VMEM((nh, dk, dk), F32)],
        compiler_params=_cparams("parallel", "arbitrary"),
        name="hgrn_prompt",
    )(proj, proj, proj, proj, lb, gn, mm)


def _split3(x):
    p1 = x.astype(BF16)
    r1 = x - p1.astype(F32)
    p2 = r1.astype(BF16)
    p3 = (r1 - p2.astype(F32)).astype(BF16)
    return p1, p2, p3


def _hg_sample_body(q_ref, f_ref, i_ref, og_ref, lb_ref, gn_ref, s0_ref, hgin_ref,
                    hg_ref, s_ref, fq_ref, *, dk):
    del hgin_ref
    step = pl.program_id(0)
    nt = q_ref.shape[0]
    tb, nh = s0_ref.shape[0], s0_ref.shape[1]

    @pl.when(step == 0)
    def _():
        lb = lb_ref[...]
        f = lb + (1.0 - lb) * jax.nn.sigmoid(f_ref[...])
        qs = _silu(q_ref[...])
        for src, base in ((f, 0), (qs, nh)):
            for h in range(nh):
                t = src[:, h * dk:(h + 1) * dk].T
                for p, piece in enumerate(_split3(t)):
                    fq_ref[(base + h) * dk:(base + h + 1) * dk, p * nt:(p + 1) * nt] = piece

    tok = lax.broadcasted_iota(jnp.int32, (3 * nt, dk), 0)
    for j in range(tb):
        n = step * tb + j
        onehot = ((tok == n) | (tok == n + nt) | (tok == n + 2 * nt)).astype(BF16)
        fq = _dot(fq_ref[...], onehot)
        irow = i_ref[pl.ds(n, 1), :]
        ogrow = og_ref[pl.ds(n, 1), :]
        for h in range(nh):
            sl = slice(h * dk, (h + 1) * dk)
            fb = fq[h * dk:(h + 1) * dk, :]
            qb = fq[(nh + h) * dk:(nh + h + 1) * dk, :]
            s1 = fb * s0_ref[j, h] + (1.0 - fb) * irow[:, sl]
            s_ref[j, h] = s1
            o = jnp.sum(qb * s1, axis=0, keepdims=True)
            hg_ref[j:j + 1, sl] = _rms(o, gn_ref[:, sl]) * _silu(ogrow[:, sl])


def _hg_sample(proj, lb, gn, s0, hg_all, *, row0, nh, col0):
    width = lb.shape[1]
    dk = width // nh
    ns = s0.shape[0]
    tb = SAMPLE_TOKENS_PER_STEP
    rb, cb = row0 // ns, col0 // width
    assert rb * ns == row0 and cb * width == col0 and ns % tb == 0

    def tok(k):
        return pl.BlockSpec((ns, width), lambda t: (rb, cb + k))

    vec = pl.BlockSpec((1, width), lambda t: (0, 0))
    sspec = pl.BlockSpec((tb, nh, dk, dk), lambda t: (t, 0, 0, 0))
    return pl.pallas_call(
        functools.partial(_hg_sample_body, dk=dk),
        out_shape=(jax.ShapeDtypeStruct(hg_all.shape, F32),
                   jax.ShapeDtypeStruct(s0.shape, F32)),
        grid=(ns // tb,),
        in_specs=[tok(0), tok(1), tok(2), tok(3), vec, vec, sspec,
                  pl.BlockSpec(memory_space=pl.ANY)],
        out_specs=(pl.BlockSpec((tb, width), lambda t: (row0 // tb + t, 0)), sspec),
        scratch_shapes=[pltpu.VMEM((2 * nh * dk, 3 * ns), BF16)],
        input_output_aliases={7: 0},
        compiler_params=_cparams("arbitrary"),
        name="hgrn_sample",
    )(proj, proj, proj, proj, lb, gn, s0, hg_all)


def _merge_body(z_ref, hg_ref, gs0_ref, gs1_ref, gh0_ref, gh1_ref, x_ref,
                wglu_ref, bglu_ref, wbs_ref, wbh_ref, wout_ref, post_ref, o_ref):
    z = z_ref[...]
    s5o = z * jax.nn.sigmoid(_dot(z.astype(BF16), wglu_ref[...]) + bglu_ref[...])
    a = _dot(s5o.astype(BF16), wbs_ref[...])
    b = _dot(hg_ref[...].astype(BF16), wbh_ref[...])
    half = gs0_ref.shape[1]
    m0 = jax.nn.sigmoid(gs0_ref[...]) * a[:, :half] + jax.nn.sigmoid(gh0_ref[...]) * b[:, :half]
    m1 = jax.nn.sigmoid(gs1_ref[...]) * a[:, half:] + jax.nn.sigmoid(gh1_ref[...]) * b[:, half:]
    mix = (_dot(m0.astype(BF16), wout_ref[:half, :]) + _dot(m1.astype(BF16), wout_ref[half:, :]))
    o_ref[...] = x_ref[...] + _rms(mix, post_ref[...])


def _merge(z, hg, proj, x, wglu, bglu, wbs, wbh, wout, post, *, col0, tm):
    m, d = x.shape
    w = z.shape[1]
    cb = col0 // w
    assert cb * w == col0 and d == 2 * w

    def gate(k):
        return pl.BlockSpec((tm, w), lambda i: (i, cb + k))

    def const(shape):
        return pl.BlockSpec(shape, lambda i: (0, 0), pipeline_mode=pl.Buffered(1))

    return pl.pallas_call(
        _merge_body,
        out_shape=jax.ShapeDtypeStruct((m, d), F32),
        grid=(m // tm,),
        in_specs=[
            pl.BlockSpec((tm, w), lambda i: (i, 0)),
            pl.BlockSpec((tm, w), lambda i: (i, 0)),
            gate(0), gate(1), gate(2), gate(3),
            pl.BlockSpec((tm, d), lambda i: (i, 0)),
            const((w, w)), const((1, w)), const((w, d)), const((w, d)), const((d, d)), const((1, d)),
        ],
        out_specs=pl.BlockSpec((tm, d), lambda i: (i, 0)),
        compiler_params=_cparams("parallel"),
        name="merge",
    )(z, hg, proj, proj, proj, proj, x, wglu, bglu, wbs, wbh, wout, post)


MERGE_ROW_TILE = 320


def kernel(x_prompt, x_sample, state_s5_re, state_s5_im, state_hgrn, ffn1_pre_norm, ffn1_w_gate, ffn1_w_up, ffn1_w_down, ffn1_post_norm, mix_pre_norm, w_in, s5_lambda_re, s5_lambda_im, s5_log_dt, s5_b_re, s5_b_im, s5_c_re, s5_c_im, s5_d, s5_w_glu, s5_b_glu, hgrn_lb_logits, hgrn_out_norm, w_branch_s5, w_branch_hgrn, w_out, mix_post_norm, ffn2_pre_norm, ffn2_w_gate, ffn2_w_up, ffn2_w_down, ffn2_post_norm):
    depth = ffn1_w_gate.shape[0]
    assert depth == 1
    batch, seq, d = x_prompt.shape
    ns = x_sample.shape[0]
    assert x_sample.shape[1] == 1
    g, p = s5_lambda_re.shape[1:]
    nh, dk = state_hgrn.shape[2], state_hgrn.shape[3]
    s5w = s5_d.shape[1]
    hgw = nh * dk
    mp = batch * seq
    m = mp + ns

    bf = lambda a: a[0].astype(BF16)
    row = lambda a: a[0].reshape(1, -1).astype(F32)

    lb_all = jnp.cumsum(jax.nn.softmax(hgrn_lb_logits.astype(F32), axis=0), axis=0)
    lb = lb_all[0].reshape(1, hgw)

    x = _ffn(x_prompt.reshape(mp, d), x_sample.reshape(ns, d), row(ffn1_pre_norm),
             bf(ffn1_w_gate), bf(ffn1_w_up), bf(ffn1_w_down), row(ffn1_post_norm),
             ns=ns, split_out=False)
    proj = _inproj(x, row(mix_pre_norm), bf(w_in))

    tw_in, tw_out, a_re, a_im = _s5_tables(
        s5_lambda_re[0], s5_lambda_im[0], s5_log_dt[0], s5_b_re[0], s5_b_im[0],
        s5_c_re[0], s5_c_im[0])
    d_skip = row(s5_d)
    z, hlast = _s5_prompt(proj, tw_in.astype(BF16), tw_out.astype(BF16), a_re, a_im, d_skip,
                          batch=batch, seq=seq, rows_total=m)
    z, s_re, s_im = _s5_sample(proj, state_s5_re[0].reshape(ns, g * p),
                               state_s5_im[0].reshape(ns, g * p),
                               tw_in, tw_out, a_re, a_im, d_skip, z, row0=mp)
    half = hlast.shape[-1] // 2
    p_re = hlast[:, :, SUBLANES - 1, :half].reshape(1, batch, g, p)
    p_im = hlast[:, :, SUBLANES - 1, half:].reshape(1, batch, g, p)

    gn = row(hgrn_out_norm)
    hg, st_p = _hg_prompt(proj, lb, gn, batch=batch, seq=seq, rows_total=m, nh=nh, col0=s5w)
    hg, st_s = _hg_sample(proj, lb, gn, state_hgrn[0], hg, row0=mp, nh=nh, col0=s5w)

    x = _merge(z, hg, proj, x, bf(s5_w_glu), row(s5_b_glu), bf(w_branch_s5), bf(w_branch_hgrn),
               bf(w_out), row(mix_post_norm), col0=s5w + 4 * hgw, tm=MERGE_ROW_TILE)
    yp, ys = _ffn(x, None, row(ffn2_pre_norm), bf(ffn2_w_gate), bf(ffn2_w_up), bf(ffn2_w_down),
                  row(ffn2_post_norm), ns=ns, split_out=True)

    return (yp.reshape(batch, seq, d), ys.reshape(ns, 1, d),
            p_re, p_im, jnp.swapaxes(st_p, -1, -2)[None],
            s_re.reshape(1, ns, g, p), s_im.reshape(1, ns, g, p), st_s[None])
```

```python
import functools

import jax
import jax.numpy as jnp
import numpy as np
from jax import lax
from jax.experimental import pallas as pl
from jax.experimental.pallas import tpu as pltpu

F32 = jnp.float32
BF16 = jnp.bfloat16
EPS = 1e-6
HIGHEST = lax.Precision.HIGHEST

LANES = 128
SUBLANES = 8
VMEM_LIMIT = 56 * 1024 * 1024

ROW_TILE = 640
FF_TILE = 512
IN_TILE = 1024
IN_ROW_TILE = 1040
HG_CHUNK = 64
SAMPLE_TOKENS_PER_STEP = 8


def _cparams(*sem):
    return pltpu.CompilerParams(dimension_semantics=sem, vmem_limit_bytes=VMEM_LIMIT)


def _rms(x, g):
    return x * lax.rsqrt(jnp.mean(x * x, axis=-1, keepdims=True) + EPS) * g


def _silu(x):
    return x * jax.nn.sigmoid(x)


def _dot(a, b):
    return jnp.dot(a, b, preferred_element_type=F32)


def _dot_hi(a, b):
    return jnp.dot(a, b, preferred_element_type=F32, precision=HIGHEST)


def _dot_nt(a, b):
    return lax.dot_general(a, b, (((1,), (1,)), ((), ())), preferred_element_type=F32)


def _dot_tn(a, b):
    return lax.dot_general(a, b, (((0,), (0,)), ((), ())), preferred_element_type=F32)


BF16_ROWS = 16


def _side_specs(arrays, nsteps, step_of):
    in_specs, out_specs, out_shapes, plan = [], [], [], []
    for a in arrays:
        r, c = a.shape
        rows = next(t for t in range(BF16_ROWS, r + 1, BF16_ROWS) if r % t == 0 and r // t <= nsteps)
        nblk = r // rows
        every = nsteps // nblk

        def imap(*g, nblk=nblk, every=every):
            return (jnp.minimum(step_of(*g) // every, nblk - 1), 0)

        in_specs.append(pl.BlockSpec((rows, c), imap))
        out_specs.append(pl.BlockSpec((rows, c), imap))
        out_shapes.append(jax.ShapeDtypeStruct((r, c), BF16))
        plan.append((nblk, every))
    return in_specs, out_specs, out_shapes, tuple(plan)


def _side_cast(step, src_refs, dst_refs, plan):
    for src, dst, (nblk, every) in zip(src_refs, dst_refs, plan):
        @pl.when((step % every == 0) & (step // every < nblk))
        def _(src=src, dst=dst):
            dst[...] = src[...].astype(BF16)


def _ffn_body(*refs, ns, split_in, split_out, side_plan):
    refs = list(refs)
    nside = len(side_plan)
    x_ref = refs.pop(0)
    xs_ref = refs.pop(0) if split_in else None
    pre_ref, wg_ref, wu_ref, wd_ref, post_ref = refs[:5]
    side_src = refs[5:5 + nside]
    o_ref = refs[5 + nside]
    os_ref = refs[6 + nside] if split_out else None
    side_dst = refs[-2 - nside:-2]
    h_ref, acc_ref = refs[-2:]
    i, k = pl.program_id(0), pl.program_id(1)
    last_i = pl.num_programs(0) - 1
    last_k = pl.num_programs(1) - 1
    cut = h_ref.shape[0] - ns
    _side_cast(i * pl.num_programs(1) + k, side_src, side_dst, side_plan)

    def swiglu_down(h):
        g = _dot(h, wg_ref[...])
        u = _dot(h, wu_ref[...])
        return _dot((_silu(g) * u).astype(BF16), wd_ref[...])

    @pl.when((i < last_i) & (k == 0))
    def _():
        h = _rms(x_ref[...], pre_ref[...]).astype(BF16)
        h_ref[...] = h
        acc_ref[...] = swiglu_down(h)

    @pl.when((i < last_i) & (k == last_k))
    def _():
        acc = acc_ref[...] + swiglu_down(h_ref[...])
        o_ref[...] = x_ref[...] + 0.5 * _rms(acc, post_ref[...])

    @pl.when((i == last_i) & (k == 0))
    def _():
        h_ref[:cut, :] = _rms(x_ref[:cut, :], pre_ref[...]).astype(BF16)
        xt = xs_ref[...] if split_in else x_ref[cut:, :]
        h_ref[cut:, :] = _rms(xt, pre_ref[...]).astype(BF16)
        acc_ref[...] = jnp.zeros_like(acc_ref)

    @pl.when(((k > 0) & (k < last_k)) | (i == last_i))
    def _():
        acc_ref[...] += swiglu_down(h_ref[...])

    @pl.when((i == last_i) & (k == last_k))
    def _():
        o_ref[:cut, :] = x_ref[:cut, :] + 0.5 * _rms(acc_ref[:cut, :], post_ref[...])
        xt = xs_ref[...] if split_in else x_ref[cut:, :]
        tail = xt + 0.5 * _rms(acc_ref[cut:, :], post_ref[...])
        if split_out:
            os_ref[...] = tail
        else:
            o_ref[cut:, :] = tail


def _ffn(x, xs, pre, wg, wu, wd, post, *, ns, split_out, side=()):
    split_in = xs is not None
    d = x.shape[1]
    m = x.shape[0] + (ns if split_in else 0)
    dff = wg.shape[1]
    tm, tf = ROW_TILE, FF_TILE
    assert m % tm == 0 and 0 < ns < tm and ns % SUBLANES == 0
    nk = dff // tf
    side_in, side_out, side_shapes, side_plan = _side_specs(
        side, (m // tm) * nk, lambda i, k: i * nk + k)
    tok = pl.BlockSpec((tm, d), lambda i, k: (i, 0))
    smp = pl.BlockSpec((ns, d), lambda i, k: (0, 0))
    vec = pl.BlockSpec((1, d), lambda i, k: (0, 0))
    in_specs = [tok] + ([smp] if split_in else []) + [
        vec,
        pl.BlockSpec((d, tf), lambda i, k: (0, k)),
        pl.BlockSpec((d, tf), lambda i, k: (0, k)),
        pl.BlockSpec((tf, d), lambda i, k: (k, 0)),
        vec,
    ] + side_in
    if split_out:
        out_shape = [jax.ShapeDtypeStruct((m - ns, d), F32), jax.ShapeDtypeStruct((ns, d), F32)]
        out_specs = [tok, smp]
    else:
        out_shape = [jax.ShapeDtypeStruct((m, d), F32)]
        out_specs = [tok]
    args = [x] + ([xs] if split_in else []) + [pre, wg, wu, wd, post] + list(side)
    sequential_rows = split_out or bool(side)
    return pl.pallas_call(
        functools.partial(_ffn_body, ns=ns, split_in=split_in, split_out=split_out,
                          side_plan=side_plan),
        out_shape=out_shape + side_shapes,
        grid=(m // tm, nk),
        in_specs=in_specs,
        out_specs=out_specs + side_out,
        scratch_shapes=[pltpu.VMEM((tm, d), BF16), pltpu.VMEM((tm, d), F32)],
        compiler_params=_cparams("arbitrary" if sequential_rows else "parallel", "arbitrary"),
        name="ffn",
    )(*args)


def _inproj_body(x_ref, g_ref, w_ref, o_ref, h_ref):
    j = pl.program_id(1)

    @pl.when(j == 0)
    def _():
        h = _rms(x_ref[...], g_ref[...]).astype(BF16)
        h_ref[...] = h
        o_ref[...] = _dot(h, w_ref[...])

    @pl.when(j > 0)
    def _():
        o_ref[...] = _dot(h_ref[...], w_ref[...])


def _inproj(x, g, w):
    m, d = x.shape
    n = w.shape[1]
    tm, tn = IN_ROW_TILE, IN_TILE
    assert m % tm == 0
    return pl.pallas_call(
        _inproj_body,
        out_shape=jax.ShapeDtypeStruct((m, n), F32),
        grid=(m // tm, n // tn),
        in_specs=[
            pl.BlockSpec((tm, d), lambda i, j: (i, 0)),
            pl.BlockSpec((1, d), lambda i, j: (0, 0)),
            pl.BlockSpec((d, tn), lambda i, j: (0, j)),
        ],
        out_specs=pl.BlockSpec((tm, tn), lambda i, j: (i, j)),
        scratch_shapes=[pltpu.VMEM((tm, d), BF16)],
        compiler_params=_cparams("parallel", "arbitrary"),
        name="inproj",
    )(x, g, w)


def _s5_tables(lam_re, lam_im, log_dt, b_re, b_im, c_re, c_im):
    g, p = lam_re.shape
    n = b_re.shape[-1]
    gpt = LANES // n
    nv = g // gpt
    dt = jnp.exp(log_dt)[:, None]
    er = jnp.exp(lam_re * dt)
    th = lam_im * dt
    a_re, a_im = er * jnp.cos(th), er * jnp.sin(th)
    den = lam_re * lam_re + lam_im * lam_im
    k_re = ((a_re - 1.0) * lam_re + a_im * lam_im) / den
    k_im = (a_im * lam_re - (a_re - 1.0) * lam_im) / den
    bb_re = k_re[..., None] * b_re - k_im[..., None] * b_im
    bb_im = k_re[..., None] * b_im + k_im[..., None] * b_re
    eye = jnp.eye(gpt, dtype=F32)

    def in_blk(bb):
        t = bb.reshape(nv, gpt, p, n)
        return jnp.einsum("ab,vapn->vanbp", eye, t).reshape(nv, gpt * n, gpt * p)

    def out_blk(cc):
        t = cc.reshape(nv, gpt, n, p)
        return jnp.einsum("ab,vanp->vapbn", eye, t).reshape(nv, gpt * p, gpt * n)

    w_in = jnp.concatenate([in_blk(bb_re), in_blk(bb_im)], axis=-1)
    w_out = jnp.concatenate([out_blk(c_re), -out_blk(c_im)], axis=1)
    return w_in, w_out, a_re.reshape(nv, 1, gpt * p), a_im.reshape(nv, 1, gpt * p)


def _cmul(ar, ai, br, bi):
    return ar * br - ai * bi, ar * bi + ai * br


def _s5_prompt_body(*refs, seg, pitch, side_plan):
    nside = len(side_plan)
    u_ref, w_ref, cm_ref, ar_ref, ai_ref, d_ref = refs[:6]
    z_ref, hl_ref = refs[6 + nside:8 + nside]
    buh_ref = refs[-1]
    _side_cast(pl.program_id(0) * pl.num_programs(1) + pl.program_id(1),
               refs[6:6 + nside], refs[8 + nside:-1], side_plan)
    nseg = SUBLANES
    nc = w_ref.shape[2] // LANES
    hc = nc // 2

    for j in range(nseg):
        bu = _dot(u_ref[j * seg:(j + 1) * seg, :].astype(BF16), w_ref[0])
        for c in range(nc):
            buh_ref[c, j * pitch:j * pitch + seg, :] = bu[:, c * LANES:(c + 1) * LANES]

    ar = [jnp.broadcast_to(ar_ref[0, :, c * LANES:(c + 1) * LANES], (nseg, LANES)) for c in range(hc)]
    ai = [jnp.broadcast_to(ai_ref[0, :, c * LANES:(c + 1) * LANES], (nseg, LANES)) for c in range(hc)]

    def rows(t):
        return pl.ds(t, nseg, stride=pitch)

    def advance(t, hs):
        out = []
        for c in range(hc):
            pr, pi = _cmul(ar[c], ai[c], hs[2 * c], hs[2 * c + 1])
            out += [pr + buh_ref[c, rows(t), :], pi + buh_ref[hc + c, rows(t), :]]
        return tuple(out)

    zeros = tuple(jnp.zeros((nseg, LANES), F32) for _ in range(2 * hc))
    ends = lax.fori_loop(0, seg, advance, zeros, unroll=4)

    pw = [(ar[c], ai[c]) for c in range(hc)]
    for _ in range(seg.bit_length() - 1):
        pw = [_cmul(r, i, r, i) for r, i in pw]

    row = lax.broadcasted_iota(jnp.int32, (nseg, LANES), 0)
    init = list(zeros)
    for j in range(1, nseg):
        for c in range(hc):
            pr, pi = _cmul(pw[c][0], pw[c][1], init[2 * c], init[2 * c + 1])
            nr = pltpu.roll(pr + ends[2 * c], 1, 0)
            ni = pltpu.roll(pi + ends[2 * c + 1], 1, 0)
            init[2 * c] = jnp.where(row == j, nr, init[2 * c])
            init[2 * c + 1] = jnp.where(row == j, ni, init[2 * c + 1])

    def advance_store(t, hs):
        new = advance(t, hs)
        for c in range(hc):
            buh_ref[c, rows(t), :] = new[2 * c]
            buh_ref[hc + c, rows(t), :] = new[2 * c + 1]
        return new

    last = lax.fori_loop(0, seg, advance_store, tuple(init), unroll=4)
    for c in range(hc):
        hl_ref[0, 0, :, c * LANES:(c + 1) * LANES] = last[2 * c]
        hl_ref[0, 0, :, (hc + c) * LANES:(hc + c + 1) * LANES] = last[2 * c + 1]

    for j in range(nseg):
        y = d_ref[...] * u_ref[j * seg:(j + 1) * seg, :]
        for c in range(nc):
            y = y + _dot(buh_ref[c, j * pitch:j * pitch + seg, :].astype(BF16),
                         cm_ref[0, c * LANES:(c + 1) * LANES, :])
        z_ref[j * seg:(j + 1) * seg, :] = jax.nn.gelu(y)


def _s5_prompt(proj, w_in, w_out, a_re, a_im, d_skip, *, batch, seq, rows_total, side=()):
    nv = w_in.shape[0]
    sw = w_in.shape[2]
    seg = seq // SUBLANES
    assert seg * SUBLANES == seq and seg & (seg - 1) == 0
    pitch = seg + SUBLANES
    side_in, side_out, side_shapes, side_plan = _side_specs(
        side, batch * nv, lambda b, v: b * nv + v)
    body = functools.partial(_s5_prompt_body, seg=seg, pitch=pitch, side_plan=side_plan)
    return pl.pallas_call(
        body,
        out_shape=[jax.ShapeDtypeStruct((rows_total, nv * LANES), F32),
                   jax.ShapeDtypeStruct((batch, nv, SUBLANES, sw), F32)] + side_shapes,
        grid=(batch, nv),
        in_specs=[
            pl.BlockSpec((seq, LANES), lambda b, v: (b, v)),
            pl.BlockSpec((1, LANES, sw), lambda b, v: (v, 0, 0)),
            pl.BlockSpec((1, sw, LANES), lambda b, v: (v, 0, 0)),
            pl.BlockSpec((1, 1, sw // 2), lambda b, v: (v, 0, 0)),
            pl.BlockSpec((1, 1, sw // 2), lambda b, v: (v, 0, 0)),
            pl.BlockSpec((1, LANES), lambda b, v: (0, v)),
        ] + side_in,
        out_specs=[pl.BlockSpec((seq, LANES), lambda b, v: (b, v)),
                   pl.BlockSpec((1, 1, SUBLANES, sw), lambda b, v: (b, v, 0, 0))] + side_out,
        scratch_shapes=[pltpu.VMEM((sw // LANES, SUBLANES * pitch, LANES), F32)],
        compiler_params=_cparams(*(("arbitrary",) * 2 if side else ("parallel",) * 2)),
        name="s5_prompt",
    )(proj, w_in, w_out, a_re, a_im, d_skip, *side)


def _s5_sample_body(u_ref, hre_ref, him_ref, w_ref, cm_ref, ar_ref, ai_ref, d_ref,
                    z_ref, ore_ref, oim_ref):
    half = ar_ref.shape[2]
    u = u_ref[...]
    bu = _dot_hi(u, w_ref[0])
    pr, pi = _cmul(ar_ref[0], ai_ref[0], hre_ref[...], him_ref[...])
    hr = pr + bu[:, :half]
    hi = pi + bu[:, half:]
    ore_ref[...] = hr
    oim_ref[...] = hi
    y = d_ref[...] * u + _dot_hi(hr, cm_ref[0, :half, :]) + _dot_hi(hi, cm_ref[0, half:, :])
    z_ref[...] = jax.nn.gelu(y)


def _s5_sample(proj, h_re, h_im, w_in, w_out, a_re, a_im, d_skip, *, row0):
    nv = w_in.shape[0]
    sw = w_in.shape[2]
    ns = h_re.shape[0]
    rb = row0 // ns
    assert rb * ns == row0
    st = pl.BlockSpec((ns, sw // 2), lambda v: (0, v))
    return pl.pallas_call(
        _s5_sample_body,
        out_shape=(jax.ShapeDtypeStruct((ns, nv * LANES), F32),
                   jax.ShapeDtypeStruct(h_re.shape, F32),
                   jax.ShapeDtypeStruct(h_im.shape, F32)),
        grid=(nv,),
        in_specs=[
            pl.BlockSpec((ns, LANES), lambda v: (rb, v)),
            st, st,
            pl.BlockSpec((1, LANES, sw), lambda v: (v, 0, 0)),
            pl.BlockSpec((1, sw, LANES), lambda v: (v, 0, 0)),
            pl.BlockSpec((1, 1, sw // 2), lambda v: (v, 0, 0)),
            pl.BlockSpec((1, 1, sw // 2), lambda v: (v, 0, 0)),
            pl.BlockSpec((1, LANES), lambda v: (0, v)),
        ],
        out_specs=(pl.BlockSpec((ns, LANES), lambda v: (0, v)), st, st),
        compiler_params=_cparams("parallel"),
        name="s5_sample",
    )(proj, h_re, h_im, w_in, w_out, a_re, a_im, d_skip)


def _hg_span_matrices(ch):
    t = np.arange(ch)[:, None]
    r = np.arange(ch)[None, :]
    mats = [r <= t, r > t]
    for v in range(ch.bit_length() - 1):
        base = (t >> v) << v
        upper = ((t >> v) & 1) == 1
        mats.append(np.where(upper, (r >= base) & (r <= t), (r > t) & (r < base + (1 << v))))
    mm = np.stack(mats).astype(np.float32)
    return jnp.asarray(np.concatenate([mm, mm, mm], axis=-1), BF16)


def _hg_prompt_body(*refs, dk, side_plan):
    nside = len(side_plan)
    q_ref, f_ref, i_ref, og_ref, lb_ref, gn_ref, mm_ref = refs[:7]
    o_ref, sfin_ref = refs[7 + nside:9 + nside]
    st_ref = refs[-1]
    _side_cast(pl.program_id(0) * pl.num_programs(1) + pl.program_id(1),
               refs[7:7 + nside], refs[9 + nside:-1], side_plan)
    c = pl.program_id(1)
    nh = st_ref.shape[0]
    ch, width = q_ref.shape
    nlev = ch.bit_length() - 1

    @pl.when(c == 0)
    def _():
        st_ref[...] = jnp.zeros_like(st_ref)

    lb = lb_ref[...]
    f = lb + (1.0 - lb) * jax.nn.sigmoid(f_ref[...])
    kk = 1.0 - f
    qs = _silu(q_ref[...])
    logf3 = jnp.concatenate(_split3(jnp.log(f)), axis=0)

    def decay(idx):
        return jnp.exp(_dot(mm_ref[idx], logf3))

    eg = decay(0)
    qg = (qs * eg).astype(BF16)
    dec = eg[ch - 1:ch, :]
    kh = (kk * decay(1)).astype(BF16)
    trow = lax.broadcasted_iota(jnp.int32, (ch, width), 0)
    lev = [(decay(2 + v) * jnp.where(((trow >> v) & 1) == 1, qs, kk)).astype(BF16)
           for v in range(nlev)]
    qb, kb, ib = qs.astype(BF16), kk.astype(BF16), i_ref[...].astype(BF16)
    r = lax.broadcasted_iota(jnp.int32, (ch, ch), 0)
    s = lax.broadcasted_iota(jnp.int32, (ch, ch), 1)
    xr = jnp.where(r > s, r ^ s, 0)
    heads = [slice(h * dk, (h + 1) * dk) for h in range(nh)]
    atts = []
    for sl in heads:
        att = jnp.where(r == s, _dot_nt(qb[:, sl], kb[:, sl]), 0.0)
        for v in range(nlev):
            att = jnp.where((xr >> v) == 1, _dot_nt(lev[v][:, sl], lev[v][:, sl]), att)
        atts.append(att.astype(BF16))
    for h, sl in enumerate(heads):
        o = _dot_nt(qg[:, sl], st_ref[h].astype(BF16)) + _dot(atts[h], ib[:, sl])
        og = og_ref[:, sl]
        o_ref[:, sl] = _rms(o, gn_ref[:, sl]) * _silu(og)
    for h, sl in enumerate(heads):
        st_ref[h] = st_ref[h] * dec[:, sl] + _dot_tn(ib[:, sl], kh[:, sl])

    @pl.when(c == pl.num_programs(1) - 1)
    def _():
        sfin_ref[0] = st_ref[...]


def _hg_prompt(proj, lb, gn, *, batch, seq, rows_total, nh, col0, side=()):
    width = lb.shape[1]
    dk = width // nh
    ch = HG_CHUNK
    nchunk = seq // ch
    cb = col0 // width
    assert cb * width == col0 and nchunk * ch == seq

    def tok(k):
        return pl.BlockSpec((ch, width), lambda b, c: (b * nchunk + c, cb + k))

    vec = pl.BlockSpec((1, width), lambda b, c: (0, 0))
    mm = _hg_span_matrices(ch)
    side_in, side_out, side_shapes, side_plan = _side_specs(
        side, batch * nchunk, lambda b, c: b * nchunk + c)
    return pl.pallas_call(
        functools.partial(_hg_prompt_body, dk=dk, side_plan=side_plan),
        out_shape=[jax.ShapeDtypeStruct((rows_total, width), F32),
                   jax.ShapeDtypeStruct((batch, nh, dk, dk), F32)] + side_shapes,
        grid=(batch, nchunk),
        in_specs=[tok(0), tok(1), tok(2), tok(3), vec, vec,
                  pl.BlockSpec(mm.shape, lambda b, c: (0, 0, 0))] + side_in,
        out_specs=[pl.BlockSpec((ch, width), lambda b, c: (b * nchunk + c, 0)),
                   pl.BlockSpec((1, nh, dk, dk), lambda b, c: (b, 0, 0, 0))] + side_out,
        scratch_shapes=[pltpu.VMEM((nh, dk, dk), F32)],
        compiler_params=_cparams("arbitrary" if side else "parallel", "arbitrary"),
        name="hgrn_prompt",
    )(proj, proj, proj, proj, lb, gn, mm, *side)


def _split3(x):
    p1 = x.astype(BF16)
    r1 = x - p1.astype(F32)
    p2 = r1.astype(BF16)
    p3 = (r1 - p2.astype(F32)).astype(BF16)
    return p1, p2, p3


def _hg_sample_body(q_ref, f_ref, i_ref, og_ref, lb_ref, gn_ref, s0_ref,
                    hg_ref, s_ref, fq_ref, *, dk):
    step = pl.program_id(0)
    nt = q_ref.shape[0]
    tb, nh = s0_ref.shape[0], s0_ref.shape[1]

    @pl.when(step == 0)
    def _():
        lb = lb_ref[...]
        f = lb + (1.0 - lb) * jax.nn.sigmoid(f_ref[...])
        qs = _silu(q_ref[...])
        for src, base in ((f, 0), (qs, nh)):
            for h in range(nh):
                t = src[:, h * dk:(h + 1) * dk].T
                for p, piece in enumerate(_split3(t)):
                    fq_ref[(base + h) * dk:(base + h + 1) * dk, p * nt:(p + 1) * nt] = piece

    tok = lax.broadcasted_iota(jnp.int32, (3 * nt, dk), 0)
    for j in range(tb):
        n = step * tb + j
        onehot = ((tok == n) | (tok == n + nt) | (tok == n + 2 * nt)).astype(BF16)
        fq = _dot(fq_ref[...], onehot)
        irow = i_ref[pl.ds(n, 1), :]
        ogrow = og_ref[pl.ds(n, 1), :]
        for h in range(nh):
            sl = slice(h * dk, (h + 1) * dk)
            fb = fq[h * dk:(h + 1) * dk, :]
            qb = fq[(nh + h) * dk:(nh + h + 1) * dk, :]
            s1 = fb * s0_ref[j, h] + (1.0 - fb) * irow[:, sl]
            s_ref[j, h] = s1
            o = jnp.sum(qb * s1, axis=0, keepdims=True)
            hg_ref[j:j + 1, sl] = _rms(o, gn_ref[:, sl]) * _silu(ogrow[:, sl])


def _hg_sample(proj, lb, gn, s0, *, row0, nh, col0):
    width = lb.shape[1]
    dk = width // nh
    ns = s0.shape[0]
    tb = SAMPLE_TOKENS_PER_STEP
    rb, cb = row0 // ns, col0 // width
    assert rb * ns == row0 and cb * width == col0 and ns % tb == 0

    def tok(k):
        return pl.BlockSpec((ns, width), lambda t: (rb, cb + k))

    vec = pl.BlockSpec((1, width), lambda t: (0, 0))
    sspec = pl.BlockSpec((tb, nh, dk, dk), lambda t: (t, 0, 0, 0))
    return pl.pallas_call(
        functools.partial(_hg_sample_body, dk=dk),
        out_shape=(jax.ShapeDtypeStruct((ns, width), F32),
                   jax.ShapeDtypeStruct(s0.shape, F32)),
        grid=(ns // tb,),
        in_specs=[tok(0), tok(1), tok(2), tok(3), vec, vec, sspec],
        out_specs=(pl.BlockSpec((tb, width), lambda t: (t, 0)), sspec),
        scratch_shapes=[pltpu.VMEM((2 * nh * dk, 3 * ns), BF16)],
        compiler_params=_cparams("arbitrary"),
        name="hgrn_sample",
    )(proj, proj, proj, proj, lb, gn, s0)


def _merge_body(z_ref, zs_ref, hg_ref, hgs_ref, gs0_ref, gs1_ref, gh0_ref, gh1_ref, x_ref,
                wglu_ref, bglu_ref, wbs_ref, wbh_ref, wout_ref, post_ref, o_ref, *, ns):
    i = pl.program_id(0)
    last_i = pl.num_programs(0) - 1
    cut = x_ref.shape[0] - ns
    half = gs0_ref.shape[1]

    def rows(z, hg, sl):
        s5o = z * jax.nn.sigmoid(_dot(z.astype(BF16), wglu_ref[...]) + bglu_ref[...])
        a = _dot(s5o.astype(BF16), wbs_ref[...])
        b = _dot(hg.astype(BF16), wbh_ref[...])
        m0 = (jax.nn.sigmoid(gs0_ref[sl, :]) * a[:, :half]
              + jax.nn.sigmoid(gh0_ref[sl, :]) * b[:, :half])
        m1 = (jax.nn.sigmoid(gs1_ref[sl, :]) * a[:, half:]
              + jax.nn.sigmoid(gh1_ref[sl, :]) * b[:, half:])
        mix = _dot(m0.astype(BF16), wout_ref[:half, :]) + _dot(m1.astype(BF16), wout_ref[half:, :])
        o_ref[sl, :] = x_ref[sl, :] + _rms(mix, post_ref[...])

    @pl.when(i < last_i)
    def _():
        rows(z_ref[...], hg_ref[...], slice(None))

    @pl.when(i == last_i)
    def _():
        rows(z_ref[:cut, :], hg_ref[:cut, :], slice(0, cut))
        rows(zs_ref[...], hgs_ref[...], slice(cut, None))


def _merge(z, zs, hg, hgs, proj, x, wglu, bglu, wbs, wbh, wout, post, *, col0, tm):
    m, d = x.shape
    w = z.shape[1]
    ns = zs.shape[0]
    cb = col0 // w
    assert cb * w == col0 and d == 2 * w and m % tm == 0 and 0 < ns < tm

    def gate(k):
        return pl.BlockSpec((tm, w), lambda i: (i, cb + k))

    def const(shape):
        return pl.BlockSpec(shape, lambda i: (0, 0), pipeline_mode=pl.Buffered(1))

    tok = pl.BlockSpec((tm, w), lambda i: (i, 0))
    return pl.pallas_call(
        functools.partial(_merge_body, ns=ns),
        out_shape=jax.ShapeDtypeStruct((m, d), F32),
        grid=(m // tm,),
        in_specs=[
            tok, const((ns, w)), tok, const((ns, w)),
            gate(0), gate(1), gate(2), gate(3),
            pl.BlockSpec((tm, d), lambda i: (i, 0)),
            const((w, w)), const((1, w)), const((w, d)), const((w, d)), const((d, d)), const((1, d)),
        ],
        out_specs=pl.BlockSpec((tm, d), lambda i: (i, 0)),
        compiler_params=_cparams("parallel"),
        name="merge",
    )(z, zs, hg, hgs, proj, proj, proj, proj, x, wglu, bglu, wbs, wbh, wout, post)


MERGE_ROW_TILE = 320


def kernel(x_prompt, x_sample, state_s5_re, state_s5_im, state_hgrn, ffn1_pre_norm, ffn1_w_gate, ffn1_w_up, ffn1_w_down, ffn1_post_norm, mix_pre_norm, w_in, s5_lambda_re, s5_lambda_im, s5_log_dt, s5_b_re, s5_b_im, s5_c_re, s5_c_im, s5_d, s5_w_glu, s5_b_glu, hgrn_lb_logits, hgrn_out_norm, w_branch_s5, w_branch_hgrn, w_out, mix_post_norm, ffn2_pre_norm, ffn2_w_gate, ffn2_w_up, ffn2_w_down, ffn2_post_norm):
    depth = ffn1_w_gate.shape[0]
    assert depth == 1
    batch, seq, d = x_prompt.shape
    ns = x_sample.shape[0]
    assert x_sample.shape[1] == 1
    g, p = s5_lambda_re.shape[1:]
    nh, dk = state_hgrn.shape[2], state_hgrn.shape[3]
    s5w = s5_d.shape[1]
    hgw = nh * dk
    mp = batch * seq
    m = mp + ns

    bf = lambda a: a[0].astype(BF16)
    row = lambda a: a[0].reshape(1, -1).astype(F32)

    lb_all = jnp.cumsum(jax.nn.softmax(hgrn_lb_logits.astype(F32), axis=0), axis=0)
    lb = lb_all[0].reshape(1, hgw)

    x, w_in_b, wglu_b, wbs_b, wbh_b, wout_b = _ffn(
        x_prompt.reshape(mp, d), x_sample.reshape(ns, d), row(ffn1_pre_norm),
        bf(ffn1_w_gate), bf(ffn1_w_up), bf(ffn1_w_down), row(ffn1_post_norm),
        ns=ns, split_out=False,
        side=(w_in[0], s5_w_glu[0], w_branch_s5[0], w_branch_hgrn[0], w_out[0]))
    proj = _inproj(x, row(mix_pre_norm), w_in_b)

    tw_in, tw_out, a_re, a_im = _s5_tables(
        s5_lambda_re[0], s5_lambda_im[0], s5_log_dt[0], s5_b_re[0], s5_b_im[0],
        s5_c_re[0], s5_c_im[0])
    d_skip = row(s5_d)
    z, hlast, wg2_b, wu2_b = _s5_prompt(
        proj, tw_in.astype(BF16), tw_out.astype(BF16), a_re, a_im, d_skip,
        batch=batch, seq=seq, rows_total=mp, side=(ffn2_w_gate[0], ffn2_w_up[0]))
    zs, s_re, s_im = _s5_sample(proj, state_s5_re[0].reshape(ns, g * p),
                                state_s5_im[0].reshape(ns, g * p),
                                tw_in, tw_out, a_re, a_im, d_skip, row0=mp)
    half = hlast.shape[-1] // 2
    p_re = hlast[:, :, SUBLANES - 1, :half].reshape(1, batch, g, p)
    p_im = hlast[:, :, SUBLANES - 1, half:].reshape(1, batch, g, p)

    gn = row(hgrn_out_norm)
    hg, st_p, wd2_b = _hg_prompt(proj, lb, gn, batch=batch, seq=seq, rows_total=mp, nh=nh,
                                 col0=s5w, side=(ffn2_w_down[0],))
    hgs, st_s = _hg_sample(proj, lb, gn, state_hgrn[0], row0=mp, nh=nh, col0=s5w)

    x = _merge(z, zs, hg, hgs, proj, x, wglu_b, row(s5_b_glu), wbs_b, wbh_b, wout_b,
               row(mix_post_norm), col0=s5w + 4 * hgw, tm=MERGE_ROW_TILE)
    yp, ys = _ffn(x, None, row(ffn2_pre_norm), wg2_b, wu2_b, wd2_b, row(ffn2_post_norm),
                  ns=ns, split_out=True)

    return (yp.reshape(batch, seq, d), ys.reshape(ns, 1, d),
            p_re, p_im, jnp.swapaxes(st_p, -1, -2)[None],
            s_re.reshape(1, ns, g, p), s_im.reshape(1, ns, g, p), st_s[None])
```

```python
import functools

import jax
import jax.numpy as jnp
import numpy as np
from jax import lax
from jax.experimental import pallas as pl
from jax.experimental.pallas import tpu as pltpu

F32 = jnp.float32
BF16 = jnp.bfloat16
EPS = 1e-6
HIGHEST = lax.Precision.HIGHEST

LANES = 128
SUBLANES = 8
VMEM_LIMIT = 56 * 1024 * 1024

ROW_TILE = 640
FF_TILE = 512
IN_TILE = 1024
IN_ROW_TILE = 1040
HG_CHUNK = 64
HG_CHUNKS_PER_STEP = 2
SAMPLE_TOKENS_PER_STEP = 8


def _cparams(*sem):
    return pltpu.CompilerParams(dimension_semantics=sem, vmem_limit_bytes=VMEM_LIMIT)


def _rms(x, g):
    return x * lax.rsqrt(jnp.mean(x * x, axis=-1, keepdims=True) + EPS) * g


def _silu(x):
    return x * jax.nn.sigmoid(x)


def _dot(a, b):
    return jnp.dot(a, b, preferred_element_type=F32)


def _dot_hi(a, b):
    return jnp.dot(a, b, preferred_element_type=F32, precision=HIGHEST)


def _dot_nt(a, b):
    return lax.dot_general(a, b, (((1,), (1,)), ((), ())), preferred_element_type=F32)


def _dot_tn(a, b):
    return lax.dot_general(a, b, (((0,), (0,)), ((), ())), preferred_element_type=F32)


BF16_ROWS = 16


def _side_specs(arrays, nsteps, step_of):
    in_specs, out_specs, out_shapes, plan = [], [], [], []
    for a in arrays:
        r, c = a.shape
        rows = next(t for t in range(BF16_ROWS, r + 1, BF16_ROWS) if r % t == 0 and r // t <= nsteps)
        nblk = r // rows
        every = nsteps // nblk

        def imap(*g, nblk=nblk, every=every):
            return (jnp.minimum(step_of(*g) // every, nblk - 1), 0)

        in_specs.append(pl.BlockSpec((rows, c), imap))
        out_specs.append(pl.BlockSpec((rows, c), imap))
        out_shapes.append(jax.ShapeDtypeStruct((r, c), BF16))
        plan.append((nblk, every))
    return in_specs, out_specs, out_shapes, tuple(plan)


def _side_cast(src_refs, dst_refs):
    for src, dst in zip(src_refs, dst_refs):
        dst[...] = src[...].astype(BF16)


def _ffn_body(*refs, ns, split_in, split_out, side_plan):
    refs = list(refs)
    nside = len(side_plan)
    x_ref = refs.pop(0)
    xs_ref = refs.pop(0) if split_in else None
    pre_ref, wg_ref, wu_ref, wd_ref, post_ref = refs[:5]
    side_src = refs[5:5 + nside]
    o_ref = refs[5 + nside]
    os_ref = refs[6 + nside] if split_out else None
    side_dst = refs[-2 - nside:-2]
    h_ref, acc_ref = refs[-2:]
    i, k = pl.program_id(0), pl.program_id(1)
    last_i = pl.num_programs(0) - 1
    last_k = pl.num_programs(1) - 1
    cut = h_ref.shape[0] - ns

    def swiglu_down(h):
        _side_cast(side_src, side_dst)
        g = _dot(h, wg_ref[...])
        u = _dot(h, wu_ref[...])
        return _dot((_silu(g) * u).astype(BF16), wd_ref[...])

    @pl.when((i < last_i) & (k == 0))
    def _():
        h = _rms(x_ref[...], pre_ref[...]).astype(BF16)
        h_ref[...] = h
        acc_ref[...] = swiglu_down(h)

    @pl.when((i < last_i) & (k == last_k))
    def _():
        acc = acc_ref[...] + swiglu_down(h_ref[...])
        o_ref[...] = x_ref[...] + 0.5 * _rms(acc, post_ref[...])

    @pl.when((i == last_i) & (k == 0))
    def _():
        h_ref[:cut, :] = _rms(x_ref[:cut, :], pre_ref[...]).astype(BF16)
        xt = xs_ref[...] if split_in else x_ref[cut:, :]
        h_ref[cut:, :] = _rms(xt, pre_ref[...]).astype(BF16)
        acc_ref[...] = jnp.zeros_like(acc_ref)

    @pl.when(((k > 0) & (k < last_k)) | (i == last_i))
    def _():
        acc_ref[...] += swiglu_down(h_ref[...])

    @pl.when((i == last_i) & (k == last_k))
    def _():
        o_ref[:cut, :] = x_ref[:cut, :] + 0.5 * _rms(acc_ref[:cut, :], post_ref[...])
        xt = xs_ref[...] if split_in else x_ref[cut:, :]
        tail = xt + 0.5 * _rms(acc_ref[cut:, :], post_ref[...])
        if split_out:
            os_ref[...] = tail
        else:
            o_ref[cut:, :] = tail


def _ffn(x, xs, pre, wg, wu, wd, post, *, ns, split_out, side=()):
    split_in = xs is not None
    d = x.shape[1]
    m = x.shape[0] + (ns if split_in else 0)
    dff = wg.shape[1]
    tm, tf = ROW_TILE, FF_TILE
    assert m % tm == 0 and 0 < ns < tm and ns % SUBLANES == 0
    nk = dff // tf
    side_in, side_out, side_shapes, side_plan = _side_specs(
        side, (m // tm) * nk, lambda i, k: i * nk + k)
    tok = pl.BlockSpec((tm, d), lambda i, k: (i, 0))
    smp = pl.BlockSpec((ns, d), lambda i, k: (0, 0))
    vec = pl.BlockSpec((1, d), lambda i, k: (0, 0))
    in_specs = [tok] + ([smp] if split_in else []) + [
        vec,
        pl.BlockSpec((d, tf), lambda i, k: (0, k)),
        pl.BlockSpec((d, tf), lambda i, k: (0, k)),
        pl.BlockSpec((tf, d), lambda i, k: (k, 0)),
        vec,
    ] + side_in
    if split_out:
        out_shape = [jax.ShapeDtypeStruct((m - ns, d), F32), jax.ShapeDtypeStruct((ns, d), F32)]
        out_specs = [tok, smp]
    else:
        out_shape = [jax.ShapeDtypeStruct((m, d), F32)]
        out_specs = [tok]
    args = [x] + ([xs] if split_in else []) + [pre, wg, wu, wd, post] + list(side)
    sequential_rows = split_out or bool(side)
    return pl.pallas_call(
        functools.partial(_ffn_body, ns=ns, split_in=split_in, split_out=split_out,
                          side_plan=side_plan),
        out_shape=out_shape + side_shapes,
        grid=(m // tm, nk),
        in_specs=in_specs,
        out_specs=out_specs + side_out,
        scratch_shapes=[pltpu.VMEM((tm, d), BF16), pltpu.VMEM((tm, d), F32)],
        compiler_params=_cparams("arbitrary" if sequential_rows else "parallel", "arbitrary"),
        name="ffn",
    )(*args)


def _inproj_body(x_ref, g_ref, w_ref, o_ref, h_ref):
    j = pl.program_id(1)

    @pl.when(j == 0)
    def _():
        h = _rms(x_ref[...], g_ref[...]).astype(BF16)
        h_ref[...] = h
        o_ref[...] = _dot(h, w_ref[...])

    @pl.when(j > 0)
    def _():
        o_ref[...] = _dot(h_ref[...], w_ref[...])


def _inproj(x, g, w):
    m, d = x.shape
    n = w.shape[1]
    tm, tn = IN_ROW_TILE, IN_TILE
    assert m % tm == 0
    return pl.pallas_call(
        _inproj_body,
        out_shape=jax.ShapeDtypeStruct((m, n), F32),
        grid=(m // tm, n // tn),
        in_specs=[
            pl.BlockSpec((tm, d), lambda i, j: (i, 0)),
            pl.BlockSpec((1, d), lambda i, j: (0, 0)),
            pl.BlockSpec((d, tn), lambda i, j: (0, j)),
        ],
        out_specs=pl.BlockSpec((tm, tn), lambda i, j: (i, j)),
        scratch_shapes=[pltpu.VMEM((tm, d), BF16)],
        compiler_params=_cparams("parallel", "arbitrary"),
        name="inproj",
    )(x, g, w)


def _s5_tables(lam_re, lam_im, log_dt, b_re, b_im, c_re, c_im):
    g, p = lam_re.shape
    n = b_re.shape[-1]
    gpt = LANES // n
    nv = g // gpt
    dt = jnp.exp(log_dt)[:, None]
    er = jnp.exp(lam_re * dt)
    th = lam_im * dt
    a_re, a_im = er * jnp.cos(th), er * jnp.sin(th)
    den = lam_re * lam_re + lam_im * lam_im
    k_re = ((a_re - 1.0) * lam_re + a_im * lam_im) / den
    k_im = (a_im * lam_re - (a_re - 1.0) * lam_im) / den
    bb_re = k_re[..., None] * b_re - k_im[..., None] * b_im
    bb_im = k_re[..., None] * b_im + k_im[..., None] * b_re
    eye = jnp.eye(gpt, dtype=F32)

    def in_blk(bb):
        t = bb.reshape(nv, gpt, p, n)
        return jnp.einsum("ab,vapn->vanbp", eye, t).reshape(nv, gpt * n, gpt * p)

    def out_blk(cc):
        t = cc.reshape(nv, gpt, n, p)
        return jnp.einsum("ab,vanp->vapbn", eye, t).reshape(nv, gpt * p, gpt * n)

    w_in = jnp.concatenate([in_blk(bb_re), in_blk(bb_im)], axis=-1)
    w_out = jnp.concatenate([out_blk(c_re), -out_blk(c_im)], axis=1)
    return w_in, w_out, a_re.reshape(nv, 1, gpt * p), a_im.reshape(nv, 1, gpt * p)


def _cmul(ar, ai, br, bi):
    return ar * br - ai * bi, ar * bi + ai * br


def _s5_prompt_body(*refs, seg, pitch, side_plan):
    nside = len(side_plan)
    u_ref, w_ref, cm_ref, ar_ref, ai_ref, d_ref = refs[:6]
    z_ref, hl_ref = refs[6 + nside:8 + nside]
    buh_ref = refs[-1]
    _side_cast(refs[6:6 + nside], refs[8 + nside:-1])
    nseg = SUBLANES
    nc = w_ref.shape[2] // LANES
    hc = nc // 2

    for j in range(nseg):
        bu = _dot(u_ref[j * seg:(j + 1) * seg, :].astype(BF16), w_ref[0])
        for c in range(nc):
            buh_ref[c, j * pitch:j * pitch + seg, :] = bu[:, c * LANES:(c + 1) * LANES]

    ar = [jnp.broadcast_to(ar_ref[0, :, c * LANES:(c + 1) * LANES], (nseg, LANES)) for c in range(hc)]
    ai = [jnp.broadcast_to(ai_ref[0, :, c * LANES:(c + 1) * LANES], (nseg, LANES)) for c in range(hc)]

    def rows(t):
        return pl.ds(t, nseg, stride=pitch)

    def advance(t, hs):
        out = []
        for c in range(hc):
            pr, pi = _cmul(ar[c], ai[c], hs[2 * c], hs[2 * c + 1])
            out += [pr + buh_ref[c, rows(t), :], pi + buh_ref[hc + c, rows(t), :]]
        return tuple(out)

    zeros = tuple(jnp.zeros((nseg, LANES), F32) for _ in range(2 * hc))
    ends = lax.fori_loop(0, seg, advance, zeros, unroll=4)

    pw = [(ar[c], ai[c]) for c in range(hc)]
    for _ in range(seg.bit_length() - 1):
        pw = [_cmul(r, i, r, i) for r, i in pw]

    row = lax.broadcasted_iota(jnp.int32, (nseg, LANES), 0)
    init = list(zeros)
    for j in range(1, nseg):
        for c in range(hc):
            pr, pi = _cmul(pw[c][0], pw[c][1], init[2 * c], init[2 * c + 1])
            nr = pltpu.roll(pr + ends[2 * c], 1, 0)
            ni = pltpu.roll(pi + ends[2 * c + 1], 1, 0)
            init[2 * c] = jnp.where(row == j, nr, init[2 * c])
            init[2 * c + 1] = jnp.where(row == j, ni, init[2 * c + 1])

    def advance_store(t, hs):
        new = advance(t, hs)
        for c in range(hc):
            buh_ref[c, rows(t), :] = new[2 * c]
            buh_ref[hc + c, rows(t), :] = new[2 * c + 1]
        return new

    last = lax.fori_loop(0, seg, advance_store, tuple(init), unroll=4)
    for c in range(hc):
        hl_ref[0, 0, :, c * LANES:(c + 1) * LANES] = last[2 * c]
        hl_ref[0, 0, :, (hc + c) * LANES:(hc + c + 1) * LANES] = last[2 * c + 1]

    for j in range(nseg):
        y = d_ref[...] * u_ref[j * seg:(j + 1) * seg, :]
        for c in range(nc):
            y = y + _dot(buh_ref[c, j * pitch:j * pitch + seg, :].astype(BF16),
                         cm_ref[0, c * LANES:(c + 1) * LANES, :])
        z_ref[j * seg:(j + 1) * seg, :] = jax.nn.gelu(y)


def _s5_prompt(proj, w_in, w_out, a_re, a_im, d_skip, *, batch, seq, rows_total, side=()):
    nv = w_in.shape[0]
    sw = w_in.shape[2]
    seg = seq // SUBLANES
    assert seg * SUBLANES == seq and seg & (seg - 1) == 0
    pitch = seg + SUBLANES
    side_in, side_out, side_shapes, side_plan = _side_specs(
        side, batch * nv, lambda b, v: b * nv + v)
    body = functools.partial(_s5_prompt_body, seg=seg, pitch=pitch, side_plan=side_plan)
    return pl.pallas_call(
        body,
        out_shape=[jax.ShapeDtypeStruct((rows_total, nv * LANES), F32),
                   jax.ShapeDtypeStruct((batch, nv, SUBLANES, sw), F32)] + side_shapes,
        grid=(batch, nv),
        in_specs=[
            pl.BlockSpec((seq, LANES), lambda b, v: (b, v)),
            pl.BlockSpec((1, LANES, sw), lambda b, v: (v, 0, 0)),
            pl.BlockSpec((1, sw, LANES), lambda b, v: (v, 0, 0)),
            pl.BlockSpec((1, 1, sw // 2), lambda b, v: (v, 0, 0)),
            pl.BlockSpec((1, 1, sw // 2), lambda b, v: (v, 0, 0)),
            pl.BlockSpec((1, LANES), lambda b, v: (0, v)),
        ] + side_in,
        out_specs=[pl.BlockSpec((seq, LANES), lambda b, v: (b, v)),
                   pl.BlockSpec((1, 1, SUBLANES, sw), lambda b, v: (b, v, 0, 0))] + side_out,
        scratch_shapes=[pltpu.VMEM((sw // LANES, SUBLANES * pitch, LANES), F32)],
        compiler_params=_cparams(*(("arbitrary",) * 2 if side else ("parallel",) * 2)),
        name="s5_prompt",
    )(proj, w_in, w_out, a_re, a_im, d_skip, *side)


def _s5_sample_body(u_ref, hre_ref, him_ref, w_ref, cm_ref, ar_ref, ai_ref, d_ref,
                    z_ref, ore_ref, oim_ref):
    half = ar_ref.shape[2]
    u = u_ref[...]
    bu = _dot_hi(u, w_ref[0])
    pr, pi = _cmul(ar_ref[0], ai_ref[0], hre_ref[...], him_ref[...])
    hr = pr + bu[:, :half]
    hi = pi + bu[:, half:]
    ore_ref[...] = hr
    oim_ref[...] = hi
    y = d_ref[...] * u + _dot_hi(hr, cm_ref[0, :half, :]) + _dot_hi(hi, cm_ref[0, half:, :])
    z_ref[...] = jax.nn.gelu(y)


def _s5_sample(proj, h_re, h_im, w_in, w_out, a_re, a_im, d_skip, *, row0):
    nv = w_in.shape[0]
    sw = w_in.shape[2]
    ns = h_re.shape[0]
    rb = row0 // ns
    assert rb * ns == row0
    st = pl.BlockSpec((ns, sw // 2), lambda v: (0, v))
    return pl.pallas_call(
        _s5_sample_body,
        out_shape=(jax.ShapeDtypeStruct((ns, nv * LANES), F32),
                   jax.ShapeDtypeStruct(h_re.shape, F32),
                   jax.ShapeDtypeStruct(h_im.shape, F32)),
        grid=(nv,),
        in_specs=[
            pl.BlockSpec((ns, LANES), lambda v: (rb, v)),
            st, st,
            pl.BlockSpec((1, LANES, sw), lambda v: (v, 0, 0)),
            pl.BlockSpec((1, sw, LANES), lambda v: (v, 0, 0)),
            pl.BlockSpec((1, 1, sw // 2), lambda v: (v, 0, 0)),
            pl.BlockSpec((1, 1, sw // 2), lambda v: (v, 0, 0)),
            pl.BlockSpec((1, LANES), lambda v: (0, v)),
        ],
        out_specs=(pl.BlockSpec((ns, LANES), lambda v: (0, v)), st, st),
        compiler_params=_cparams("parallel"),
        name="s5_sample",
    )(proj, h_re, h_im, w_in, w_out, a_re, a_im, d_skip)


def _hg_span_matrices(ch):
    t = np.arange(ch)[:, None]
    r = np.arange(ch)[None, :]
    mats = [r <= t, r > t]
    for v in range(ch.bit_length() - 1):
        base = (t >> v) << v
        upper = ((t >> v) & 1) == 1
        mats.append(np.where(upper, (r >= base) & (r <= t), (r > t) & (r < base + (1 << v))))
    mm = np.stack(mats).astype(np.float32)
    return jnp.asarray(np.concatenate([mm, mm], axis=-1), BF16)


def _hg_prompt_body(*refs, dk, side_plan):
    nside = len(side_plan)
    q_ref, f_ref, i_ref, og_ref, lb_ref, gn_ref, mm_ref = refs[:7]
    o_ref, sfin_ref = refs[7 + nside:9 + nside]
    st_ref = refs[-1]
    _side_cast(refs[7:7 + nside], refs[9 + nside:-1])
    c = pl.program_id(1)
    nh = st_ref.shape[0]
    ch = mm_ref.shape[1]
    nsub = q_ref.shape[0] // ch
    width = q_ref.shape[1]
    nlev = ch.bit_length() - 1

    @pl.when(c == 0)
    def _():
        st_ref[...] = jnp.zeros_like(st_ref)

    lb = lb_ref[...]
    f = lb + (1.0 - lb) * jax.nn.sigmoid(f_ref[...])
    kk_all = 1.0 - f
    qs_all = _silu(q_ref[...])
    logf_pieces = _split3(jnp.log(f))[:2]
    ib_all = i_ref[...].astype(BF16)
    r = lax.broadcasted_iota(jnp.int32, (ch, ch), 0)
    s = lax.broadcasted_iota(jnp.int32, (ch, ch), 1)
    xr = jnp.where(r > s, r ^ s, 0)
    heads = [slice(h * dk, (h + 1) * dk) for h in range(nh)]

    def upper_runs(v):
        m = 1 << v
        return [(b0, b0 + m) for b0 in range(m, ch, 2 * m)]

    def mix_rows(v, qs, kk):
        m = 1 << v
        if m >= SUBLANES:
            return jnp.concatenate(
                [(qs if (b0 // m) & 1 else kk)[b0:b0 + m] for b0 in range(0, ch, m)], axis=0)
        pick = ((lax.broadcasted_iota(jnp.int32, (1, SUBLANES, width), 1) >> v) & 1) == 1
        shape3 = (ch // SUBLANES, SUBLANES, width)
        return jnp.where(pick, qs.reshape(shape3), kk.reshape(shape3)).reshape(ch, width)

    staged = []
    for sub in range(nsub):
        rows = slice(sub * ch, (sub + 1) * ch)
        kk, qs = kk_all[rows], qs_all[rows]
        logf3 = jnp.concatenate([p[rows] for p in logf_pieces], axis=0)

        def decay(idx):
            return jnp.exp(_dot(mm_ref[idx], logf3))

        eg = decay(0)
        qg = (qs * eg).astype(BF16)
        dec = eg[ch - 1:ch, :]
        kh = (kk * decay(1)).astype(BF16)
        lev = [(decay(2 + v) * mix_rows(v, qs, kk)).astype(BF16) for v in range(nlev)]
        qb, kb = qs.astype(BF16), kk.astype(BF16)
        atts = []
        for sl in heads:
            att = jnp.where(r == s, _dot_nt(qb[:, sl], kb[:, sl]), 0.0)
            for v in range(nlev):
                x = lev[v][:, sl]
                m = 1 << v
                if m >= BF16_ROWS:
                    runs = upper_runs(v)
                    p = _dot_nt(jnp.concatenate([x[a:b] for a, b in runs], axis=0), x)
                    parts = []
                    for n in range(len(runs)):
                        parts += [jnp.zeros((m, ch), F32), p[n * m:(n + 1) * m]]
                    p = jnp.concatenate(parts, axis=0)
                else:
                    p = _dot_nt(x, x)
                att = jnp.where((xr >> v) == 1, p, att)
            atts.append(att.astype(BF16))
        staged.append((rows, qg, kh, dec, atts))

    for rows, qg, kh, dec, atts in staged:
        ib = ib_all[rows]
        for h, sl in enumerate(heads):
            o = _dot_nt(qg[:, sl], st_ref[h].astype(BF16)) + _dot(atts[h], ib[:, sl])
            og = og_ref[rows, sl]
            o_ref[rows, sl] = _rms(o, gn_ref[:, sl]) * _silu(og)
        for h, sl in enumerate(heads):
            st_ref[h] = st_ref[h] * dec[:, sl] + _dot_tn(ib[:, sl], kh[:, sl])

    @pl.when(c == pl.num_programs(1) - 1)
    def _():
        sfin_ref[0] = st_ref[...]


def _hg_prompt(proj, lb, gn, *, batch, seq, rows_total, nh, col0, side=()):
    width = lb.shape[1]
    dk = width // nh
    ch = HG_CHUNK * HG_CHUNKS_PER_STEP
    nchunk = seq // ch
    cb = col0 // width
    assert cb * width == col0 and nchunk * ch == seq

    def tok(k):
        return pl.BlockSpec((ch, width), lambda b, c: (b * nchunk + c, cb + k))

    vec = pl.BlockSpec((1, width), lambda b, c: (0, 0))
    mm = _hg_span_matrices(HG_CHUNK)
    side_in, side_out, side_shapes, side_plan = _side_specs(
        side, batch * nchunk, lambda b, c: b * nchunk + c)
    return pl.pallas_call(
        functools.partial(_hg_prompt_body, dk=dk, side_plan=side_plan),
        out_shape=[jax.ShapeDtypeStruct((rows_total, width), F32),
                   jax.ShapeDtypeStruct((batch, nh, dk, dk), F32)] + side_shapes,
        grid=(batch, nchunk),
        in_specs=[tok(0), tok(1), tok(2), tok(3), vec, vec,
                  pl.BlockSpec(mm.shape, lambda b, c: (0, 0, 0))] + side_in,
        out_specs=[pl.BlockSpec((ch, width), lambda b, c: (b * nchunk + c, 0)),
                   pl.BlockSpec((1, nh, dk, dk), lambda b, c: (b, 0, 0, 0))] + side_out,
        scratch_shapes=[pltpu.VMEM((nh, dk, dk), F32)],
        compiler_params=_cparams("arbitrary" if side else "parallel", "arbitrary"),
        name="hgrn_prompt",
    )(proj, proj, proj, proj, lb, gn, mm, *side)


def _split3(x):
    p1 = x.astype(BF16)
    r1 = x - p1.astype(F32)
    p2 = r1.astype(BF16)
    p3 = (r1 - p2.astype(F32)).astype(BF16)
    return p1, p2, p3


def _hg_sample_body(q_ref, f_ref, i_ref, og_ref, lb_ref, gn_ref, s0_ref,
                    hg_ref, s_ref, fq_ref, *, dk):
    step = pl.program_id(0)
    nt = q_ref.shape[0]
    tb, nh = s0_ref.shape[0], s0_ref.shape[1]

    @pl.when(step == 0)
    def _():
        lb = lb_ref[...]
        f = lb + (1.0 - lb) * jax.nn.sigmoid(f_ref[...])
        qs = _silu(q_ref[...])
        for src, base in ((f, 0), (qs, nh)):
            for h in range(nh):
                t = src[:, h * dk:(h + 1) * dk].T
                for p, piece in enumerate(_split3(t)):
                    fq_ref[(base + h) * dk:(base + h + 1) * dk, p * nt:(p + 1) * nt] = piece

    tok = lax.broadcasted_iota(jnp.int32, (3 * nt, dk), 0)
    for j in range(tb):
        n = step * tb + j
        onehot = ((tok == n) | (tok == n + nt) | (tok == n + 2 * nt)).astype(BF16)
        fq = _dot(fq_ref[...], onehot)
        irow = i_ref[pl.ds(n, 1), :]
        ogrow = og_ref[pl.ds(n, 1), :]
        for h in range(nh):
            sl = slice(h * dk, (h + 1) * dk)
            fb = fq[h * dk:(h + 1) * dk, :]
            qb = fq[(nh + h) * dk:(nh + h + 1) * dk, :]
            s1 = fb * s0_ref[j, h] + (1.0 - fb) * irow[:, sl]
            s_ref[j, h] = s1
            o = jnp.sum(qb * s1, axis=0, keepdims=True)
            hg_ref[j:j + 1, sl] = _rms(o, gn_ref[:, sl]) * _silu(ogrow[:, sl])


def _hg_sample(proj, lb, gn, s0, *, row0, nh, col0):
    width = lb.shape[1]
    dk = width // nh
    ns = s0.shape[0]
    tb = SAMPLE_TOKENS_PER_STEP
    rb, cb = row0 // ns, col0 // width
    assert rb * ns == row0 and cb * width == col0 and ns % tb == 0

    def tok(k):
        return pl.BlockSpec((ns, width), lambda t: (rb, cb + k))

    vec = pl.BlockSpec((1, width), lambda t: (0, 0))
    sspec = pl.BlockSpec((tb, nh, dk, dk), lambda t: (t, 0, 0, 0))
    return pl.pallas_call(
        functools.partial(_hg_sample_body, dk=dk),
        out_shape=(jax.ShapeDtypeStruct((ns, width), F32),
                   jax.ShapeDtypeStruct(s0.shape, F32)),
        grid=(ns // tb,),
        in_specs=[tok(0), tok(1), tok(2), tok(3), vec, vec, sspec],
        out_specs=(pl.BlockSpec((tb, width), lambda t: (t, 0)), sspec),
        scratch_shapes=[pltpu.VMEM((2 * nh * dk, 3 * ns), BF16)],
        compiler_params=_cparams("arbitrary"),
        name="hgrn_sample",
    )(proj, proj, proj, proj, lb, gn, s0)


def _merge_body(z_ref, zs_ref, hg_ref, hgs_ref, gs0_ref, gs1_ref, gh0_ref, gh1_ref, x_ref,
                wglu_ref, bglu_ref, wbs_ref, wbh_ref, wout_ref, post_ref, o_ref, *, ns):
    i = pl.program_id(0)
    last_i = pl.num_programs(0) - 1
    cut = x_ref.shape[0] - ns
    half = gs0_ref.shape[1]

    def rows(z, hg, sl):
        s5o = z * jax.nn.sigmoid(_dot(z.astype(BF16), wglu_ref[...]) + bglu_ref[...])
        a = _dot(s5o.astype(BF16), wbs_ref[...])
        b = _dot(hg.astype(BF16), wbh_ref[...])
        m0 = (jax.nn.sigmoid(gs0_ref[sl, :]) * a[:, :half]
              + jax.nn.sigmoid(gh0_ref[sl, :]) * b[:, :half])
        m1 = (jax.nn.sigmoid(gs1_ref[sl, :]) * a[:, half:]
              + jax.nn.sigmoid(gh1_ref[sl, :]) * b[:, half:])
        mix = _dot(m0.astype(BF16), wout_ref[:half, :]) + _dot(m1.astype(BF16), wout_ref[half:, :])
        o_ref[sl, :] = x_ref[sl, :] + _rms(mix, post_ref[...])

    @pl.when(i < last_i)
    def _():
        rows(z_ref[...], hg_ref[...], slice(None))

    @pl.when(i == last_i)
    def _():
        rows(z_ref[:cut, :], hg_ref[:cut, :], slice(0, cut))
        rows(zs_ref[...], hgs_ref[...], slice(cut, None))


def _merge(z, zs, hg, hgs, proj, x, wglu, bglu, wbs, wbh, wout, post, *, col0, tm):
    m, d = x.shape
    w = z.shape[1]
    ns = zs.shape[0]
    cb = col0 // w
    assert cb * w == col0 and d == 2 * w and m % tm == 0 and 0 < ns < tm

    def gate(k):
        return pl.BlockSpec((tm, w), lambda i: (i, cb + k))

    def const(shape):
        return pl.BlockSpec(shape, lambda i: (0, 0), pipeline_mode=pl.Buffered(1))

    tok = pl.BlockSpec((tm, w), lambda i: (i, 0))
    return pl.pallas_call(
        functools.partial(_merge_body, ns=ns),
        out_shape=jax.ShapeDtypeStruct((m, d), F32),
        grid=(m // tm,),
        in_specs=[
            tok, const((ns, w)), tok, const((ns, w)),
            gate(0), gate(1), gate(2), gate(3),
            pl.BlockSpec((tm, d), lambda i: (i, 0)),
            const((w, w)), const((1, w)), const((w, d)), const((w, d)), const((d, d)), const((1, d)),
        ],
        out_specs=pl.BlockSpec((tm, d), lambda i: (i, 0)),
        compiler_params=_cparams("parallel"),
        name="merge",
    )(z, zs, hg, hgs, proj, proj, proj, proj, x, wglu, bglu, wbs, wbh, wout, post)


MERGE_ROW_TILE = 320


def kernel(x_prompt, x_sample, state_s5_re, state_s5_im, state_hgrn, ffn1_pre_norm, ffn1_w_gate, ffn1_w_up, ffn1_w_down, ffn1_post_norm, mix_pre_norm, w_in, s5_lambda_re, s5_lambda_im, s5_log_dt, s5_b_re, s5_b_im, s5_c_re, s5_c_im, s5_d, s5_w_glu, s5_b_glu, hgrn_lb_logits, hgrn_out_norm, w_branch_s5, w_branch_hgrn, w_out, mix_post_norm, ffn2_pre_norm, ffn2_w_gate, ffn2_w_up, ffn2_w_down, ffn2_post_norm):
    depth = ffn1_w_gate.shape[0]
    assert depth == 1
    batch, seq, d = x_prompt.shape
    ns = x_sample.shape[0]
    assert x_sample.shape[1] == 1
    g, p = s5_lambda_re.shape[1:]
    nh, dk = state_hgrn.shape[2], state_hgrn.shape[3]
    s5w = s5_d.shape[1]
    hgw = nh * dk
    mp = batch * seq
    m = mp + ns

    bf = lambda a: a[0].astype(BF16)
    row = lambda a: a[0].reshape(1, -1).astype(F32)

    lb_all = jnp.cumsum(jax.nn.softmax(hgrn_lb_logits.astype(F32), axis=0), axis=0)
    lb = lb_all[0].reshape(1, hgw)

    x, w_in_b, wglu_b, wbs_b, wbh_b, wout_b = _ffn(
        x_prompt.reshape(mp, d), x_sample.reshape(ns, d), row(ffn1_pre_norm),
        bf(ffn1_w_gate), bf(ffn1_w_up), bf(ffn1_w_down), row(ffn1_post_norm),
        ns=ns, split_out=False,
        side=(w_in[0], s5_w_glu[0], w_branch_s5[0], w_branch_hgrn[0], w_out[0]))
    proj = _inproj(x, row(mix_pre_norm), w_in_b)

    tw_in, tw_out, a_re, a_im = _s5_tables(
        s5_lambda_re[0], s5_lambda_im[0], s5_log_dt[0], s5_b_re[0], s5_b_im[0],
        s5_c_re[0], s5_c_im[0])
    d_skip = row(s5_d)
    z, hlast, wg2_b, wu2_b = _s5_prompt(
        proj, tw_in.astype(BF16), tw_out.astype(BF16), a_re, a_im, d_skip,
        batch=batch, seq=seq, rows_total=mp, side=(ffn2_w_gate[0], ffn2_w_up[0]))
    zs, s_re, s_im = _s5_sample(proj, state_s5_re[0].reshape(ns, g * p),
                                state_s5_im[0].reshape(ns, g * p),
                                tw_in, tw_out, a_re, a_im, d_skip, row0=mp)
    half = hlast.shape[-1] // 2
    p_re = hlast[:, :, SUBLANES - 1, :half].reshape(1, batch, g, p)
    p_im = hlast[:, :, SUBLANES - 1, half:].reshape(1, batch, g, p)

    gn = row(hgrn_out_norm)
    hg, st_p, wd2_b = _hg_prompt(proj, lb, gn, batch=batch, seq=seq, rows_total=mp, nh=nh,
                                 col0=s5w, side=(ffn2_w_down[0],))
    hgs, st_s = _hg_sample(proj, lb, gn, state_hgrn[0], row0=mp, nh=nh, col0=s5w)

    x = _merge(z, zs, hg, hgs, proj, x, wglu_b, row(s5_b_glu), wbs_b, wbh_b, wout_b,
               row(mix_post_norm), col0=s5w + 4 * hgw, tm=MERGE_ROW_TILE)
    yp, ys = _ffn(x, None, row(ffn2_pre_norm), wg2_b, wu2_b, wd2_b, row(ffn2_post_norm),
                  ns=ns, split_out=True)

    return (yp.reshape(batch, seq, d), ys.reshape(ns, 1, d),
            p_re, p_im, jnp.swapaxes(st_p, -1, -2)[None],
            s_re.reshape(1, ns, g, p), s_im.reshape(1, ns, g, p), st_s[None])
```

```python
import functools

import jax
import jax.numpy as jnp
import numpy as np
from jax import lax
from jax.experimental import pallas as pl
from jax.experimental.pallas import tpu as pltpu

F32 = jnp.float32
BF16 = jnp.bfloat16
EPS = 1e-6
HIGHEST = lax.Precision.HIGHEST

LANES = 128
SUBLANES = 8
VMEM_LIMIT = 56 * 1024 * 1024

ROW_TILE = 640
FF_TILE = 512
IN_TILE = 1024
IN_ROW_TILE = 1040
S5_SCAN_UNROLL = 8
HG_CHUNK = 64
HG_CHUNKS_PER_STEP = 2
SAMPLE_TOKENS_PER_STEP = 8


def _cparams(*sem):
    return pltpu.CompilerParams(dimension_semantics=sem, vmem_limit_bytes=VMEM_LIMIT)


def _rms(x, g):
    return x * lax.rsqrt(jnp.mean(x * x, axis=-1, keepdims=True) + EPS) * g


def _silu(x):
    return x * jax.nn.sigmoid(x)


def _dot(a, b):
    return jnp.dot(a, b, preferred_element_type=F32)


def _dot_hi(a, b):
    return jnp.dot(a, b, preferred_element_type=F32, precision=HIGHEST)


def _dot_nt(a, b):
    return lax.dot_general(a, b, (((1,), (1,)), ((), ())), preferred_element_type=F32)


def _dot_tn(a, b):
    return lax.dot_general(a, b, (((0,), (0,)), ((), ())), preferred_element_type=F32)


BF16_ROWS = 16


def _side_specs(arrays, nsteps, step_of):
    in_specs, out_specs, out_shapes, plan = [], [], [], []
    for a in arrays:
        r, c = a.shape
        rows = next(t for t in range(BF16_ROWS, r + 1, BF16_ROWS) if r % t == 0 and r // t <= nsteps)
        nblk = r // rows
        every = nsteps // nblk

        def imap(*g, nblk=nblk, every=every):
            return (jnp.minimum(step_of(*g) // every, nblk - 1), 0)

        in_specs.append(pl.BlockSpec((rows, c), imap))
        out_specs.append(pl.BlockSpec((rows, c), imap))
        out_shapes.append(jax.ShapeDtypeStruct((r, c), BF16))
        plan.append((nblk, every))
    return in_specs, out_specs, out_shapes, tuple(plan)


def _side_cast(src_refs, dst_refs):
    for src, dst in zip(src_refs, dst_refs):
        dst[...] = src[...].astype(BF16)


def _ffn_body(*refs, ns, split_in, split_out, side_plan):
    refs = list(refs)
    nside = len(side_plan)
    x_ref = refs.pop(0)
    xs_ref = refs.pop(0) if split_in else None
    pre_ref, wg_ref, wu_ref, wd_ref, post_ref = refs[:5]
    side_src = refs[5:5 + nside]
    o_ref = refs[5 + nside]
    os_ref = refs[6 + nside] if split_out else None
    side_dst = refs[-2 - nside:-2]
    h_ref, acc_ref = refs[-2:]
    i, k = pl.program_id(0), pl.program_id(1)
    last_i = pl.num_programs(0) - 1
    last_k = pl.num_programs(1) - 1
    cut = h_ref.shape[0] - ns

    def swiglu_down(h):
        _side_cast(side_src, side_dst)
        g = _dot(h, wg_ref[...])
        u = _dot(h, wu_ref[...])
        return _dot((_silu(g) * u).astype(BF16), wd_ref[...])

    @pl.when((i < last_i) & (k == 0))
    def _():
        h = _rms(x_ref[...], pre_ref[...]).astype(BF16)
        h_ref[...] = h
        acc_ref[...] = swiglu_down(h)

    @pl.when((i < last_i) & (k == last_k))
    def _():
        acc = acc_ref[...] + swiglu_down(h_ref[...])
        o_ref[...] = x_ref[...] + 0.5 * _rms(acc, post_ref[...])

    @pl.when((i == last_i) & (k == 0))
    def _():
        h_ref[:cut, :] = _rms(x_ref[:cut, :], pre_ref[...]).astype(BF16)
        xt = xs_ref[...] if split_in else x_ref[cut:, :]
        h_ref[cut:, :] = _rms(xt, pre_ref[...]).astype(BF16)
        acc_ref[...] = jnp.zeros_like(acc_ref)

    @pl.when(((k > 0) & (k < last_k)) | (i == last_i))
    def _():
        acc_ref[...] += swiglu_down(h_ref[...])

    @pl.when((i == last_i) & (k == last_k))
    def _():
        o_ref[:cut, :] = x_ref[:cut, :] + 0.5 * _rms(acc_ref[:cut, :], post_ref[...])
        xt = xs_ref[...] if split_in else x_ref[cut:, :]
        tail = xt + 0.5 * _rms(acc_ref[cut:, :], post_ref[...])
        if split_out:
            os_ref[...] = tail
        else:
            o_ref[cut:, :] = tail


def _ffn(x, xs, pre, wg, wu, wd, post, *, ns, split_out, side=()):
    split_in = xs is not None
    d = x.shape[1]
    m = x.shape[0] + (ns if split_in else 0)
    dff = wg.shape[1]
    tm, tf = ROW_TILE, FF_TILE
    assert m % tm == 0 and 0 < ns < tm and ns % SUBLANES == 0
    nk = dff // tf
    side_in, side_out, side_shapes, side_plan = _side_specs(
        side, (m // tm) * nk, lambda i, k: i * nk + k)
    tok = pl.BlockSpec((tm, d), lambda i, k: (i, 0))
    smp = pl.BlockSpec((ns, d), lambda i, k: (0, 0))
    vec = pl.BlockSpec((1, d), lambda i, k: (0, 0))
    in_specs = [tok] + ([smp] if split_in else []) + [
        vec,
        pl.BlockSpec((d, tf), lambda i, k: (0, k)),
        pl.BlockSpec((d, tf), lambda i, k: (0, k)),
        pl.BlockSpec((tf, d), lambda i, k: (k, 0)),
        vec,
    ] + side_in
    if split_out:
        out_shape = [jax.ShapeDtypeStruct((m - ns, d), F32), jax.ShapeDtypeStruct((ns, d), F32)]
        out_specs = [tok, smp]
    else:
        out_shape = [jax.ShapeDtypeStruct((m, d), F32)]
        out_specs = [tok]
    args = [x] + ([xs] if split_in else []) + [pre, wg, wu, wd, post] + list(side)
    sequential_rows = split_out or bool(side)
    return pl.pallas_call(
        functools.partial(_ffn_body, ns=ns, split_in=split_in, split_out=split_out,
                          side_plan=side_plan),
        out_shape=out_shape + side_shapes,
        grid=(m // tm, nk),
        in_specs=in_specs,
        out_specs=out_specs + side_out,
        scratch_shapes=[pltpu.VMEM((tm, d), BF16), pltpu.VMEM((tm, d), F32)],
        compiler_params=_cparams("arbitrary" if sequential_rows else "parallel", "arbitrary"),
        name="ffn",
    )(*args)


def _inproj_body(x_ref, g_ref, w_ref, o_ref, h_ref):
    j = pl.program_id(1)

    @pl.when(j == 0)
    def _():
        h = _rms(x_ref[...], g_ref[...]).astype(BF16)
        h_ref[...] = h
        o_ref[...] = _dot(h, w_ref[...])

    @pl.when(j > 0)
    def _():
        o_ref[...] = _dot(h_ref[...], w_ref[...])


def _inproj(x, g, w):
    m, d = x.shape
    n = w.shape[1]
    tm, tn = IN_ROW_TILE, IN_TILE
    assert m % tm == 0
    return pl.pallas_call(
        _inproj_body,
        out_shape=jax.ShapeDtypeStruct((m, n), F32),
        grid=(m // tm, n // tn),
        in_specs=[
            pl.BlockSpec((tm, d), lambda i, j: (i, 0)),
            pl.BlockSpec((1, d), lambda i, j: (0, 0)),
            pl.BlockSpec((d, tn), lambda i, j: (0, j)),
        ],
        out_specs=pl.BlockSpec((tm, tn), lambda i, j: (i, j)),
        scratch_shapes=[pltpu.VMEM((tm, d), BF16)],
        compiler_params=_cparams("parallel", "arbitrary"),
        name="inproj",
    )(x, g, w)


def _s5_tables(lam_re, lam_im, log_dt, b_re, b_im, c_re, c_im):
    g, p = lam_re.shape
    n = b_re.shape[-1]
    gpt = LANES // n
    nv = g // gpt
    dt = jnp.exp(log_dt)[:, None]
    er = jnp.exp(lam_re * dt)
    th = lam_im * dt
    a_re, a_im = er * jnp.cos(th), er * jnp.sin(th)
    den = lam_re * lam_re + lam_im * lam_im
    k_re = ((a_re - 1.0) * lam_re + a_im * lam_im) / den
    k_im = (a_im * lam_re - (a_re - 1.0) * lam_im) / den
    bb_re = k_re[..., None] * b_re - k_im[..., None] * b_im
    bb_im = k_re[..., None] * b_im + k_im[..., None] * b_re
    eye = jnp.eye(gpt, dtype=F32)

    def in_blk(bb):
        t = bb.reshape(nv, gpt, p, n)
        return jnp.einsum("ab,vapn->vanbp", eye, t).reshape(nv, gpt * n, gpt * p)

    def out_blk(cc):
        t = cc.reshape(nv, gpt, n, p)
        return jnp.einsum("ab,vanp->vapbn", eye, t).reshape(nv, gpt * p, gpt * n)

    w_in = jnp.concatenate([in_blk(bb_re), in_blk(bb_im)], axis=-1)
    w_out = jnp.concatenate([out_blk(c_re), -out_blk(c_im)], axis=1)
    return w_in, w_out, a_re.reshape(nv, 1, gpt * p), a_im.reshape(nv, 1, gpt * p)


def _cmul(ar, ai, br, bi):
    return ar * br - ai * bi, ar * bi + ai * br


def _s5_prompt_body(*refs, seg, side_plan):
    nside = len(side_plan)
    u_ref, w_ref, cm_ref, ar_ref, ai_ref, d_ref = refs[:6]
    z_ref, hl_ref = refs[6 + nside:8 + nside]
    buh_ref = refs[-1]
    _side_cast(refs[6:6 + nside], refs[8 + nside:-1])
    nseg = SUBLANES
    nc = w_ref.shape[2] // LANES
    hc = nc // 2

    def seg_rows(j):
        return pl.ds(j, seg, stride=nseg)

    for j in range(nseg):
        bu = _dot(u_ref[j * seg:(j + 1) * seg, :].astype(BF16), w_ref[0])
        for c in range(nc):
            buh_ref[c, seg_rows(j), :] = bu[:, c * LANES:(c + 1) * LANES]

    ar = [jnp.broadcast_to(ar_ref[0, :, c * LANES:(c + 1) * LANES], (nseg, LANES)) for c in range(hc)]
    ai = [jnp.broadcast_to(ai_ref[0, :, c * LANES:(c + 1) * LANES], (nseg, LANES)) for c in range(hc)]

    def rows(t):
        return pl.ds(pl.multiple_of(t * nseg, nseg), nseg)

    def advance(t, hs):
        out = []
        for c in range(hc):
            pr, pi = _cmul(ar[c], ai[c], hs[2 * c], hs[2 * c + 1])
            out += [pr + buh_ref[c, rows(t), :], pi + buh_ref[hc + c, rows(t), :]]
        return tuple(out)

    zeros = tuple(jnp.zeros((nseg, LANES), F32) for _ in range(2 * hc))
    ends = lax.fori_loop(0, seg, advance, zeros, unroll=S5_SCAN_UNROLL)

    pw = [(ar[c], ai[c]) for c in range(hc)]
    for _ in range(seg.bit_length() - 1):
        pw = [_cmul(r, i, r, i) for r, i in pw]

    row = lax.broadcasted_iota(jnp.int32, (nseg, LANES), 0)
    init = list(zeros)
    for j in range(1, nseg):
        for c in range(hc):
            pr, pi = _cmul(pw[c][0], pw[c][1], init[2 * c], init[2 * c + 1])
            nr = pltpu.roll(pr + ends[2 * c], 1, 0)
            ni = pltpu.roll(pi + ends[2 * c + 1], 1, 0)
            init[2 * c] = jnp.where(row == j, nr, init[2 * c])
            init[2 * c + 1] = jnp.where(row == j, ni, init[2 * c + 1])

    def advance_store(t, hs):
        new = advance(t, hs)
        for c in range(hc):
            buh_ref[c, rows(t), :] = new[2 * c]
            buh_ref[hc + c, rows(t), :] = new[2 * c + 1]
        return new

    last = lax.fori_loop(0, seg, advance_store, tuple(init), unroll=S5_SCAN_UNROLL)
    for c in range(hc):
        hl_ref[0, 0, :, c * LANES:(c + 1) * LANES] = last[2 * c]
        hl_ref[0, 0, :, (hc + c) * LANES:(hc + c + 1) * LANES] = last[2 * c + 1]

    for j in range(nseg):
        y = d_ref[...] * u_ref[j * seg:(j + 1) * seg, :]
        for c in range(nc):
            y = y + _dot(buh_ref[c, seg_rows(j), :].astype(BF16),
                         cm_ref[0, c * LANES:(c + 1) * LANES, :])
        z_ref[j * seg:(j + 1) * seg, :] = jax.nn.gelu(y)


def _s5_prompt(proj, w_in, w_out, a_re, a_im, d_skip, *, batch, seq, rows_total, side=()):
    nv = w_in.shape[0]
    sw = w_in.shape[2]
    seg = seq // SUBLANES
    assert seg * SUBLANES == seq and seg & (seg - 1) == 0
    side_in, side_out, side_shapes, side_plan = _side_specs(
        side, batch * nv, lambda b, v: b * nv + v)
    body = functools.partial(_s5_prompt_body, seg=seg, side_plan=side_plan)
    return pl.pallas_call(
        body,
        out_shape=[jax.ShapeDtypeStruct((rows_total, nv * LANES), F32),
                   jax.ShapeDtypeStruct((batch, nv, SUBLANES, sw), F32)] + side_shapes,
        grid=(batch, nv),
        in_specs=[
            pl.BlockSpec((seq, LANES), lambda b, v: (b, v)),
            pl.BlockSpec((1, LANES, sw), lambda b, v: (v, 0, 0)),
            pl.BlockSpec((1, sw, LANES), lambda b, v: (v, 0, 0)),
            pl.BlockSpec((1, 1, sw // 2), lambda b, v: (v, 0, 0)),
            pl.BlockSpec((1, 1, sw // 2), lambda b, v: (v, 0, 0)),
            pl.BlockSpec((1, LANES), lambda b, v: (0, v)),
        ] + side_in,
        out_specs=[pl.BlockSpec((seq, LANES), lambda b, v: (b, v)),
                   pl.BlockSpec((1, 1, SUBLANES, sw), lambda b, v: (b, v, 0, 0))] + side_out,
        scratch_shapes=[pltpu.VMEM((sw // LANES, seq, LANES), F32)],
        compiler_params=_cparams(*(("arbitrary",) * 2 if side else ("parallel",) * 2)),
        name="s5_prompt",
    )(proj, w_in, w_out, a_re, a_im, d_skip, *side)


def _s5_sample_body(u_ref, hre_ref, him_ref, w_ref, cm_ref, ar_ref, ai_ref, d_ref,
                    z_ref, ore_ref, oim_ref):
    half = ar_ref.shape[2]
    u = u_ref[...]
    bu = _dot_hi(u, w_ref[0])
    pr, pi = _cmul(ar_ref[0], ai_ref[0], hre_ref[...], him_ref[...])
    hr = pr + bu[:, :half]
    hi = pi + bu[:, half:]
    ore_ref[...] = hr
    oim_ref[...] = hi
    y = d_ref[...] * u + _dot_hi(hr, cm_ref[0, :half, :]) + _dot_hi(hi, cm_ref[0, half:, :])
    z_ref[...] = jax.nn.gelu(y)


def _s5_sample(proj, h_re, h_im, w_in, w_out, a_re, a_im, d_skip, *, row0):
    nv = w_in.shape[0]
    sw = w_in.shape[2]
    ns = h_re.shape[0]
    rb = row0 // ns
    assert rb * ns == row0
    st = pl.BlockSpec((ns, sw // 2), lambda v: (0, v))
    return pl.pallas_call(
        _s5_sample_body,
        out_shape=(jax.ShapeDtypeStruct((ns, nv * LANES), F32),
                   jax.ShapeDtypeStruct(h_re.shape, F32),
                   jax.ShapeDtypeStruct(h_im.shape, F32)),
        grid=(nv,),
        in_specs=[
            pl.BlockSpec((ns, LANES), lambda v: (rb, v)),
            st, st,
            pl.BlockSpec((1, LANES, sw), lambda v: (v, 0, 0)),
            pl.BlockSpec((1, sw, LANES), lambda v: (v, 0, 0)),
            pl.BlockSpec((1, 1, sw // 2), lambda v: (v, 0, 0)),
            pl.BlockSpec((1, 1, sw // 2), lambda v: (v, 0, 0)),
            pl.BlockSpec((1, LANES), lambda v: (0, v)),
        ],
        out_specs=(pl.BlockSpec((ns, LANES), lambda v: (0, v)), st, st),
        compiler_params=_cparams("parallel"),
        name="s5_sample",
    )(proj, h_re, h_im, w_in, w_out, a_re, a_im, d_skip)


def _hg_span_matrices(ch):
    t = np.arange(ch)[:, None]
    r = np.arange(ch)[None, :]
    mats = [r <= t, r > t]
    for v in range(1, ch.bit_length() - 1):
        base = (t >> v) << v
        upper = ((t >> v) & 1) == 1
        mats.append(np.where(upper, (r >= base) & (r <= t), (r > t) & (r < base + (1 << v))))
    mm = np.stack(mats).astype(np.float32)
    return jnp.asarray(np.concatenate([mm, mm], axis=-1), BF16)


def _hg_prompt_body(*refs, dk, side_plan):
    nside = len(side_plan)
    q_ref, f_ref, i_ref, og_ref, lb_ref, gn_ref, mm_ref = refs[:7]
    o_ref, sfin_ref = refs[7 + nside:9 + nside]
    st_ref = refs[-1]
    _side_cast(refs[7:7 + nside], refs[9 + nside:-1])
    c = pl.program_id(1)
    nh = st_ref.shape[0]
    ch = mm_ref.shape[1]
    nsub = q_ref.shape[0] // ch
    width = q_ref.shape[1]
    nlev = ch.bit_length() - 1

    @pl.when(c == 0)
    def _():
        st_ref[...] = jnp.zeros_like(st_ref)

    lb = lb_ref[...]
    f_all = lb + (1.0 - lb) * jax.nn.sigmoid(f_ref[...])
    kk_all = 1.0 - f_all
    qs_all = _silu(q_ref[...])
    logf_pieces = _split3(jnp.log(f_all))[:2]
    ib_all = i_ref[...].astype(BF16)
    pairs = [slice(2 * j * dk, 2 * (j + 1) * dk) for j in range(nh // 2)]
    r = lax.broadcasted_iota(jnp.int32, (ch, 2 * ch), 0)
    s = lax.broadcasted_iota(jnp.int32, (ch, 2 * ch), 1) & (ch - 1)
    xr = jnp.where(r > s, r ^ s, 0)
    zero_k = jnp.zeros((ch, dk), BF16)
    zero_s = jnp.zeros((dk, dk), BF16)

    def block_diag(a, b, z):
        return jnp.concatenate([jnp.concatenate([a, z], axis=1),
                                jnp.concatenate([z, b], axis=1)], axis=0)

    def pair_scores(lhs, x):
        return _dot_nt(lhs, block_diag(x[:, :dk], x[:, dk:], zero_k))

    def upper_runs(v):
        m = 1 << v
        return [(b0, b0 + m) for b0 in range(m, ch, 2 * m)]

    def mix_rows(v, qs, kk):
        m = 1 << v
        if m >= SUBLANES:
            return jnp.concatenate(
                [(qs if (b0 // m) & 1 else kk)[b0:b0 + m] for b0 in range(0, ch, m)], axis=0)
        pick = ((lax.broadcasted_iota(jnp.int32, (1, SUBLANES, width), 1) >> v) & 1) == 1
        shape3 = (ch // SUBLANES, SUBLANES, width)
        return jnp.where(pick, qs.reshape(shape3), kk.reshape(shape3)).reshape(ch, width)

    staged = []
    for sub in range(nsub):
        rows = slice(sub * ch, (sub + 1) * ch)
        kk, qs = kk_all[rows], qs_all[rows]
        logf2 = jnp.concatenate([p[rows] for p in logf_pieces], axis=0)

        def decay(idx):
            return jnp.exp(_dot(mm_ref[idx], logf2))

        eg = decay(0)
        qg = (qs * eg).astype(BF16)
        dec = eg[ch - 1:ch, :]
        kh = (kk * decay(1)).astype(BF16)
        lev = [mix_rows(0, qs * f_all[rows], kk).astype(BF16)]
        lev += [(decay(1 + v) * mix_rows(v, qs, kk)).astype(BF16) for v in range(1, nlev)]
        qb, kb = qs.astype(BF16), kk.astype(BF16)
        atts = []
        for sl in pairs:
            att = jnp.where(r == s, pair_scores(qb[:, sl], kb[:, sl]), 0.0)
            for v in range(nlev):
                x = lev[v][:, sl]
                m = 1 << v
                if m >= BF16_ROWS:
                    runs = upper_runs(v)
                    p = pair_scores(jnp.concatenate([x[a:b] for a, b in runs], axis=0), x)
                    parts = []
                    for n in range(len(runs)):
                        parts += [jnp.zeros((m, 2 * ch), F32), p[n * m:(n + 1) * m]]
                    p = jnp.concatenate(parts, axis=0)
                else:
                    p = pair_scores(x, x)
                att = jnp.where((xr >> v) == 1, p, att)
            atts.append(att.astype(BF16))
        staged.append((rows, qg, kh, dec, atts))

    for rows, qg, kh, dec, atts in staged:
        ib = ib_all[rows]
        for j, sl in enumerate(pairs):
            st2 = block_diag(st_ref[2 * j].astype(BF16), st_ref[2 * j + 1].astype(BF16), zero_s)
            ib2 = block_diag(ib[:, sl][:, :dk], ib[:, sl][:, dk:], zero_k)
            o = _dot_nt(qg[:, sl], st2) + _dot(atts[j], ib2)
            for e in range(2):
                hs = slice((2 * j + e) * dk, (2 * j + e + 1) * dk)
                og = og_ref[rows, hs]
                o_ref[rows, hs] = _rms(o[:, e * dk:(e + 1) * dk], gn_ref[:, hs]) * _silu(og)
        for h in range(nh):
            hs = slice(h * dk, (h + 1) * dk)
            st_ref[h] = st_ref[h] * dec[:, hs] + _dot_tn(ib[:, hs], kh[:, hs])

    @pl.when(c == pl.num_programs(1) - 1)
    def _():
        sfin_ref[0] = st_ref[...]


def _hg_prompt(proj, lb, gn, *, batch, seq, rows_total, nh, col0, side=()):
    width = lb.shape[1]
    dk = width // nh
    ch = HG_CHUNK * HG_CHUNKS_PER_STEP
    nchunk = seq // ch
    cb = col0 // width
    assert cb * width == col0 and nchunk * ch == seq

    def tok(k):
        return pl.BlockSpec((ch, width), lambda b, c: (b * nchunk + c, cb + k))

    vec = pl.BlockSpec((1, width), lambda b, c: (0, 0))
    mm = _hg_span_matrices(HG_CHUNK)
    side_in, side_out, side_shapes, side_plan = _side_specs(
        side, batch * nchunk, lambda b, c: b * nchunk + c)
    return pl.pallas_call(
        functools.partial(_hg_prompt_body, dk=dk, side_plan=side_plan),
        out_shape=[jax.ShapeDtypeStruct((rows_total, width), F32),
                   jax.ShapeDtypeStruct((batch, nh, dk, dk), F32)] + side_shapes,
        grid=(batch, nchunk),
        in_specs=[tok(0), tok(1), tok(2), tok(3), vec, vec,
                  pl.BlockSpec(mm.shape, lambda b, c: (0, 0, 0))] + side_in,
        out_specs=[pl.BlockSpec((ch, width), lambda b, c: (b * nchunk + c, 0)),
                   pl.BlockSpec((1, nh, dk, dk), lambda b, c: (b, 0, 0, 0))] + side_out,
        scratch_shapes=[pltpu.VMEM((nh, dk, dk), F32)],
        compiler_params=_cparams("arbitrary" if side else "parallel", "arbitrary"),
        name="hgrn_prompt",
    )(proj, proj, proj, proj, lb, gn, mm, *side)


def _split3(x):
    p1 = x.astype(BF16)
    r1 = x - p1.astype(F32)
    p2 = r1.astype(BF16)
    p3 = (r1 - p2.astype(F32)).astype(BF16)
    return p1, p2, p3


def _hg_sample_body(q_ref, f_ref, i_ref, og_ref, lb_ref, gn_ref, s0_ref,
                    hg_ref, s_ref, fq_ref, *, dk):
    step = pl.program_id(0)
    nt = q_ref.shape[0]
    tb, nh = s0_ref.shape[0], s0_ref.shape[1]

    @pl.when(step == 0)
    def _():
        lb = lb_ref[...]
        f = lb + (1.0 - lb) * jax.nn.sigmoid(f_ref[...])
        qs = _silu(q_ref[...])
        for src, base in ((f, 0), (qs, nh)):
            for h in range(nh):
                t = src[:, h * dk:(h + 1) * dk].T
                for p, piece in enumerate(_split3(t)):
                    fq_ref[(base + h) * dk:(base + h + 1) * dk, p * nt:(p + 1) * nt] = piece

    tok = lax.broadcasted_iota(jnp.int32, (3 * nt, dk), 0)
    for j in range(tb):
        n = step * tb + j
        onehot = ((tok == n) | (tok == n + nt) | (tok == n + 2 * nt)).astype(BF16)
        fq = _dot(fq_ref[...], onehot)
        irow = i_ref[pl.ds(n, 1), :]
        ogrow = og_ref[pl.ds(n, 1), :]
        for h in range(nh):
            sl = slice(h * dk, (h + 1) * dk)
            fb = fq[h * dk:(h + 1) * dk, :]
            qb = fq[(nh + h) * dk:(nh + h + 1) * dk, :]
            s1 = fb * s0_ref[j, h] + (1.0 - fb) * irow[:, sl]
            s_ref[j, h] = s1
            o = jnp.sum(qb * s1, axis=0, keepdims=True)
            hg_ref[j:j + 1, sl] = _rms(o, gn_ref[:, sl]) * _silu(ogrow[:, sl])


def _hg_sample(proj, lb, gn, s0, *, row0, nh, col0):
    width = lb.shape[1]
    dk = width // nh
    ns = s0.shape[0]
    tb = SAMPLE_TOKENS_PER_STEP
    rb, cb = row0 // ns, col0 // width
    assert rb * ns == row0 and cb * width == col0 and ns % tb == 0

    def tok(k):
        return pl.BlockSpec((ns, width), lambda t: (rb, cb + k))

    vec = pl.BlockSpec((1, width), lambda t: (0, 0))
    sspec = pl.BlockSpec((tb, nh, dk, dk), lambda t: (t, 0, 0, 0))
    return pl.pallas_call(
        functools.partial(_hg_sample_body, dk=dk),
        out_shape=(jax.ShapeDtypeStruct((ns, width), F32),
                   jax.ShapeDtypeStruct(s0.shape, F32)),
        grid=(ns // tb,),
        in_specs=[tok(0), tok(1), tok(2), tok(3), vec, vec, sspec],
        out_specs=(pl.BlockSpec((tb, width), lambda t: (t, 0)), sspec),
        scratch_shapes=[pltpu.VMEM((2 * nh * dk, 3 * ns), BF16)],
        compiler_params=_cparams("arbitrary"),
        name="hgrn_sample",
    )(proj, proj, proj, proj, lb, gn, s0)


def _merge_body(z_ref, zs_ref, hg_ref, hgs_ref, gs0_ref, gs1_ref, gh0_ref, gh1_ref, x_ref,
                wglu_ref, bglu_ref, wbs_ref, wbh_ref, wout_ref, post_ref, o_ref, *, ns):
    i = pl.program_id(0)
    last_i = pl.num_programs(0) - 1
    cut = x_ref.shape[0] - ns
    half = gs0_ref.shape[1]

    def rows(z, hg, sl):
        s5o = z * jax.nn.sigmoid(_dot(z.astype(BF16), wglu_ref[...]) + bglu_ref[...])
        a = _dot(s5o.astype(BF16), wbs_ref[...])
        b = _dot(hg.astype(BF16), wbh_ref[...])
        m0 = (jax.nn.sigmoid(gs0_ref[sl, :]) * a[:, :half]
              + jax.nn.sigmoid(gh0_ref[sl, :]) * b[:, :half])
        m1 = (jax.nn.sigmoid(gs1_ref[sl, :]) * a[:, half:]
              + jax.nn.sigmoid(gh1_ref[sl, :]) * b[:, half:])
        mix = _dot(m0.astype(BF16), wout_ref[:half, :]) + _dot(m1.astype(BF16), wout_ref[half:, :])
        o_ref[sl, :] = x_ref[sl, :] + _rms(mix, post_ref[...])

    @pl.when(i < last_i)
    def _():
        rows(z_ref[...], hg_ref[...], slice(None))

    @pl.when(i == last_i)
    def _():
        rows(z_ref[:cut, :], hg_ref[:cut, :], slice(0, cut))
        rows(zs_ref[...], hgs_ref[...], slice(cut, None))


def _merge(z, zs, hg, hgs, proj, x, wglu, bglu, wbs, wbh, wout, post, *, col0, tm):
    m, d = x.shape
    w = z.shape[1]
    ns = zs.shape[0]
    cb = col0 // w
    assert cb * w == col0 and d == 2 * w and m % tm == 0 and 0 < ns < tm

    def gate(k):
        return pl.BlockSpec((tm, w), lambda i: (i, cb + k))

    def const(shape):
        return pl.BlockSpec(shape, lambda i: (0, 0), pipeline_mode=pl.Buffered(1))

    tok = pl.BlockSpec((tm, w), lambda i: (i, 0))
    return pl.pallas_call(
        functools.partial(_merge_body, ns=ns),
        out_shape=jax.ShapeDtypeStruct((m, d), F32),
        grid=(m // tm,),
        in_specs=[
            tok, const((ns, w)), tok, const((ns, w)),
            gate(0), gate(1), gate(2), gate(3),
            pl.BlockSpec((tm, d), lambda i: (i, 0)),
            const((w, w)), const((1, w)), const((w, d)), const((w, d)), const((d, d)), const((1, d)),
        ],
        out_specs=pl.BlockSpec((tm, d), lambda i: (i, 0)),
        compiler_params=_cparams("parallel"),
        name="merge",
    )(z, zs, hg, hgs, proj, proj, proj, proj, x, wglu, bglu, wbs, wbh, wout, post)


MERGE_ROW_TILE = 320


def kernel(x_prompt, x_sample, state_s5_re, state_s5_im, state_hgrn, ffn1_pre_norm, ffn1_w_gate, ffn1_w_up, ffn1_w_down, ffn1_post_norm, mix_pre_norm, w_in, s5_lambda_re, s5_lambda_im, s5_log_dt, s5_b_re, s5_b_im, s5_c_re, s5_c_im, s5_d, s5_w_glu, s5_b_glu, hgrn_lb_logits, hgrn_out_norm, w_branch_s5, w_branch_hgrn, w_out, mix_post_norm, ffn2_pre_norm, ffn2_w_gate, ffn2_w_up, ffn2_w_down, ffn2_post_norm):
    depth = ffn1_w_gate.shape[0]
    assert depth == 1
    batch, seq, d = x_prompt.shape
    ns = x_sample.shape[0]
    assert x_sample.shape[1] == 1
    g, p = s5_lambda_re.shape[1:]
    nh, dk = state_hgrn.shape[2], state_hgrn.shape[3]
    s5w = s5_d.shape[1]
    hgw = nh * dk
    mp = batch * seq
    m = mp + ns

    bf = lambda a: a[0].astype(BF16)
    row = lambda a: a[0].reshape(1, -1).astype(F32)

    lb_all = jnp.cumsum(jax.nn.softmax(hgrn_lb_logits.astype(F32), axis=0), axis=0)
    lb = lb_all[0].reshape(1, hgw)

    x, w_in_b = _ffn(
        x_prompt.reshape(mp, d), x_sample.reshape(ns, d), row(ffn1_pre_norm),
        bf(ffn1_w_gate), bf(ffn1_w_up), bf(ffn1_w_down), row(ffn1_post_norm),
        ns=ns, split_out=False, side=(w_in[0],))
    proj = _inproj(x, row(mix_pre_norm), w_in_b)

    tw_in, tw_out, a_re, a_im = _s5_tables(
        s5_lambda_re[0], s5_lambda_im[0], s5_log_dt[0], s5_b_re[0], s5_b_im[0],
        s5_c_re[0], s5_c_im[0])
    d_skip = row(s5_d)
    z, hlast, wg2_b, wu2_b, wglu_b, wbs_b = _s5_prompt(
        proj, tw_in.astype(BF16), tw_out.astype(BF16), a_re, a_im, d_skip,
        batch=batch, seq=seq, rows_total=mp,
        side=(ffn2_w_gate[0], ffn2_w_up[0], s5_w_glu[0], w_branch_s5[0]))
    zs, s_re, s_im = _s5_sample(proj, state_s5_re[0].reshape(ns, g * p),
                                state_s5_im[0].reshape(ns, g * p),
                                tw_in, tw_out, a_re, a_im, d_skip, row0=mp)
    half = hlast.shape[-1] // 2
    p_re = hlast[:, :, SUBLANES - 1, :half].reshape(1, batch, g, p)
    p_im = hlast[:, :, SUBLANES - 1, half:].reshape(1, batch, g, p)

    gn = row(hgrn_out_norm)
    hg, st_p, wd2_b, wbh_b, wout_b = _hg_prompt(
        proj, lb, gn, batch=batch, seq=seq, rows_total=mp, nh=nh, col0=s5w,
        side=(ffn2_w_down[0], w_branch_hgrn[0], w_out[0]))
    hgs, st_s = _hg_sample(proj, lb, gn, state_hgrn[0], row0=mp, nh=nh, col0=s5w)

    x = _merge(z, zs, hg, hgs, proj, x, wglu_b, row(s5_b_glu), wbs_b, wbh_b, wout_b,
               row(mix_post_norm), col0=s5w + 4 * hgw, tm=MERGE_ROW_TILE)
    yp, ys = _ffn(x, None, row(ffn2_pre_norm), wg2_b, wu2_b, wd2_b, row(ffn2_post_norm),
                  ns=ns, split_out=True)

    return (yp.reshape(batch, seq, d), ys.reshape(ns, 1, d),
            p_re, p_im, jnp.swapaxes(st_p, -1, -2)[None],
            s_re.reshape(1, ns, g, p), s_im.reshape(1, ns, g, p), st_s[None])
```

```python
import functools

import jax
import jax.numpy as jnp
import numpy as np
from jax import lax
from jax.experimental import pallas as pl
from jax.experimental.pallas import tpu as pltpu

F32 = jnp.float32
BF16 = jnp.bfloat16
EPS = 1e-6
HIGHEST = lax.Precision.HIGHEST

LANES = 128
SUBLANES = 8
VMEM_LIMIT = 60 * 1024 * 1024

ROW_TILE = 832
FF_TILE = 512
IN_TILE = 1024
IN_ROW_TILE = 1040
S5_SCAN_UNROLL = 8
HG_CHUNK = 64
HG_CHUNKS_PER_STEP = 2
SAMPLE_TOKENS_PER_STEP = 8


def _cparams(*sem):
    return pltpu.CompilerParams(dimension_semantics=sem, vmem_limit_bytes=VMEM_LIMIT)


def _rms(x, g):
    return x * lax.rsqrt(jnp.mean(x * x, axis=-1, keepdims=True) + EPS) * g


def _silu(x):
    return x * jax.nn.sigmoid(x)


def _dot(a, b):
    return jnp.dot(a, b, preferred_element_type=F32)


def _dot_hi(a, b):
    return jnp.dot(a, b, preferred_element_type=F32, precision=HIGHEST)


def _dot_nt(a, b):
    return lax.dot_general(a, b, (((1,), (1,)), ((), ())), preferred_element_type=F32)


def _dot_tn(a, b):
    return lax.dot_general(a, b, (((0,), (0,)), ((), ())), preferred_element_type=F32)


BF16_ROWS = 16


def _side_specs(arrays, nsteps, step_of):
    in_specs, out_specs, out_shapes, plan = [], [], [], []
    for a in arrays:
        r, c = a.shape
        rows = next(t for t in range(BF16_ROWS, r + 1, BF16_ROWS) if r % t == 0 and r // t <= nsteps)
        nblk = r // rows
        every = nsteps // nblk

        def imap(*g, nblk=nblk, every=every):
            return (jnp.minimum(step_of(*g) // every, nblk - 1), 0)

        in_specs.append(pl.BlockSpec((rows, c), imap))
        out_specs.append(pl.BlockSpec((rows, c), imap))
        out_shapes.append(jax.ShapeDtypeStruct((r, c), BF16))
        plan.append((nblk, every))
    return in_specs, out_specs, out_shapes, tuple(plan)


def _side_cast(src_refs, dst_refs):
    for src, dst in zip(src_refs, dst_refs):
        dst[...] = src[...].astype(BF16)


def _ffn_body(*refs, ns, split_in, split_out, side_plan):
    refs = list(refs)
    nside = len(side_plan)
    x_ref = refs.pop(0)
    xs_ref = refs.pop(0) if split_in else None
    pre_ref, wg_ref, wu_ref, wd_ref, post_ref = refs[:5]
    side_src = refs[5:5 + nside]
    o_ref = refs[5 + nside]
    os_ref = refs[6 + nside] if split_out else None
    side_dst = refs[-2 - nside:-2]
    h_ref, acc_ref = refs[-2:]
    i, k = pl.program_id(0), pl.program_id(1)
    last_i = pl.num_programs(0) - 1
    last_k = pl.num_programs(1) - 1
    cut = h_ref.shape[0] - ns

    def swiglu_down(h):
        _side_cast(side_src, side_dst)
        g = _dot(h, wg_ref[...])
        u = _dot(h, wu_ref[...])
        return _dot((_silu(g) * u).astype(BF16), wd_ref[...])

    @pl.when((i < last_i) & (k == 0))
    def _():
        h = _rms(x_ref[...], pre_ref[...]).astype(BF16)
        h_ref[...] = h
        acc_ref[...] = swiglu_down(h)

    @pl.when((i < last_i) & (k == last_k))
    def _():
        acc = acc_ref[...] + swiglu_down(h_ref[...])
        o_ref[...] = x_ref[...] + 0.5 * _rms(acc, post_ref[...])

    @pl.when((i == last_i) & (k == 0))
    def _():
        h_ref[:cut, :] = _rms(x_ref[:cut, :], pre_ref[...]).astype(BF16)
        xt = xs_ref[...] if split_in else x_ref[cut:, :]
        h_ref[cut:, :] = _rms(xt, pre_ref[...]).astype(BF16)
        acc_ref[...] = jnp.zeros_like(acc_ref)

    @pl.when(((k > 0) & (k < last_k)) | (i == last_i))
    def _():
        acc_ref[...] += swiglu_down(h_ref[...])

    @pl.when((i == last_i) & (k == last_k))
    def _():
        o_ref[:cut, :] = x_ref[:cut, :] + 0.5 * _rms(acc_ref[:cut, :], post_ref[...])
        xt = xs_ref[...] if split_in else x_ref[cut:, :]
        tail = xt + 0.5 * _rms(acc_ref[cut:, :], post_ref[...])
        if split_out:
            os_ref[...] = tail
        else:
            o_ref[cut:, :] = tail


def _ffn(x, xs, pre, wg, wu, wd, post, *, ns, split_out, side=()):
    split_in = xs is not None
    d = x.shape[1]
    m = x.shape[0] + (ns if split_in else 0)
    dff = wg.shape[1]
    tm, tf = ROW_TILE, FF_TILE
    assert m % tm == 0 and 0 < ns < tm and ns % SUBLANES == 0
    nk = dff // tf
    side_in, side_out, side_shapes, side_plan = _side_specs(
        side, (m // tm) * nk, lambda i, k: i * nk + k)
    tok = pl.BlockSpec((tm, d), lambda i, k: (i, 0))
    smp = pl.BlockSpec((ns, d), lambda i, k: (0, 0))
    vec = pl.BlockSpec((1, d), lambda i, k: (0, 0))
    in_specs = [tok] + ([smp] if split_in else []) + [
        vec,
        pl.BlockSpec((d, tf), lambda i, k: (0, k)),
        pl.BlockSpec((d, tf), lambda i, k: (0, k)),
        pl.BlockSpec((tf, d), lambda i, k: (k, 0)),
        vec,
    ] + side_in
    if split_out:
        out_shape = [jax.ShapeDtypeStruct((m - ns, d), F32), jax.ShapeDtypeStruct((ns, d), F32)]
        out_specs = [tok, smp]
    else:
        out_shape = [jax.ShapeDtypeStruct((m, d), F32)]
        out_specs = [tok]
    args = [x] + ([xs] if split_in else []) + [pre, wg, wu, wd, post] + list(side)
    sequential_rows = split_out or bool(side)
    return pl.pallas_call(
        functools.partial(_ffn_body, ns=ns, split_in=split_in, split_out=split_out,
                          side_plan=side_plan),
        out_shape=out_shape + side_shapes,
        grid=(m // tm, nk),
        in_specs=in_specs,
        out_specs=out_specs + side_out,
        scratch_shapes=[pltpu.VMEM((tm, d), BF16), pltpu.VMEM((tm, d), F32)],
        compiler_params=_cparams("arbitrary" if sequential_rows else "parallel", "arbitrary"),
        name="ffn",
    )(*args)


def _inproj_body(x_ref, g_ref, w_ref, o_ref, h_ref):
    j = pl.program_id(1)

    @pl.when(j == 0)
    def _():
        h = _rms(x_ref[...], g_ref[...]).astype(BF16)
        h_ref[...] = h
        o_ref[...] = _dot(h, w_ref[...])

    @pl.when(j > 0)
    def _():
        o_ref[...] = _dot(h_ref[...], w_ref[...])


def _inproj(x, g, w):
    m, d = x.shape
    n = w.shape[1]
    tm, tn = IN_ROW_TILE, IN_TILE
    assert m % tm == 0
    return pl.pallas_call(
        _inproj_body,
        out_shape=jax.ShapeDtypeStruct((m, n), F32),
        grid=(m // tm, n // tn),
        in_specs=[
            pl.BlockSpec((tm, d), lambda i, j: (i, 0)),
            pl.BlockSpec((1, d), lambda i, j: (0, 0)),
            pl.BlockSpec((d, tn), lambda i, j: (0, j)),
        ],
        out_specs=pl.BlockSpec((tm, tn), lambda i, j: (i, j)),
        scratch_shapes=[pltpu.VMEM((tm, d), BF16)],
        compiler_params=_cparams("parallel", "arbitrary"),
        name="inproj",
    )(x, g, w)


def _s5_tables(lam_re, lam_im, log_dt, b_re, b_im, c_re, c_im):
    g, p = lam_re.shape
    n = b_re.shape[-1]
    gpt = LANES // n
    nv = g // gpt
    dt = jnp.exp(log_dt)[:, None]
    er = jnp.exp(lam_re * dt)
    th = lam_im * dt
    a_re, a_im = er * jnp.cos(th), er * jnp.sin(th)
    den = lam_re * lam_re + lam_im * lam_im
    k_re = ((a_re - 1.0) * lam_re + a_im * lam_im) / den
    k_im = (a_im * lam_re - (a_re - 1.0) * lam_im) / den
    bb_re = k_re[..., None] * b_re - k_im[..., None] * b_im
    bb_im = k_re[..., None] * b_im + k_im[..., None] * b_re
    eye = jnp.eye(gpt, dtype=F32)

    def in_blk(bb):
        t = bb.reshape(nv, gpt, p, n)
        return jnp.einsum("ab,vapn->vanbp", eye, t).reshape(nv, gpt * n, gpt * p)

    def out_blk(cc):
        t = cc.reshape(nv, gpt, n, p)
        return jnp.einsum("ab,vanp->vapbn", eye, t).reshape(nv, gpt * p, gpt * n)

    w_in = jnp.concatenate([in_blk(bb_re), in_blk(bb_im)], axis=-1)
    w_out = jnp.concatenate([out_blk(c_re), -out_blk(c_im)], axis=1)
    return w_in, w_out, a_re.reshape(nv, 1, gpt * p), a_im.reshape(nv, 1, gpt * p)


def _cmul(ar, ai, br, bi):
    return ar * br - ai * bi, ar * bi + ai * br


def _s5_prompt_body(*refs, seg, side_plan):
    nside = len(side_plan)
    u_ref, w_ref, cm_ref, ar_ref, ai_ref, d_ref = refs[:6]
    z_ref, hl_ref = refs[6 + nside:8 + nside]
    buh_ref = refs[-1]
    _side_cast(refs[6:6 + nside], refs[8 + nside:-1])
    nseg = SUBLANES
    nc = w_ref.shape[2] // LANES
    hc = nc // 2

    def seg_rows(j):
        return pl.ds(j, seg, stride=nseg)

    for j in range(nseg):
        bu = _dot(u_ref[j * seg:(j + 1) * seg, :].astype(BF16), w_ref[0])
        for c in range(nc):
            buh_ref[c, seg_rows(j), :] = bu[:, c * LANES:(c + 1) * LANES]

    ar = [jnp.broadcast_to(ar_ref[0, :, c * LANES:(c + 1) * LANES], (nseg, LANES)) for c in range(hc)]
    ai = [jnp.broadcast_to(ai_ref[0, :, c * LANES:(c + 1) * LANES], (nseg, LANES)) for c in range(hc)]

    def rows(t):
        return pl.ds(pl.multiple_of(t * nseg, nseg), nseg)

    def advance(t, hs):
        out = []
        for c in range(hc):
            pr, pi = _cmul(ar[c], ai[c], hs[2 * c], hs[2 * c + 1])
            out += [pr + buh_ref[c, rows(t), :], pi + buh_ref[hc + c, rows(t), :]]
        return tuple(out)

    zeros = tuple(jnp.zeros((nseg, LANES), F32) for _ in range(2 * hc))
    ends = lax.fori_loop(0, seg, advance, zeros, unroll=S5_SCAN_UNROLL)

    pw = [(ar[c], ai[c]) for c in range(hc)]
    for _ in range(seg.bit_length() - 1):
        pw = [_cmul(r, i, r, i) for r, i in pw]

    row = lax.broadcasted_iota(jnp.int32, (nseg, LANES), 0)
    init = list(zeros)
    for j in range(1, nseg):
        for c in range(hc):
            pr, pi = _cmul(pw[c][0], pw[c][1], init[2 * c], init[2 * c + 1])
            nr = pltpu.roll(pr + ends[2 * c], 1, 0)
            ni = pltpu.roll(pi + ends[2 * c + 1], 1, 0)
            init[2 * c] = jnp.where(row == j, nr, init[2 * c])
            init[2 * c + 1] = jnp.where(row == j, ni, init[2 * c + 1])

    def advance_store(t, hs):
        new = advance(t, hs)
        for c in range(hc):
            buh_ref[c, rows(t), :] = new[2 * c]
            buh_ref[hc + c, rows(t), :] = new[2 * c + 1]
        return new

    last = lax.fori_loop(0, seg, advance_store, tuple(init), unroll=S5_SCAN_UNROLL)
    for c in range(hc):
        hl_ref[0, 0, :, c * LANES:(c + 1) * LANES] = last[2 * c]
        hl_ref[0, 0, :, (hc + c) * LANES:(hc + c + 1) * LANES] = last[2 * c + 1]

    for j in range(nseg):
        y = d_ref[...] * u_ref[j * seg:(j + 1) * seg, :]
        for c in range(nc):
            y = y + _dot(buh_ref[c, seg_rows(j), :].astype(BF16),
                         cm_ref[0, c * LANES:(c + 1) * LANES, :])
        z_ref[j * seg:(j + 1) * seg, :] = jax.nn.gelu(y)


def _s5_prompt(proj, w_in, w_out, a_re, a_im, d_skip, *, batch, seq, rows_total, side=()):
    nv = w_in.shape[0]
    sw = w_in.shape[2]
    seg = seq // SUBLANES
    assert seg * SUBLANES == seq and seg & (seg - 1) == 0
    side_in, side_out, side_shapes, side_plan = _side_specs(
        side, batch * nv, lambda b, v: b * nv + v)
    body = functools.partial(_s5_prompt_body, seg=seg, side_plan=side_plan)
    return pl.pallas_call(
        body,
        out_shape=[jax.ShapeDtypeStruct((rows_total, nv * LANES), F32),
                   jax.ShapeDtypeStruct((batch, nv, SUBLANES, sw), F32)] + side_shapes,
        grid=(batch, nv),
        in_specs=[
            pl.BlockSpec((seq, LANES), lambda b, v: (b, v)),
            pl.BlockSpec((1, LANES, sw), lambda b, v: (v, 0, 0)),
            pl.BlockSpec((1, sw, LANES), lambda b, v: (v, 0, 0)),
            pl.BlockSpec((1, 1, sw // 2), lambda b, v: (v, 0, 0)),
            pl.BlockSpec((1, 1, sw // 2), lambda b, v: (v, 0, 0)),
            pl.BlockSpec((1, LANES), lambda b, v: (0, v)),
        ] + side_in,
        out_specs=[pl.BlockSpec((seq, LANES), lambda b, v: (b, v)),
                   pl.BlockSpec((1, 1, SUBLANES, sw), lambda b, v: (b, v, 0, 0))] + side_out,
        scratch_shapes=[pltpu.VMEM((sw // LANES, seq, LANES), F32)],
        compiler_params=_cparams(*(("arbitrary",) * 2 if side else ("parallel",) * 2)),
        name="s5_prompt",
    )(proj, w_in, w_out, a_re, a_im, d_skip, *side)


def _s5_sample_body(u_ref, hre_ref, him_ref, w_ref, cm_ref, ar_ref, ai_ref, d_ref,
                    z_ref, ore_ref, oim_ref):
    half = ar_ref.shape[2]
    u = u_ref[...]
    bu = _dot_hi(u, w_ref[0])
    pr, pi = _cmul(ar_ref[0], ai_ref[0], hre_ref[...], him_ref[...])
    hr = pr + bu[:, :half]
    hi = pi + bu[:, half:]
    ore_ref[...] = hr
    oim_ref[...] = hi
    y = d_ref[...] * u + _dot_hi(hr, cm_ref[0, :half, :]) + _dot_hi(hi, cm_ref[0, half:, :])
    z_ref[...] = jax.nn.gelu(y)


def _s5_sample(proj, h_re, h_im, w_in, w_out, a_re, a_im, d_skip, *, row0):
    nv = w_in.shape[0]
    sw = w_in.shape[2]
    ns = h_re.shape[0]
    rb = row0 // ns
    assert rb * ns == row0
    st = pl.BlockSpec((ns, sw // 2), lambda v: (0, v))
    return pl.pallas_call(
        _s5_sample_body,
        out_shape=(jax.ShapeDtypeStruct((ns, nv * LANES), F32),
                   jax.ShapeDtypeStruct(h_re.shape, F32),
                   jax.ShapeDtypeStruct(h_im.shape, F32)),
        grid=(nv,),
        in_specs=[
            pl.BlockSpec((ns, LANES), lambda v: (rb, v)),
            st, st,
            pl.BlockSpec((1, LANES, sw), lambda v: (v, 0, 0)),
            pl.BlockSpec((1, sw, LANES), lambda v: (v, 0, 0)),
            pl.BlockSpec((1, 1, sw // 2), lambda v: (v, 0, 0)),
            pl.BlockSpec((1, 1, sw // 2), lambda v: (v, 0, 0)),
            pl.BlockSpec((1, LANES), lambda v: (0, v)),
        ],
        out_specs=(pl.BlockSpec((ns, LANES), lambda v: (0, v)), st, st),
        compiler_params=_cparams("parallel"),
        name="s5_sample",
    )(proj, h_re, h_im, w_in, w_out, a_re, a_im, d_skip)


def _hg_span_matrices(ch):
    t = np.arange(ch)[:, None]
    r = np.arange(ch)[None, :]
    mats = [r <= t, r > t]
    for v in range(1, ch.bit_length() - 1):
        base = (t >> v) << v
        upper = ((t >> v) & 1) == 1
        mats.append(np.where(upper, (r >= base) & (r <= t), (r > t) & (r < base + (1 << v))))
    mm = np.stack(mats).astype(np.float32)
    return jnp.asarray(np.concatenate([mm, mm], axis=-1), BF16)


def _hg_prompt_body(*refs, dk, side_plan):
    nside = len(side_plan)
    q_ref, f_ref, i_ref, og_ref, lb_ref, gn_ref, mm_ref = refs[:7]
    o_ref, sfin_ref = refs[7 + nside:9 + nside]
    st_ref = refs[-1]
    _side_cast(refs[7:7 + nside], refs[9 + nside:-1])
    c = pl.program_id(1)
    nh = st_ref.shape[0]
    ch = mm_ref.shape[1]
    nsub = q_ref.shape[0] // ch
    width = q_ref.shape[1]
    nlev = ch.bit_length() - 1

    @pl.when(c == 0)
    def _():
        st_ref[...] = jnp.zeros_like(st_ref)

    lb = lb_ref[...]
    f_all = lb + (1.0 - lb) * jax.nn.sigmoid(f_ref[...])
    kk_all = 1.0 - f_all
    qs_all = _silu(q_ref[...])
    logf_pieces = _split3(jnp.log(f_all))[:2]
    ib_all = i_ref[...].astype(BF16)
    pairs = [slice(2 * j * dk, 2 * (j + 1) * dk) for j in range(nh // 2)]
    r = lax.broadcasted_iota(jnp.int32, (ch, 2 * ch), 0)
    s = lax.broadcasted_iota(jnp.int32, (ch, 2 * ch), 1) & (ch - 1)
    xr = jnp.where(r > s, r ^ s, 0)
    zero_k = jnp.zeros((ch, dk), BF16)
    zero_s = jnp.zeros((dk, dk), BF16)

    def block_diag(a, b, z):
        return jnp.concatenate([jnp.concatenate([a, z], axis=1),
                                jnp.concatenate([z, b], axis=1)], axis=0)

    def pair_scores(lhs, x):
        return _dot_nt(lhs, block_diag(x[:, :dk], x[:, dk:], zero_k))

    def upper_runs(v):
        m = 1 << v
        return [(b0, b0 + m) for b0 in range(m, ch, 2 * m)]

    def mix_rows(v, qs, kk):
        m = 1 << v
        if m >= SUBLANES:
            return jnp.concatenate(
                [(qs if (b0 // m) & 1 else kk)[b0:b0 + m] for b0 in range(0, ch, m)], axis=0)
        pick = ((lax.broadcasted_iota(jnp.int32, (1, SUBLANES, width), 1) >> v) & 1) == 1
        shape3 = (ch // SUBLANES, SUBLANES, width)
        return jnp.where(pick, qs.reshape(shape3), kk.reshape(shape3)).reshape(ch, width)

    staged = []
    for sub in range(nsub):
        rows = slice(sub * ch, (sub + 1) * ch)
        kk, qs = kk_all[rows], qs_all[rows]
        logf2 = jnp.concatenate([p[rows] for p in logf_pieces], axis=0)

        def decay(idx):
            return jnp.exp(_dot(mm_ref[idx], logf2))

        eg = decay(0)
        qg = (qs * eg).astype(BF16)
        dec = eg[ch - 1:ch, :]
        kh = (kk * decay(1)).astype(BF16)
        lev = [mix_rows(0, qs * f_all[rows], kk).astype(BF16)]
        lev += [(decay(1 + v) * mix_rows(v, qs, kk)).astype(BF16) for v in range(1, nlev)]
        qb, kb = qs.astype(BF16), kk.astype(BF16)
        atts = []
        for sl in pairs:
            att = jnp.where(r == s, pair_scores(qb[:, sl], kb[:, sl]), 0.0)
            for v in range(nlev):
                x = lev[v][:, sl]
                m = 1 << v
                if m >= BF16_ROWS:
                    runs = upper_runs(v)
                    p = pair_scores(jnp.concatenate([x[a:b] for a, b in runs], axis=0), x)
                    parts = []
                    for n in range(len(runs)):
                        parts += [jnp.zeros((m, 2 * ch), F32), p[n * m:(n + 1) * m]]
                    p = jnp.concatenate(parts, axis=0)
                else:
                    p = pair_scores(x, x)
                att = jnp.where((xr >> v) == 1, p, att)
            atts.append(att.astype(BF16))
        staged.append((rows, qg, kh, dec, atts))

    for rows, qg, kh, dec, atts in staged:
        ib = ib_all[rows]
        for j, sl in enumerate(pairs):
            st2 = block_diag(st_ref[2 * j].astype(BF16), st_ref[2 * j + 1].astype(BF16), zero_s)
            ib2 = block_diag(ib[:, sl][:, :dk], ib[:, sl][:, dk:], zero_k)
            o = _dot_nt(qg[:, sl], st2) + _dot(atts[j], ib2)
            for e in range(2):
                hs = slice((2 * j + e) * dk, (2 * j + e + 1) * dk)
                og = og_ref[rows, hs]
                o_ref[rows, hs] = _rms(o[:, e * dk:(e + 1) * dk], gn_ref[:, hs]) * _silu(og)
        for h in range(nh):
            hs = slice(h * dk, (h + 1) * dk)
            st_ref[h] = st_ref[h] * dec[:, hs] + _dot_tn(ib[:, hs], kh[:, hs])

    @pl.when(c == pl.num_programs(1) - 1)
    def _():
        sfin_ref[0] = st_ref[...]


def _hg_prompt(proj, lb, gn, *, batch, seq, rows_total, nh, col0, side=()):
    width = lb.shape[1]
    dk = width // nh
    ch = HG_CHUNK * HG_CHUNKS_PER_STEP
    nchunk = seq // ch
    cb = col0 // width
    assert cb * width == col0 and nchunk * ch == seq

    def tok(k):
        return pl.BlockSpec((ch, width), lambda b, c: (b * nchunk + c, cb + k))

    vec = pl.BlockSpec((1, width), lambda b, c: (0, 0))
    mm = _hg_span_matrices(HG_CHUNK)
    side_in, side_out, side_shapes, side_plan = _side_specs(
        side, batch * nchunk, lambda b, c: b * nchunk + c)
    return pl.pallas_call(
        functools.partial(_hg_prompt_body, dk=dk, side_plan=side_plan),
        out_shape=[jax.ShapeDtypeStruct((rows_total, width), F32),
                   jax.ShapeDtypeStruct((batch, nh, dk, dk), F32)] + side_shapes,
        grid=(batch, nchunk),
        in_specs=[tok(0), tok(1), tok(2), tok(3), vec, vec,
                  pl.BlockSpec(mm.shape, lambda b, c: (0, 0, 0))] + side_in,
        out_specs=[pl.BlockSpec((ch, width), lambda b, c: (b * nchunk + c, 0)),
                   pl.BlockSpec((1, nh, dk, dk), lambda b, c: (b, 0, 0, 0))] + side_out,
        scratch_shapes=[pltpu.VMEM((nh, dk, dk), F32)],
        compiler_params=_cparams("arbitrary" if side else "parallel", "arbitrary"),
        name="hgrn_prompt",
    )(proj, proj, proj, proj, lb, gn, mm, *side)


def _split3(x):
    p1 = x.astype(BF16)
    r1 = x - p1.astype(F32)
    p2 = r1.astype(BF16)
    p3 = (r1 - p2.astype(F32)).astype(BF16)
    return p1, p2, p3


def _hg_sample_body(q_ref, f_ref, i_ref, og_ref, lb_ref, gn_ref, s0_ref,
                    hg_ref, s_ref, fq_ref, *, dk):
    step = pl.program_id(0)
    nt = q_ref.shape[0]
    tb, nh = s0_ref.shape[0], s0_ref.shape[1]

    @pl.when(step == 0)
    def _():
        lb = lb_ref[...]
        f = lb + (1.0 - lb) * jax.nn.sigmoid(f_ref[...])
        qs = _silu(q_ref[...])
        for src, base in ((f, 0), (qs, nh)):
            for h in range(nh):
                t = src[:, h * dk:(h + 1) * dk].T
                for p, piece in enumerate(_split3(t)):
                    fq_ref[(base + h) * dk:(base + h + 1) * dk, p * nt:(p + 1) * nt] = piece

    tok = lax.broadcasted_iota(jnp.int32, (3 * nt, dk), 0)
    for j in range(tb):
        n = step * tb + j
        onehot = ((tok == n) | (tok == n + nt) | (tok == n + 2 * nt)).astype(BF16)
        fq = _dot(fq_ref[...], onehot)
        irow = i_ref[pl.ds(n, 1), :]
        ogrow = og_ref[pl.ds(n, 1), :]
        for h in range(nh):
            sl = slice(h * dk, (h + 1) * dk)
            fb = fq[h * dk:(h + 1) * dk, :]
            qb = fq[(nh + h) * dk:(nh + h + 1) * dk, :]
            s1 = fb * s0_ref[j, h] + (1.0 - fb) * irow[:, sl]
            s_ref[j, h] = s1
            o = jnp.sum(qb * s1, axis=0, keepdims=True)
            hg_ref[j:j + 1, sl] = _rms(o, gn_ref[:, sl]) * _silu(ogrow[:, sl])


def _hg_sample(proj, lb, gn, s0, *, row0, nh, col0):
    width = lb.shape[1]
    dk = width // nh
    ns = s0.shape[0]
    tb = SAMPLE_TOKENS_PER_STEP
    rb, cb = row0 // ns, col0 // width
    assert rb * ns == row0 and cb * width == col0 and ns % tb == 0

    def tok(k):
        return pl.BlockSpec((ns, width), lambda t: (rb, cb + k))

    vec = pl.BlockSpec((1, width), lambda t: (0, 0))
    sspec = pl.BlockSpec((tb, nh, dk, dk), lambda t: (t, 0, 0, 0))
    return pl.pallas_call(
        functools.partial(_hg_sample_body, dk=dk),
        out_shape=(jax.ShapeDtypeStruct((ns, width), F32),
                   jax.ShapeDtypeStruct(s0.shape, F32)),
        grid=(ns // tb,),
        in_specs=[tok(0), tok(1), tok(2), tok(3), vec, vec, sspec],
        out_specs=(pl.BlockSpec((tb, width), lambda t: (t, 0)), sspec),
        scratch_shapes=[pltpu.VMEM((2 * nh * dk, 3 * ns), BF16)],
        compiler_params=_cparams("arbitrary"),
        name="hgrn_sample",
    )(proj, proj, proj, proj, lb, gn, s0)


def _merge_body(z_ref, zs_ref, hg_ref, hgs_ref, gs0_ref, gs1_ref, gh0_ref, gh1_ref, x_ref,
                wglu_ref, bglu_ref, wbs_ref, wbh_ref, wout_ref, post_ref, o_ref, *, ns):
    i = pl.program_id(0)
    last_i = pl.num_programs(0) - 1
    cut = x_ref.shape[0] - ns
    half = gs0_ref.shape[1]

    def rows(z, hg, sl):
        s5o = z * jax.nn.sigmoid(_dot(z.astype(BF16), wglu_ref[...]) + bglu_ref[...])
        a = _dot(s5o.astype(BF16), wbs_ref[...])
        b = _dot(hg.astype(BF16), wbh_ref[...])
        m0 = (jax.nn.sigmoid(gs0_ref[sl, :]) * a[:, :half]
              + jax.nn.sigmoid(gh0_ref[sl, :]) * b[:, :half])
        m1 = (jax.nn.sigmoid(gs1_ref[sl, :]) * a[:, half:]
              + jax.nn.sigmoid(gh1_ref[sl, :]) * b[:, half:])
        mix = _dot(m0.astype(BF16), wout_ref[:half, :]) + _dot(m1.astype(BF16), wout_ref[half:, :])
        o_ref[sl, :] = x_ref[sl, :] + _rms(mix, post_ref[...])

    @pl.when(i < last_i)
    def _():
        rows(z_ref[...], hg_ref[...], slice(None))

    @pl.when(i == last_i)
    def _():
        rows(z_ref[:cut, :], hg_ref[:cut, :], slice(0, cut))
        rows(zs_ref[...], hgs_ref[...], slice(cut, None))


def _merge(z, zs, hg, hgs, proj, x, wglu, bglu, wbs, wbh, wout, post, *, col0, tm):
    m, d = x.shape
    w = z.shape[1]
    ns = zs.shape[0]
    cb = col0 // w
    assert cb * w == col0 and d == 2 * w and m % tm == 0 and 0 < ns < tm

    def gate(k):
        return pl.BlockSpec((tm, w), lambda i: (i, cb + k))

    def const(shape):
        return pl.BlockSpec(shape, lambda i: (0, 0), pipeline_mode=pl.Buffered(1))

    tok = pl.BlockSpec((tm, w), lambda i: (i, 0))
    return pl.pallas_call(
        functools.partial(_merge_body, ns=ns),
        out_shape=jax.ShapeDtypeStruct((m, d), F32),
        grid=(m // tm,),
        in_specs=[
            tok, const((ns, w)), tok, const((ns, w)),
            gate(0), gate(1), gate(2), gate(3),
            pl.BlockSpec((tm, d), lambda i: (i, 0)),
            const((w, w)), const((1, w)), const((w, d)), const((w, d)), const((d, d)), const((1, d)),
        ],
        out_specs=pl.BlockSpec((tm, d), lambda i: (i, 0)),
        compiler_params=_cparams("parallel"),
        name="merge",
    )(z, zs, hg, hgs, proj, proj, proj, proj, x, wglu, bglu, wbs, wbh, wout, post)


MERGE_ROW_TILE = 320


def kernel(x_prompt, x_sample, state_s5_re, state_s5_im, state_hgrn, ffn1_pre_norm, ffn1_w_gate, ffn1_w_up, ffn1_w_down, ffn1_post_norm, mix_pre_norm, w_in, s5_lambda_re, s5_lambda_im, s5_log_dt, s5_b_re, s5_b_im, s5_c_re, s5_c_im, s5_d, s5_w_glu, s5_b_glu, hgrn_lb_logits, hgrn_out_norm, w_branch_s5, w_branch_hgrn, w_out, mix_post_norm, ffn2_pre_norm, ffn2_w_gate, ffn2_w_up, ffn2_w_down, ffn2_post_norm):
    depth = ffn1_w_gate.shape[0]
    assert depth == 1
    batch, seq, d = x_prompt.shape
    ns = x_sample.shape[0]
    assert x_sample.shape[1] == 1
    g, p = s5_lambda_re.shape[1:]
    nh, dk = state_hgrn.shape[2], state_hgrn.shape[3]
    s5w = s5_d.shape[1]
    hgw = nh * dk
    mp = batch * seq
    m = mp + ns

    bf = lambda a: a[0].astype(BF16)
    row = lambda a: a[0].reshape(1, -1).astype(F32)

    lb_all = jnp.cumsum(jax.nn.softmax(hgrn_lb_logits.astype(F32), axis=0), axis=0)
    lb = lb_all[0].reshape(1, hgw)

    x, w_in_b = _ffn(
        x_prompt.reshape(mp, d), x_sample.reshape(ns, d), row(ffn1_pre_norm),
        bf(ffn1_w_gate), bf(ffn1_w_up), bf(ffn1_w_down), row(ffn1_post_norm),
        ns=ns, split_out=False, side=(w_in[0],))
    proj = _inproj(x, row(mix_pre_norm), w_in_b)

    tw_in, tw_out, a_re, a_im = _s5_tables(
        s5_lambda_re[0], s5_lambda_im[0], s5_log_dt[0], s5_b_re[0], s5_b_im[0],
        s5_c_re[0], s5_c_im[0])
    d_skip = row(s5_d)
    z, hlast, wg2_b, wu2_b, wd2_b, wglu_b, wbs_b, wbh_b, wout_b = _s5_prompt(
        proj, tw_in.astype(BF16), tw_out.astype(BF16), a_re, a_im, d_skip,
        batch=batch, seq=seq, rows_total=mp,
        side=(ffn2_w_gate[0], ffn2_w_up[0], ffn2_w_down[0], s5_w_glu[0], w_branch_s5[0],
              w_branch_hgrn[0], w_out[0]))
    zs, s_re, s_im = _s5_sample(proj, state_s5_re[0].reshape(ns, g * p),
                                state_s5_im[0].reshape(ns, g * p),
                                tw_in, tw_out, a_re, a_im, d_skip, row0=mp)
    half = hlast.shape[-1] // 2
    p_re = hlast[:, :, SUBLANES - 1, :half].reshape(1, batch, g, p)
    p_im = hlast[:, :, SUBLANES - 1, half:].reshape(1, batch, g, p)

    gn = row(hgrn_out_norm)
    hg, st_p = _hg_prompt(proj, lb, gn, batch=batch, seq=seq, rows_total=mp, nh=nh, col0=s5w)
    hgs, st_s = _hg_sample(proj, lb, gn, state_hgrn[0], row0=mp, nh=nh, col0=s5w)

    x = _merge(z, zs, hg, hgs, proj, x, wglu_b, row(s5_b_glu), wbs_b, wbh_b, wout_b,
               row(mix_post_norm), col0=s5w + 4 * hgw, tm=MERGE_ROW_TILE)
    yp, ys = _ffn(x, None, row(ffn2_pre_norm), wg2_b, wu2_b, wd2_b, row(ffn2_post_norm),
                  ns=ns, split_out=True)

    return (yp.reshape(batch, seq, d), ys.reshape(ns, 1, d),
            p_re, p_im, jnp.swapaxes(st_p, -1, -2)[None],
            s_re.reshape(1, ns, g, p), s_im.reshape(1, ns, g, p), st_s[None])
```

```python
import functools

import jax
import jax.numpy as jnp
import numpy as np
from jax import lax
from jax.experimental import pallas as pl
from jax.experimental.pallas import tpu as pltpu

F32 = jnp.float32
BF16 = jnp.bfloat16
EPS = 1e-6
HIGHEST = lax.Precision.HIGHEST

LANES = 128
SUBLANES = 8
VMEM_LIMIT = 60 * 1024 * 1024

ROW_TILE = 832
FF_TILE = 512
IN_TILE = 1024
IN_ROW_TILE = 1040
S5_SCAN_UNROLL = 8
HG_CHUNK = 64
HG_CHUNKS_PER_STEP = 4
SAMPLE_TOKENS_PER_STEP = 8


def _cparams(*sem):
    return pltpu.CompilerParams(dimension_semantics=sem, vmem_limit_bytes=VMEM_LIMIT)


def _rms(x, g):
    return x * lax.rsqrt(jnp.mean(x * x, axis=-1, keepdims=True) + EPS) * g


def _silu(x):
    return x * jax.nn.sigmoid(x)


def _dot(a, b):
    return jnp.dot(a, b, preferred_element_type=F32)


def _dot_hi(a, b):
    return jnp.dot(a, b, preferred_element_type=F32, precision=HIGHEST)


def _dot_nt(a, b):
    return lax.dot_general(a, b, (((1,), (1,)), ((), ())), preferred_element_type=F32)


def _dot_tn(a, b):
    return lax.dot_general(a, b, (((0,), (0,)), ((), ())), preferred_element_type=F32)


BF16_ROWS = 16


def _side_specs(arrays, nsteps, step_of):
    in_specs, out_specs, out_shapes, plan = [], [], [], []
    for a in arrays:
        r, c = a.shape
        rows = next(t for t in range(BF16_ROWS, r + 1, BF16_ROWS) if r % t == 0 and r // t <= nsteps)
        nblk = r // rows
        every = nsteps // nblk

        def imap(*g, nblk=nblk, every=every):
            return (jnp.minimum(step_of(*g) // every, nblk - 1), 0)

        in_specs.append(pl.BlockSpec((rows, c), imap))
        out_specs.append(pl.BlockSpec((rows, c), imap))
        out_shapes.append(jax.ShapeDtypeStruct((r, c), BF16))
        plan.append((nblk, every))
    return in_specs, out_specs, out_shapes, tuple(plan)


def _side_cast(src_refs, dst_refs):
    for src, dst in zip(src_refs, dst_refs):
        dst[...] = src[...].astype(BF16)


def _ffn_body(*refs, ns, split_in, split_out, side_plan):
    refs = list(refs)
    nside = len(side_plan)
    x_ref = refs.pop(0)
    xs_ref = refs.pop(0) if split_in else None
    pre_ref, wg_ref, wu_ref, wd_ref, post_ref = refs[:5]
    side_src = refs[5:5 + nside]
    o_ref = refs[5 + nside]
    os_ref = refs[6 + nside] if split_out else None
    side_dst = refs[-2 - nside:-2]
    h_ref, acc_ref = refs[-2:]
    i, k = pl.program_id(0), pl.program_id(1)
    last_i = pl.num_programs(0) - 1
    last_k = pl.num_programs(1) - 1
    cut = h_ref.shape[0] - ns

    def swiglu_down(h):
        _side_cast(side_src, side_dst)
        g = _dot(h, wg_ref[...])
        u = _dot(h, wu_ref[...])
        return _dot((_silu(g) * u).astype(BF16), wd_ref[...])

    @pl.when((i < last_i) & (k == 0))
    def _():
        h = _rms(x_ref[...], pre_ref[...]).astype(BF16)
        h_ref[...] = h
        acc_ref[...] = swiglu_down(h)

    @pl.when((i < last_i) & (k == last_k))
    def _():
        acc = acc_ref[...] + swiglu_down(h_ref[...])
        o_ref[...] = x_ref[...] + 0.5 * _rms(acc, post_ref[...])

    @pl.when((i == last_i) & (k == 0))
    def _():
        h_ref[:cut, :] = _rms(x_ref[:cut, :], pre_ref[...]).astype(BF16)
        xt = xs_ref[...] if split_in else x_ref[cut:, :]
        h_ref[cut:, :] = _rms(xt, pre_ref[...]).astype(BF16)
        acc_ref[...] = jnp.zeros_like(acc_ref)

    @pl.when(((k > 0) & (k < last_k)) | (i == last_i))
    def _():
        acc_ref[...] += swiglu_down(h_ref[...])

    @pl.when((i == last_i) & (k == last_k))
    def _():
        o_ref[:cut, :] = x_ref[:cut, :] + 0.5 * _rms(acc_ref[:cut, :], post_ref[...])
        xt = xs_ref[...] if split_in else x_ref[cut:, :]
        tail = xt + 0.5 * _rms(acc_ref[cut:, :], post_ref[...])
        if split_out:
            os_ref[...] = tail
        else:
            o_ref[cut:, :] = tail


def _ffn(x, xs, pre, wg, wu, wd, post, *, ns, split_out, side=()):
    split_in = xs is not None
    d = x.shape[1]
    m = x.shape[0] + (ns if split_in else 0)
    dff = wg.shape[1]
    tm, tf = ROW_TILE, FF_TILE
    assert m % tm == 0 and 0 < ns < tm and ns % SUBLANES == 0
    nk = dff // tf
    side_in, side_out, side_shapes, side_plan = _side_specs(
        side, (m // tm) * nk, lambda i, k: i * nk + k)
    tok = pl.BlockSpec((tm, d), lambda i, k: (i, 0))
    smp = pl.BlockSpec((ns, d), lambda i, k: (0, 0))
    vec = pl.BlockSpec((1, d), lambda i, k: (0, 0))
    in_specs = [tok] + ([smp] if split_in else []) + [
        vec,
        pl.BlockSpec((d, tf), lambda i, k: (0, k)),
        pl.BlockSpec((d, tf), lambda i, k: (0, k)),
        pl.BlockSpec((tf, d), lambda i, k: (k, 0)),
        vec,
    ] + side_in
    if split_out:
        out_shape = [jax.ShapeDtypeStruct((m - ns, d), F32), jax.ShapeDtypeStruct((ns, d), F32)]
        out_specs = [tok, smp]
    else:
        out_shape = [jax.ShapeDtypeStruct((m, d), F32)]
        out_specs = [tok]
    args = [x] + ([xs] if split_in else []) + [pre, wg, wu, wd, post] + list(side)
    sequential_rows = split_out or bool(side)
    return pl.pallas_call(
        functools.partial(_ffn_body, ns=ns, split_in=split_in, split_out=split_out,
                          side_plan=side_plan),
        out_shape=out_shape + side_shapes,
        grid=(m // tm, nk),
        in_specs=in_specs,
        out_specs=out_specs + side_out,
        scratch_shapes=[pltpu.VMEM((tm, d), BF16), pltpu.VMEM((tm, d), F32)],
        compiler_params=_cparams("arbitrary" if sequential_rows else "parallel", "arbitrary"),
        name="ffn",
    )(*args)


def _inproj_body(x_ref, g_ref, w_ref, o_ref, h_ref):
    j = pl.program_id(1)

    @pl.when(j == 0)
    def _():
        h = _rms(x_ref[...], g_ref[...]).astype(BF16)
        h_ref[...] = h
        o_ref[...] = _dot(h, w_ref[...])

    @pl.when(j > 0)
    def _():
        o_ref[...] = _dot(h_ref[...], w_ref[...])


def _inproj(x, g, w):
    m, d = x.shape
    n = w.shape[1]
    tm, tn = IN_ROW_TILE, IN_TILE
    assert m % tm == 0
    return pl.pallas_call(
        _inproj_body,
        out_shape=jax.ShapeDtypeStruct((m, n), F32),
        grid=(m // tm, n // tn),
        in_specs=[
            pl.BlockSpec((tm, d), lambda i, j: (i, 0)),
            pl.BlockSpec((1, d), lambda i, j: (0, 0)),
            pl.BlockSpec((d, tn), lambda i, j: (0, j)),
        ],
        out_specs=pl.BlockSpec((tm, tn), lambda i, j: (i, j)),
        scratch_shapes=[pltpu.VMEM((tm, d), BF16)],
        compiler_params=_cparams("parallel", "arbitrary"),
        name="inproj",
    )(x, g, w)


def _s5_tables(lam_re, lam_im, log_dt, b_re, b_im, c_re, c_im):
    g, p = lam_re.shape
    n = b_re.shape[-1]
    gpt = LANES // n
    nv = g // gpt
    dt = jnp.exp(log_dt)[:, None]
    er = jnp.exp(lam_re * dt)
    th = lam_im * dt
    a_re, a_im = er * jnp.cos(th), er * jnp.sin(th)
    den = lam_re * lam_re + lam_im * lam_im
    k_re = ((a_re - 1.0) * lam_re + a_im * lam_im) / den
    k_im = (a_im * lam_re - (a_re - 1.0) * lam_im) / den
    bb_re = k_re[..., None] * b_re - k_im[..., None] * b_im
    bb_im = k_re[..., None] * b_im + k_im[..., None] * b_re
    eye = jnp.eye(gpt, dtype=F32)

    def in_blk(bb):
        t = bb.reshape(nv, gpt, p, n)
        return jnp.einsum("ab,vapn->vanbp", eye, t).reshape(nv, gpt * n, gpt * p)

    def out_blk(cc):
        t = cc.reshape(nv, gpt, n, p)
        return jnp.einsum("ab,vanp->vapbn", eye, t).reshape(nv, gpt * p, gpt * n)

    w_in = jnp.concatenate([in_blk(bb_re), in_blk(bb_im)], axis=-1)
    w_out = jnp.concatenate([out_blk(c_re), -out_blk(c_im)], axis=1)
    return w_in, w_out, a_re.reshape(nv, 1, gpt * p), a_im.reshape(nv, 1, gpt * p)


def _cmul(ar, ai, br, bi):
    return ar * br - ai * bi, ar * bi + ai * br


def _s5_prompt_body(*refs, seg, side_plan):
    nside = len(side_plan)
    u_ref, w_ref, cm_ref, ar_ref, ai_ref, d_ref = refs[:6]
    z_ref, hl_ref = refs[6 + nside:8 + nside]
    buh_ref, ut_ref, pad_ref = refs[-3:]
    _side_cast(refs[6:6 + nside], refs[8 + nside:-3])
    nseg = SUBLANES
    nc = w_ref.shape[2] // LANES
    hc = nc // 2
    pitch = pad_ref.shape[0] // nseg

    def gather(t, _):
        ut_ref[pl.ds(pl.multiple_of(t * nseg, nseg), nseg), :] = (
            pad_ref[pl.ds(t, nseg, stride=pitch), :])
        return _

    def scatter(t, _):
        pad_ref[pl.ds(t, nseg, stride=pitch), :] = (
            ut_ref[pl.ds(pl.multiple_of(t * nseg, nseg), nseg), :])
        return _

    for j in range(nseg):
        pad_ref[j * pitch:j * pitch + seg, :] = u_ref[j * seg:(j + 1) * seg, :]
    lax.fori_loop(0, seg, gather, None, unroll=S5_SCAN_UNROLL)

    nblk = nseg
    steps = seg // nseg

    def project_in(b):
        blk = slice(b * seg, (b + 1) * seg)
        bu = _dot(ut_ref[blk, :].astype(BF16), w_ref[0])
        for c in range(nc):
            buh_ref[c, blk, :] = bu[:, c * LANES:(c + 1) * LANES]

    def project_out(b):
        blk = slice(b * seg, (b + 1) * seg)
        y = d_ref[...] * ut_ref[blk, :]
        for c in range(nc):
            y = y + _dot(buh_ref[c, blk, :].astype(BF16), cm_ref[0, c * LANES:(c + 1) * LANES, :])
        ut_ref[blk, :] = jax.nn.gelu(y)

    ar = [jnp.broadcast_to(ar_ref[0, :, c * LANES:(c + 1) * LANES], (nseg, LANES)) for c in range(hc)]
    ai = [jnp.broadcast_to(ai_ref[0, :, c * LANES:(c + 1) * LANES], (nseg, LANES)) for c in range(hc)]

    def advance(t, hs, store):
        rows = slice(t * nseg, (t + 1) * nseg)
        out = []
        for c in range(hc):
            pr, pi = _cmul(ar[c], ai[c], hs[2 * c], hs[2 * c + 1])
            out += [pr + buh_ref[c, rows, :], pi + buh_ref[hc + c, rows, :]]
        if store:
            for c in range(hc):
                buh_ref[c, rows, :] = out[2 * c]
                buh_ref[hc + c, rows, :] = out[2 * c + 1]
        return out

    zeros = [jnp.zeros((nseg, LANES), F32) for _ in range(2 * hc)]
    hs = zeros
    project_in(0)
    for b in range(nblk):
        if b + 1 < nblk:
            project_in(b + 1)
        for t in range(b * steps, (b + 1) * steps):
            hs = advance(t, hs, store=False)
    ends = hs

    pw = [(ar[c], ai[c]) for c in range(hc)]
    for _ in range(seg.bit_length() - 1):
        pw = [_cmul(r, i, r, i) for r, i in pw]

    row = lax.broadcasted_iota(jnp.int32, (nseg, LANES), 0)
    init = list(zeros)
    for j in range(1, nseg):
        for c in range(hc):
            pr, pi = _cmul(pw[c][0], pw[c][1], init[2 * c], init[2 * c + 1])
            nr = pltpu.roll(pr + ends[2 * c], 1, 0)
            ni = pltpu.roll(pi + ends[2 * c + 1], 1, 0)
            init[2 * c] = jnp.where(row == j, nr, init[2 * c])
            init[2 * c + 1] = jnp.where(row == j, ni, init[2 * c + 1])

    hs = init
    for b in range(nblk):
        for t in range(b * steps, (b + 1) * steps):
            hs = advance(t, hs, store=True)
        if b > 0:
            project_out(b - 1)
    project_out(nblk - 1)
    for c in range(hc):
        hl_ref[0, 0, :, c * LANES:(c + 1) * LANES] = hs[2 * c]
        hl_ref[0, 0, :, (hc + c) * LANES:(hc + c + 1) * LANES] = hs[2 * c + 1]

    lax.fori_loop(0, seg, scatter, None, unroll=S5_SCAN_UNROLL)
    for j in range(nseg):
        z_ref[j * seg:(j + 1) * seg, :] = pad_ref[j * pitch:j * pitch + seg, :]


def _s5_prompt(proj, w_in, w_out, a_re, a_im, d_skip, *, batch, seq, rows_total, side=()):
    nv = w_in.shape[0]
    sw = w_in.shape[2]
    seg = seq // SUBLANES
    assert seg * SUBLANES == seq and seg & (seg - 1) == 0
    side_in, side_out, side_shapes, side_plan = _side_specs(
        side, batch * nv, lambda b, v: b * nv + v)
    body = functools.partial(_s5_prompt_body, seg=seg, side_plan=side_plan)
    return pl.pallas_call(
        body,
        out_shape=[jax.ShapeDtypeStruct((rows_total, nv * LANES), F32),
                   jax.ShapeDtypeStruct((batch, nv, SUBLANES, sw), F32)] + side_shapes,
        grid=(batch, nv),
        in_specs=[
            pl.BlockSpec((seq, LANES), lambda b, v: (b, v)),
            pl.BlockSpec((1, LANES, sw), lambda b, v: (v, 0, 0)),
            pl.BlockSpec((1, sw, LANES), lambda b, v: (v, 0, 0)),
            pl.BlockSpec((1, 1, sw // 2), lambda b, v: (v, 0, 0)),
            pl.BlockSpec((1, 1, sw // 2), lambda b, v: (v, 0, 0)),
            pl.BlockSpec((1, LANES), lambda b, v: (0, v)),
        ] + side_in,
        out_specs=[pl.BlockSpec((seq, LANES), lambda b, v: (b, v)),
                   pl.BlockSpec((1, 1, SUBLANES, sw), lambda b, v: (b, v, 0, 0))] + side_out,
        scratch_shapes=[pltpu.VMEM((sw // LANES, seq, LANES), F32),
                        pltpu.VMEM((seq, LANES), F32),
                        pltpu.VMEM((SUBLANES * (seg + SUBLANES), LANES), F32)],
        compiler_params=_cparams(*(("arbitrary",) * 2 if side else ("parallel",) * 2)),
        name="s5_prompt",
    )(proj, w_in, w_out, a_re, a_im, d_skip, *side)


def _s5_sample_body(u_ref, hre_ref, him_ref, w_ref, cm_ref, ar_ref, ai_ref, d_ref,
                    z_ref, ore_ref, oim_ref):
    half = ar_ref.shape[2]
    u = u_ref[...]
    bu = _dot_hi(u, w_ref[0])
    pr, pi = _cmul(ar_ref[0], ai_ref[0], hre_ref[...], him_ref[...])
    hr = pr + bu[:, :half]
    hi = pi + bu[:, half:]
    ore_ref[...] = hr
    oim_ref[...] = hi
    y = d_ref[...] * u + _dot_hi(hr, cm_ref[0, :half, :]) + _dot_hi(hi, cm_ref[0, half:, :])
    z_ref[...] = jax.nn.gelu(y)


def _s5_sample(proj, h_re, h_im, w_in, w_out, a_re, a_im, d_skip, *, row0):
    nv = w_in.shape[0]
    sw = w_in.shape[2]
    ns = h_re.shape[0]
    rb = row0 // ns
    assert rb * ns == row0
    st = pl.BlockSpec((ns, sw // 2), lambda v: (0, v))
    return pl.pallas_call(
        _s5_sample_body,
        out_shape=(jax.ShapeDtypeStruct((ns, nv * LANES), F32),
                   jax.ShapeDtypeStruct(h_re.shape, F32),
                   jax.ShapeDtypeStruct(h_im.shape, F32)),
        grid=(nv,),
        in_specs=[
            pl.BlockSpec((ns, LANES), lambda v: (rb, v)),
            st, st,
            pl.BlockSpec((1, LANES, sw), lambda v: (v, 0, 0)),
            pl.BlockSpec((1, sw, LANES), lambda v: (v, 0, 0)),
            pl.BlockSpec((1, 1, sw // 2), lambda v: (v, 0, 0)),
            pl.BlockSpec((1, 1, sw // 2), lambda v: (v, 0, 0)),
            pl.BlockSpec((1, LANES), lambda v: (0, v)),
        ],
        out_specs=(pl.BlockSpec((ns, LANES), lambda v: (0, v)), st, st),
        compiler_params=_cparams("parallel"),
        name="s5_sample",
    )(proj, h_re, h_im, w_in, w_out, a_re, a_im, d_skip)


def _hg_span_matrices(ch):
    t = np.arange(ch)[:, None]
    r = np.arange(ch)[None, :]
    mats = [r <= t, r > t]
    for v in range(1, ch.bit_length() - 1):
        base = (t >> v) << v
        upper = ((t >> v) & 1) == 1
        mats.append(np.where(upper, (r >= base) & (r <= t), (r > t) & (r < base + (1 << v))))
    mm = np.stack(mats).astype(np.float32)
    return jnp.asarray(np.concatenate([mm, mm], axis=-1), BF16)


def _hg_prompt_body(*refs, dk, side_plan):
    nside = len(side_plan)
    q_ref, f_ref, i_ref, og_ref, lb_ref, gn_ref, mm_ref = refs[:7]
    o_ref, sfin_ref = refs[7 + nside:9 + nside]
    st_ref = refs[-1]
    _side_cast(refs[7:7 + nside], refs[9 + nside:-1])
    c = pl.program_id(1)
    nh = st_ref.shape[0]
    ch = mm_ref.shape[1]
    nsub = q_ref.shape[0] // ch
    width = q_ref.shape[1]
    nlev = ch.bit_length() - 1

    @pl.when(c == 0)
    def _():
        st_ref[...] = jnp.zeros_like(st_ref)

    lb = lb_ref[...]
    f_all = lb + (1.0 - lb) * jax.nn.sigmoid(f_ref[...])
    kk_all = 1.0 - f_all
    qs_all = _silu(q_ref[...])
    logf_pieces = _split3(jnp.log(f_all))[:2]
    ib_all = i_ref[...].astype(BF16)
    pairs = [slice(2 * j * dk, 2 * (j + 1) * dk) for j in range(nh // 2)]
    r = lax.broadcasted_iota(jnp.int32, (ch, 2 * ch), 0)
    s = lax.broadcasted_iota(jnp.int32, (ch, 2 * ch), 1) & (ch - 1)
    xr = jnp.where(r > s, r ^ s, 0)
    zero_k = jnp.zeros((ch, dk), BF16)
    zero_s = jnp.zeros((dk, dk), BF16)

    def block_diag(a, b, z):
        return jnp.concatenate([jnp.concatenate([a, z], axis=1),
                                jnp.concatenate([z, b], axis=1)], axis=0)

    def pair_scores(lhs, x):
        return _dot_nt(lhs, block_diag(x[:, :dk], x[:, dk:], zero_k))

    def upper_runs(v):
        m = 1 << v
        return [(b0, b0 + m) for b0 in range(m, ch, 2 * m)]

    def mix_rows(v, qs, kk):
        m = 1 << v
        if m >= SUBLANES:
            return jnp.concatenate(
                [(qs if (b0 // m) & 1 else kk)[b0:b0 + m] for b0 in range(0, ch, m)], axis=0)
        pick = ((lax.broadcasted_iota(jnp.int32, (1, SUBLANES, width), 1) >> v) & 1) == 1
        shape3 = (ch // SUBLANES, SUBLANES, width)
        return jnp.where(pick, qs.reshape(shape3), kk.reshape(shape3)).reshape(ch, width)

    staged = []
    for sub in range(nsub):
        rows = slice(sub * ch, (sub + 1) * ch)
        kk, qs = kk_all[rows], qs_all[rows]
        logf2 = jnp.concatenate([p[rows] for p in logf_pieces], axis=0)

        def decay(idx):
            return jnp.exp(_dot(mm_ref[idx], logf2))

        eg = decay(0)
        qg = (qs * eg).astype(BF16)
        dec = eg[ch - 1:ch, :]
        kh = (kk * decay(1)).astype(BF16)
        lev = [mix_rows(0, qs * f_all[rows], kk).astype(BF16)]
        lev += [(decay(1 + v) * mix_rows(v, qs, kk)).astype(BF16) for v in range(1, nlev)]
        qb, kb = qs.astype(BF16), kk.astype(BF16)
        atts = []
        for sl in pairs:
            att = jnp.where(r == s, pair_scores(qb[:, sl], kb[:, sl]), 0.0)
            for v in range(nlev):
                x = lev[v][:, sl]
                m = 1 << v
                if m >= BF16_ROWS:
                    runs = upper_runs(v)
                    p = pair_scores(jnp.concatenate([x[a:b] for a, b in runs], axis=0), x)
                    parts = []
                    for n in range(len(runs)):
                        parts += [jnp.zeros((m, 2 * ch), F32), p[n * m:(n + 1) * m]]
                    p = jnp.concatenate(parts, axis=0)
                else:
                    p = pair_scores(x, x)
                att = jnp.where((xr >> v) == 1, p, att)
            atts.append(att.astype(BF16))
        staged.append((rows, qg, kh, dec, atts))

    for rows, qg, kh, dec, atts in staged:
        ib = ib_all[rows]
        for j, sl in enumerate(pairs):
            st2 = block_diag(st_ref[2 * j].astype(BF16), st_ref[2 * j + 1].astype(BF16), zero_s)
            ib2 = block_diag(ib[:, sl][:, :dk], ib[:, sl][:, dk:], zero_k)
            o = _dot_nt(qg[:, sl], st2) + _dot(atts[j], ib2)
            for e in range(2):
                hs = slice((2 * j + e) * dk, (2 * j + e + 1) * dk)
                og = og_ref[rows, hs]
                o_ref[rows, hs] = _rms(o[:, e * dk:(e + 1) * dk], gn_ref[:, hs]) * _silu(og)
        for h in range(nh):
            hs = slice(h * dk, (h + 1) * dk)
            st_ref[h] = st_ref[h] * dec[:, hs] + _dot_tn(ib[:, hs], kh[:, hs])

    @pl.when(c == pl.num_programs(1) - 1)
    def _():
        sfin_ref[0] = st_ref[...]


def _hg_prompt(proj, lb, gn, *, batch, seq, rows_total, nh, col0, side=()):
    width = lb.shape[1]
    dk = width // nh
    ch = HG_CHUNK * HG_CHUNKS_PER_STEP
    nchunk = seq // ch
    cb = col0 // width
    assert cb * width == col0 and nchunk * ch == seq

    def tok(k):
        return pl.BlockSpec((ch, width), lambda b, c: (b * nchunk + c, cb + k))

    vec = pl.BlockSpec((1, width), lambda b, c: (0, 0))
    mm = _hg_span_matrices(HG_CHUNK)
    side_in, side_out, side_shapes, side_plan = _side_specs(
        side, batch * nchunk, lambda b, c: b * nchunk + c)
    return pl.pallas_call(
        functools.partial(_hg_prompt_body, dk=dk, side_plan=side_plan),
        out_shape=[jax.ShapeDtypeStruct((rows_total, width), F32),
                   jax.ShapeDtypeStruct((batch, nh, dk, dk), F32)] + side_shapes,
        grid=(batch, nchunk),
        in_specs=[tok(0), tok(1), tok(2), tok(3), vec, vec,
                  pl.BlockSpec(mm.shape, lambda b, c: (0, 0, 0))] + side_in,
        out_specs=[pl.BlockSpec((ch, width), lambda b, c: (b * nchunk + c, 0)),
                   pl.BlockSpec((1, nh, dk, dk), lambda b, c: (b, 0, 0, 0))] + side_out,
        scratch_shapes=[pltpu.VMEM((nh, dk, dk), F32)],
        compiler_params=_cparams("arbitrary" if side else "parallel", "arbitrary"),
        name="hgrn_prompt",
    )(proj, proj, proj, proj, lb, gn, mm, *side)


def _split3(x):
    p1 = x.astype(BF16)
    r1 = x - p1.astype(F32)
    p2 = r1.astype(BF16)
    p3 = (r1 - p2.astype(F32)).astype(BF16)
    return p1, p2, p3


def _hg_sample_body(q_ref, f_ref, i_ref, og_ref, lb_ref, gn_ref, s0_ref,
                    hg_ref, s_ref, fq_ref, *, dk):
    step = pl.program_id(0)
    nt = q_ref.shape[0]
    tb, nh = s0_ref.shape[0], s0_ref.shape[1]

    @pl.when(step == 0)
    def _():
        lb = lb_ref[...]
        f = lb + (1.0 - lb) * jax.nn.sigmoid(f_ref[...])
        qs = _silu(q_ref[...])
        for src, base in ((f, 0), (qs, nh)):
            for h in range(nh):
                t = src[:, h * dk:(h + 1) * dk].T
                for p, piece in enumerate(_split3(t)):
                    fq_ref[(base + h) * dk:(base + h + 1) * dk, p * nt:(p + 1) * nt] = piece

    tok = lax.broadcasted_iota(jnp.int32, (3 * nt, dk), 0)
    for j in range(tb):
        n = step * tb + j
        onehot = ((tok == n) | (tok == n + nt) | (tok == n + 2 * nt)).astype(BF16)
        fq = _dot(fq_ref[...], onehot)
        irow = i_ref[pl.ds(n, 1), :]
        ogrow = og_ref[pl.ds(n, 1), :]
        for h in range(nh):
            sl = slice(h * dk, (h + 1) * dk)
            fb = fq[h * dk:(h + 1) * dk, :]
            qb = fq[(nh + h) * dk:(nh + h + 1) * dk, :]
            s1 = fb * s0_ref[j, h] + (1.0 - fb) * irow[:, sl]
            s_ref[j, h] = s1
            o = jnp.sum(qb * s1, axis=0, keepdims=True)
            hg_ref[j:j + 1, sl] = _rms(o, gn_ref[:, sl]) * _silu(ogrow[:, sl])


def _hg_sample(proj, lb, gn, s0, *, row0, nh, col0):
    width = lb.shape[1]
    dk = width // nh
    ns = s0.shape[0]
    tb = SAMPLE_TOKENS_PER_STEP
    rb, cb = row0 // ns, col0 // width
    assert rb * ns == row0 and cb * width == col0 and ns % tb == 0

    def tok(k):
        return pl.BlockSpec((ns, width), lambda t: (rb, cb + k))

    vec = pl.BlockSpec((1, width), lambda t: (0, 0))
    sspec = pl.BlockSpec((tb, nh, dk, dk), lambda t: (t, 0, 0, 0))
    return pl.pallas_call(
        functools.partial(_hg_sample_body, dk=dk),
        out_shape=(jax.ShapeDtypeStruct((ns, width), F32),
                   jax.ShapeDtypeStruct(s0.shape, F32)),
        grid=(ns // tb,),
        in_specs=[tok(0), tok(1), tok(2), tok(3), vec, vec, sspec],
        out_specs=(pl.BlockSpec((tb, width), lambda t: (t, 0)), sspec),
        scratch_shapes=[pltpu.VMEM((2 * nh * dk, 3 * ns), BF16)],
        compiler_params=_cparams("arbitrary"),
        name="hgrn_sample",
    )(proj, proj, proj, proj, lb, gn, s0)


def _merge_body(z_ref, zs_ref, hg_ref, hgs_ref, gs0_ref, gs1_ref, gh0_ref, gh1_ref, x_ref,
                wglu_ref, bglu_ref, wbs_ref, wbh_ref, wout_ref, post_ref, o_ref, *, ns):
    i = pl.program_id(0)
    last_i = pl.num_programs(0) - 1
    cut = x_ref.shape[0] - ns
    half = gs0_ref.shape[1]

    def rows(z, hg, sl):
        s5o = z * jax.nn.sigmoid(_dot(z.astype(BF16), wglu_ref[...]) + bglu_ref[...])
        a = _dot(s5o.astype(BF16), wbs_ref[...])
        b = _dot(hg.astype(BF16), wbh_ref[...])
        m0 = (jax.nn.sigmoid(gs0_ref[sl, :]) * a[:, :half]
              + jax.nn.sigmoid(gh0_ref[sl, :]) * b[:, :half])
        m1 = (jax.nn.sigmoid(gs1_ref[sl, :]) * a[:, half:]
              + jax.nn.sigmoid(gh1_ref[sl, :]) * b[:, half:])
        mix = _dot(m0.astype(BF16), wout_ref[:half, :]) + _dot(m1.astype(BF16), wout_ref[half:, :])
        o_ref[sl, :] = x_ref[sl, :] + _rms(mix, post_ref[...])

    @pl.when(i < last_i)
    def _():
        rows(z_ref[...], hg_ref[...], slice(None))

    @pl.when(i == last_i)
    def _():
        rows(z_ref[:cut, :], hg_ref[:cut, :], slice(0, cut))
        rows(zs_ref[...], hgs_ref[...], slice(cut, None))


def _merge(z, zs, hg, hgs, proj, x, wglu, bglu, wbs, wbh, wout, post, *, col0, tm):
    m, d = x.shape
    w = z.shape[1]
    ns = zs.shape[0]
    cb = col0 // w
    assert cb * w == col0 and d == 2 * w and m % tm == 0 and 0 < ns < tm

    def gate(k):
        return pl.BlockSpec((tm, w), lambda i: (i, cb + k))

    def const(shape):
        return pl.BlockSpec(shape, lambda i: (0, 0), pipeline_mode=pl.Buffered(1))

    tok = pl.BlockSpec((tm, w), lambda i: (i, 0))
    return pl.pallas_call(
        functools.partial(_merge_body, ns=ns),
        out_shape=jax.ShapeDtypeStruct((m, d), F32),
        grid=(m // tm,),
        in_specs=[
            tok, const((ns, w)), tok, const((ns, w)),
            gate(0), gate(1), gate(2), gate(3),
            pl.BlockSpec((tm, d), lambda i: (i, 0)),
            const((w, w)), const((1, w)), const((w, d)), const((w, d)), const((d, d)), const((1, d)),
        ],
        out_specs=pl.BlockSpec((tm, d), lambda i: (i, 0)),
        compiler_params=_cparams("parallel"),
        name="merge",
    )(z, zs, hg, hgs, proj, proj, proj, proj, x, wglu, bglu, wbs, wbh, wout, post)


MERGE_ROW_TILE = 320


def kernel(x_prompt, x_sample, state_s5_re, state_s5_im, state_hgrn, ffn1_pre_norm, ffn1_w_gate, ffn1_w_up, ffn1_w_down, ffn1_post_norm, mix_pre_norm, w_in, s5_lambda_re, s5_lambda_im, s5_log_dt, s5_b_re, s5_b_im, s5_c_re, s5_c_im, s5_d, s5_w_glu, s5_b_glu, hgrn_lb_logits, hgrn_out_norm, w_branch_s5, w_branch_hgrn, w_out, mix_post_norm, ffn2_pre_norm, ffn2_w_gate, ffn2_w_up, ffn2_w_down, ffn2_post_norm):
    depth = ffn1_w_gate.shape[0]
    assert depth == 1
    batch, seq, d = x_prompt.shape
    ns = x_sample.shape[0]
    assert x_sample.shape[1] == 1
    g, p = s5_lambda_re.shape[1:]
    nh, dk = state_hgrn.shape[2], state_hgrn.shape[3]
    s5w = s5_d.shape[1]
    hgw = nh * dk
    mp = batch * seq
    m = mp + ns

    bf = lambda a: a[0].astype(BF16)
    row = lambda a: a[0].reshape(1, -1).astype(F32)

    lb_all = jnp.cumsum(jax.nn.softmax(hgrn_lb_logits.astype(F32), axis=0), axis=0)
    lb = lb_all[0].reshape(1, hgw)

    x, w_in_b = _ffn(
        x_prompt.reshape(mp, d), x_sample.reshape(ns, d), row(ffn1_pre_norm),
        bf(ffn1_w_gate), bf(ffn1_w_up), bf(ffn1_w_down), row(ffn1_post_norm),
        ns=ns, split_out=False, side=(w_in[0],))
    proj = _inproj(x, row(mix_pre_norm), w_in_b)

    tw_in, tw_out, a_re, a_im = _s5_tables(
        s5_lambda_re[0], s5_lambda_im[0], s5_log_dt[0], s5_b_re[0], s5_b_im[0],
        s5_c_re[0], s5_c_im[0])
    d_skip = row(s5_d)
    z, hlast, wg2_b, wu2_b, wd2_b, wglu_b, wbs_b, wbh_b, wout_b = _s5_prompt(
        proj, tw_in.astype(BF16), tw_out.astype(BF16), a_re, a_im, d_skip,
        batch=batch, seq=seq, rows_total=mp,
        side=(ffn2_w_gate[0], ffn2_w_up[0], ffn2_w_down[0], s5_w_glu[0], w_branch_s5[0],
              w_branch_hgrn[0], w_out[0]))
    zs, s_re, s_im = _s5_sample(proj, state_s5_re[0].reshape(ns, g * p),
                                state_s5_im[0].reshape(ns, g * p),
                                tw_in, tw_out, a_re, a_im, d_skip, row0=mp)
    half = hlast.shape[-1] // 2
    p_re = hlast[:, :, SUBLANES - 1, :half].reshape(1, batch, g, p)
    p_im = hlast[:, :, SUBLANES - 1, half:].reshape(1, batch, g, p)

    gn = row(hgrn_out_norm)
    hg, st_p = _hg_prompt(proj, lb, gn, batch=batch, seq=seq, rows_total=mp, nh=nh, col0=s5w)
    hgs, st_s = _hg_sample(proj, lb, gn, state_hgrn[0], row0=mp, nh=nh, col0=s5w)

    x = _merge(z, zs, hg, hgs, proj, x, wglu_b, row(s5_b_glu), wbs_b, wbh_b, wout_b,
               row(mix_post_norm), col0=s5w + 4 * hgw, tm=MERGE_ROW_TILE)
    yp, ys = _ffn(x, None, row(ffn2_pre_norm), wg2_b, wu2_b, wd2_b, row(ffn2_post_norm),
                  ns=ns, split_out=True)

    return (yp.reshape(batch, seq, d), ys.reshape(ns, 1, d),
            p_re, p_im, jnp.swapaxes(st_p, -1, -2)[None],
            s_re.reshape(1, ns, g, p), s_im.reshape(1, ns, g, p), st_s[None])
```

```python
import functools

import jax
import jax.numpy as jnp
import numpy as np
from jax import lax
from jax.experimental import pallas as pl
from jax.experimental.pallas import tpu as pltpu

F32 = jnp.float32
BF16 = jnp.bfloat16
EPS = 1e-6
HIGHEST = lax.Precision.HIGHEST

LANES = 128
SUBLANES = 8
VMEM_LIMIT = 60 * 1024 * 1024

FFN_TILES_BF16 = (832, 512)
FFN_TILES_F32 = (1040, 256)
IN_TILE = 1024
IN_ROW_TILE = 1040
S5_SCAN_UNROLL = 8
HG_CHUNK = 64
HG_CHUNKS_PER_STEP = 4
SAMPLE_TOKENS_PER_STEP = 8


def _cparams(*sem):
    return pltpu.CompilerParams(dimension_semantics=sem, vmem_limit_bytes=VMEM_LIMIT)


def _rms(x, g):
    return x * lax.rsqrt(jnp.mean(x * x, axis=-1, keepdims=True) + EPS) * g


def _silu(x):
    return x * jax.nn.sigmoid(x)


def _dot(a, b):
    return jnp.dot(a, b, preferred_element_type=F32)


def _dot_hi(a, b):
    return jnp.dot(a, b, preferred_element_type=F32, precision=HIGHEST)


def _dot_nt(a, b):
    return lax.dot_general(a, b, (((1,), (1,)), ((), ())), preferred_element_type=F32)


def _dot_tn(a, b):
    return lax.dot_general(a, b, (((0,), (0,)), ((), ())), preferred_element_type=F32)


BF16_ROWS = 16


def _side_specs(arrays, nsteps, step_of):
    in_specs, out_specs, out_shapes, plan = [], [], [], []
    for a in arrays:
        r, c = a.shape
        rows = next(t for t in range(BF16_ROWS, r + 1, BF16_ROWS) if r % t == 0 and r // t <= nsteps)
        nblk = r // rows
        every = nsteps // nblk

        def imap(*g, nblk=nblk, every=every):
            return (jnp.minimum(step_of(*g) // every, nblk - 1), 0)

        in_specs.append(pl.BlockSpec((rows, c), imap))
        out_specs.append(pl.BlockSpec((rows, c), imap))
        out_shapes.append(jax.ShapeDtypeStruct((r, c), BF16))
        plan.append((nblk, every))
    return in_specs, out_specs, out_shapes, tuple(plan)


def _side_cast(src_refs, dst_refs):
    for src, dst in zip(src_refs, dst_refs):
        dst[...] = src[...].astype(BF16)


def _ffn_body(*refs, ns, split_in, split_out, side_plan):
    refs = list(refs)
    nside = len(side_plan)
    x_ref = refs.pop(0)
    xs_ref = refs.pop(0) if split_in else None
    pre_ref, wg_ref, wu_ref, wd_ref, post_ref = refs[:5]
    side_src = refs[5:5 + nside]
    o_ref = refs[5 + nside]
    os_ref = refs[6 + nside] if split_out else None
    side_dst = refs[-1 - nside:-1]
    h_ref = refs[-1]
    acc_ref = o_ref
    i, k = pl.program_id(0), pl.program_id(1)
    last_i = pl.num_programs(0) - 1
    last_k = pl.num_programs(1) - 1
    cut = h_ref.shape[0] - ns

    def swiglu_down(h):
        _side_cast(side_src, side_dst)
        g = _dot(h, wg_ref[...].astype(BF16))
        u = _dot(h, wu_ref[...].astype(BF16))
        return _dot((_silu(g) * u).astype(BF16), wd_ref[...].astype(BF16))

    @pl.when((i < last_i) & (k == 0))
    def _():
        h = _rms(x_ref[...], pre_ref[...]).astype(BF16)
        h_ref[...] = h
        acc_ref[...] = swiglu_down(h)

    @pl.when((i < last_i) & (k == last_k))
    def _():
        acc = acc_ref[...] + swiglu_down(h_ref[...])
        o_ref[...] = x_ref[...] + 0.5 * _rms(acc, post_ref[...])

    @pl.when((i == last_i) & (k == 0))
    def _():
        h_ref[:cut, :] = _rms(x_ref[:cut, :], pre_ref[...]).astype(BF16)
        xt = xs_ref[...] if split_in else x_ref[cut:, :]
        h_ref[cut:, :] = _rms(xt, pre_ref[...]).astype(BF16)
        acc_ref[...] = jnp.zeros_like(acc_ref)

    @pl.when(((k > 0) & (k < last_k)) | (i == last_i))
    def _():
        acc_ref[...] += swiglu_down(h_ref[...])

    @pl.when((i == last_i) & (k == last_k))
    def _():
        o_ref[:cut, :] = x_ref[:cut, :] + 0.5 * _rms(acc_ref[:cut, :], post_ref[...])
        xt = xs_ref[...] if split_in else x_ref[cut:, :]
        tail = xt + 0.5 * _rms(acc_ref[cut:, :], post_ref[...])
        if split_out:
            os_ref[...] = tail
        else:
            o_ref[cut:, :] = tail


def _ffn(x, xs, pre, wg, wu, wd, post, *, ns, split_out, side=()):
    split_in = xs is not None
    d = x.shape[1]
    m = x.shape[0] + (ns if split_in else 0)
    dff = wg.shape[1]
    tm, tf = (FFN_TILES_F32 if wg.dtype == F32 else FFN_TILES_BF16)
    assert m % tm == 0 and dff % tf == 0 and 0 < ns < tm and ns % BF16_ROWS == 0
    nk = dff // tf
    side_in, side_out, side_shapes, side_plan = _side_specs(
        side, (m // tm) * nk, lambda i, k: i * nk + k)
    tok = pl.BlockSpec((tm, d), lambda i, k: (i, 0))
    smp = pl.BlockSpec((ns, d), lambda i, k: (0, 0))
    vec = pl.BlockSpec((1, d), lambda i, k: (0, 0))
    in_specs = [tok] + ([smp] if split_in else []) + [
        vec,
        pl.BlockSpec((d, tf), lambda i, k: (0, k)),
        pl.BlockSpec((d, tf), lambda i, k: (0, k)),
        pl.BlockSpec((tf, d), lambda i, k: (k, 0)),
        vec,
    ] + side_in
    if split_out:
        out_shape = [jax.ShapeDtypeStruct((m - ns, d), F32), jax.ShapeDtypeStruct((ns, d), F32)]
        out_specs = [tok, smp]
    else:
        out_shape = [jax.ShapeDtypeStruct((m, d), F32)]
        out_specs = [tok]
    args = [x] + ([xs] if split_in else []) + [pre, wg, wu, wd, post] + list(side)
    sequential_rows = split_out or bool(side)
    return pl.pallas_call(
        functools.partial(_ffn_body, ns=ns, split_in=split_in, split_out=split_out,
                          side_plan=side_plan),
        out_shape=out_shape + side_shapes,
        grid=(m // tm, nk),
        in_specs=in_specs,
        out_specs=out_specs + side_out,
        scratch_shapes=[pltpu.VMEM((tm, d), BF16)],
        compiler_params=_cparams("arbitrary" if sequential_rows else "parallel", "arbitrary"),
        name="ffn",
    )(*args)


def _inproj_body(x_ref, g_ref, w_ref, o_ref, h_ref):
    j = pl.program_id(1)

    @pl.when(j == 0)
    def _():
        h = _rms(x_ref[...], g_ref[...]).astype(BF16)
        h_ref[...] = h
        o_ref[...] = _dot(h, w_ref[...])

    @pl.when(j > 0)
    def _():
        o_ref[...] = _dot(h_ref[...], w_ref[...])


def _inproj(x, g, w):
    m, d = x.shape
    n = w.shape[1]
    tm, tn = IN_ROW_TILE, IN_TILE
    assert m % tm == 0
    return pl.pallas_call(
        _inproj_body,
        out_shape=jax.ShapeDtypeStruct((m, n), F32),
        grid=(m // tm, n // tn),
        in_specs=[
            pl.BlockSpec((tm, d), lambda i, j: (i, 0)),
            pl.BlockSpec((1, d), lambda i, j: (0, 0)),
            pl.BlockSpec((d, tn), lambda i, j: (0, j)),
        ],
        out_specs=pl.BlockSpec((tm, tn), lambda i, j: (i, j)),
        scratch_shapes=[pltpu.VMEM((tm, d), BF16)],
        compiler_params=_cparams("parallel", "arbitrary"),
        name="inproj",
    )(x, g, w)


def _s5_tables(lam_re, lam_im, log_dt, b_re, b_im, c_re, c_im):
    g, p = lam_re.shape
    n = b_re.shape[-1]
    gpt = LANES // n
    nv = g // gpt
    dt = jnp.exp(log_dt)[:, None]
    er = jnp.exp(lam_re * dt)
    th = lam_im * dt
    a_re, a_im = er * jnp.cos(th), er * jnp.sin(th)
    den = lam_re * lam_re + lam_im * lam_im
    k_re = ((a_re - 1.0) * lam_re + a_im * lam_im) / den
    k_im = (a_im * lam_re - (a_re - 1.0) * lam_im) / den
    bb_re = k_re[..., None] * b_re - k_im[..., None] * b_im
    bb_im = k_re[..., None] * b_im + k_im[..., None] * b_re
    eye = jnp.eye(gpt, dtype=F32)

    def in_blk(bb):
        t = bb.reshape(nv, gpt, p, n)
        return jnp.einsum("ab,vapn->vanbp", eye, t).reshape(nv, gpt * n, gpt * p)

    def out_blk(cc):
        t = cc.reshape(nv, gpt, n, p)
        return jnp.einsum("ab,vanp->vapbn", eye, t).reshape(nv, gpt * p, gpt * n)

    w_in = jnp.concatenate([in_blk(bb_re), in_blk(bb_im)], axis=-1)
    w_out = jnp.concatenate([out_blk(c_re), -out_blk(c_im)], axis=1)
    return w_in, w_out, a_re.reshape(nv, 1, gpt * p), a_im.reshape(nv, 1, gpt * p)


def _cmul(ar, ai, br, bi):
    return ar * br - ai * bi, ar * bi + ai * br


def _s5_prompt_body(*refs, seg, side_plan):
    nside = len(side_plan)
    u_ref, w_ref, cm_ref, ar_ref, ai_ref, d_ref = refs[:6]
    z_ref, hl_ref = refs[6 + nside:8 + nside]
    buh_ref, ut_ref, pad_ref = refs[-3:]
    _side_cast(refs[6:6 + nside], refs[8 + nside:-3])
    nseg = SUBLANES
    nc = w_ref.shape[2] // LANES
    hc = nc // 2
    pitch = pad_ref.shape[0] // nseg

    def gather(t, _):
        ut_ref[pl.ds(pl.multiple_of(t * nseg, nseg), nseg), :] = (
            pad_ref[pl.ds(t, nseg, stride=pitch), :])
        return _

    def scatter(t, _):
        pad_ref[pl.ds(t, nseg, stride=pitch), :] = (
            ut_ref[pl.ds(pl.multiple_of(t * nseg, nseg), nseg), :])
        return _

    for j in range(nseg):
        pad_ref[j * pitch:j * pitch + seg, :] = u_ref[j * seg:(j + 1) * seg, :]
    lax.fori_loop(0, seg, gather, None, unroll=S5_SCAN_UNROLL)

    nblk = nseg
    steps = seg // nseg

    def project_in(b):
        blk = slice(b * seg, (b + 1) * seg)
        bu = _dot(ut_ref[blk, :].astype(BF16), w_ref[0])
        for c in range(nc):
            buh_ref[c, blk, :] = bu[:, c * LANES:(c + 1) * LANES]

    def project_out(b):
        blk = slice(b * seg, (b + 1) * seg)
        y = d_ref[...] * ut_ref[blk, :]
        for c in range(nc):
            y = y + _dot(buh_ref[c, blk, :].astype(BF16), cm_ref[0, c * LANES:(c + 1) * LANES, :])
        ut_ref[blk, :] = jax.nn.gelu(y)

    ar = [jnp.broadcast_to(ar_ref[0, :, c * LANES:(c + 1) * LANES], (nseg, LANES)) for c in range(hc)]
    ai = [jnp.broadcast_to(ai_ref[0, :, c * LANES:(c + 1) * LANES], (nseg, LANES)) for c in range(hc)]

    def advance(t, hs, store):
        rows = slice(t * nseg, (t + 1) * nseg)
        out = []
        for c in range(hc):
            pr, pi = _cmul(ar[c], ai[c], hs[2 * c], hs[2 * c + 1])
            out += [pr + buh_ref[c, rows, :], pi + buh_ref[hc + c, rows, :]]
        if store:
            for c in range(hc):
                buh_ref[c, rows, :] = out[2 * c]
                buh_ref[hc + c, rows, :] = out[2 * c + 1]
        return out

    zeros = [jnp.zeros((nseg, LANES), F32) for _ in range(2 * hc)]
    hs = zeros
    project_in(0)
    for b in range(nblk):
        if b + 1 < nblk:
            project_in(b + 1)
        for t in range(b * steps, (b + 1) * steps):
            hs = advance(t, hs, store=False)
    ends = hs

    pw = [(ar[c], ai[c]) for c in range(hc)]
    for _ in range(seg.bit_length() - 1):
        pw = [_cmul(r, i, r, i) for r, i in pw]

    row = lax.broadcasted_iota(jnp.int32, (nseg, LANES), 0)
    init = list(zeros)
    for j in range(1, nseg):
        for c in range(hc):
            pr, pi = _cmul(pw[c][0], pw[c][1], init[2 * c], init[2 * c + 1])
            nr = pltpu.roll(pr + ends[2 * c], 1, 0)
            ni = pltpu.roll(pi + ends[2 * c + 1], 1, 0)
            init[2 * c] = jnp.where(row == j, nr, init[2 * c])
            init[2 * c + 1] = jnp.where(row == j, ni, init[2 * c + 1])

    hs = init
    for b in range(nblk):
        for t in range(b * steps, (b + 1) * steps):
            hs = advance(t, hs, store=True)
        if b > 0:
            project_out(b - 1)
    project_out(nblk - 1)
    for c in range(hc):
        hl_ref[0, 0, :, c * LANES:(c + 1) * LANES] = hs[2 * c]
        hl_ref[0, 0, :, (hc + c) * LANES:(hc + c + 1) * LANES] = hs[2 * c + 1]

    lax.fori_loop(0, seg, scatter, None, unroll=S5_SCAN_UNROLL)
    for j in range(nseg):
        z_ref[j * seg:(j + 1) * seg, :] = pad_ref[j * pitch:j * pitch + seg, :]


def _s5_prompt(proj, w_in, w_out, a_re, a_im, d_skip, *, batch, seq, rows_total, side=()):
    nv = w_in.shape[0]
    sw = w_in.shape[2]
    seg = seq // SUBLANES
    assert seg * SUBLANES == seq and seg & (seg - 1) == 0
    side_in, side_out, side_shapes, side_plan = _side_specs(
        side, batch * nv, lambda b, v: b * nv + v)
    body = functools.partial(_s5_prompt_body, seg=seg, side_plan=side_plan)
    return pl.pallas_call(
        body,
        out_shape=[jax.ShapeDtypeStruct((rows_total, nv * LANES), F32),
                   jax.ShapeDtypeStruct((batch, nv, SUBLANES, sw), F32)] + side_shapes,
        grid=(batch, nv),
        in_specs=[
            pl.BlockSpec((seq, LANES), lambda b, v: (b, v)),
            pl.BlockSpec((1, LANES, sw), lambda b, v: (v, 0, 0)),
            pl.BlockSpec((1, sw, LANES), lambda b, v: (v, 0, 0)),
            pl.BlockSpec((1, 1, sw // 2), lambda b, v: (v, 0, 0)),
            pl.BlockSpec((1, 1, sw // 2), lambda b, v: (v, 0, 0)),
            pl.BlockSpec((1, LANES), lambda b, v: (0, v)),
        ] + side_in,
        out_specs=[pl.BlockSpec((seq, LANES), lambda b, v: (b, v)),
                   pl.BlockSpec((1, 1, SUBLANES, sw), lambda b, v: (b, v, 0, 0))] + side_out,
        scratch_shapes=[pltpu.VMEM((sw // LANES, seq, LANES), F32),
                        pltpu.VMEM((seq, LANES), F32),
                        pltpu.VMEM((SUBLANES * (seg + SUBLANES), LANES), F32)],
        compiler_params=_cparams(*(("arbitrary",) * 2 if side else ("parallel",) * 2)),
        name="s5_prompt",
    )(proj, w_in, w_out, a_re, a_im, d_skip, *side)


def _s5_sample_body(u_ref, hre_ref, him_ref, w_ref, cm_ref, ar_ref, ai_ref, d_ref,
                    z_ref, ore_ref, oim_ref):
    half = ar_ref.shape[2]
    u = u_ref[...]
    bu = _dot_hi(u, w_ref[0])
    pr, pi = _cmul(ar_ref[0], ai_ref[0], hre_ref[...], him_ref[...])
    hr = pr + bu[:, :half]
    hi = pi + bu[:, half:]
    ore_ref[...] = hr
    oim_ref[...] = hi
    y = d_ref[...] * u + _dot_hi(hr, cm_ref[0, :half, :]) + _dot_hi(hi, cm_ref[0, half:, :])
    z_ref[...] = jax.nn.gelu(y)


def _s5_sample(proj, h_re, h_im, w_in, w_out, a_re, a_im, d_skip, *, row0):
    nv = w_in.shape[0]
    sw = w_in.shape[2]
    ns = h_re.shape[0]
    rb = row0 // ns
    assert rb * ns == row0
    st = pl.BlockSpec((ns, sw // 2), lambda v: (0, v))
    return pl.pallas_call(
        _s5_sample_body,
        out_shape=(jax.ShapeDtypeStruct((ns, nv * LANES), F32),
                   jax.ShapeDtypeStruct(h_re.shape, F32),
                   jax.ShapeDtypeStruct(h_im.shape, F32)),
        grid=(nv,),
        in_specs=[
            pl.BlockSpec((ns, LANES), lambda v: (rb, v)),
            st, st,
            pl.BlockSpec((1, LANES, sw), lambda v: (v, 0, 0)),
            pl.BlockSpec((1, sw, LANES), lambda v: (v, 0, 0)),
            pl.BlockSpec((1, 1, sw // 2), lambda v: (v, 0, 0)),
            pl.BlockSpec((1, 1, sw // 2), lambda v: (v, 0, 0)),
            pl.BlockSpec((1, LANES), lambda v: (0, v)),
        ],
        out_specs=(pl.BlockSpec((ns, LANES), lambda v: (0, v)), st, st),
        compiler_params=_cparams("parallel"),
        name="s5_sample",
    )(proj, h_re, h_im, w_in, w_out, a_re, a_im, d_skip)


def _hg_span_matrices(ch):
    t = np.arange(ch)[:, None]
    r = np.arange(ch)[None, :]
    mats = [r <= t, r > t]
    for v in range(1, ch.bit_length() - 1):
        base = (t >> v) << v
        upper = ((t >> v) & 1) == 1
        mats.append(np.where(upper, (r >= base) & (r <= t), (r > t) & (r < base + (1 << v))))
    mm = np.stack(mats).astype(np.float32)
    return jnp.asarray(np.concatenate([mm, mm], axis=-1), BF16)


def _hg_prompt_body(*refs, dk, side_plan):
    nside = len(side_plan)
    q_ref, f_ref, i_ref, og_ref, lb_ref, gn_ref, mm_ref = refs[:7]
    o_ref, sfin_ref = refs[7 + nside:9 + nside]
    st_ref = refs[-1]
    _side_cast(refs[7:7 + nside], refs[9 + nside:-1])
    c = pl.program_id(1)
    nh = st_ref.shape[0]
    ch = mm_ref.shape[1]
    nsub = q_ref.shape[0] // ch
    width = q_ref.shape[1]
    nlev = ch.bit_length() - 1

    @pl.when(c == 0)
    def _():
        st_ref[...] = jnp.zeros_like(st_ref)

    lb = lb_ref[...]
    f_all = lb + (1.0 - lb) * jax.nn.sigmoid(f_ref[...])
    kk_all = 1.0 - f_all
    qs_all = _silu(q_ref[...])
    logf_pieces = _split3(jnp.log(f_all))[:2]
    ib_all = i_ref[...].astype(BF16)
    pairs = [slice(2 * j * dk, 2 * (j + 1) * dk) for j in range(nh // 2)]
    heads = [slice(h * dk, (h + 1) * dk) for h in range(nh)]
    r = lax.broadcasted_iota(jnp.int32, (ch, 2 * ch), 0)
    s = lax.broadcasted_iota(jnp.int32, (ch, 2 * ch), 1) & (ch - 1)
    xr = jnp.where(r > s, r ^ s, 0)
    zero_k = jnp.zeros((ch, dk), BF16)
    zero_s = jnp.zeros((dk, dk), BF16)

    def block_diag(a, b, z):
        return jnp.concatenate([jnp.concatenate([a, z], axis=1),
                                jnp.concatenate([z, b], axis=1)], axis=0)

    def pair_scores(lhs, x):
        return _dot_nt(lhs, block_diag(x[:, :dk], x[:, dk:], zero_k))

    def upper_runs(v):
        m = 1 << v
        return [(b0, b0 + m) for b0 in range(m, ch, 2 * m)]

    def mix_rows(v, qs, kk):
        m = 1 << v
        if m >= SUBLANES:
            return jnp.concatenate(
                [(qs if (b0 // m) & 1 else kk)[b0:b0 + m] for b0 in range(0, ch, m)], axis=0)
        pick = ((lax.broadcasted_iota(jnp.int32, (1, SUBLANES, width), 1) >> v) & 1) == 1
        shape3 = (ch // SUBLANES, SUBLANES, width)
        return jnp.where(pick, qs.reshape(shape3), kk.reshape(shape3)).reshape(ch, width)

    staged = []
    for sub in range(nsub):
        rows = slice(sub * ch, (sub + 1) * ch)
        kk, qs = kk_all[rows], qs_all[rows]
        logf2 = jnp.concatenate([p[rows] for p in logf_pieces], axis=0)

        def decay(idx):
            return jnp.exp(_dot(mm_ref[idx], logf2))

        eg = decay(0)
        qg = (qs * eg).astype(BF16)
        dec = eg[ch - 1:ch, :]
        kh = (kk * decay(1)).astype(BF16)
        lev = [mix_rows(0, qs * f_all[rows], kk).astype(BF16)]
        lev += [(decay(1 + v) * mix_rows(v, qs, kk)).astype(BF16) for v in range(1, nlev)]
        qb, kb = qs.astype(BF16), kk.astype(BF16)
        atts = []
        for sl in pairs:
            att = jnp.where(r == s, pair_scores(qb[:, sl], kb[:, sl]), 0.0)
            for v in range(nlev):
                x = lev[v][:, sl]
                m = 1 << v
                if m >= BF16_ROWS:
                    runs = upper_runs(v)
                    p = pair_scores(jnp.concatenate([x[a:b] for a, b in runs], axis=0), x)
                    parts = []
                    for n in range(len(runs)):
                        parts += [jnp.zeros((m, 2 * ch), F32), p[n * m:(n + 1) * m]]
                    p = jnp.concatenate(parts, axis=0)
                else:
                    p = pair_scores(x, x)
                att = jnp.where((xr >> v) == 1, p, att)
            atts.append(att.astype(BF16))
        staged.append((rows, qg, kh, dec, atts))

    for rows, qg, kh, dec, atts in staged:
        ib = ib_all[rows]
        for j, sl in enumerate(pairs):
            st2 = block_diag(st_ref[2 * j].astype(BF16), st_ref[2 * j + 1].astype(BF16), zero_s)
            ib2 = block_diag(ib[:, sl][:, :dk], ib[:, sl][:, dk:], zero_k)
            o = _dot_nt(qg[:, sl], st2) + _dot(atts[j], ib2)
            for e in range(2):
                hs = heads[2 * j + e]
                og = og_ref[rows, hs]
                o_ref[rows, hs] = _rms(o[:, e * dk:(e + 1) * dk], gn_ref[:, hs]) * _silu(og)
        for h, hs in enumerate(heads):
            st_ref[h] = st_ref[h] * dec[:, hs] + _dot_tn(ib[:, hs], kh[:, hs])

    @pl.when(c == pl.num_programs(1) - 1)
    def _():
        sfin_ref[0] = st_ref[...]


def _hg_prompt(proj, lb, gn, *, batch, seq, rows_total, nh, col0, side=()):
    width = lb.shape[1]
    dk = width // nh
    ch = HG_CHUNK * HG_CHUNKS_PER_STEP
    nchunk = seq // ch
    cb = col0 // width
    assert cb * width == col0 and nchunk * ch == seq

    def tok(k):
        return pl.BlockSpec((ch, width), lambda b, c: (b * nchunk + c, cb + k))

    vec = pl.BlockSpec((1, width), lambda b, c: (0, 0))
    mm = _hg_span_matrices(HG_CHUNK)
    side_in, side_out, side_shapes, side_plan = _side_specs(
        side, batch * nchunk, lambda b, c: b * nchunk + c)
    return pl.pallas_call(
        functools.partial(_hg_prompt_body, dk=dk, side_plan=side_plan),
        out_shape=[jax.ShapeDtypeStruct((rows_total, width), F32),
                   jax.ShapeDtypeStruct((batch, nh, dk, dk), F32)] + side_shapes,
        grid=(batch, nchunk),
        in_specs=[tok(0), tok(1), tok(2), tok(3), vec, vec,
                  pl.BlockSpec(mm.shape, lambda b, c: (0, 0, 0))] + side_in,
        out_specs=[pl.BlockSpec((ch, width), lambda b, c: (b * nchunk + c, 0)),
                   pl.BlockSpec((1, nh, dk, dk), lambda b, c: (b, 0, 0, 0))] + side_out,
        scratch_shapes=[pltpu.VMEM((nh, dk, dk), F32)],
        compiler_params=_cparams("arbitrary" if side else "parallel", "arbitrary"),
        name="hgrn_prompt",
    )(proj, proj, proj, proj, lb, gn, mm, *side)


def _split3(x):
    p1 = x.astype(BF16)
    r1 = x - p1.astype(F32)
    p2 = r1.astype(BF16)
    p3 = (r1 - p2.astype(F32)).astype(BF16)
    return p1, p2, p3


def _hg_sample_body(q_ref, f_ref, i_ref, og_ref, lb_ref, gn_ref, s0_ref,
                    hg_ref, s_ref, fq_ref, *, dk):
    step = pl.program_id(0)
    nt = q_ref.shape[0]
    tb, nh = s0_ref.shape[0], s0_ref.shape[1]

    @pl.when(step == 0)
    def _():
        lb = lb_ref[...]
        f = lb + (1.0 - lb) * jax.nn.sigmoid(f_ref[...])
        qs = _silu(q_ref[...])
        for src, base in ((f, 0), (qs, nh)):
            for h in range(nh):
                t = src[:, h * dk:(h + 1) * dk].T
                for p, piece in enumerate(_split3(t)):
                    fq_ref[(base + h) * dk:(base + h + 1) * dk, p * nt:(p + 1) * nt] = piece

    tok = lax.broadcasted_iota(jnp.int32, (3 * nt, dk), 0)
    for j in range(tb):
        n = step * tb + j
        onehot = ((tok == n) | (tok == n + nt) | (tok == n + 2 * nt)).astype(BF16)
        fq = _dot(fq_ref[...], onehot)
        irow = i_ref[pl.ds(n, 1), :]
        ogrow = og_ref[pl.ds(n, 1), :]
        for h in range(nh):
            sl = slice(h * dk, (h + 1) * dk)
            fb = fq[h * dk:(h + 1) * dk, :]
            qb = fq[(nh + h) * dk:(nh + h + 1) * dk, :]
            s1 = fb * s0_ref[j, h] + (1.0 - fb) * irow[:, sl]
            s_ref[j, h] = s1
            o = jnp.sum(qb * s1, axis=0, keepdims=True)
            hg_ref[j:j + 1, sl] = _rms(o, gn_ref[:, sl]) * _silu(ogrow[:, sl])


def _hg_sample(proj, lb, gn, s0, *, row0, nh, col0):
    width = lb.shape[1]
    dk = width // nh
    ns = s0.shape[0]
    tb = SAMPLE_TOKENS_PER_STEP
    rb, cb = row0 // ns, col0 // width
    assert rb * ns == row0 and cb * width == col0 and ns % tb == 0

    def tok(k):
        return pl.BlockSpec((ns, width), lambda t: (rb, cb + k))

    vec = pl.BlockSpec((1, width), lambda t: (0, 0))
    sspec = pl.BlockSpec((tb, nh, dk, dk), lambda t: (t, 0, 0, 0))
    return pl.pallas_call(
        functools.partial(_hg_sample_body, dk=dk),
        out_shape=(jax.ShapeDtypeStruct((ns, width), F32),
                   jax.ShapeDtypeStruct(s0.shape, F32)),
        grid=(ns // tb,),
        in_specs=[tok(0), tok(1), tok(2), tok(3), vec, vec, sspec],
        out_specs=(pl.BlockSpec((tb, width), lambda t: (t, 0)), sspec),
        scratch_shapes=[pltpu.VMEM((2 * nh * dk, 3 * ns), BF16)],
        compiler_params=_cparams("arbitrary"),
        name="hgrn_sample",
    )(proj, proj, proj, proj, lb, gn, s0)


def _merge_body(z_ref, zs_ref, hg_ref, hgs_ref, gs0_ref, gs1_ref, gh0_ref, gh1_ref, x_ref,
                wglu_ref, bglu_ref, wbs_ref, wbh_ref, wout_ref, post_ref, o_ref, *, ns):
    i = pl.program_id(0)
    last_i = pl.num_programs(0) - 1
    cut = x_ref.shape[0] - ns
    half = gs0_ref.shape[1]

    def rows(z, hg, sl):
        s5o = z * jax.nn.sigmoid(_dot(z.astype(BF16), wglu_ref[...]) + bglu_ref[...])
        a = _dot(s5o.astype(BF16), wbs_ref[...])
        b = _dot(hg.astype(BF16), wbh_ref[...])
        m0 = (jax.nn.sigmoid(gs0_ref[sl, :]) * a[:, :half]
              + jax.nn.sigmoid(gh0_ref[sl, :]) * b[:, :half])
        m1 = (jax.nn.sigmoid(gs1_ref[sl, :]) * a[:, half:]
              + jax.nn.sigmoid(gh1_ref[sl, :]) * b[:, half:])
        mix = _dot(m0.astype(BF16), wout_ref[:half, :]) + _dot(m1.astype(BF16), wout_ref[half:, :])
        o_ref[sl, :] = x_ref[sl, :] + _rms(mix, post_ref[...])

    @pl.when(i < last_i)
    def _():
        rows(z_ref[...], hg_ref[...], slice(None))

    @pl.when(i == last_i)
    def _():
        rows(z_ref[:cut, :], hg_ref[:cut, :], slice(0, cut))
        rows(zs_ref[...], hgs_ref[...], slice(cut, None))


def _merge(z, zs, hg, hgs, proj, x, wglu, bglu, wbs, wbh, wout, post, *, col0, tm):
    m, d = x.shape
    w = z.shape[1]
    ns = zs.shape[0]
    cb = col0 // w
    assert cb * w == col0 and d == 2 * w and m % tm == 0 and 0 < ns < tm

    def gate(k):
        return pl.BlockSpec((tm, w), lambda i: (i, cb + k))

    def const(shape):
        return pl.BlockSpec(shape, lambda i: (0, 0), pipeline_mode=pl.Buffered(1))

    tok = pl.BlockSpec((tm, w), lambda i: (i, 0))
    return pl.pallas_call(
        functools.partial(_merge_body, ns=ns),
        out_shape=jax.ShapeDtypeStruct((m, d), F32),
        grid=(m // tm,),
        in_specs=[
            tok, const((ns, w)), tok, const((ns, w)),
            gate(0), gate(1), gate(2), gate(3),
            pl.BlockSpec((tm, d), lambda i: (i, 0)),
            const((w, w)), const((1, w)), const((w, d)), const((w, d)), const((d, d)), const((1, d)),
        ],
        out_specs=pl.BlockSpec((tm, d), lambda i: (i, 0)),
        compiler_params=_cparams("parallel"),
        name="merge",
    )(z, zs, hg, hgs, proj, proj, proj, proj, x, wglu, bglu, wbs, wbh, wout, post)


MERGE_ROW_TILE = 320


def kernel(x_prompt, x_sample, state_s5_re, state_s5_im, state_hgrn, ffn1_pre_norm, ffn1_w_gate, ffn1_w_up, ffn1_w_down, ffn1_post_norm, mix_pre_norm, w_in, s5_lambda_re, s5_lambda_im, s5_log_dt, s5_b_re, s5_b_im, s5_c_re, s5_c_im, s5_d, s5_w_glu, s5_b_glu, hgrn_lb_logits, hgrn_out_norm, w_branch_s5, w_branch_hgrn, w_out, mix_post_norm, ffn2_pre_norm, ffn2_w_gate, ffn2_w_up, ffn2_w_down, ffn2_post_norm):
    depth = ffn1_w_gate.shape[0]
    assert depth == 1
    batch, seq, d = x_prompt.shape
    ns = x_sample.shape[0]
    assert x_sample.shape[1] == 1
    g, p = s5_lambda_re.shape[1:]
    nh, dk = state_hgrn.shape[2], state_hgrn.shape[3]
    s5w = s5_d.shape[1]
    hgw = nh * dk
    mp = batch * seq
    m = mp + ns

    bf = lambda a: a[0].astype(BF16)
    row = lambda a: a.reshape(1, -1)

    lb_all = jnp.cumsum(jax.nn.softmax(hgrn_lb_logits.astype(F32), axis=0), axis=0)
    lb = lb_all[0].reshape(1, hgw)

    x, w_in_b = _ffn(
        x_prompt.reshape(mp, d), x_sample.reshape(ns, d), row(ffn1_pre_norm),
        ffn1_w_gate[0], ffn1_w_up[0], ffn1_w_down[0], row(ffn1_post_norm),
        ns=ns, split_out=False, side=(w_in[0],))
    proj = _inproj(x, row(mix_pre_norm), w_in_b)

    tw_in, tw_out, a_re, a_im = _s5_tables(
        s5_lambda_re[0], s5_lambda_im[0], s5_log_dt[0], s5_b_re[0], s5_b_im[0],
        s5_c_re[0], s5_c_im[0])
    d_skip = row(s5_d)
    z, hlast, wg2_b, wu2_b, wd2_b, wglu_b, wbs_b, wbh_b, wout_b = _s5_prompt(
        proj, tw_in.astype(BF16), tw_out.astype(BF16), a_re, a_im, d_skip,
        batch=batch, seq=seq, rows_total=mp,
        side=(ffn2_w_gate[0], ffn2_w_up[0], ffn2_w_down[0], s5_w_glu[0], w_branch_s5[0],
              w_branch_hgrn[0], w_out[0]))
    zs, s_re, s_im = _s5_sample(proj, state_s5_re[0].reshape(ns, g * p),
                                state_s5_im[0].reshape(ns, g * p),
                                tw_in, tw_out, a_re, a_im, d_skip, row0=mp)
    half = hlast.shape[-1] // 2
    p_re = hlast[:, :, SUBLANES - 1, :half].reshape(1, batch, g, p)
    p_im = hlast[:, :, SUBLANES - 1, half:].reshape(1, batch, g, p)

    gn = row(hgrn_out_norm)
    hg, st_p = _hg_prompt(proj, lb, gn, batch=batch, seq=seq, rows_total=mp, nh=nh, col0=s5w)
    hgs, st_s = _hg_sample(proj, lb, gn, state_hgrn[0], row0=mp, nh=nh, col0=s5w)

    x = _merge(z, zs, hg, hgs, proj, x, wglu_b, row(s5_b_glu), wbs_b, wbh_b, wout_b,
               row(mix_post_norm), col0=s5w + 4 * hgw, tm=MERGE_ROW_TILE)
    yp, ys = _ffn(x, None, row(ffn2_pre_norm), wg2_b, wu2_b, wd2_b, row(ffn2_post_norm),
                  ns=ns, split_out=True)

    return (yp.reshape(batch, seq, d), ys.reshape(ns, 1, d),
            p_re, p_im, jnp.swapaxes(st_p, -1, -2)[None],
            s_re.reshape(1, ns, g, p), s_im.reshape(1, ns, g, p), st_s[None])
```

```python
import functools

import jax
import jax.numpy as jnp
import numpy as np
from jax import lax
from jax.experimental import pallas as pl
from jax.experimental.pallas import tpu as pltpu

F32 = jnp.float32
BF16 = jnp.bfloat16
EPS = 1e-6
HIGHEST = lax.Precision.HIGHEST

LANES = 128
SUBLANES = 8
VMEM_LIMIT = 60 * 1024 * 1024

FFN_TILES_BF16 = (832, 512)
FFN_TILES_F32 = (1040, 256)
IN_TILE = 1024
IN_ROW_TILE = 1040
S5_SCAN_UNROLL = 8
HG_CHUNK = 64
HG_CHUNKS_PER_STEP = 4
SAMPLE_TOKENS_PER_STEP = 8


def _cparams(*sem):
    return pltpu.CompilerParams(dimension_semantics=sem, vmem_limit_bytes=VMEM_LIMIT)


def _rms(x, g):
    return x * lax.rsqrt(jnp.mean(x * x, axis=-1, keepdims=True) + EPS) * g


def _silu(x):
    return x * jax.nn.sigmoid(x)


def _dot(a, b):
    return jnp.dot(a, b, preferred_element_type=F32)


def _dot_hi(a, b):
    return jnp.dot(a, b, preferred_element_type=F32, precision=HIGHEST)


def _dot_nt(a, b):
    return lax.dot_general(a, b, (((1,), (1,)), ((), ())), preferred_element_type=F32)


def _dot_tn(a, b):
    return lax.dot_general(a, b, (((0,), (0,)), ((), ())), preferred_element_type=F32)


BF16_ROWS = 16


def _side_specs(arrays, nsteps, step_of):
    in_specs, out_specs, out_shapes, plan = [], [], [], []
    for a in arrays:
        r, c = a.shape
        rows = next(t for t in range(BF16_ROWS, r + 1, BF16_ROWS) if r % t == 0 and r // t <= nsteps)
        nblk = r // rows
        every = nsteps // nblk

        def imap(*g, nblk=nblk, every=every):
            return (jnp.minimum(step_of(*g) // every, nblk - 1), 0)

        in_specs.append(pl.BlockSpec((rows, c), imap))
        out_specs.append(pl.BlockSpec((rows, c), imap))
        out_shapes.append(jax.ShapeDtypeStruct((r, c), BF16))
        plan.append((nblk, every))
    return in_specs, out_specs, out_shapes, tuple(plan)


def _side_cast(src_refs, dst_refs):
    for src, dst in zip(src_refs, dst_refs):
        dst[...] = src[...].astype(BF16)


def _ffn_body(*refs, ns, split_in, split_out, side_plan):
    refs = list(refs)
    nside = len(side_plan)
    x_ref = refs.pop(0)
    xs_ref = refs.pop(0) if split_in else None
    pre_ref, wg_ref, wu_ref, wd_ref, post_ref = refs[:5]
    side_src = refs[5:5 + nside]
    o_ref = refs[5 + nside]
    os_ref = refs[6 + nside] if split_out else None
    side_dst = refs[-1 - nside:-1]
    h_ref = refs[-1]
    acc_ref = o_ref
    i, k = pl.program_id(0), pl.program_id(1)
    last_i = pl.num_programs(0) - 1
    last_k = pl.num_programs(1) - 1
    cut = h_ref.shape[0] - ns

    def swiglu_down(h):
        _side_cast(side_src, side_dst)
        g = _dot(h, wg_ref[...].astype(BF16))
        u = _dot(h, wu_ref[...].astype(BF16))
        return _dot((_silu(g) * u).astype(BF16), wd_ref[...].astype(BF16))

    @pl.when((i < last_i) & (k == 0))
    def _():
        h = _rms(x_ref[...], pre_ref[...]).astype(BF16)
        h_ref[...] = h
        acc_ref[...] = swiglu_down(h)

    @pl.when((i < last_i) & (k == last_k))
    def _():
        acc = acc_ref[...] + swiglu_down(h_ref[...])
        o_ref[...] = x_ref[...] + 0.5 * _rms(acc, post_ref[...])

    @pl.when((i == last_i) & (k == 0))
    def _():
        h_ref[:cut, :] = _rms(x_ref[:cut, :], pre_ref[...]).astype(BF16)
        xt = xs_ref[...] if split_in else x_ref[cut:, :]
        h_ref[cut:, :] = _rms(xt, pre_ref[...]).astype(BF16)
        acc_ref[...] = jnp.zeros_like(acc_ref)

    @pl.when(((k > 0) & (k < last_k)) | (i == last_i))
    def _():
        acc_ref[...] += swiglu_down(h_ref[...])

    @pl.when((i == last_i) & (k == last_k))
    def _():
        o_ref[:cut, :] = x_ref[:cut, :] + 0.5 * _rms(acc_ref[:cut, :], post_ref[...])
        xt = xs_ref[...] if split_in else x_ref[cut:, :]
        tail = xt + 0.5 * _rms(acc_ref[cut:, :], post_ref[...])
        if split_out:
            os_ref[...] = tail
        else:
            o_ref[cut:, :] = tail


def _ffn(x, xs, pre, wg, wu, wd, post, *, ns, split_out, side=()):
    split_in = xs is not None
    d = x.shape[1]
    m = x.shape[0] + (ns if split_in else 0)
    dff = wg.shape[1]
    tm, tf = (FFN_TILES_F32 if wg.dtype == F32 else FFN_TILES_BF16)
    assert m % tm == 0 and dff % tf == 0 and 0 < ns < tm and ns % BF16_ROWS == 0
    nk = dff // tf
    side_in, side_out, side_shapes, side_plan = _side_specs(
        side, (m // tm) * nk, lambda i, k: i * nk + k)
    tok = pl.BlockSpec((tm, d), lambda i, k: (i, 0))
    smp = pl.BlockSpec((ns, d), lambda i, k: (0, 0))
    vec = pl.BlockSpec((1, d), lambda i, k: (0, 0))
    in_specs = [tok] + ([smp] if split_in else []) + [
        vec,
        pl.BlockSpec((d, tf), lambda i, k: (0, k)),
        pl.BlockSpec((d, tf), lambda i, k: (0, k)),
        pl.BlockSpec((tf, d), lambda i, k: (k, 0)),
        vec,
    ] + side_in
    if split_out:
        out_shape = [jax.ShapeDtypeStruct((m - ns, d), F32), jax.ShapeDtypeStruct((ns, d), F32)]
        out_specs = [tok, smp]
    else:
        out_shape = [jax.ShapeDtypeStruct((m, d), F32)]
        out_specs = [tok]
    args = [x] + ([xs] if split_in else []) + [pre, wg, wu, wd, post] + list(side)
    sequential_rows = split_out or bool(side)
    return pl.pallas_call(
        functools.partial(_ffn_body, ns=ns, split_in=split_in, split_out=split_out,
                          side_plan=side_plan),
        out_shape=out_shape + side_shapes,
        grid=(m // tm, nk),
        in_specs=in_specs,
        out_specs=out_specs + side_out,
        scratch_shapes=[pltpu.VMEM((tm, d), BF16)],
        compiler_params=_cparams("arbitrary" if sequential_rows else "parallel", "arbitrary"),
        name="ffn",
    )(*args)


def _inproj_body(x_ref, g_ref, w_ref, o_ref, h_ref):
    j = pl.program_id(1)

    @pl.when(j == 0)
    def _():
        h = _rms(x_ref[...], g_ref[...]).astype(BF16)
        h_ref[...] = h
        o_ref[...] = _dot(h, w_ref[...])

    @pl.when(j > 0)
    def _():
        o_ref[...] = _dot(h_ref[...], w_ref[...])


def _inproj(x, g, w):
    m, d = x.shape
    n = w.shape[1]
    tm, tn = IN_ROW_TILE, IN_TILE
    assert m % tm == 0
    return pl.pallas_call(
        _inproj_body,
        out_shape=jax.ShapeDtypeStruct((m, n), F32),
        grid=(m // tm, n // tn),
        in_specs=[
            pl.BlockSpec((tm, d), lambda i, j: (i, 0)),
            pl.BlockSpec((1, d), lambda i, j: (0, 0)),
            pl.BlockSpec((d, tn), lambda i, j: (0, j)),
        ],
        out_specs=pl.BlockSpec((tm, tn), lambda i, j: (i, j)),
        scratch_shapes=[pltpu.VMEM((tm, d), BF16)],
        compiler_params=_cparams("parallel", "arbitrary"),
        name="inproj",
    )(x, g, w)


def _s5_tables(lam_re, lam_im, log_dt, b_re, b_im, c_re, c_im):
    g, p = lam_re.shape
    n = b_re.shape[-1]
    gpt = LANES // n
    nv = g // gpt
    dt = jnp.exp(log_dt)[:, None]
    er = jnp.exp(lam_re * dt)
    th = lam_im * dt
    a_re, a_im = er * jnp.cos(th), er * jnp.sin(th)
    den = lam_re * lam_re + lam_im * lam_im
    k_re = ((a_re - 1.0) * lam_re + a_im * lam_im) / den
    k_im = (a_im * lam_re - (a_re - 1.0) * lam_im) / den
    bb_re = k_re[..., None] * b_re - k_im[..., None] * b_im
    bb_im = k_re[..., None] * b_im + k_im[..., None] * b_re
    eye = jnp.eye(gpt, dtype=F32)

    def in_blk(bb):
        t = bb.reshape(nv, gpt, p, n)
        return jnp.einsum("ab,vapn->vanbp", eye, t).reshape(nv, gpt * n, gpt * p)

    def out_blk(cc):
        t = cc.reshape(nv, gpt, n, p)
        return jnp.einsum("ab,vanp->vapbn", eye, t).reshape(nv, gpt * p, gpt * n)

    w_in = jnp.concatenate([in_blk(bb_re), in_blk(bb_im)], axis=-1)
    w_out = jnp.concatenate([out_blk(c_re), -out_blk(c_im)], axis=1)
    return w_in, w_out, a_re.reshape(nv, 1, gpt * p), a_im.reshape(nv, 1, gpt * p)


def _cmul(ar, ai, br, bi):
    return ar * br - ai * bi, ar * bi + ai * br


def _s5_prompt_body(*refs, seg, side_plan):
    nside = len(side_plan)
    u_ref, w_ref, cm_ref, ar_ref, ai_ref, d_ref = refs[:6]
    z_ref, hl_ref = refs[6 + nside:8 + nside]
    buh_ref, ut_ref, pad_ref = refs[-3:]
    _side_cast(refs[6:6 + nside], refs[8 + nside:-3])
    nseg = SUBLANES
    nc = w_ref.shape[2] // LANES
    hc = nc // 2
    pitch = pad_ref.shape[0] // nseg

    def gather(t, _):
        ut_ref[pl.ds(pl.multiple_of(t * nseg, nseg), nseg), :] = (
            pad_ref[pl.ds(t, nseg, stride=pitch), :])
        return _

    def scatter(t, _):
        pad_ref[pl.ds(t, nseg, stride=pitch), :] = (
            ut_ref[pl.ds(pl.multiple_of(t * nseg, nseg), nseg), :])
        return _

    for j in range(nseg):
        pad_ref[j * pitch:j * pitch + seg, :] = u_ref[j * seg:(j + 1) * seg, :]
    lax.fori_loop(0, seg, gather, None, unroll=S5_SCAN_UNROLL)

    nblk = nseg
    steps = seg // nseg

    def project_in(b):
        blk = slice(b * seg, (b + 1) * seg)
        bu = _dot(ut_ref[blk, :].astype(BF16), w_ref[0])
        for c in range(nc):
            buh_ref[c, blk, :] = bu[:, c * LANES:(c + 1) * LANES]

    def project_out(b):
        blk = slice(b * seg, (b + 1) * seg)
        y = d_ref[...] * ut_ref[blk, :]
        for c in range(nc):
            y = y + _dot(buh_ref[c, blk, :].astype(BF16), cm_ref[0, c * LANES:(c + 1) * LANES, :])
        ut_ref[blk, :] = jax.nn.gelu(y)

    ar = [jnp.broadcast_to(ar_ref[0, :, c * LANES:(c + 1) * LANES], (nseg, LANES)) for c in range(hc)]
    ai = [jnp.broadcast_to(ai_ref[0, :, c * LANES:(c + 1) * LANES], (nseg, LANES)) for c in range(hc)]

    def advance(t, hs, store):
        rows = slice(t * nseg, (t + 1) * nseg)
        out = []
        for c in range(hc):
            pr, pi = _cmul(ar[c], ai[c], hs[2 * c], hs[2 * c + 1])
            out += [pr + buh_ref[c, rows, :], pi + buh_ref[hc + c, rows, :]]
        if store:
            for c in range(hc):
                buh_ref[c, rows, :] = out[2 * c]
                buh_ref[hc + c, rows, :] = out[2 * c + 1]
        return out

    zeros = [jnp.zeros((nseg, LANES), F32) for _ in range(2 * hc)]
    hs = zeros
    project_in(0)
    for b in range(nblk):
        if b + 1 < nblk:
            project_in(b + 1)
        for t in range(b * steps, (b + 1) * steps):
            hs = advance(t, hs, store=False)
    ends = hs

    pw = [(ar[c], ai[c]) for c in range(hc)]
    for _ in range(seg.bit_length() - 1):
        pw = [_cmul(r, i, r, i) for r, i in pw]

    row = lax.broadcasted_iota(jnp.int32, (nseg, LANES), 0)
    init = list(zeros)
    for j in range(1, nseg):
        for c in range(hc):
            pr, pi = _cmul(pw[c][0], pw[c][1], init[2 * c], init[2 * c + 1])
            nr = pltpu.roll(pr + ends[2 * c], 1, 0)
            ni = pltpu.roll(pi + ends[2 * c + 1], 1, 0)
            init[2 * c] = jnp.where(row == j, nr, init[2 * c])
            init[2 * c + 1] = jnp.where(row == j, ni, init[2 * c + 1])

    hs = init
    for b in range(nblk):
        for t in range(b * steps, (b + 1) * steps):
            hs = advance(t, hs, store=True)
        if b > 0:
            project_out(b - 1)
    project_out(nblk - 1)
    for c in range(hc):
        hl_ref[0, 0, :, c * LANES:(c + 1) * LANES] = hs[2 * c]
        hl_ref[0, 0, :, (hc + c) * LANES:(hc + c + 1) * LANES] = hs[2 * c + 1]

    lax.fori_loop(0, seg, scatter, None, unroll=S5_SCAN_UNROLL)
    for j in range(nseg):
        z_ref[j * seg:(j + 1) * seg, :] = pad_ref[j * pitch:j * pitch + seg, :]


def _s5_prompt(proj, w_in, w_out, a_re, a_im, d_skip, *, batch, seq, rows_total, side=()):
    nv = w_in.shape[0]
    sw = w_in.shape[2]
    seg = seq // SUBLANES
    assert seg * SUBLANES == seq and seg & (seg - 1) == 0
    side_in, side_out, side_shapes, side_plan = _side_specs(
        side, batch * nv, lambda b, v: b * nv + v)
    body = functools.partial(_s5_prompt_body, seg=seg, side_plan=side_plan)
    return pl.pallas_call(
        body,
        out_shape=[jax.ShapeDtypeStruct((rows_total, nv * LANES), F32),
                   jax.ShapeDtypeStruct((batch, nv, SUBLANES, sw), F32)] + side_shapes,
        grid=(batch, nv),
        in_specs=[
            pl.BlockSpec((seq, LANES), lambda b, v: (b, v)),
            pl.BlockSpec((1, LANES, sw), lambda b, v: (v, 0, 0)),
            pl.BlockSpec((1, sw, LANES), lambda b, v: (v, 0, 0)),
            pl.BlockSpec((1, 1, sw // 2), lambda b, v: (v, 0, 0)),
            pl.BlockSpec((1, 1, sw // 2), lambda b, v: (v, 0, 0)),
            pl.BlockSpec((1, LANES), lambda b, v: (0, v)),
        ] + side_in,
        out_specs=[pl.BlockSpec((seq, LANES), lambda b, v: (b, v)),
                   pl.BlockSpec((1, 1, SUBLANES, sw), lambda b, v: (b, v, 0, 0))] + side_out,
        scratch_shapes=[pltpu.VMEM((sw // LANES, seq, LANES), F32),
                        pltpu.VMEM((seq, LANES), F32),
                        pltpu.VMEM((SUBLANES * (seg + SUBLANES), LANES), F32)],
        compiler_params=_cparams(*(("arbitrary",) * 2 if side else ("parallel",) * 2)),
        name="s5_prompt",
    )(proj, w_in, w_out, a_re, a_im, d_skip, *side)


def _s5_sample_body(u_ref, hre_ref, him_ref, w_ref, cm_ref, ar_ref, ai_ref, d_ref,
                    z_ref, ore_ref, oim_ref):
    half = ar_ref.shape[2]
    u = u_ref[...]
    bu = _dot_hi(u, w_ref[0])
    pr, pi = _cmul(ar_ref[0], ai_ref[0], hre_ref[...], him_ref[...])
    hr = pr + bu[:, :half]
    hi = pi + bu[:, half:]
    ore_ref[...] = hr
    oim_ref[...] = hi
    y = d_ref[...] * u + _dot_hi(hr, cm_ref[0, :half, :]) + _dot_hi(hi, cm_ref[0, half:, :])
    z_ref[...] = jax.nn.gelu(y)


def _s5_sample(proj, h_re, h_im, w_in, w_out, a_re, a_im, d_skip, *, row0):
    nv = w_in.shape[0]
    sw = w_in.shape[2]
    ns = h_re.shape[0]
    rb = row0 // ns
    assert rb * ns == row0
    st = pl.BlockSpec((ns, sw // 2), lambda v: (0, v))
    return pl.pallas_call(
        _s5_sample_body,
        out_shape=(jax.ShapeDtypeStruct((ns, nv * LANES), F32),
                   jax.ShapeDtypeStruct(h_re.shape, F32),
                   jax.ShapeDtypeStruct(h_im.shape, F32)),
        grid=(nv,),
        in_specs=[
            pl.BlockSpec((ns, LANES), lambda v: (rb, v)),
            st, st,
            pl.BlockSpec((1, LANES, sw), lambda v: (v, 0, 0)),
            pl.BlockSpec((1, sw, LANES), lambda v: (v, 0, 0)),
            pl.BlockSpec((1, 1, sw // 2), lambda v: (v, 0, 0)),
            pl.BlockSpec((1, 1, sw // 2), lambda v: (v, 0, 0)),
            pl.BlockSpec((1, LANES), lambda v: (0, v)),
        ],
        out_specs=(pl.BlockSpec((ns, LANES), lambda v: (0, v)), st, st),
        compiler_params=_cparams("parallel"),
        name="s5_sample",
    )(proj, h_re, h_im, w_in, w_out, a_re, a_im, d_skip)


def _hg_span_matrices(ch):
    t = np.arange(ch)[:, None]
    r = np.arange(ch)[None, :]
    mats = [r <= t, r > t]
    for v in range(1, ch.bit_length() - 1):
        base = (t >> v) << v
        upper = ((t >> v) & 1) == 1
        mats.append(np.where(upper, (r >= base) & (r <= t), (r > t) & (r < base + (1 << v))))
    mm = np.stack(mats).astype(np.float32)
    return jnp.asarray(np.concatenate([mm, mm], axis=-1), BF16)


def _hg_prompt_body(*refs, dk, side_plan):
    nside = len(side_plan)
    q_ref, f_ref, i_ref, og_ref, lb_ref, gn_ref, mm_ref = refs[:7]
    o_ref, sfin_ref = refs[7 + nside:9 + nside]
    st_ref = refs[-1]
    _side_cast(refs[7:7 + nside], refs[9 + nside:-1])
    c = pl.program_id(1)
    nh = st_ref.shape[0]
    ch = mm_ref.shape[1]
    nsub = q_ref.shape[0] // ch
    width = q_ref.shape[1]
    nlev = ch.bit_length() - 1

    @pl.when(c == 0)
    def _():
        st_ref[...] = jnp.zeros_like(st_ref)

    lb = lb_ref[...]
    f_all = lb + (1.0 - lb) * jax.nn.sigmoid(f_ref[...])
    kk_all = 1.0 - f_all
    qs_all = _silu(q_ref[...])
    logf_pieces = _split3(jnp.log(f_all))[:2]
    ib_all = i_ref[...].astype(BF16)
    pairs = [slice(2 * j * dk, 2 * (j + 1) * dk) for j in range(nh // 2)]
    heads = [slice(h * dk, (h + 1) * dk) for h in range(nh)]
    r = lax.broadcasted_iota(jnp.int32, (ch, 2 * ch), 0)
    s = lax.broadcasted_iota(jnp.int32, (ch, 2 * ch), 1) & (ch - 1)
    xr = jnp.where(r > s, r ^ s, 0)
    zero_k = jnp.zeros((ch, dk), BF16)
    zero_s = jnp.zeros((dk, dk), BF16)

    def block_diag(a, b, z):
        return jnp.concatenate([jnp.concatenate([a, z], axis=1),
                                jnp.concatenate([z, b], axis=1)], axis=0)

    def pair_scores(lhs, x):
        return _dot_nt(lhs, block_diag(x[:, :dk], x[:, dk:], zero_k))

    def upper_runs(v):
        m = 1 << v
        return [(b0, b0 + m) for b0 in range(m, ch, 2 * m)]

    def mix_rows(v, qs, kk):
        m = 1 << v
        if m >= SUBLANES:
            return jnp.concatenate(
                [(qs if (b0 // m) & 1 else kk)[b0:b0 + m] for b0 in range(0, ch, m)], axis=0)
        pick = ((lax.broadcasted_iota(jnp.int32, (1, SUBLANES, width), 1) >> v) & 1) == 1
        shape3 = (ch // SUBLANES, SUBLANES, width)
        return jnp.where(pick, qs.reshape(shape3), kk.reshape(shape3)).reshape(ch, width)

    staged = []
    for sub in range(nsub):
        rows = slice(sub * ch, (sub + 1) * ch)
        kk, qs = kk_all[rows], qs_all[rows]
        logf2 = jnp.concatenate([p[rows] for p in logf_pieces], axis=0)

        def decay(idx):
            return jnp.exp(_dot(mm_ref[idx], logf2))

        eg = decay(0)
        qg = (qs * eg).astype(BF16)
        dec = eg[ch - 1:ch, :]
        kh = (kk * decay(1)).astype(BF16)
        lev = [mix_rows(0, qs * f_all[rows], kk).astype(BF16)]
        lev += [(decay(1 + v) * mix_rows(v, qs, kk)).astype(BF16) for v in range(1, nlev)]
        qb, kb = qs.astype(BF16), kk.astype(BF16)
        atts = []
        for sl in pairs:
            att = jnp.where(r == s, pair_scores(qb[:, sl], kb[:, sl]), 0.0)
            for v in range(nlev):
                x = lev[v][:, sl]
                m = 1 << v
                if m >= BF16_ROWS:
                    runs = upper_runs(v)
                    p = pair_scores(jnp.concatenate([x[a:b] for a, b in runs], axis=0), x)
                    parts = []
                    for n in range(len(runs)):
                        parts += [jnp.zeros((m, 2 * ch), F32), p[n * m:(n + 1) * m]]
                    p = jnp.concatenate(parts, axis=0)
                else:
                    p = pair_scores(x, x)
                att = jnp.where((xr >> v) == 1, p, att)
            atts.append(att.astype(BF16))
        staged.append((rows, qg, kh, dec, atts))

    for rows, qg, kh, dec, atts in staged:
        ib = ib_all[rows]
        for j, sl in enumerate(pairs):
            st2 = block_diag(st_ref[2 * j].astype(BF16), st_ref[2 * j + 1].astype(BF16), zero_s)
            ib2 = block_diag(ib[:, sl][:, :dk], ib[:, sl][:, dk:], zero_k)
            o = _dot_nt(qg[:, sl], st2) + _dot(atts[j], ib2)
            for e in range(2):
                hs = heads[2 * j + e]
                og = og_ref[rows, hs]
                o_ref[rows, hs] = _rms(o[:, e * dk:(e + 1) * dk], gn_ref[:, hs]) * _silu(og)
        for h, hs in enumerate(heads):
            st_ref[h] = st_ref[h] * dec[:, hs] + _dot_tn(ib[:, hs], kh[:, hs])

    @pl.when(c == pl.num_programs(1) - 1)
    def _():
        sfin_ref[0] = st_ref[...]


def _hg_prompt(proj, lb, gn, *, batch, seq, rows_total, nh, col0, side=()):
    width = lb.shape[1]
    dk = width // nh
    ch = HG_CHUNK * HG_CHUNKS_PER_STEP
    nchunk = seq // ch
    cb = col0 // width
    assert cb * width == col0 and nchunk * ch == seq

    def tok(k):
        return pl.BlockSpec((ch, width), lambda b, c: (b * nchunk + c, cb + k))

    vec = pl.BlockSpec((1, width), lambda b, c: (0, 0))
    mm = _hg_span_matrices(HG_CHUNK)
    side_in, side_out, side_shapes, side_plan = _side_specs(
        side, batch * nchunk, lambda b, c: b * nchunk + c)
    return pl.pallas_call(
        functools.partial(_hg_prompt_body, dk=dk, side_plan=side_plan),
        out_shape=[jax.ShapeDtypeStruct((rows_total, width), F32),
                   jax.ShapeDtypeStruct((batch, nh, dk, dk), F32)] + side_shapes,
        grid=(batch, nchunk),
        in_specs=[tok(0), tok(1), tok(2), tok(3), vec, vec,
                  pl.BlockSpec(mm.shape, lambda b, c: (0, 0, 0))] + side_in,
        out_specs=[pl.BlockSpec((ch, width), lambda b, c: (b * nchunk + c, 0)),
                   pl.BlockSpec((1, nh, dk, dk), lambda b, c: (b, 0, 0, 0))] + side_out,
        scratch_shapes=[pltpu.VMEM((nh, dk, dk), F32)],
        compiler_params=_cparams("arbitrary" if side else "parallel", "arbitrary"),
        name="hgrn_prompt",
    )(proj, proj, proj, proj, lb, gn, mm, *side)


def _split3(x):
    p1 = x.astype(BF16)
    r1 = x - p1.astype(F32)
    p2 = r1.astype(BF16)
    p3 = (r1 - p2.astype(F32)).astype(BF16)
    return p1, p2, p3


def _hg_sample_body(q_ref, f_ref, i_ref, og_ref, lb_ref, gn_ref, s0_ref,
                    hg_ref, s_ref, fq_ref, *, dk):
    step = pl.program_id(0)
    nt = q_ref.shape[0]
    tb, nh = s0_ref.shape[0], s0_ref.shape[1]
    npiece = fq_ref.shape[1] // nt

    @pl.when(step == 0)
    def _():
        lb = lb_ref[...]
        f = lb + (1.0 - lb) * jax.nn.sigmoid(f_ref[...])
        qs = _silu(q_ref[...])
        for src, base in ((f, 0), (qs, nh)):
            for h in range(nh):
                t = src[:, h * dk:(h + 1) * dk].T
                for p, piece in enumerate(_split3(t)[:npiece]):
                    fq_ref[(base + h) * dk:(base + h + 1) * dk, p * nt:(p + 1) * nt] = piece

    tok = lax.broadcasted_iota(jnp.int32, (npiece * nt, 2 * dk), 0) & (nt - 1)
    second = lax.broadcasted_iota(jnp.int32, (npiece * nt, 2 * dk), 1) >= dk
    for j0 in range(0, tb, 2):
        n0 = step * tb + j0
        onehot = (tok == jnp.where(second, n0 + 1, n0)).astype(BF16)
        fq2 = _dot(fq_ref[...], onehot)
        for e in range(2):
            j, n = j0 + e, n0 + e
            fq = fq2[:, e * dk:(e + 1) * dk]
            irow = i_ref[pl.ds(n, 1), :]
            ogrow = og_ref[pl.ds(n, 1), :]
            for h in range(nh):
                sl = slice(h * dk, (h + 1) * dk)
                fb = fq[h * dk:(h + 1) * dk, :]
                qb = fq[(nh + h) * dk:(nh + h + 1) * dk, :]
                s1 = fb * s0_ref[j, h] + (1.0 - fb) * irow[:, sl]
                s_ref[j, h] = s1
                o = jnp.sum(qb * s1, axis=0, keepdims=True)
                hg_ref[j:j + 1, sl] = _rms(o, gn_ref[:, sl]) * _silu(ogrow[:, sl])


def _hg_sample(proj, lb, gn, s0, *, row0, nh, col0):
    width = lb.shape[1]
    dk = width // nh
    ns = s0.shape[0]
    tb = SAMPLE_TOKENS_PER_STEP
    rb, cb = row0 // ns, col0 // width
    assert rb * ns == row0 and cb * width == col0 and ns % tb == 0
    assert ns & (ns - 1) == 0 and tb % 2 == 0 and dk == LANES

    def tok(k):
        return pl.BlockSpec((ns, width), lambda t: (rb, cb + k))

    vec = pl.BlockSpec((1, width), lambda t: (0, 0))
    sspec = pl.BlockSpec((tb, nh, dk, dk), lambda t: (t, 0, 0, 0))
    return pl.pallas_call(
        functools.partial(_hg_sample_body, dk=dk),
        out_shape=(jax.ShapeDtypeStruct((ns, width), F32),
                   jax.ShapeDtypeStruct(s0.shape, F32)),
        grid=(ns // tb,),
        in_specs=[tok(0), tok(1), tok(2), tok(3), vec, vec, sspec],
        out_specs=(pl.BlockSpec((tb, width), lambda t: (t, 0)), sspec),
        scratch_shapes=[pltpu.VMEM((2 * nh * dk, 2 * ns), BF16)],
        compiler_params=_cparams("arbitrary"),
        name="hgrn_sample",
    )(proj, proj, proj, proj, lb, gn, s0)


def _merge_body(z_ref, zs_ref, hg_ref, hgs_ref, gs0_ref, gs1_ref, gh0_ref, gh1_ref, x_ref,
                wglu_ref, bglu_ref, wbs_ref, wbh_ref, wout_ref, post_ref, o_ref, *, ns):
    i = pl.program_id(0)
    last_i = pl.num_programs(0) - 1
    cut = x_ref.shape[0] - ns
    half = gs0_ref.shape[1]

    def rows(z, hg, sl):
        s5o = z * jax.nn.sigmoid(_dot(z.astype(BF16), wglu_ref[...]) + bglu_ref[...])
        a = _dot(s5o.astype(BF16), wbs_ref[...])
        b = _dot(hg.astype(BF16), wbh_ref[...])
        m0 = (jax.nn.sigmoid(gs0_ref[sl, :]) * a[:, :half]
              + jax.nn.sigmoid(gh0_ref[sl, :]) * b[:, :half])
        m1 = (jax.nn.sigmoid(gs1_ref[sl, :]) * a[:, half:]
              + jax.nn.sigmoid(gh1_ref[sl, :]) * b[:, half:])
        mix = _dot(m0.astype(BF16), wout_ref[:half, :]) + _dot(m1.astype(BF16), wout_ref[half:, :])
        o_ref[sl, :] = x_ref[sl, :] + _rms(mix, post_ref[...])

    @pl.when(i < last_i)
    def _():
        rows(z_ref[...], hg_ref[...], slice(None))

    @pl.when(i == last_i)
    def _():
        rows(z_ref[:cut, :], hg_ref[:cut, :], slice(0, cut))
        rows(zs_ref[...], hgs_ref[...], slice(cut, None))


def _merge(z, zs, hg, hgs, proj, x, wglu, bglu, wbs, wbh, wout, post, *, col0, tm):
    m, d = x.shape
    w = z.shape[1]
    ns = zs.shape[0]
    cb = col0 // w
    assert cb * w == col0 and d == 2 * w and m % tm == 0 and 0 < ns < tm

    def gate(k):
        return pl.BlockSpec((tm, w), lambda i: (i, cb + k))

    def const(shape):
        return pl.BlockSpec(shape, lambda i: (0, 0), pipeline_mode=pl.Buffered(1))

    tok = pl.BlockSpec((tm, w), lambda i: (i, 0))
    return pl.pallas_call(
        functools.partial(_merge_body, ns=ns),
        out_shape=jax.ShapeDtypeStruct((m, d), F32),
        grid=(m // tm,),
        in_specs=[
            tok, const((ns, w)), tok, const((ns, w)),
            gate(0), gate(1), gate(2), gate(3),
            pl.BlockSpec((tm, d), lambda i: (i, 0)),
            const((w, w)), const((1, w)), const((w, d)), const((w, d)), const((d, d)), const((1, d)),
        ],
        out_specs=pl.BlockSpec((tm, d), lambda i: (i, 0)),
        compiler_params=_cparams("parallel"),
        name="merge",
    )(z, zs, hg, hgs, proj, proj, proj, proj, x, wglu, bglu, wbs, wbh, wout, post)


MERGE_ROW_TILE = 320


def kernel(x_prompt, x_sample, state_s5_re, state_s5_im, state_hgrn, ffn1_pre_norm, ffn1_w_gate, ffn1_w_up, ffn1_w_down, ffn1_post_norm, mix_pre_norm, w_in, s5_lambda_re, s5_lambda_im, s5_log_dt, s5_b_re, s5_b_im, s5_c_re, s5_c_im, s5_d, s5_w_glu, s5_b_glu, hgrn_lb_logits, hgrn_out_norm, w_branch_s5, w_branch_hgrn, w_out, mix_post_norm, ffn2_pre_norm, ffn2_w_gate, ffn2_w_up, ffn2_w_down, ffn2_post_norm):
    depth = ffn1_w_gate.shape[0]
    assert depth == 1
    batch, seq, d = x_prompt.shape
    ns = x_sample.shape[0]
    assert x_sample.shape[1] == 1
    g, p = s5_lambda_re.shape[1:]
    nh, dk = state_hgrn.shape[2], state_hgrn.shape[3]
    s5w = s5_d.shape[1]
    hgw = nh * dk
    mp = batch * seq
    m = mp + ns

    bf = lambda a: a[0].astype(BF16)
    row = lambda a: a.reshape(1, -1)

    lb_all = jnp.cumsum(jax.nn.softmax(hgrn_lb_logits.astype(F32), axis=0), axis=0)
    lb = lb_all[0].reshape(1, hgw)

    x, w_in_b = _ffn(
        x_prompt.reshape(mp, d), x_sample.reshape(ns, d), row(ffn1_pre_norm),
        ffn1_w_gate[0], ffn1_w_up[0], ffn1_w_down[0], row(ffn1_post_norm),
        ns=ns, split_out=False, side=(w_in[0],))
    proj = _inproj(x, row(mix_pre_norm), w_in_b)

    tw_in, tw_out, a_re, a_im = _s5_tables(
        s5_lambda_re[0], s5_lambda_im[0], s5_log_dt[0], s5_b_re[0], s5_b_im[0],
        s5_c_re[0], s5_c_im[0])
    d_skip = row(s5_d)
    z, hlast, wg2_b, wu2_b, wd2_b, wglu_b, wbs_b, wbh_b, wout_b = _s5_prompt(
        proj, tw_in.astype(BF16), tw_out.astype(BF16), a_re, a_im, d_skip,
        batch=batch, seq=seq, rows_total=mp,
        side=(ffn2_w_gate[0], ffn2_w_up[0], ffn2_w_down[0], s5_w_glu[0], w_branch_s5[0],
              w_branch_hgrn[0], w_out[0]))
    zs, s_re, s_im = _s5_sample(proj, state_s5_re[0].reshape(ns, g * p),
                                state_s5_im[0].reshape(ns, g * p),
                                tw_in, tw_out, a_re, a_im, d_skip, row0=mp)
    half = hlast.shape[-1] // 2
    p_re = hlast[:, :, SUBLANES - 1, :half].reshape(1, batch, g, p)
    p_im = hlast[:, :, SUBLANES - 1, half:].reshape(1, batch, g, p)

    gn = row(hgrn_out_norm)
    hg, st_p = _hg_prompt(proj, lb, gn, batch=batch, seq=seq, rows_total=mp, nh=nh, col0=s5w)
    hgs, st_s = _hg_sample(proj, lb, gn, state_hgrn[0], row0=mp, nh=nh, col0=s5w)

    x = _merge(z, zs, hg, hgs, proj, x, wglu_b, row(s5_b_glu), wbs_b, wbh_b, wout_b,
               row(mix_post_norm), col0=s5w + 4 * hgw, tm=MERGE_ROW_TILE)
    yp, ys = _ffn(x, None, row(ffn2_pre_norm), wg2_b, wu2_b, wd2_b, row(ffn2_post_norm),
                  ns=ns, split_out=True)

    return (yp.reshape(batch, seq, d), ys.reshape(ns, 1, d),
            p_re, p_im, jnp.swapaxes(st_p, -1, -2)[None],
            s_re.reshape(1, ns, g, p), s_im.reshape(1, ns, g, p), st_s[None])
```

```python
import functools

import jax
import jax.numpy as jnp
import numpy as np
from jax import lax
from jax.experimental import pallas as pl
from jax.experimental.pallas import tpu as pltpu

F32 = jnp.float32
BF16 = jnp.bfloat16
EPS = 1e-6
HIGHEST = lax.Precision.HIGHEST

LANES = 128
SUBLANES = 8
VMEM_LIMIT = 60 * 1024 * 1024

FFN_TILES_BF16 = (1040, 512)
FFN_TILES_F32 = (1040, 256)
IN_TILE = 1024
IN_ROW_TILE = 1040
S5_SCAN_UNROLL = 8
HG_CHUNK = 64
HG_CHUNKS_PER_STEP = 4
SAMPLE_TOKENS_PER_STEP = 8


def _cparams(*sem):
    return pltpu.CompilerParams(dimension_semantics=sem, vmem_limit_bytes=VMEM_LIMIT)


def _rms(x, g):
    return x * lax.rsqrt(jnp.mean(x * x, axis=-1, keepdims=True) + EPS) * g


def _silu(x):
    return x * jax.nn.sigmoid(x)


def _dot(a, b):
    return jnp.dot(a, b, preferred_element_type=F32)


def _dot_hi(a, b):
    return jnp.dot(a, b, preferred_element_type=F32, precision=HIGHEST)


def _dot_nt(a, b):
    return lax.dot_general(a, b, (((1,), (1,)), ((), ())), preferred_element_type=F32)


def _dot_tn(a, b):
    return lax.dot_general(a, b, (((0,), (0,)), ((), ())), preferred_element_type=F32)


BF16_ROWS = 16


def _side_specs(arrays, nsteps, step_of):
    in_specs, out_specs, out_shapes, plan = [], [], [], []
    for a in arrays:
        r, c = a.shape
        rows = next(t for t in range(BF16_ROWS, r + 1, BF16_ROWS) if r % t == 0 and r // t <= nsteps)
        nblk = r // rows
        every = nsteps // nblk

        def imap(*g, nblk=nblk, every=every):
            return (jnp.minimum(step_of(*g) // every, nblk - 1), 0)

        in_specs.append(pl.BlockSpec((rows, c), imap))
        out_specs.append(pl.BlockSpec((rows, c), imap))
        out_shapes.append(jax.ShapeDtypeStruct((r, c), BF16))
        plan.append((nblk, every))
    return in_specs, out_specs, out_shapes, tuple(plan)


def _side_cast(src_refs, dst_refs):
    for src, dst in zip(src_refs, dst_refs):
        dst[...] = src[...].astype(BF16)


def _ffn_body(*refs, ns, split_in, split_out, side_plan):
    refs = list(refs)
    nside = len(side_plan)
    x_ref = refs.pop(0)
    xs_ref = refs.pop(0) if split_in else None
    pre_ref, wg_ref, wu_ref, wd_ref, post_ref = refs[:5]
    side_src = refs[5:5 + nside]
    o_ref = refs[5 + nside]
    os_ref = refs[6 + nside] if split_out else None
    side_dst = refs[-1 - nside:-1]
    h_ref = refs[-1]
    acc_ref = o_ref
    i, k = pl.program_id(0), pl.program_id(1)
    last_i = pl.num_programs(0) - 1
    last_k = pl.num_programs(1) - 1
    cut = h_ref.shape[0] - ns

    def swiglu_down(h):
        _side_cast(side_src, side_dst)
        g = _dot(h, wg_ref[...].astype(BF16))
        u = _dot(h, wu_ref[...].astype(BF16))
        return _dot((_silu(g) * u).astype(BF16), wd_ref[...].astype(BF16))

    @pl.when((i < last_i) & (k == 0))
    def _():
        h = _rms(x_ref[...], pre_ref[...]).astype(BF16)
        h_ref[...] = h
        acc_ref[...] = swiglu_down(h)

    @pl.when((i < last_i) & (k == last_k))
    def _():
        acc = acc_ref[...] + swiglu_down(h_ref[...])
        o_ref[...] = x_ref[...] + 0.5 * _rms(acc, post_ref[...])

    @pl.when((i == last_i) & (k == 0))
    def _():
        h_ref[:cut, :] = _rms(x_ref[:cut, :], pre_ref[...]).astype(BF16)
        xt = xs_ref[...] if split_in else x_ref[cut:, :]
        h_ref[cut:, :] = _rms(xt, pre_ref[...]).astype(BF16)
        acc_ref[...] = jnp.zeros_like(acc_ref)

    @pl.when(((k > 0) & (k < last_k)) | (i == last_i))
    def _():
        acc_ref[...] += swiglu_down(h_ref[...])

    @pl.when((i == last_i) & (k == last_k))
    def _():
        o_ref[:cut, :] = x_ref[:cut, :] + 0.5 * _rms(acc_ref[:cut, :], post_ref[...])
        xt = xs_ref[...] if split_in else x_ref[cut:, :]
        tail = xt + 0.5 * _rms(acc_ref[cut:, :], post_ref[...])
        if split_out:
            os_ref[...] = tail
        else:
            o_ref[cut:, :] = tail


def _ffn(x, xs, pre, wg, wu, wd, post, *, ns, split_out, side=()):
    split_in = xs is not None
    d = x.shape[1]
    m = x.shape[0] + (ns if split_in else 0)
    dff = wg.shape[1]
    tm, tf = (FFN_TILES_F32 if wg.dtype == F32 else FFN_TILES_BF16)
    assert m % tm == 0 and dff % tf == 0 and 0 < ns < tm and ns % BF16_ROWS == 0
    nk = dff // tf
    side_in, side_out, side_shapes, side_plan = _side_specs(
        side, (m // tm) * nk, lambda i, k: i * nk + k)
    tok = pl.BlockSpec((tm, d), lambda i, k: (i, 0))
    smp = pl.BlockSpec((ns, d), lambda i, k: (0, 0))
    vec = pl.BlockSpec((1, d), lambda i, k: (0, 0))
    in_specs = [tok] + ([smp] if split_in else []) + [
        vec,
        pl.BlockSpec((d, tf), lambda i, k: (0, k)),
        pl.BlockSpec((d, tf), lambda i, k: (0, k)),
        pl.BlockSpec((tf, d), lambda i, k: (k, 0)),
        vec,
    ] + side_in
    if split_out:
        out_shape = [jax.ShapeDtypeStruct((m - ns, d), F32), jax.ShapeDtypeStruct((ns, d), F32)]
        out_specs = [tok, smp]
    else:
        out_shape = [jax.ShapeDtypeStruct((m, d), F32)]
        out_specs = [tok]
    args = [x] + ([xs] if split_in else []) + [pre, wg, wu, wd, post] + list(side)
    sequential_rows = split_out or bool(side)
    return pl.pallas_call(
        functools.partial(_ffn_body, ns=ns, split_in=split_in, split_out=split_out,
                          side_plan=side_plan),
        out_shape=out_shape + side_shapes,
        grid=(m // tm, nk),
        in_specs=in_specs,
        out_specs=out_specs + side_out,
        scratch_shapes=[pltpu.VMEM((tm, d), BF16)],
        compiler_params=_cparams("arbitrary" if sequential_rows else "parallel", "arbitrary"),
        name="ffn",
    )(*args)


def _inproj_body(x_ref, g_ref, w_ref, o_ref, h_ref):
    j = pl.program_id(1)

    @pl.when(j == 0)
    def _():
        h = _rms(x_ref[...], g_ref[...]).astype(BF16)
        h_ref[...] = h
        o_ref[...] = _dot(h, w_ref[...])

    @pl.when(j > 0)
    def _():
        o_ref[...] = _dot(h_ref[...], w_ref[...])


def _inproj(x, g, w):
    m, d = x.shape
    n = w.shape[1]
    tm, tn = IN_ROW_TILE, IN_TILE
    assert m % tm == 0
    return pl.pallas_call(
        _inproj_body,
        out_shape=jax.ShapeDtypeStruct((m, n), F32),
        grid=(m // tm, n // tn),
        in_specs=[
            pl.BlockSpec((tm, d), lambda i, j: (i, 0)),
            pl.BlockSpec((1, d), lambda i, j: (0, 0)),
            pl.BlockSpec((d, tn), lambda i, j: (0, j)),
        ],
        out_specs=pl.BlockSpec((tm, tn), lambda i, j: (i, j)),
        scratch_shapes=[pltpu.VMEM((tm, d), BF16)],
        compiler_params=_cparams("parallel", "arbitrary"),
        name="inproj",
    )(x, g, w)


def _s5_tables(lam_re, lam_im, log_dt, b_re, b_im, c_re, c_im):
    g, p = lam_re.shape
    n = b_re.shape[-1]
    gpt = LANES // n
    nv = g // gpt
    dt = jnp.exp(log_dt)[:, None]
    er = jnp.exp(lam_re * dt)
    th = lam_im * dt
    a_re, a_im = er * jnp.cos(th), er * jnp.sin(th)
    den = lam_re * lam_re + lam_im * lam_im
    k_re = ((a_re - 1.0) * lam_re + a_im * lam_im) / den
    k_im = (a_im * lam_re - (a_re - 1.0) * lam_im) / den
    bb_re = k_re[..., None] * b_re - k_im[..., None] * b_im
    bb_im = k_re[..., None] * b_im + k_im[..., None] * b_re
    eye = jnp.eye(gpt, dtype=F32)

    def in_blk(bb):
        t = bb.reshape(nv, gpt, p, n)
        return jnp.einsum("ab,vapn->vanbp", eye, t).reshape(nv, gpt * n, gpt * p)

    def out_blk(cc):
        t = cc.reshape(nv, gpt, n, p)
        return jnp.einsum("ab,vanp->vapbn", eye, t).reshape(nv, gpt * p, gpt * n)

    w_in = jnp.concatenate([in_blk(bb_re), in_blk(bb_im)], axis=-1)
    w_out = jnp.concatenate([out_blk(c_re), -out_blk(c_im)], axis=1)
    return w_in, w_out, a_re.reshape(nv, 1, gpt * p), a_im.reshape(nv, 1, gpt * p)


def _cmul(ar, ai, br, bi):
    return ar * br - ai * bi, ar * bi + ai * br


def _s5_prompt_body(*refs, seg, side_plan):
    nside = len(side_plan)
    u_ref, w_ref, cm_ref, ar_ref, ai_ref, d_ref = refs[:6]
    z_ref, hl_ref = refs[6 + nside:8 + nside]
    buh_ref, ut_ref, pad_ref = refs[-3:]
    _side_cast(refs[6:6 + nside], refs[8 + nside:-3])
    nseg = SUBLANES
    nc = w_ref.shape[2] // LANES
    hc = nc // 2
    pitch = pad_ref.shape[0] // nseg

    def gather(t, _):
        ut_ref[pl.ds(pl.multiple_of(t * nseg, nseg), nseg), :] = (
            pad_ref[pl.ds(t, nseg, stride=pitch), :])
        return _

    def scatter(t, _):
        pad_ref[pl.ds(t, nseg, stride=pitch), :] = (
            ut_ref[pl.ds(pl.multiple_of(t * nseg, nseg), nseg), :])
        return _

    for j in range(nseg):
        pad_ref[j * pitch:j * pitch + seg, :] = u_ref[j * seg:(j + 1) * seg, :]
    lax.fori_loop(0, seg, gather, None, unroll=S5_SCAN_UNROLL)

    nblk = nseg
    steps = seg // nseg

    def project_in(b):
        blk = slice(b * seg, (b + 1) * seg)
        bu = _dot(ut_ref[blk, :].astype(BF16), w_ref[0])
        for c in range(nc):
            buh_ref[c, blk, :] = bu[:, c * LANES:(c + 1) * LANES]

    def project_out(b):
        blk = slice(b * seg, (b + 1) * seg)
        y = d_ref[...] * ut_ref[blk, :]
        for c in range(nc):
            y = y + _dot(buh_ref[c, blk, :].astype(BF16), cm_ref[0, c * LANES:(c + 1) * LANES, :])
        ut_ref[blk, :] = jax.nn.gelu(y)

    ar = [jnp.broadcast_to(ar_ref[0, :, c * LANES:(c + 1) * LANES], (nseg, LANES)) for c in range(hc)]
    ai = [jnp.broadcast_to(ai_ref[0, :, c * LANES:(c + 1) * LANES], (nseg, LANES)) for c in range(hc)]

    def advance(t, hs, store):
        rows = slice(t * nseg, (t + 1) * nseg)
        out = []
        for c in range(hc):
            pr, pi = _cmul(ar[c], ai[c], hs[2 * c], hs[2 * c + 1])
            out += [pr + buh_ref[c, rows, :], pi + buh_ref[hc + c, rows, :]]
        if store:
            for c in range(hc):
                buh_ref[c, rows, :] = out[2 * c]
                buh_ref[hc + c, rows, :] = out[2 * c + 1]
        return out

    zeros = [jnp.zeros((nseg, LANES), F32) for _ in range(2 * hc)]
    hs = zeros
    project_in(0)
    for b in range(nblk):
        if b + 1 < nblk:
            project_in(b + 1)
        for t in range(b * steps, (b + 1) * steps):
            hs = advance(t, hs, store=False)
    ends = hs

    pw = [(ar[c], ai[c]) for c in range(hc)]
    for _ in range(seg.bit_length() - 1):
        pw = [_cmul(r, i, r, i) for r, i in pw]

    row = lax.broadcasted_iota(jnp.int32, (nseg, LANES), 0)
    init = list(zeros)
    for j in range(1, nseg):
        for c in range(hc):
            pr, pi = _cmul(pw[c][0], pw[c][1], init[2 * c], init[2 * c + 1])
            nr = pltpu.roll(pr + ends[2 * c], 1, 0)
            ni = pltpu.roll(pi + ends[2 * c + 1], 1, 0)
            init[2 * c] = jnp.where(row == j, nr, init[2 * c])
            init[2 * c + 1] = jnp.where(row == j, ni, init[2 * c + 1])

    hs = init
    for b in range(nblk):
        for t in range(b * steps, (b + 1) * steps):
            hs = advance(t, hs, store=True)
        if b > 0:
            project_out(b - 1)
    project_out(nblk - 1)
    for c in range(hc):
        hl_ref[0, 0, :, c * LANES:(c + 1) * LANES] = hs[2 * c]
        hl_ref[0, 0, :, (hc + c) * LANES:(hc + c + 1) * LANES] = hs[2 * c + 1]

    lax.fori_loop(0, seg, scatter, None, unroll=S5_SCAN_UNROLL)
    for j in range(nseg):
        z_ref[j * seg:(j + 1) * seg, :] = pad_ref[j * pitch:j * pitch + seg, :]


def _s5_prompt(proj, w_in, w_out, a_re, a_im, d_skip, *, batch, seq, rows_total, side=()):
    nv = w_in.shape[0]
    sw = w_in.shape[2]
    seg = seq // SUBLANES
    assert seg * SUBLANES == seq and seg & (seg - 1) == 0
    side_in, side_out, side_shapes, side_plan = _side_specs(
        side, batch * nv, lambda b, v: b * nv + v)
    body = functools.partial(_s5_prompt_body, seg=seg, side_plan=side_plan)
    return pl.pallas_call(
        body,
        out_shape=[jax.ShapeDtypeStruct((rows_total, nv * LANES), F32),
                   jax.ShapeDtypeStruct((batch, nv, SUBLANES, sw), F32)] + side_shapes,
        grid=(batch, nv),
        in_specs=[
            pl.BlockSpec((seq, LANES), lambda b, v: (b, v)),
            pl.BlockSpec((1, LANES, sw), lambda b, v: (v, 0, 0)),
            pl.BlockSpec((1, sw, LANES), lambda b, v: (v, 0, 0)),
            pl.BlockSpec((1, 1, sw // 2), lambda b, v: (v, 0, 0)),
            pl.BlockSpec((1, 1, sw // 2), lambda b, v: (v, 0, 0)),
            pl.BlockSpec((1, LANES), lambda b, v: (0, v)),
        ] + side_in,
        out_specs=[pl.BlockSpec((seq, LANES), lambda b, v: (b, v)),
                   pl.BlockSpec((1, 1, SUBLANES, sw), lambda b, v: (b, v, 0, 0))] + side_out,
        scratch_shapes=[pltpu.VMEM((sw // LANES, seq, LANES), F32),
                        pltpu.VMEM((seq, LANES), F32),
                        pltpu.VMEM((SUBLANES * (seg + SUBLANES), LANES), F32)],
        compiler_params=_cparams(*(("arbitrary",) * 2 if side else ("parallel",) * 2)),
        name="s5_prompt",
    )(proj, w_in, w_out, a_re, a_im, d_skip, *side)


def _s5_sample_body(u_ref, hre_ref, him_ref, w_ref, cm_ref, ar_ref, ai_ref, d_ref,
                    z_ref, ore_ref, oim_ref):
    half = ar_ref.shape[2]
    u = u_ref[...]
    bu = _dot_hi(u, w_ref[0])
    pr, pi = _cmul(ar_ref[0], ai_ref[0], hre_ref[...], him_ref[...])
    hr = pr + bu[:, :half]
    hi = pi + bu[:, half:]
    ore_ref[...] = hr
    oim_ref[...] = hi
    y = d_ref[...] * u + _dot_hi(hr, cm_ref[0, :half, :]) + _dot_hi(hi, cm_ref[0, half:, :])
    z_ref[...] = jax.nn.gelu(y)


def _s5_sample(proj, h_re, h_im, w_in, w_out, a_re, a_im, d_skip, *, row0):
    nv = w_in.shape[0]
    sw = w_in.shape[2]
    ns = h_re.shape[0]
    rb = row0 // ns
    assert rb * ns == row0
    st = pl.BlockSpec((ns, sw // 2), lambda v: (0, v))
    return pl.pallas_call(
        _s5_sample_body,
        out_shape=(jax.ShapeDtypeStruct((ns, nv * LANES), F32),
                   jax.ShapeDtypeStruct(h_re.shape, F32),
                   jax.ShapeDtypeStruct(h_im.shape, F32)),
        grid=(nv,),
        in_specs=[
            pl.BlockSpec((ns, LANES), lambda v: (rb, v)),
            st, st,
            pl.BlockSpec((1, LANES, sw), lambda v: (v, 0, 0)),
            pl.BlockSpec((1, sw, LANES), lambda v: (v, 0, 0)),
            pl.BlockSpec((1, 1, sw // 2), lambda v: (v, 0, 0)),
            pl.BlockSpec((1, 1, sw // 2), lambda v: (v, 0, 0)),
            pl.BlockSpec((1, LANES), lambda v: (0, v)),
        ],
        out_specs=(pl.BlockSpec((ns, LANES), lambda v: (0, v)), st, st),
        compiler_params=_cparams("parallel"),
        name="s5_sample",
    )(proj, h_re, h_im, w_in, w_out, a_re, a_im, d_skip)


def _hg_span_matrices(ch):
    t = np.arange(ch)[:, None]
    r = np.arange(ch)[None, :]
    mats = [r <= t, r > t]
    for v in range(1, ch.bit_length() - 1):
        base = (t >> v) << v
        upper = ((t >> v) & 1) == 1
        mats.append(np.where(upper, (r >= base) & (r <= t), (r > t) & (r < base + (1 << v))))
    mm = np.stack(mats).astype(np.float32)
    return jnp.asarray(np.concatenate([mm, mm], axis=-1), BF16)


def _hg_prompt_body(*refs, dk, side_plan):
    nside = len(side_plan)
    q_ref, f_ref, i_ref, og_ref, lb_ref, gn_ref, mm_ref = refs[:7]
    o_ref, sfin_ref = refs[7 + nside:9 + nside]
    st_ref = refs[-1]
    _side_cast(refs[7:7 + nside], refs[9 + nside:-1])
    c = pl.program_id(1)
    nh = st_ref.shape[0]
    ch = mm_ref.shape[1]
    nsub = q_ref.shape[0] // ch
    width = q_ref.shape[1]
    nlev = ch.bit_length() - 1

    @pl.when(c == 0)
    def _():
        st_ref[...] = jnp.zeros_like(st_ref)

    lb = lb_ref[...]
    f_all = lb + (1.0 - lb) * jax.nn.sigmoid(f_ref[...])
    kk_all = 1.0 - f_all
    qs_all = _silu(q_ref[...])
    logf_pieces = _split3(jnp.log2(f_all))[:2]
    ib_all = i_ref[...].astype(BF16)
    pairs = [slice(2 * j * dk, 2 * (j + 1) * dk) for j in range(nh // 2)]
    heads = [slice(h * dk, (h + 1) * dk) for h in range(nh)]
    r = lax.broadcasted_iota(jnp.int32, (ch, 2 * ch), 0)
    s = lax.broadcasted_iota(jnp.int32, (ch, 2 * ch), 1) & (ch - 1)
    xr = jnp.where(r > s, r ^ s, 0)
    zero_k = jnp.zeros((ch, dk), BF16)
    zero_s = jnp.zeros((dk, dk), BF16)

    def block_diag(a, b, z):
        return jnp.concatenate([jnp.concatenate([a, z], axis=1),
                                jnp.concatenate([z, b], axis=1)], axis=0)

    def pair_scores(lhs, x):
        return _dot_nt(lhs, block_diag(x[:, :dk], x[:, dk:], zero_k))

    def upper_runs(v):
        m = 1 << v
        return [(b0, b0 + m) for b0 in range(m, ch, 2 * m)]

    def mix_rows(v, qs, kk):
        m = 1 << v
        if m >= SUBLANES:
            return jnp.concatenate(
                [(qs if (b0 // m) & 1 else kk)[b0:b0 + m] for b0 in range(0, ch, m)], axis=0)
        pick = ((lax.broadcasted_iota(jnp.int32, (1, SUBLANES, width), 1) >> v) & 1) == 1
        shape3 = (ch // SUBLANES, SUBLANES, width)
        return jnp.where(pick, qs.reshape(shape3), kk.reshape(shape3)).reshape(ch, width)

    staged = []
    for sub in range(nsub):
        rows = slice(sub * ch, (sub + 1) * ch)
        kk, qs = kk_all[rows], qs_all[rows]
        logf2 = jnp.concatenate([p[rows] for p in logf_pieces], axis=0)

        def decay(idx):
            return jnp.exp2(_dot(mm_ref[idx], logf2))

        eg = decay(0)
        qg = (qs * eg).astype(BF16)
        dec = eg[ch - 1:ch, :]
        kh = (kk * decay(1)).astype(BF16)
        lev = [mix_rows(0, qs * f_all[rows], kk).astype(BF16)]
        lev += [(decay(1 + v) * mix_rows(v, qs, kk)).astype(BF16) for v in range(1, nlev)]
        qb, kb = qs.astype(BF16), kk.astype(BF16)
        atts = []
        for sl in pairs:
            att = jnp.where(r == s, pair_scores(qb[:, sl], kb[:, sl]), 0.0)
            for v in range(nlev):
                x = lev[v][:, sl]
                m = 1 << v
                if m >= BF16_ROWS:
                    runs = upper_runs(v)
                    p = pair_scores(jnp.concatenate([x[a:b] for a, b in runs], axis=0), x)
                    parts = []
                    for n in range(len(runs)):
                        parts += [jnp.zeros((m, 2 * ch), F32), p[n * m:(n + 1) * m]]
                    p = jnp.concatenate(parts, axis=0)
                else:
                    p = pair_scores(x, x)
                att = jnp.where((xr >> v) == 1, p, att)
            atts.append(att.astype(BF16))
        staged.append((rows, qg, kh, dec, atts))

    for rows, qg, kh, dec, atts in staged:
        ib = ib_all[rows]
        for j, sl in enumerate(pairs):
            st2 = block_diag(st_ref[2 * j].astype(BF16), st_ref[2 * j + 1].astype(BF16), zero_s)
            ib2 = block_diag(ib[:, sl][:, :dk], ib[:, sl][:, dk:], zero_k)
            o = _dot_nt(qg[:, sl], st2) + _dot(atts[j], ib2)
            for e in range(2):
                hs = heads[2 * j + e]
                og = og_ref[rows, hs]
                o_ref[rows, hs] = _rms(o[:, e * dk:(e + 1) * dk], gn_ref[:, hs]) * _silu(og)
        for h, hs in enumerate(heads):
            st_ref[h] = st_ref[h] * dec[:, hs] + _dot_tn(ib[:, hs], kh[:, hs])

    @pl.when(c == pl.num_programs(1) - 1)
    def _():
        sfin_ref[0] = st_ref[...]


def _hg_prompt(proj, lb, gn, *, batch, seq, rows_total, nh, col0, side=()):
    width = lb.shape[1]
    dk = width // nh
    ch = HG_CHUNK * HG_CHUNKS_PER_STEP
    nchunk = seq // ch
    cb = col0 // width
    assert cb * width == col0 and nchunk * ch == seq

    def tok(k):
        return pl.BlockSpec((ch, width), lambda b, c: (b * nchunk + c, cb + k))

    vec = pl.BlockSpec((1, width), lambda b, c: (0, 0))
    mm = _hg_span_matrices(HG_CHUNK)
    side_in, side_out, side_shapes, side_plan = _side_specs(
        side, batch * nchunk, lambda b, c: b * nchunk + c)
    return pl.pallas_call(
        functools.partial(_hg_prompt_body, dk=dk, side_plan=side_plan),
        out_shape=[jax.ShapeDtypeStruct((rows_total, width), F32),
                   jax.ShapeDtypeStruct((batch, nh, dk, dk), F32)] + side_shapes,
        grid=(batch, nchunk),
        in_specs=[tok(0), tok(1), tok(2), tok(3), vec, vec,
                  pl.BlockSpec(mm.shape, lambda b, c: (0, 0, 0))] + side_in,
        out_specs=[pl.BlockSpec((ch, width), lambda b, c: (b * nchunk + c, 0)),
                   pl.BlockSpec((1, nh, dk, dk), lambda b, c: (b, 0, 0, 0))] + side_out,
        scratch_shapes=[pltpu.VMEM((nh, dk, dk), F32)],
        compiler_params=_cparams("arbitrary" if side else "parallel", "arbitrary"),
        name="hgrn_prompt",
    )(proj, proj, proj, proj, lb, gn, mm, *side)


def _split3(x):
    p1 = x.astype(BF16)
    r1 = x - p1.astype(F32)
    p2 = r1.astype(BF16)
    p3 = (r1 - p2.astype(F32)).astype(BF16)
    return p1, p2, p3


def _hg_sample_body(q_ref, f_ref, i_ref, og_ref, lb_ref, gn_ref, s0_ref,
                    hg_ref, s_ref, fq_ref, *, dk):
    step = pl.program_id(0)
    nt = q_ref.shape[0]
    tb, nh = s0_ref.shape[0], s0_ref.shape[1]
    npiece = fq_ref.shape[1] // nt

    @pl.when(step == 0)
    def _():
        lb = lb_ref[...]
        f = lb + (1.0 - lb) * jax.nn.sigmoid(f_ref[...])
        qs = _silu(q_ref[...])
        for src, base in ((f, 0), (qs, nh)):
            for h in range(nh):
                t = src[:, h * dk:(h + 1) * dk].T
                for p, piece in enumerate(_split3(t)[:npiece]):
                    fq_ref[(base + h) * dk:(base + h + 1) * dk, p * nt:(p + 1) * nt] = piece

    tok = lax.broadcasted_iota(jnp.int32, (npiece * nt, 2 * dk), 0) & (nt - 1)
    second = lax.broadcasted_iota(jnp.int32, (npiece * nt, 2 * dk), 1) >= dk
    for j0 in range(0, tb, 2):
        n0 = step * tb + j0
        onehot = (tok == jnp.where(second, n0 + 1, n0)).astype(BF16)
        fq2 = _dot(fq_ref[...], onehot)
        for e in range(2):
            j, n = j0 + e, n0 + e
            fq = fq2[:, e * dk:(e + 1) * dk]
            irow = i_ref[pl.ds(n, 1), :]
            ogrow = og_ref[pl.ds(n, 1), :]
            for h in range(nh):
                sl = slice(h * dk, (h + 1) * dk)
                fb = fq[h * dk:(h + 1) * dk, :]
                qb = fq[(nh + h) * dk:(nh + h + 1) * dk, :]
                s1 = fb * s0_ref[j, h] + (1.0 - fb) * irow[:, sl]
                s_ref[j, h] = s1
                o = jnp.sum(qb * s1, axis=0, keepdims=True)
                hg_ref[j:j + 1, sl] = _rms(o, gn_ref[:, sl]) * _silu(ogrow[:, sl])


def _hg_sample(proj, lb, gn, s0, *, row0, nh, col0):
    width = lb.shape[1]
    dk = width // nh
    ns = s0.shape[0]
    tb = SAMPLE_TOKENS_PER_STEP
    rb, cb = row0 // ns, col0 // width
    assert rb * ns == row0 and cb * width == col0 and ns % tb == 0
    assert ns & (ns - 1) == 0 and tb % 2 == 0 and dk == LANES

    def tok(k):
        return pl.BlockSpec((ns, width), lambda t: (rb, cb + k))

    vec = pl.BlockSpec((1, width), lambda t: (0, 0))
    sspec = pl.BlockSpec((tb, nh, dk, dk), lambda t: (t, 0, 0, 0))
    return pl.pallas_call(
        functools.partial(_hg_sample_body, dk=dk),
        out_shape=(jax.ShapeDtypeStruct((ns, width), F32),
                   jax.ShapeDtypeStruct(s0.shape, F32)),
        grid=(ns // tb,),
        in_specs=[tok(0), tok(1), tok(2), tok(3), vec, vec, sspec],
        out_specs=(pl.BlockSpec((tb, width), lambda t: (t, 0)), sspec),
        scratch_shapes=[pltpu.VMEM((2 * nh * dk, 2 * ns), BF16)],
        compiler_params=_cparams("arbitrary"),
        name="hgrn_sample",
    )(proj, proj, proj, proj, lb, gn, s0)


def _merge_body(z_ref, zs_ref, hg_ref, hgs_ref, gs0_ref, gs1_ref, gh0_ref, gh1_ref, x_ref,
                wglu_ref, bglu_ref, wbs_ref, wbh_ref, wout_ref, post_ref, o_ref, *, ns):
    i = pl.program_id(0)
    last_i = pl.num_programs(0) - 1
    cut = x_ref.shape[0] - ns
    half = gs0_ref.shape[1]

    def rows(z, hg, sl):
        s5o = z * jax.nn.sigmoid(_dot(z.astype(BF16), wglu_ref[...]) + bglu_ref[...])
        a = _dot(s5o.astype(BF16), wbs_ref[...])
        b = _dot(hg.astype(BF16), wbh_ref[...])
        m0 = (jax.nn.sigmoid(gs0_ref[sl, :]) * a[:, :half]
              + jax.nn.sigmoid(gh0_ref[sl, :]) * b[:, :half])
        m1 = (jax.nn.sigmoid(gs1_ref[sl, :]) * a[:, half:]
              + jax.nn.sigmoid(gh1_ref[sl, :]) * b[:, half:])
        mix = _dot(m0.astype(BF16), wout_ref[:half, :]) + _dot(m1.astype(BF16), wout_ref[half:, :])
        o_ref[sl, :] = x_ref[sl, :] + _rms(mix, post_ref[...])

    @pl.when(i < last_i)
    def _():
        rows(z_ref[...], hg_ref[...], slice(None))

    @pl.when(i == last_i)
    def _():
        rows(z_ref[:cut, :], hg_ref[:cut, :], slice(0, cut))
        rows(zs_ref[...], hgs_ref[...], slice(cut, None))


def _merge(z, zs, hg, hgs, proj, x, wglu, bglu, wbs, wbh, wout, post, *, col0, tm):
    m, d = x.shape
    w = z.shape[1]
    ns = zs.shape[0]
    cb = col0 // w
    assert cb * w == col0 and d == 2 * w and m % tm == 0 and 0 < ns < tm

    def gate(k):
        return pl.BlockSpec((tm, w), lambda i: (i, cb + k))

    def const(shape):
        return pl.BlockSpec(shape, lambda i: (0, 0), pipeline_mode=pl.Buffered(1))

    tok = pl.BlockSpec((tm, w), lambda i: (i, 0))
    return pl.pallas_call(
        functools.partial(_merge_body, ns=ns),
        out_shape=jax.ShapeDtypeStruct((m, d), F32),
        grid=(m // tm,),
        in_specs=[
            tok, const((ns, w)), tok, const((ns, w)),
            gate(0), gate(1), gate(2), gate(3),
            pl.BlockSpec((tm, d), lambda i: (i, 0)),
            const((w, w)), const((1, w)), const((w, d)), const((w, d)), const((d, d)), const((1, d)),
        ],
        out_specs=pl.BlockSpec((tm, d), lambda i: (i, 0)),
        compiler_params=_cparams("parallel"),
        name="merge",
    )(z, zs, hg, hgs, proj, proj, proj, proj, x, wglu, bglu, wbs, wbh, wout, post)


MERGE_ROW_TILE = 416


def kernel(x_prompt, x_sample, state_s5_re, state_s5_im, state_hgrn, ffn1_pre_norm, ffn1_w_gate, ffn1_w_up, ffn1_w_down, ffn1_post_norm, mix_pre_norm, w_in, s5_lambda_re, s5_lambda_im, s5_log_dt, s5_b_re, s5_b_im, s5_c_re, s5_c_im, s5_d, s5_w_glu, s5_b_glu, hgrn_lb_logits, hgrn_out_norm, w_branch_s5, w_branch_hgrn, w_out, mix_post_norm, ffn2_pre_norm, ffn2_w_gate, ffn2_w_up, ffn2_w_down, ffn2_post_norm):
    depth = ffn1_w_gate.shape[0]
    assert depth == 1
    batch, seq, d = x_prompt.shape
    ns = x_sample.shape[0]
    assert x_sample.shape[1] == 1
    g, p = s5_lambda_re.shape[1:]
    nh, dk = state_hgrn.shape[2], state_hgrn.shape[3]
    s5w = s5_d.shape[1]
    hgw = nh * dk
    mp = batch * seq
    m = mp + ns

    bf = lambda a: a[0].astype(BF16)
    row = lambda a: a.reshape(1, -1)

    lb_all = jnp.cumsum(jax.nn.softmax(hgrn_lb_logits.astype(F32), axis=0), axis=0)
    lb = lb_all[0].reshape(1, hgw)

    x, w_in_b = _ffn(
        x_prompt.reshape(mp, d), x_sample.reshape(ns, d), row(ffn1_pre_norm),
        ffn1_w_gate[0], ffn1_w_up[0], ffn1_w_down[0], row(ffn1_post_norm),
        ns=ns, split_out=False, side=(w_in[0],))
    proj = _inproj(x, row(mix_pre_norm), w_in_b)

    tw_in, tw_out, a_re, a_im = _s5_tables(
        s5_lambda_re[0], s5_lambda_im[0], s5_log_dt[0], s5_b_re[0], s5_b_im[0],
        s5_c_re[0], s5_c_im[0])
    d_skip = row(s5_d)
    z, hlast, wg2_b, wu2_b, wd2_b, wglu_b, wbs_b, wbh_b, wout_b = _s5_prompt(
        proj, tw_in.astype(BF16), tw_out.astype(BF16), a_re, a_im, d_skip,
        batch=batch, seq=seq, rows_total=mp,
        side=(ffn2_w_gate[0], ffn2_w_up[0], ffn2_w_down[0], s5_w_glu[0], w_branch_s5[0],
              w_branch_hgrn[0], w_out[0]))
    zs, s_re, s_im = _s5_sample(proj, state_s5_re[0].reshape(ns, g * p),
                                state_s5_im[0].reshape(ns, g * p),
                                tw_in, tw_out, a_re, a_im, d_skip, row0=mp)
    half = hlast.shape[-1] // 2
    p_re = hlast[:, :, SUBLANES - 1, :half].reshape(1, batch, g, p)
    p_im = hlast[:, :, SUBLANES - 1, half:].reshape(1, batch, g, p)

    gn = row(hgrn_out_norm)
    hg, st_p = _hg_prompt(proj, lb, gn, batch=batch, seq=seq, rows_total=mp, nh=nh, col0=s5w)
    hgs, st_s = _hg_sample(proj, lb, gn, state_hgrn[0], row0=mp, nh=nh, col0=s5w)

    x = _merge(z, zs, hg, hgs, proj, x, wglu_b, row(s5_b_glu), wbs_b, wbh_b, wout_b,
               row(mix_post_norm), col0=s5w + 4 * hgw, tm=MERGE_ROW_TILE)
    yp, ys = _ffn(x, None, row(ffn2_pre_norm), wg2_b, wu2_b, wd2_b, row(ffn2_post_norm),
                  ns=ns, split_out=True)

    return (yp.reshape(batch, seq, d), ys.reshape(ns, 1, d),
            p_re, p_im, jnp.swapaxes(st_p, -1, -2)[None],
            s_re.reshape(1, ns, g, p), s_im.reshape(1, ns, g, p), st_s[None])
```

```python
import functools

import jax
import jax.numpy as jnp
import numpy as np
from jax import lax
from jax.experimental import pallas as pl
from jax.experimental.pallas import tpu as pltpu

F32 = jnp.float32
BF16 = jnp.bfloat16
EPS = 1e-6
HIGHEST = lax.Precision.HIGHEST

LANES = 128
SUBLANES = 8
VMEM_LIMIT = 60 * 1024 * 1024

FFN_TILES_BF16 = (832, 512)
FFN_TILES_F32 = (1040, 256)
IN_TILE = 1536
IN_ROW_TILE = 1040
S5_SCAN_UNROLL = 8
HG_CHUNK = 64
HG_CHUNKS_PER_STEP = 4
SAMPLE_TOKENS_PER_STEP = 8


def _cparams(*sem):
    return pltpu.CompilerParams(dimension_semantics=sem, vmem_limit_bytes=VMEM_LIMIT)


def _rms(x, g):
    return x * lax.rsqrt(jnp.mean(x * x, axis=-1, keepdims=True) + EPS) * g


def _silu(x):
    return x * jax.nn.sigmoid(x)


def _dot(a, b):
    return jnp.dot(a, b, preferred_element_type=F32)


def _dot_hi(a, b):
    return jnp.dot(a, b, preferred_element_type=F32, precision=HIGHEST)


def _dot_nt(a, b):
    return lax.dot_general(a, b, (((1,), (1,)), ((), ())), preferred_element_type=F32)


def _dot_tn(a, b):
    return lax.dot_general(a, b, (((0,), (0,)), ((), ())), preferred_element_type=F32)


BF16_ROWS = 16


def _side_specs(arrays, nsteps, step_of):
    in_specs, out_specs, out_shapes, plan = [], [], [], []
    for a in arrays:
        r, c = a.shape
        rows = next(t for t in range(BF16_ROWS, r + 1, BF16_ROWS) if r % t == 0 and r // t <= nsteps)
        nblk = r // rows
        every = nsteps // nblk

        def imap(*g, nblk=nblk, every=every):
            return (jnp.minimum(step_of(*g) // every, nblk - 1), 0)

        in_specs.append(pl.BlockSpec((rows, c), imap))
        out_specs.append(pl.BlockSpec((rows, c), imap))
        out_shapes.append(jax.ShapeDtypeStruct((r, c), BF16))
        plan.append((nblk, every))
    return in_specs, out_specs, out_shapes, tuple(plan)


def _side_cast(src_refs, dst_refs):
    for src, dst in zip(src_refs, dst_refs):
        dst[...] = src[...].astype(BF16)


def _ffn_body(*refs, ns, split_in, split_out, side_plan):
    refs = list(refs)
    nside = len(side_plan)
    x_ref = refs.pop(0)
    xs_ref = refs.pop(0) if split_in else None
    pre_ref, wg_ref, wu_ref, wd_ref, post_ref = refs[:5]
    side_src = refs[5:5 + nside]
    o_ref = refs[5 + nside]
    os_ref = refs[6 + nside] if split_out else None
    side_dst = refs[-1 - nside:-1]
    h_ref = refs[-1]
    acc_ref = o_ref
    i, k = pl.program_id(0), pl.program_id(1)
    last_i = pl.num_programs(0) - 1
    last_k = pl.num_programs(1) - 1
    cut = h_ref.shape[0] - ns

    def swiglu_down(h):
        _side_cast(side_src, side_dst)
        g = _dot(h, wg_ref[...].astype(BF16))
        u = _dot(h, wu_ref[...].astype(BF16))
        return _dot((_silu(g) * u).astype(BF16), wd_ref[...].astype(BF16))

    @pl.when((i < last_i) & (k == 0))
    def _():
        h = _rms(x_ref[...], pre_ref[...]).astype(BF16)
        h_ref[...] = h
        acc_ref[...] = swiglu_down(h)

    @pl.when((i < last_i) & (k == last_k))
    def _():
        acc = acc_ref[...] + swiglu_down(h_ref[...])
        o_ref[...] = x_ref[...] + 0.5 * _rms(acc, post_ref[...])

    @pl.when((i == last_i) & (k == 0))
    def _():
        h_ref[:cut, :] = _rms(x_ref[:cut, :], pre_ref[...]).astype(BF16)
        xt = xs_ref[...] if split_in else x_ref[cut:, :]
        h_ref[cut:, :] = _rms(xt, pre_ref[...]).astype(BF16)
        acc_ref[...] = jnp.zeros_like(acc_ref)

    @pl.when(((k > 0) & (k < last_k)) | (i == last_i))
    def _():
        acc_ref[...] += swiglu_down(h_ref[...])

    @pl.when((i == last_i) & (k == last_k))
    def _():
        o_ref[:cut, :] = x_ref[:cut, :] + 0.5 * _rms(acc_ref[:cut, :], post_ref[...])
        xt = xs_ref[...] if split_in else x_ref[cut:, :]
        tail = xt + 0.5 * _rms(acc_ref[cut:, :], post_ref[...])
        if split_out:
            os_ref[...] = tail
        else:
            o_ref[cut:, :] = tail


def _ffn(x, xs, pre, wg, wu, wd, post, *, ns, split_out, side=()):
    split_in = xs is not None
    d = x.shape[1]
    m = x.shape[0] + (ns if split_in else 0)
    dff = wg.shape[1]
    tm, tf = (FFN_TILES_F32 if wg.dtype == F32 else FFN_TILES_BF16)
    assert m % tm == 0 and dff % tf == 0 and 0 < ns < tm and ns % BF16_ROWS == 0
    nk = dff // tf
    side_in, side_out, side_shapes, side_plan = _side_specs(
        side, (m // tm) * nk, lambda i, k: i * nk + k)
    tok = pl.BlockSpec((tm, d), lambda i, k: (i, 0))
    smp = pl.BlockSpec((ns, d), lambda i, k: (0, 0))
    vec = pl.BlockSpec((1, d), lambda i, k: (0, 0))
    in_specs = [tok] + ([smp] if split_in else []) + [
        vec,
        pl.BlockSpec((d, tf), lambda i, k: (0, k)),
        pl.BlockSpec((d, tf), lambda i, k: (0, k)),
        pl.BlockSpec((tf, d), lambda i, k: (k, 0)),
        vec,
    ] + side_in
    if split_out:
        out_shape = [jax.ShapeDtypeStruct((m - ns, d), F32), jax.ShapeDtypeStruct((ns, d), F32)]
        out_specs = [tok, smp]
    else:
        out_shape = [jax.ShapeDtypeStruct((m, d), F32)]
        out_specs = [tok]
    args = [x] + ([xs] if split_in else []) + [pre, wg, wu, wd, post] + list(side)
    sequential_rows = split_out or bool(side)
    return pl.pallas_call(
        functools.partial(_ffn_body, ns=ns, split_in=split_in, split_out=split_out,
                          side_plan=side_plan),
        out_shape=out_shape + side_shapes,
        grid=(m // tm, nk),
        in_specs=in_specs,
        out_specs=out_specs + side_out,
        scratch_shapes=[pltpu.VMEM((tm, d), BF16)],
        compiler_params=_cparams("arbitrary" if sequential_rows else "parallel", "arbitrary"),
        name="ffn",
    )(*args)


def _inproj_body(x_ref, g_ref, w_ref, o_ref, h_ref):
    j = pl.program_id(1)

    @pl.when(j == 0)
    def _():
        h = _rms(x_ref[...], g_ref[...]).astype(BF16)
        h_ref[...] = h
        o_ref[...] = _dot(h, w_ref[...])

    @pl.when(j > 0)
    def _():
        o_ref[...] = _dot(h_ref[...], w_ref[...])


def _inproj(x, g, w):
    m, d = x.shape
    n = w.shape[1]
    tm, tn = IN_ROW_TILE, IN_TILE
    assert m % tm == 0
    return pl.pallas_call(
        _inproj_body,
        out_shape=jax.ShapeDtypeStruct((m, n), F32),
        grid=(m // tm, n // tn),
        in_specs=[
            pl.BlockSpec((tm, d), lambda i, j: (i, 0)),
            pl.BlockSpec((1, d), lambda i, j: (0, 0)),
            pl.BlockSpec((d, tn), lambda i, j: (0, j)),
        ],
        out_specs=pl.BlockSpec((tm, tn), lambda i, j: (i, j)),
        scratch_shapes=[pltpu.VMEM((tm, d), BF16)],
        compiler_params=_cparams("parallel", "arbitrary"),
        name="inproj",
    )(x, g, w)


def _s5_tables(lam_re, lam_im, log_dt, b_re, b_im, c_re, c_im):
    g, p = lam_re.shape
    n = b_re.shape[-1]
    gpt = LANES // n
    nv = g // gpt
    dt = jnp.exp(log_dt)[:, None]
    er = jnp.exp(lam_re * dt)
    th = lam_im * dt
    a_re, a_im = er * jnp.cos(th), er * jnp.sin(th)
    den = lam_re * lam_re + lam_im * lam_im
    k_re = ((a_re - 1.0) * lam_re + a_im * lam_im) / den
    k_im = (a_im * lam_re - (a_re - 1.0) * lam_im) / den
    bb_re = k_re[..., None] * b_re - k_im[..., None] * b_im
    bb_im = k_re[..., None] * b_im + k_im[..., None] * b_re
    eye = jnp.eye(gpt, dtype=F32)

    def in_blk(bb):
        t = bb.reshape(nv, gpt, p, n)
        return jnp.einsum("ab,vapn->vanbp", eye, t).reshape(nv, gpt * n, gpt * p)

    def out_blk(cc):
        t = cc.reshape(nv, gpt, n, p)
        return jnp.einsum("ab,vanp->vapbn", eye, t).reshape(nv, gpt * p, gpt * n)

    w_in = jnp.concatenate([in_blk(bb_re), in_blk(bb_im)], axis=-1)
    w_out = jnp.concatenate([out_blk(c_re), -out_blk(c_im)], axis=1)
    return w_in, w_out, a_re.reshape(nv, 1, gpt * p), a_im.reshape(nv, 1, gpt * p)


def _cmul(ar, ai, br, bi):
    return ar * br - ai * bi, ar * bi + ai * br


def _s5_prompt_body(*refs, seg, side_plan):
    nside = len(side_plan)
    u_ref, w_ref, cm_ref, ar_ref, ai_ref, d_ref = refs[:6]
    z_ref, hl_ref = refs[6 + nside:8 + nside]
    buh_ref, ut_ref, pad_ref = refs[-3:]
    _side_cast(refs[6:6 + nside], refs[8 + nside:-3])
    nseg = SUBLANES
    nc = w_ref.shape[2] // LANES
    hc = nc // 2
    pitch = pad_ref.shape[0] // nseg

    def gather(t, _):
        ut_ref[pl.ds(pl.multiple_of(t * nseg, nseg), nseg), :] = (
            pad_ref[pl.ds(t, nseg, stride=pitch), :])
        return _

    def scatter(t, _):
        pad_ref[pl.ds(t, nseg, stride=pitch), :] = (
            ut_ref[pl.ds(pl.multiple_of(t * nseg, nseg), nseg), :])
        return _

    for j in range(nseg):
        pad_ref[j * pitch:j * pitch + seg, :] = u_ref[j * seg:(j + 1) * seg, :]
    lax.fori_loop(0, seg, gather, None, unroll=S5_SCAN_UNROLL)

    nblk = nseg
    steps = seg // nseg

    def project_in(b):
        blk = slice(b * seg, (b + 1) * seg)
        bu = _dot(ut_ref[blk, :].astype(BF16), w_ref[0])
        for c in range(nc):
            buh_ref[c, blk, :] = bu[:, c * LANES:(c + 1) * LANES]

    def project_out(b):
        blk = slice(b * seg, (b + 1) * seg)
        y = d_ref[...] * ut_ref[blk, :]
        for c in range(nc):
            y = y + _dot(buh_ref[c, blk, :].astype(BF16), cm_ref[0, c * LANES:(c + 1) * LANES, :])
        ut_ref[blk, :] = jax.nn.gelu(y)

    ar = [jnp.broadcast_to(ar_ref[0, :, c * LANES:(c + 1) * LANES], (nseg, LANES)) for c in range(hc)]
    ai = [jnp.broadcast_to(ai_ref[0, :, c * LANES:(c + 1) * LANES], (nseg, LANES)) for c in range(hc)]

    def advance(t, hs, store):
        rows = slice(t * nseg, (t + 1) * nseg)
        out = []
        for c in range(hc):
            pr, pi = _cmul(ar[c], ai[c], hs[2 * c], hs[2 * c + 1])
            out += [pr + buh_ref[c, rows, :], pi + buh_ref[hc + c, rows, :]]
        if store:
            for c in range(hc):
                buh_ref[c, rows, :] = out[2 * c]
                buh_ref[hc + c, rows, :] = out[2 * c + 1]
        return out

    zeros = [jnp.zeros((nseg, LANES), F32) for _ in range(2 * hc)]
    hs = zeros
    project_in(0)
    for b in range(nblk):
        if b + 1 < nblk:
            project_in(b + 1)
        for t in range(b * steps, (b + 1) * steps):
            hs = advance(t, hs, store=False)
    ends = hs

    pw = [(ar[c], ai[c]) for c in range(hc)]
    for _ in range(seg.bit_length() - 1):
        pw = [_cmul(r, i, r, i) for r, i in pw]

    row = lax.broadcasted_iota(jnp.int32, (nseg, LANES), 0)
    init = list(zeros)
    for j in range(1, nseg):
        for c in range(hc):
            pr, pi = _cmul(pw[c][0], pw[c][1], init[2 * c], init[2 * c + 1])
            nr = pltpu.roll(pr + ends[2 * c], 1, 0)
            ni = pltpu.roll(pi + ends[2 * c + 1], 1, 0)
            init[2 * c] = jnp.where(row == j, nr, init[2 * c])
            init[2 * c + 1] = jnp.where(row == j, ni, init[2 * c + 1])

    hs = init
    for b in range(nblk):
        for t in range(b * steps, (b + 1) * steps):
            hs = advance(t, hs, store=True)
        if b > 0:
            project_out(b - 1)
    project_out(nblk - 1)
    for c in range(hc):
        hl_ref[0, 0, :, c * LANES:(c + 1) * LANES] = hs[2 * c]
        hl_ref[0, 0, :, (hc + c) * LANES:(hc + c + 1) * LANES] = hs[2 * c + 1]

    lax.fori_loop(0, seg, scatter, None, unroll=S5_SCAN_UNROLL)
    for j in range(nseg):
        z_ref[j * seg:(j + 1) * seg, :] = pad_ref[j * pitch:j * pitch + seg, :]


def _s5_prompt(proj, w_in, w_out, a_re, a_im, d_skip, *, batch, seq, rows_total, side=()):
    nv = w_in.shape[0]
    sw = w_in.shape[2]
    seg = seq // SUBLANES
    assert seg * SUBLANES == seq and seg & (seg - 1) == 0
    side_in, side_out, side_shapes, side_plan = _side_specs(
        side, batch * nv, lambda b, v: b * nv + v)
    body = functools.partial(_s5_prompt_body, seg=seg, side_plan=side_plan)
    return pl.pallas_call(
        body,
        out_shape=[jax.ShapeDtypeStruct((rows_total, nv * LANES), F32),
                   jax.ShapeDtypeStruct((batch, nv, SUBLANES, sw), F32)] + side_shapes,
        grid=(batch, nv),
        in_specs=[
            pl.BlockSpec((seq, LANES), lambda b, v: (b, v)),
            pl.BlockSpec((1, LANES, sw), lambda b, v: (v, 0, 0)),
            pl.BlockSpec((1, sw, LANES), lambda b, v: (v, 0, 0)),
            pl.BlockSpec((1, 1, sw // 2), lambda b, v: (v, 0, 0)),
            pl.BlockSpec((1, 1, sw // 2), lambda b, v: (v, 0, 0)),
            pl.BlockSpec((1, LANES), lambda b, v: (0, v)),
        ] + side_in,
        out_specs=[pl.BlockSpec((seq, LANES), lambda b, v: (b, v)),
                   pl.BlockSpec((1, 1, SUBLANES, sw), lambda b, v: (b, v, 0, 0))] + side_out,
        scratch_shapes=[pltpu.VMEM((sw // LANES, seq, LANES), F32),
                        pltpu.VMEM((seq, LANES), F32),
                        pltpu.VMEM((SUBLANES * (seg + SUBLANES), LANES), F32)],
        compiler_params=_cparams(*(("arbitrary",) * 2 if side else ("parallel",) * 2)),
        name="s5_prompt",
    )(proj, w_in, w_out, a_re, a_im, d_skip, *side)


def _s5_sample_body(u_ref, hre_ref, him_ref, w_ref, cm_ref, ar_ref, ai_ref, d_ref,
                    z_ref, ore_ref, oim_ref):
    half = ar_ref.shape[2]
    u = u_ref[...]
    bu = _dot_hi(u, w_ref[0])
    pr, pi = _cmul(ar_ref[0], ai_ref[0], hre_ref[...], him_ref[...])
    hr = pr + bu[:, :half]
    hi = pi + bu[:, half:]
    ore_ref[...] = hr
    oim_ref[...] = hi
    y = d_ref[...] * u + _dot_hi(hr, cm_ref[0, :half, :]) + _dot_hi(hi, cm_ref[0, half:, :])
    z_ref[...] = jax.nn.gelu(y)


def _s5_sample(proj, h_re, h_im, w_in, w_out, a_re, a_im, d_skip, *, row0):
    nv = w_in.shape[0]
    sw = w_in.shape[2]
    ns = h_re.shape[0]
    rb = row0 // ns
    assert rb * ns == row0
    st = pl.BlockSpec((ns, sw // 2), lambda v: (0, v))
    return pl.pallas_call(
        _s5_sample_body,
        out_shape=(jax.ShapeDtypeStruct((ns, nv * LANES), F32),
                   jax.ShapeDtypeStruct(h_re.shape, F32),
                   jax.ShapeDtypeStruct(h_im.shape, F32)),
        grid=(nv,),
        in_specs=[
            pl.BlockSpec((ns, LANES), lambda v: (rb, v)),
            st, st,
            pl.BlockSpec((1, LANES, sw), lambda v: (v, 0, 0)),
            pl.BlockSpec((1, sw, LANES), lambda v: (v, 0, 0)),
            pl.BlockSpec((1, 1, sw // 2), lambda v: (v, 0, 0)),
            pl.BlockSpec((1, 1, sw // 2), lambda v: (v, 0, 0)),
            pl.BlockSpec((1, LANES), lambda v: (0, v)),
        ],
        out_specs=(pl.BlockSpec((ns, LANES), lambda v: (0, v)), st, st),
        compiler_params=_cparams("parallel"),
        name="s5_sample",
    )(proj, h_re, h_im, w_in, w_out, a_re, a_im, d_skip)


def _hg_span_matrices(ch):
    t = np.arange(ch)[:, None]
    r = np.arange(ch)[None, :]
    mats = [r <= t, r > t]
    for v in range(1, ch.bit_length() - 1):
        base = (t >> v) << v
        upper = ((t >> v) & 1) == 1
        mats.append(np.where(upper, (r >= base) & (r <= t), (r > t) & (r < base + (1 << v))))
    mm = np.stack(mats).astype(np.float32)
    return jnp.asarray(np.concatenate([mm, mm], axis=-1), BF16)


def _hg_prompt_body(*refs, dk, side_plan):
    nside = len(side_plan)
    q_ref, f_ref, i_ref, og_ref, lb_ref, gn_ref, mm_ref = refs[:7]
    o_ref, sfin_ref = refs[7 + nside:9 + nside]
    st_ref = refs[-1]
    _side_cast(refs[7:7 + nside], refs[9 + nside:-1])
    c = pl.program_id(1)
    nh = st_ref.shape[0]
    ch = mm_ref.shape[1]
    nsub = q_ref.shape[0] // ch
    width = q_ref.shape[1]
    nlev = ch.bit_length() - 1

    @pl.when(c == 0)
    def _():
        st_ref[...] = jnp.zeros_like(st_ref)

    lb = lb_ref[...]
    f_all = lb + (1.0 - lb) * jax.nn.sigmoid(f_ref[...])
    kk_all = 1.0 - f_all
    qs_all = _silu(q_ref[...])
    logf_pieces = _split3(jnp.log2(f_all))[:2]
    ib_all = i_ref[...].astype(BF16)
    pairs = [slice(2 * j * dk, 2 * (j + 1) * dk) for j in range(nh // 2)]
    heads = [slice(h * dk, (h + 1) * dk) for h in range(nh)]
    r = lax.broadcasted_iota(jnp.int32, (ch, 2 * ch), 0)
    s = lax.broadcasted_iota(jnp.int32, (ch, 2 * ch), 1) & (ch - 1)
    xr = jnp.where(r > s, r ^ s, 0)
    zero_k = jnp.zeros((ch, dk), BF16)
    zero_s = jnp.zeros((dk, dk), BF16)

    def block_diag(a, b, z):
        return jnp.concatenate([jnp.concatenate([a, z], axis=1),
                                jnp.concatenate([z, b], axis=1)], axis=0)

    def pair_scores(lhs, x):
        return _dot_nt(lhs, block_diag(x[:, :dk], x[:, dk:], zero_k))

    def upper_runs(v):
        m = 1 << v
        return [(b0, b0 + m) for b0 in range(m, ch, 2 * m)]

    def mix_rows(v, qs, kk):
        m = 1 << v
        if m >= SUBLANES:
            return jnp.concatenate(
                [(qs if (b0 // m) & 1 else kk)[b0:b0 + m] for b0 in range(0, ch, m)], axis=0)
        pick = ((lax.broadcasted_iota(jnp.int32, (1, SUBLANES, width), 1) >> v) & 1) == 1
        shape3 = (ch // SUBLANES, SUBLANES, width)
        return jnp.where(pick, qs.reshape(shape3), kk.reshape(shape3)).reshape(ch, width)

    staged = []
    for sub in range(nsub):
        rows = slice(sub * ch, (sub + 1) * ch)
        kk, qs = kk_all[rows], qs_all[rows]
        logf2 = jnp.concatenate([p[rows] for p in logf_pieces], axis=0)

        def decay(idx):
            return jnp.exp2(_dot(mm_ref[idx], logf2))

        eg = decay(0)
        qg = (qs * eg).astype(BF16)
        dec = eg[ch - 1:ch, :]
        kh = (kk * decay(1)).astype(BF16)
        lev = [mix_rows(0, qs * f_all[rows], kk).astype(BF16)]
        lev += [(decay(1 + v) * mix_rows(v, qs, kk)).astype(BF16) for v in range(1, nlev)]
        qb, kb = qs.astype(BF16), kk.astype(BF16)
        atts = []
        for sl in pairs:
            att = jnp.where(r == s, pair_scores(qb[:, sl], kb[:, sl]), 0.0)
            for v in range(nlev):
                x = lev[v][:, sl]
                m = 1 << v
                if m >= BF16_ROWS:
                    runs = upper_runs(v)
                    p = pair_scores(jnp.concatenate([x[a:b] for a, b in runs], axis=0), x)
                    parts = []
                    for n in range(len(runs)):
                        parts += [jnp.zeros((m, 2 * ch), F32), p[n * m:(n + 1) * m]]
                    p = jnp.concatenate(parts, axis=0)
                else:
                    p = pair_scores(x, x)
                att = jnp.where((xr >> v) == 1, p, att)
            atts.append(att.astype(BF16))
        staged.append((rows, qg, kh, dec, atts))

    for rows, qg, kh, dec, atts in staged:
        ib = ib_all[rows]
        for j, sl in enumerate(pairs):
            st2 = block_diag(st_ref[2 * j].astype(BF16), st_ref[2 * j + 1].astype(BF16), zero_s)
            ib2 = block_diag(ib[:, sl][:, :dk], ib[:, sl][:, dk:], zero_k)
            o = _dot_nt(qg[:, sl], st2) + _dot(atts[j], ib2)
            for e in range(2):
                hs = heads[2 * j + e]
                og = og_ref[rows, hs]
                o_ref[rows, hs] = _rms(o[:, e * dk:(e + 1) * dk], gn_ref[:, hs]) * _silu(og)
        for h, hs in enumerate(heads):
            st_ref[h] = st_ref[h] * dec[:, hs] + _dot_tn(ib[:, hs], kh[:, hs])

    @pl.when(c == pl.num_programs(1) - 1)
    def _():
        sfin_ref[0] = st_ref[...]


def _hg_prompt(proj, lb, gn, *, batch, seq, rows_total, nh, col0, side=()):
    width = lb.shape[1]
    dk = width // nh
    ch = HG_CHUNK * HG_CHUNKS_PER_STEP
    nchunk = seq // ch
    cb = col0 // width
    assert cb * width == col0 and nchunk * ch == seq

    def tok(k):
        return pl.BlockSpec((ch, width), lambda b, c: (b * nchunk + c, cb + k))

    vec = pl.BlockSpec((1, width), lambda b, c: (0, 0))
    mm = _hg_span_matrices(HG_CHUNK)
    side_in, side_out, side_shapes, side_plan = _side_specs(
        side, batch * nchunk, lambda b, c: b * nchunk + c)
    return pl.pallas_call(
        functools.partial(_hg_prompt_body, dk=dk, side_plan=side_plan),
        out_shape=[jax.ShapeDtypeStruct((rows_total, width), F32),
                   jax.ShapeDtypeStruct((batch, nh, dk, dk), F32)] + side_shapes,
        grid=(batch, nchunk),
        in_specs=[tok(0), tok(1), tok(2), tok(3), vec, vec,
                  pl.BlockSpec(mm.shape, lambda b, c: (0, 0, 0))] + side_in,
        out_specs=[pl.BlockSpec((ch, width), lambda b, c: (b * nchunk + c, 0)),
                   pl.BlockSpec((1, nh, dk, dk), lambda b, c: (b, 0, 0, 0))] + side_out,
        scratch_shapes=[pltpu.VMEM((nh, dk, dk), F32)],
        compiler_params=_cparams("arbitrary" if side else "parallel", "arbitrary"),
        name="hgrn_prompt",
    )(proj, proj, proj, proj, lb, gn, mm, *side)


def _split3(x):
    p1 = x.astype(BF16)
    r1 = x - p1.astype(F32)
    p2 = r1.astype(BF16)
    p3 = (r1 - p2.astype(F32)).astype(BF16)
    return p1, p2, p3


def _hg_sample_body(q_ref, f_ref, i_ref, og_ref, lb_ref, gn_ref, s0_ref,
                    hg_ref, s_ref, fq_ref, *, dk):
    step = pl.program_id(0)
    nt = q_ref.shape[0]
    tb, nh = s0_ref.shape[0], s0_ref.shape[1]
    npiece = fq_ref.shape[1] // nt

    @pl.when(step == 0)
    def _():
        lb = lb_ref[...]
        f = lb + (1.0 - lb) * jax.nn.sigmoid(f_ref[...])
        qs = _silu(q_ref[...])
        for src, base in ((f, 0), (qs, nh)):
            for h in range(nh):
                t = src[:, h * dk:(h + 1) * dk].T
                for p, piece in enumerate(_split3(t)[:npiece]):
                    fq_ref[(base + h) * dk:(base + h + 1) * dk, p * nt:(p + 1) * nt] = piece

    tok = lax.broadcasted_iota(jnp.int32, (npiece * nt, 2 * dk), 0) & (nt - 1)
    second = lax.broadcasted_iota(jnp.int32, (npiece * nt, 2 * dk), 1) >= dk
    for j0 in range(0, tb, 2):
        n0 = step * tb + j0
        onehot = (tok == jnp.where(second, n0 + 1, n0)).astype(BF16)
        fq2 = _dot(fq_ref[...], onehot)
        for e in range(2):
            j, n = j0 + e, n0 + e
            fq = fq2[:, e * dk:(e + 1) * dk]
            irow = i_ref[pl.ds(n, 1), :]
            ogrow = og_ref[pl.ds(n, 1), :]
            for h in range(nh):
                sl = slice(h * dk, (h + 1) * dk)
                fb = fq[h * dk:(h + 1) * dk, :]
                qb = fq[(nh + h) * dk:(nh + h + 1) * dk, :]
                s1 = fb * s0_ref[j, h] + (1.0 - fb) * irow[:, sl]
                s_ref[j, h] = s1
                o = jnp.sum(qb * s1, axis=0, keepdims=True)
                hg_ref[j:j + 1, sl] = _rms(o, gn_ref[:, sl]) * _silu(ogrow[:, sl])


def _hg_sample(proj, lb, gn, s0, *, row0, nh, col0):
    width = lb.shape[1]
    dk = width // nh
    ns = s0.shape[0]
    tb = SAMPLE_TOKENS_PER_STEP
    rb, cb = row0 // ns, col0 // width
    assert rb * ns == row0 and cb * width == col0 and ns % tb == 0
    assert ns & (ns - 1) == 0 and tb % 2 == 0 and dk == LANES

    def tok(k):
        return pl.BlockSpec((ns, width), lambda t: (rb, cb + k))

    vec = pl.BlockSpec((1, width), lambda t: (0, 0))
    sspec = pl.BlockSpec((tb, nh, dk, dk), lambda t: (t, 0, 0, 0))
    return pl.pallas_call(
        functools.partial(_hg_sample_body, dk=dk),
        out_shape=(jax.ShapeDtypeStruct((ns, width), F32),
                   jax.ShapeDtypeStruct(s0.shape, F32)),
        grid=(ns // tb,),
        in_specs=[tok(0), tok(1), tok(2), tok(3), vec, vec, sspec],
        out_specs=(pl.BlockSpec((tb, width), lambda t: (t, 0)), sspec),
        scratch_shapes=[pltpu.VMEM((2 * nh * dk, 2 * ns), BF16)],
        compiler_params=_cparams("arbitrary"),
        name="hgrn_sample",
    )(proj, proj, proj, proj, lb, gn, s0)


def _merge_body(z_ref, zs_ref, hg_ref, hgs_ref, gs0_ref, gs1_ref, gh0_ref, gh1_ref, x_ref,
                wglu_ref, bglu_ref, wbs_ref, wbh_ref, wout_ref, post_ref, o_ref, *, ns):
    i = pl.program_id(0)
    last_i = pl.num_programs(0) - 1
    cut = x_ref.shape[0] - ns
    half = gs0_ref.shape[1]

    def rows(z, hg, sl):
        s5o = z * jax.nn.sigmoid(_dot(z.astype(BF16), wglu_ref[...]) + bglu_ref[...])
        a = _dot(s5o.astype(BF16), wbs_ref[...])
        b = _dot(hg.astype(BF16), wbh_ref[...])
        m0 = (jax.nn.sigmoid(gs0_ref[sl, :]) * a[:, :half]
              + jax.nn.sigmoid(gh0_ref[sl, :]) * b[:, :half])
        m1 = (jax.nn.sigmoid(gs1_ref[sl, :]) * a[:, half:]
              + jax.nn.sigmoid(gh1_ref[sl, :]) * b[:, half:])
        mix = _dot(m0.astype(BF16), wout_ref[:half, :]) + _dot(m1.astype(BF16), wout_ref[half:, :])
        o_ref[sl, :] = x_ref[sl, :] + _rms(mix, post_ref[...])

    @pl.when(i < last_i)
    def _():
        rows(z_ref[...], hg_ref[...], slice(None))

    @pl.when(i == last_i)
    def _():
        rows(z_ref[:cut, :], hg_ref[:cut, :], slice(0, cut))
        rows(zs_ref[...], hgs_ref[...], slice(cut, None))


def _merge(z, zs, hg, hgs, proj, x, wglu, bglu, wbs, wbh, wout, post, *, col0, tm):
    m, d = x.shape
    w = z.shape[1]
    ns = zs.shape[0]
    cb = col0 // w
    assert cb * w == col0 and d == 2 * w and m % tm == 0 and 0 < ns < tm

    def gate(k):
        return pl.BlockSpec((tm, w), lambda i: (i, cb + k))

    def const(shape):
        return pl.BlockSpec(shape, lambda i: (0, 0), pipeline_mode=pl.Buffered(1))

    tok = pl.BlockSpec((tm, w), lambda i: (i, 0))
    return pl.pallas_call(
        functools.partial(_merge_body, ns=ns),
        out_shape=jax.ShapeDtypeStruct((m, d), F32),
        grid=(m // tm,),
        in_specs=[
            tok, const((ns, w)), tok, const((ns, w)),
            gate(0), gate(1), gate(2), gate(3),
            pl.BlockSpec((tm, d), lambda i: (i, 0)),
            const((w, w)), const((1, w)), const((w, d)), const((w, d)), const((d, d)), const((1, d)),
        ],
        out_specs=pl.BlockSpec((tm, d), lambda i: (i, 0)),
        compiler_params=_cparams("parallel"),
        name="merge",
    )(z, zs, hg, hgs, proj, proj, proj, proj, x, wglu, bglu, wbs, wbh, wout, post)


MERGE_ROW_TILE = 416


def kernel(x_prompt, x_sample, state_s5_re, state_s5_im, state_hgrn, ffn1_pre_norm, ffn1_w_gate, ffn1_w_up, ffn1_w_down, ffn1_post_norm, mix_pre_norm, w_in, s5_lambda_re, s5_lambda_im, s5_log_dt, s5_b_re, s5_b_im, s5_c_re, s5_c_im, s5_d, s5_w_glu, s5_b_glu, hgrn_lb_logits, hgrn_out_norm, w_branch_s5, w_branch_hgrn, w_out, mix_post_norm, ffn2_pre_norm, ffn2_w_gate, ffn2_w_up, ffn2_w_down, ffn2_post_norm):
    depth = ffn1_w_gate.shape[0]
    assert depth == 1
    batch, seq, d = x_prompt.shape
    ns = x_sample.shape[0]
    assert x_sample.shape[1] == 1
    g, p = s5_lambda_re.shape[1:]
    nh, dk = state_hgrn.shape[2], state_hgrn.shape[3]
    s5w = s5_d.shape[1]
    hgw = nh * dk
    mp = batch * seq
    m = mp + ns

    bf = lambda a: a[0].astype(BF16)
    row = lambda a: a.reshape(1, -1)

    lb_all = jnp.cumsum(jax.nn.softmax(hgrn_lb_logits.astype(F32), axis=0), axis=0)
    lb = lb_all[0].reshape(1, hgw)

    x, w_in_b = _ffn(
        x_prompt.reshape(mp, d), x_sample.reshape(ns, d), row(ffn1_pre_norm),
        ffn1_w_gate[0], ffn1_w_up[0], ffn1_w_down[0], row(ffn1_post_norm),
        ns=ns, split_out=False, side=(w_in[0],))
    proj = _inproj(x, row(mix_pre_norm), w_in_b)

    tw_in, tw_out, a_re, a_im = _s5_tables(
        s5_lambda_re[0], s5_lambda_im[0], s5_log_dt[0], s5_b_re[0], s5_b_im[0],
        s5_c_re[0], s5_c_im[0])
    d_skip = row(s5_d)
    z, hlast, wg2_b, wu2_b, wd2_b, wglu_b, wbs_b, wbh_b, wout_b = _s5_prompt(
        proj, tw_in.astype(BF16), tw_out.astype(BF16), a_re, a_im, d_skip,
        batch=batch, seq=seq, rows_total=mp,
        side=(ffn2_w_gate[0], ffn2_w_up[0], ffn2_w_down[0], s5_w_glu[0], w_branch_s5[0],
              w_branch_hgrn[0], w_out[0]))
    zs, s_re, s_im = _s5_sample(proj, state_s5_re[0].reshape(ns, g * p),
                                state_s5_im[0].reshape(ns, g * p),
                                tw_in, tw_out, a_re, a_im, d_skip, row0=mp)
    half = hlast.shape[-1] // 2
    p_re = hlast[:, :, SUBLANES - 1, :half].reshape(1, batch, g, p)
    p_im = hlast[:, :, SUBLANES - 1, half:].reshape(1, batch, g, p)

    gn = row(hgrn_out_norm)
    hg, st_p = _hg_prompt(proj, lb, gn, batch=batch, seq=seq, rows_total=mp, nh=nh, col0=s5w)
    hgs, st_s = _hg_sample(proj, lb, gn, state_hgrn[0], row0=mp, nh=nh, col0=s5w)

    x = _merge(z, zs, hg, hgs, proj, x, wglu_b, row(s5_b_glu), wbs_b, wbh_b, wout_b,
               row(mix_post_norm), col0=s5w + 4 * hgw, tm=MERGE_ROW_TILE)
    yp, ys = _ffn(x, None, row(ffn2_pre_norm), wg2_b, wu2_b, wd2_b, row(ffn2_post_norm),
                  ns=ns, split_out=True)

    return (yp.reshape(batch, seq, d), ys.reshape(ns, 1, d),
            p_re, p_im, jnp.swapaxes(st_p, -1, -2)[None],
            s_re.reshape(1, ns, g, p), s_im.reshape(1, ns, g, p), st_s[None])
```

```python
import functools

import jax
import jax.numpy as jnp
import numpy as np
from jax import lax
from jax.experimental import pallas as pl
from jax.experimental.pallas import tpu as pltpu

F32 = jnp.float32
BF16 = jnp.bfloat16
EPS = 1e-6
HIGHEST = lax.Precision.HIGHEST

LANES = 128
SUBLANES = 8
VMEM_LIMIT = 60 * 1024 * 1024

FFN_TILES_BF16 = (832, 512)
FFN_TILES_F32 = (1040, 256)
IN_TILE = 1536
IN_ROW_TILE = 1040
S5_SCAN_UNROLL = 8
HG_CHUNK = 64
HG_CHUNKS_PER_STEP = 4
SAMPLE_TOKENS_PER_STEP = 8


def _cparams(*sem):
    return pltpu.CompilerParams(dimension_semantics=sem, vmem_limit_bytes=VMEM_LIMIT)


def _rms(x, g):
    return x * lax.rsqrt(jnp.mean(x * x, axis=-1, keepdims=True) + EPS) * g


def _silu(x):
    return x * jax.nn.sigmoid(x)


def _dot(a, b):
    return jnp.dot(a, b, preferred_element_type=F32)


def _dot_hi(a, b):
    return jnp.dot(a, b, preferred_element_type=F32, precision=HIGHEST)


def _dot_nt(a, b):
    return lax.dot_general(a, b, (((1,), (1,)), ((), ())), preferred_element_type=F32)


def _dot_tn(a, b):
    return lax.dot_general(a, b, (((0,), (0,)), ((), ())), preferred_element_type=F32)


BF16_ROWS = 16


def _side_specs(arrays, nsteps, step_of):
    in_specs, out_specs, out_shapes, plan = [], [], [], []
    for a in arrays:
        r, c = a.shape
        rows = next(t for t in range(BF16_ROWS, r + 1, BF16_ROWS) if r % t == 0 and r // t <= nsteps)
        nblk = r // rows
        every = nsteps // nblk

        def imap(*g, nblk=nblk, every=every):
            return (jnp.minimum(step_of(*g) // every, nblk - 1), 0)

        in_specs.append(pl.BlockSpec((rows, c), imap))
        out_specs.append(pl.BlockSpec((rows, c), imap))
        out_shapes.append(jax.ShapeDtypeStruct((r, c), BF16))
        plan.append((nblk, every))
    return in_specs, out_specs, out_shapes, tuple(plan)


def _side_cast(src_refs, dst_refs):
    for src, dst in zip(src_refs, dst_refs):
        dst[...] = src[...].astype(BF16)


def _ffn_body(*refs, ns, split_in, split_out, side_plan):
    refs = list(refs)
    nside = len(side_plan)
    x_ref = refs.pop(0)
    xs_ref = refs.pop(0) if split_in else None
    pre_ref, wg_ref, wu_ref, wd_ref, post_ref = refs[:5]
    side_src = refs[5:5 + nside]
    o_ref = refs[5 + nside]
    os_ref = refs[6 + nside] if split_out else None
    side_dst = refs[-1 - nside:-1]
    h_ref = refs[-1]
    acc_ref = o_ref
    i, k = pl.program_id(0), pl.program_id(1)
    last_i = pl.num_programs(0) - 1
    last_k = pl.num_programs(1) - 1
    cut = h_ref.shape[0] - ns

    def swiglu_down(h):
        _side_cast(side_src, side_dst)
        g = _dot(h, wg_ref[...].astype(BF16))
        u = _dot(h, wu_ref[...].astype(BF16))
        return _dot((_silu(g) * u).astype(BF16), wd_ref[...].astype(BF16))

    @pl.when((i < last_i) & (k == 0))
    def _():
        h = _rms(x_ref[...], pre_ref[...]).astype(BF16)
        h_ref[...] = h
        acc_ref[...] = swiglu_down(h)

    @pl.when((i < last_i) & (k == last_k))
    def _():
        acc = acc_ref[...] + swiglu_down(h_ref[...])
        o_ref[...] = x_ref[...] + 0.5 * _rms(acc, post_ref[...])

    @pl.when((i == last_i) & (k == 0))
    def _():
        h_ref[:cut, :] = _rms(x_ref[:cut, :], pre_ref[...]).astype(BF16)
        xt = xs_ref[...] if split_in else x_ref[cut:, :]
        h_ref[cut:, :] = _rms(xt, pre_ref[...]).astype(BF16)
        acc_ref[...] = jnp.zeros_like(acc_ref)

    @pl.when(((k > 0) & (k < last_k)) | (i == last_i))
    def _():
        acc_ref[...] += swiglu_down(h_ref[...])

    @pl.when((i == last_i) & (k == last_k))
    def _():
        o_ref[:cut, :] = x_ref[:cut, :] + 0.5 * _rms(acc_ref[:cut, :], post_ref[...])
        xt = xs_ref[...] if split_in else x_ref[cut:, :]
        tail = xt + 0.5 * _rms(acc_ref[cut:, :], post_ref[...])
        if split_out:
            os_ref[...] = tail
        else:
            o_ref[cut:, :] = tail


def _ffn(x, xs, pre, wg, wu, wd, post, *, ns, split_out, side=()):
    split_in = xs is not None
    d = x.shape[1]
    m = x.shape[0] + (ns if split_in else 0)
    dff = wg.shape[1]
    tm, tf = (FFN_TILES_F32 if wg.dtype == F32 else FFN_TILES_BF16)
    assert m % tm == 0 and dff % tf == 0 and 0 < ns < tm and ns % BF16_ROWS == 0
    nk = dff // tf
    side_in, side_out, side_shapes, side_plan = _side_specs(
        side, (m // tm) * nk, lambda i, k: i * nk + k)
    tok = pl.BlockSpec((tm, d), lambda i, k: (i, 0))
    smp = pl.BlockSpec((ns, d), lambda i, k: (0, 0))
    vec = pl.BlockSpec((1, d), lambda i, k: (0, 0))
    in_specs = [tok] + ([smp] if split_in else []) + [
        vec,
        pl.BlockSpec((d, tf), lambda i, k: (0, k)),
        pl.BlockSpec((d, tf), lambda i, k: (0, k)),
        pl.BlockSpec((tf, d), lambda i, k: (k, 0)),
        vec,
    ] + side_in
    if split_out:
        out_shape = [jax.ShapeDtypeStruct((m - ns, d), F32), jax.ShapeDtypeStruct((ns, d), F32)]
        out_specs = [tok, smp]
    else:
        out_shape = [jax.ShapeDtypeStruct((m, d), F32)]
        out_specs = [tok]
    args = [x] + ([xs] if split_in else []) + [pre, wg, wu, wd, post] + list(side)
    sequential_rows = split_out or bool(side)
    return pl.pallas_call(
        functools.partial(_ffn_body, ns=ns, split_in=split_in, split_out=split_out,
                          side_plan=side_plan),
        out_shape=out_shape + side_shapes,
        grid=(m // tm, nk),
        in_specs=in_specs,
        out_specs=out_specs + side_out,
        scratch_shapes=[pltpu.VMEM((tm, d), BF16)],
        compiler_params=_cparams("arbitrary" if sequential_rows else "parallel", "arbitrary"),
        name="ffn",
    )(*args)


def _inproj_body(x_ref, g_ref, w_ref, o_ref, h_ref):
    j = pl.program_id(1)

    @pl.when(j == 0)
    def _():
        h = _rms(x_ref[...], g_ref[...]).astype(BF16)
        h_ref[...] = h
        o_ref[...] = _dot(h, w_ref[...])

    @pl.when(j > 0)
    def _():
        o_ref[...] = _dot(h_ref[...], w_ref[...])


def _inproj(x, g, w):
    m, d = x.shape
    n = w.shape[1]
    tm, tn = IN_ROW_TILE, IN_TILE
    assert m % tm == 0
    return pl.pallas_call(
        _inproj_body,
        out_shape=jax.ShapeDtypeStruct((m, n), F32),
        grid=(m // tm, n // tn),
        in_specs=[
            pl.BlockSpec((tm, d), lambda i, j: (i, 0)),
            pl.BlockSpec((1, d), lambda i, j: (0, 0)),
            pl.BlockSpec((d, tn), lambda i, j: (0, j)),
        ],
        out_specs=pl.BlockSpec((tm, tn), lambda i, j: (i, j)),
        scratch_shapes=[pltpu.VMEM((tm, d), BF16)],
        compiler_params=_cparams("parallel", "arbitrary"),
        name="inproj",
    )(x, g, w)


def _s5_tables(lam_re, lam_im, log_dt, b_re, b_im, c_re, c_im):
    g, p = lam_re.shape
    n = b_re.shape[-1]
    gpt = LANES // n
    nv = g // gpt
    dt = jnp.exp(log_dt)[:, None]
    er = jnp.exp(lam_re * dt)
    th = lam_im * dt
    a_re, a_im = er * jnp.cos(th), er * jnp.sin(th)
    den = lam_re * lam_re + lam_im * lam_im
    k_re = ((a_re - 1.0) * lam_re + a_im * lam_im) / den
    k_im = (a_im * lam_re - (a_re - 1.0) * lam_im) / den
    bb_re = k_re[..., None] * b_re - k_im[..., None] * b_im
    bb_im = k_re[..., None] * b_im + k_im[..., None] * b_re
    eye = jnp.eye(gpt, dtype=F32)
    bb = jnp.stack([bb_re, bb_im]).reshape(2, nv, gpt, p, n)
    cc = jnp.stack([c_re, -c_im]).reshape(2, nv, gpt, n, p)
    w_in = jnp.einsum("ab,cvapn->vancbp", eye, bb).reshape(nv, gpt * n, 2 * gpt * p)
    w_out = jnp.einsum("ab,cvanp->vcapbn", eye, cc).reshape(nv, 2 * gpt * p, gpt * n)
    return w_in, w_out, a_re.reshape(nv, 1, gpt * p), a_im.reshape(nv, 1, gpt * p)


def _cmul(ar, ai, br, bi):
    return ar * br - ai * bi, ar * bi + ai * br


def _s5_prompt_body(*refs, seg, side_plan):
    nside = len(side_plan)
    u_ref, w_ref, cm_ref, ar_ref, ai_ref, d_ref = refs[:6]
    z_ref, hl_ref = refs[6 + nside:8 + nside]
    buh_ref, ut_ref, pad_ref = refs[-3:]
    nseg = SUBLANES
    nc = w_ref.shape[2] // LANES
    hc = nc // 2
    pitch = pad_ref.shape[0] // nseg
    nblk = nseg
    steps = seg // nseg

    def gather(b):
        for t in range(b * steps, (b + 1) * steps):
            ut_ref[t * nseg:(t + 1) * nseg, :] = pad_ref[pl.ds(t, nseg, stride=pitch), :]

    def scatter(b):
        for t in range(b * steps, (b + 1) * steps):
            pad_ref[pl.ds(t, nseg, stride=pitch), :] = ut_ref[t * nseg:(t + 1) * nseg, :]

    for j in range(nseg):
        pad_ref[j * pitch:j * pitch + seg, :] = u_ref[j * seg:(j + 1) * seg, :]

    def project_in(b):
        blk = slice(b * seg, (b + 1) * seg)
        bu = _dot(ut_ref[blk, :].astype(BF16), w_ref[0])
        for c in range(nc):
            buh_ref[c, blk, :] = bu[:, c * LANES:(c + 1) * LANES]

    def project_out(b):
        blk = slice(b * seg, (b + 1) * seg)
        y = d_ref[...] * ut_ref[blk, :]
        for c in range(nc):
            y = y + _dot(buh_ref[c, blk, :].astype(BF16), cm_ref[0, c * LANES:(c + 1) * LANES, :])
        ut_ref[blk, :] = jax.nn.gelu(y)

    ar = [jnp.broadcast_to(ar_ref[0, :, c * LANES:(c + 1) * LANES], (nseg, LANES)) for c in range(hc)]
    ai = [jnp.broadcast_to(ai_ref[0, :, c * LANES:(c + 1) * LANES], (nseg, LANES)) for c in range(hc)]

    def advance(t, hs, store):
        rows = slice(t * nseg, (t + 1) * nseg)
        out = []
        for c in range(hc):
            pr, pi = _cmul(ar[c], ai[c], hs[2 * c], hs[2 * c + 1])
            out += [pr + buh_ref[c, rows, :], pi + buh_ref[hc + c, rows, :]]
        if store:
            for c in range(hc):
                buh_ref[c, rows, :] = out[2 * c]
                buh_ref[hc + c, rows, :] = out[2 * c + 1]
        return out

    zeros = [jnp.zeros((nseg, LANES), F32) for _ in range(2 * hc)]
    hs = zeros
    gather(0)
    project_in(0)
    for b in range(nblk):
        if b + 1 < nblk:
            gather(b + 1)
            project_in(b + 1)
        for t in range(b * steps, (b + 1) * steps):
            hs = advance(t, hs, store=False)
    ends = hs

    pw = [(ar[c], ai[c]) for c in range(hc)]
    for _ in range(seg.bit_length() - 1):
        pw = [_cmul(r, i, r, i) for r, i in pw]

    row = lax.broadcasted_iota(jnp.int32, (nseg, LANES), 0)
    init = list(zeros)
    for j in range(1, nseg):
        for c in range(hc):
            pr, pi = _cmul(pw[c][0], pw[c][1], init[2 * c], init[2 * c + 1])
            nr = pltpu.roll(pr + ends[2 * c], 1, 0)
            ni = pltpu.roll(pi + ends[2 * c + 1], 1, 0)
            init[2 * c] = jnp.where(row == j, nr, init[2 * c])
            init[2 * c + 1] = jnp.where(row == j, ni, init[2 * c + 1])

    hs = init
    for b in range(nblk):
        for t in range(b * steps, (b + 1) * steps):
            hs = advance(t, hs, store=True)
        if b > 0:
            project_out(b - 1)
            scatter(b - 1)
    for c in range(hc):
        hl_ref[0, 0, :, c * LANES:(c + 1) * LANES] = hs[2 * c]
        hl_ref[0, 0, :, (hc + c) * LANES:(hc + c + 1) * LANES] = hs[2 * c + 1]
    _side_cast(refs[6:6 + nside], refs[8 + nside:-3])
    project_out(nblk - 1)
    scatter(nblk - 1)
    for j in range(nseg):
        z_ref[j * seg:(j + 1) * seg, :] = pad_ref[j * pitch:j * pitch + seg, :]


def _s5_prompt(proj, w_in, w_out, a_re, a_im, d_skip, *, batch, seq, rows_total, side=()):
    nv = w_in.shape[0]
    sw = w_in.shape[2]
    seg = seq // SUBLANES
    assert seg * SUBLANES == seq and seg & (seg - 1) == 0
    side_in, side_out, side_shapes, side_plan = _side_specs(
        side, batch * nv, lambda b, v: b * nv + v)
    body = functools.partial(_s5_prompt_body, seg=seg, side_plan=side_plan)
    return pl.pallas_call(
        body,
        out_shape=[jax.ShapeDtypeStruct((rows_total, nv * LANES), F32),
                   jax.ShapeDtypeStruct((batch, nv, SUBLANES, sw), F32)] + side_shapes,
        grid=(batch, nv),
        in_specs=[
            pl.BlockSpec((seq, LANES), lambda b, v: (b, v)),
            pl.BlockSpec((1, LANES, sw), lambda b, v: (v, 0, 0)),
            pl.BlockSpec((1, sw, LANES), lambda b, v: (v, 0, 0)),
            pl.BlockSpec((1, 1, sw // 2), lambda b, v: (v, 0, 0)),
            pl.BlockSpec((1, 1, sw // 2), lambda b, v: (v, 0, 0)),
            pl.BlockSpec((1, LANES), lambda b, v: (0, v)),
        ] + side_in,
        out_specs=[pl.BlockSpec((seq, LANES), lambda b, v: (b, v)),
                   pl.BlockSpec((1, 1, SUBLANES, sw), lambda b, v: (b, v, 0, 0))] + side_out,
        scratch_shapes=[pltpu.VMEM((sw // LANES, seq, LANES), F32),
                        pltpu.VMEM((seq, LANES), F32),
                        pltpu.VMEM((SUBLANES * (seg + SUBLANES), LANES), F32)],
        compiler_params=_cparams(*(("arbitrary",) * 2 if side else ("parallel",) * 2)),
        name="s5_prompt",
    )(proj, w_in, w_out, a_re, a_im, d_skip, *side)


def _s5_sample_body(u_ref, hre_ref, him_ref, w_ref, cm_ref, ar_ref, ai_ref, d_ref,
                    z_ref, ore_ref, oim_ref):
    half = ar_ref.shape[2]
    u = u_ref[...]
    bu = _dot_hi(u, w_ref[0])
    pr, pi = _cmul(ar_ref[0], ai_ref[0], hre_ref[...], him_ref[...])
    hr = pr + bu[:, :half]
    hi = pi + bu[:, half:]
    ore_ref[...] = hr
    oim_ref[...] = hi
    y = d_ref[...] * u + _dot_hi(hr, cm_ref[0, :half, :]) + _dot_hi(hi, cm_ref[0, half:, :])
    z_ref[...] = jax.nn.gelu(y)


def _s5_sample(proj, h_re, h_im, w_in, w_out, a_re, a_im, d_skip, *, row0):
    nv = w_in.shape[0]
    sw = w_in.shape[2]
    ns = h_re.shape[0]
    rb = row0 // ns
    assert rb * ns == row0
    st = pl.BlockSpec((ns, sw // 2), lambda v: (0, v))
    return pl.pallas_call(
        _s5_sample_body,
        out_shape=(jax.ShapeDtypeStruct((ns, nv * LANES), F32),
                   jax.ShapeDtypeStruct(h_re.shape, F32),
                   jax.ShapeDtypeStruct(h_im.shape, F32)),
        grid=(nv,),
        in_specs=[
            pl.BlockSpec((ns, LANES), lambda v: (rb, v)),
            st, st,
            pl.BlockSpec((1, LANES, sw), lambda v: (v, 0, 0)),
            pl.BlockSpec((1, sw, LANES), lambda v: (v, 0, 0)),
            pl.BlockSpec((1, 1, sw // 2), lambda v: (v, 0, 0)),
            pl.BlockSpec((1, 1, sw // 2), lambda v: (v, 0, 0)),
            pl.BlockSpec((1, LANES), lambda v: (0, v)),
        ],
        out_specs=(pl.BlockSpec((ns, LANES), lambda v: (0, v)), st, st),
        compiler_params=_cparams("parallel"),
        name="s5_sample",
    )(proj, h_re, h_im, w_in, w_out, a_re, a_im, d_skip)


def _hg_span_matrices(ch):
    t = np.arange(ch)[:, None]
    r = np.arange(ch)[None, :]
    mats = [r <= t, r > t]
    for v in range(1, ch.bit_length() - 1):
        base = (t >> v) << v
        upper = ((t >> v) & 1) == 1
        mats.append(np.where(upper, (r >= base) & (r <= t), (r > t) & (r < base + (1 << v))))
    mm = np.stack(mats).astype(np.float32)
    return jnp.asarray(np.concatenate([mm, mm], axis=-1), BF16)


def _hg_prompt_body(*refs, dk, side_plan):
    nside = len(side_plan)
    q_ref, f_ref, i_ref, og_ref, lb_ref, gn_ref, mm_ref = refs[:7]
    o_ref, sfin_ref = refs[7 + nside:9 + nside]
    st_ref = refs[-1]
    _side_cast(refs[7:7 + nside], refs[9 + nside:-1])
    c = pl.program_id(1)
    nh = st_ref.shape[0]
    ch = mm_ref.shape[1]
    nsub = q_ref.shape[0] // ch
    width = q_ref.shape[1]
    nlev = ch.bit_length() - 1

    @pl.when(c == 0)
    def _():
        st_ref[...] = jnp.zeros_like(st_ref)

    lb = lb_ref[...]
    f_all = lb + (1.0 - lb) * jax.nn.sigmoid(f_ref[...])
    kk_all = 1.0 - f_all
    qs_all = _silu(q_ref[...])
    logf_pieces = _split3(jnp.log2(f_all))[:2]
    ib_all = i_ref[...].astype(BF16)
    pairs = [slice(2 * j * dk, 2 * (j + 1) * dk) for j in range(nh // 2)]
    heads = [slice(h * dk, (h + 1) * dk) for h in range(nh)]
    r = lax.broadcasted_iota(jnp.int32, (ch, 2 * ch), 0)
    s = lax.broadcasted_iota(jnp.int32, (ch, 2 * ch), 1) & (ch - 1)
    xr = jnp.where(r > s, r ^ s, 0)
    zero_k = jnp.zeros((ch, dk), BF16)
    zero_s = jnp.zeros((dk, dk), BF16)

    def block_diag(a, b, z):
        return jnp.concatenate([jnp.concatenate([a, z], axis=1),
                                jnp.concatenate([z, b], axis=1)], axis=0)

    def pair_scores(lhs, x):
        return _dot_nt(lhs, block_diag(x[:, :dk], x[:, dk:], zero_k))

    def upper_runs(v):
        m = 1 << v
        return [(b0, b0 + m) for b0 in range(m, ch, 2 * m)]

    def mix_rows(v, qs, kk):
        m = 1 << v
        if m >= SUBLANES:
            return jnp.concatenate(
                [(qs if (b0 // m) & 1 else kk)[b0:b0 + m] for b0 in range(0, ch, m)], axis=0)
        pick = ((lax.broadcasted_iota(jnp.int32, (1, SUBLANES, width), 1) >> v) & 1) == 1
        shape3 = (ch // SUBLANES, SUBLANES, width)
        return jnp.where(pick, qs.reshape(shape3), kk.reshape(shape3)).reshape(ch, width)

    staged = []
    for sub in range(nsub):
        rows = slice(sub * ch, (sub + 1) * ch)
        kk, qs = kk_all[rows], qs_all[rows]
        logf2 = jnp.concatenate([p[rows] for p in logf_pieces], axis=0)

        def decay(idx):
            return jnp.exp2(_dot(mm_ref[idx], logf2))

        eg = decay(0)
        qg = (qs * eg).astype(BF16)
        dec = eg[ch - 1:ch, :]
        kh = (kk * decay(1)).astype(BF16)
        lev = [mix_rows(0, qs * f_all[rows], kk).astype(BF16)]
        lev += [(decay(1 + v) * mix_rows(v, qs, kk)).astype(BF16) for v in range(1, nlev)]
        qb, kb = qs.astype(BF16), kk.astype(BF16)
        atts = []
        for sl in pairs:
            att = jnp.where(r == s, pair_scores(qb[:, sl], kb[:, sl]), 0.0)
            for v in range(nlev):
                x = lev[v][:, sl]
                m = 1 << v
                if m >= BF16_ROWS:
                    runs = upper_runs(v)
                    p = pair_scores(jnp.concatenate([x[a:b] for a, b in runs], axis=0), x)
                    parts = []
                    for n in range(len(runs)):
                        parts += [jnp.zeros((m, 2 * ch), F32), p[n * m:(n + 1) * m]]
                    p = jnp.concatenate(parts, axis=0)
                else:
                    p = pair_scores(x, x)
                att = jnp.where((xr >> v) == 1, p, att)
            atts.append(att.astype(BF16))
        staged.append((rows, qg, kh, dec, atts))

    for rows, qg, kh, dec, atts in staged:
        ib = ib_all[rows]
        for j, sl in enumerate(pairs):
            st2 = block_diag(st_ref[2 * j].astype(BF16), st_ref[2 * j + 1].astype(BF16), zero_s)
            ib2 = block_diag(ib[:, sl][:, :dk], ib[:, sl][:, dk:], zero_k)
            o = _dot_nt(qg[:, sl], st2) + _dot(atts[j], ib2)
            for e in range(2):
                hs = heads[2 * j + e]
                og = og_ref[rows, hs]
                o_ref[rows, hs] = _rms(o[:, e * dk:(e + 1) * dk], gn_ref[:, hs]) * _silu(og)
        for h, hs in enumerate(heads):
            st_ref[h] = st_ref[h] * dec[:, hs] + _dot_tn(ib[:, hs], kh[:, hs])

    @pl.when(c == pl.num_programs(1) - 1)
    def _():
        sfin_ref[0] = st_ref[...]


def _hg_prompt(proj, lb, gn, *, batch, seq, rows_total, nh, col0, side=()):
    width = lb.shape[1]
    dk = width // nh
    ch = HG_CHUNK * HG_CHUNKS_PER_STEP
    nchunk = seq // ch
    cb = col0 // width
    assert cb * width == col0 and nchunk * ch == seq

    def tok(k):
        return pl.BlockSpec((ch, width), lambda b, c: (b * nchunk + c, cb + k))

    vec = pl.BlockSpec((1, width), lambda b, c: (0, 0))
    mm = _hg_span_matrices(HG_CHUNK)
    side_in, side_out, side_shapes, side_plan = _side_specs(
        side, batch * nchunk, lambda b, c: b * nchunk + c)
    return pl.pallas_call(
        functools.partial(_hg_prompt_body, dk=dk, side_plan=side_plan),
        out_shape=[jax.ShapeDtypeStruct((rows_total, width), F32),
                   jax.ShapeDtypeStruct((batch, nh, dk, dk), F32)] + side_shapes,
        grid=(batch, nchunk),
        in_specs=[tok(0), tok(1), tok(2), tok(3), vec, vec,
                  pl.BlockSpec(mm.shape, lambda b, c: (0, 0, 0))] + side_in,
        out_specs=[pl.BlockSpec((ch, width), lambda b, c: (b * nchunk + c, 0)),
                   pl.BlockSpec((1, nh, dk, dk), lambda b, c: (b, 0, 0, 0))] + side_out,
        scratch_shapes=[pltpu.VMEM((nh, dk, dk), F32)],
        compiler_params=_cparams("arbitrary" if side else "parallel", "arbitrary"),
        name="hgrn_prompt",
    )(proj, proj, proj, proj, lb, gn, mm, *side)


def _split3(x):
    p1 = x.astype(BF16)
    r1 = x - p1.astype(F32)
    p2 = r1.astype(BF16)
    p3 = (r1 - p2.astype(F32)).astype(BF16)
    return p1, p2, p3


def _hg_sample_body(q_ref, f_ref, i_ref, og_ref, lb_ref, gn_ref, s0_ref,
                    hg_ref, s_ref, fq_ref, *, dk):
    step = pl.program_id(0)
    nt = q_ref.shape[0]
    tb, nh = s0_ref.shape[0], s0_ref.shape[1]
    npiece = fq_ref.shape[1] // nt

    @pl.when(step == 0)
    def _():
        lb = lb_ref[...]
        f = lb + (1.0 - lb) * jax.nn.sigmoid(f_ref[...])
        qs = _silu(q_ref[...])
        for src, base in ((f, 0), (qs, nh)):
            for h in range(nh):
                t = src[:, h * dk:(h + 1) * dk].T
                for p, piece in enumerate(_split3(t)[:npiece]):
                    fq_ref[(base + h) * dk:(base + h + 1) * dk, p * nt:(p + 1) * nt] = piece

    tok = lax.broadcasted_iota(jnp.int32, (npiece * nt, 2 * dk), 0) & (nt - 1)
    second = lax.broadcasted_iota(jnp.int32, (npiece * nt, 2 * dk), 1) >= dk
    for j0 in range(0, tb, 2):
        n0 = step * tb + j0
        onehot = (tok == jnp.where(second, n0 + 1, n0)).astype(BF16)
        fq2 = _dot(fq_ref[...], onehot)
        for e in range(2):
            j, n = j0 + e, n0 + e
            fq = fq2[:, e * dk:(e + 1) * dk]
            irow = i_ref[pl.ds(n, 1), :]
            ogrow = og_ref[pl.ds(n, 1), :]
            for h in range(nh):
                sl = slice(h * dk, (h + 1) * dk)
                fb = fq[h * dk:(h + 1) * dk, :]
                qb = fq[(nh + h) * dk:(nh + h + 1) * dk, :]
                s1 = fb * s0_ref[j, h] + (1.0 - fb) * irow[:, sl]
                s_ref[j, h] = s1
                o = jnp.sum(qb * s1, axis=0, keepdims=True)
                hg_ref[j:j + 1, sl] = _rms(o, gn_ref[:, sl]) * _silu(ogrow[:, sl])


def _hg_sample(proj, lb, gn, s0, *, row0, nh, col0):
    width = lb.shape[1]
    dk = width // nh
    ns = s0.shape[0]
    tb = SAMPLE_TOKENS_PER_STEP
    rb, cb = row0 // ns, col0 // width
    assert rb * ns == row0 and cb * width == col0 and ns % tb == 0
    assert ns & (ns - 1) == 0 and tb % 2 == 0 and dk == LANES

    def tok(k):
        return pl.BlockSpec((ns, width), lambda t: (rb, cb + k))

    vec = pl.BlockSpec((1, width), lambda t: (0, 0))
    sspec = pl.BlockSpec((tb, nh, dk, dk), lambda t: (t, 0, 0, 0))
    return pl.pallas_call(
        functools.partial(_hg_sample_body, dk=dk),
        out_shape=(jax.ShapeDtypeStruct((ns, width), F32),
                   jax.ShapeDtypeStruct(s0.shape, F32)),
        grid=(ns // tb,),
        in_specs=[tok(0), tok(1), tok(2), tok(3), vec, vec, sspec],
        out_specs=(pl.BlockSpec((tb, width), lambda t: (t, 0)), sspec),
        scratch_shapes=[pltpu.VMEM((2 * nh * dk, 2 * ns), BF16)],
        compiler_params=_cparams("arbitrary"),
        name="hgrn_sample",
    )(proj, proj, proj, proj, lb, gn, s0)


def _merge_body(z_ref, zs_ref, hg_ref, hgs_ref, gs0_ref, gs1_ref, gh0_ref, gh1_ref, x_ref,
                wglu_ref, bglu_ref, wbs_ref, wbh_ref, wout_ref, post_ref, o_ref, *, ns):
    i = pl.program_id(0)
    last_i = pl.num_programs(0) - 1
    cut = x_ref.shape[0] - ns
    half = gs0_ref.shape[1]

    def rows(z, hg, sl):
        s5o = z * jax.nn.sigmoid(_dot(z.astype(BF16), wglu_ref[...]) + bglu_ref[...])
        a = _dot(s5o.astype(BF16), wbs_ref[...])
        b = _dot(hg.astype(BF16), wbh_ref[...])
        m0 = (jax.nn.sigmoid(gs0_ref[sl, :]) * a[:, :half]
              + jax.nn.sigmoid(gh0_ref[sl, :]) * b[:, :half])
        m1 = (jax.nn.sigmoid(gs1_ref[sl, :]) * a[:, half:]
              + jax.nn.sigmoid(gh1_ref[sl, :]) * b[:, half:])
        mix = _dot(m0.astype(BF16), wout_ref[:half, :]) + _dot(m1.astype(BF16), wout_ref[half:, :])
        o_ref[sl, :] = x_ref[sl, :] + _rms(mix, post_ref[...])

    @pl.when(i < last_i)
    def _():
        rows(z_ref[...], hg_ref[...], slice(None))

    @pl.when(i == last_i)
    def _():
        rows(z_ref[:cut, :], hg_ref[:cut, :], slice(0, cut))
        rows(zs_ref[...], hgs_ref[...], slice(cut, None))


def _merge(z, zs, hg, hgs, proj, x, wglu, bglu, wbs, wbh, wout, post, *, col0, tm):
    m, d = x.shape
    w = z.shape[1]
    ns = zs.shape[0]
    cb = col0 // w
    assert cb * w == col0 and d == 2 * w and m % tm == 0 and 0 < ns < tm

    def gate(k):
        return pl.BlockSpec((tm, w), lambda i: (i, cb + k))

    def const(shape):
        return pl.BlockSpec(shape, lambda i: (0, 0), pipeline_mode=pl.Buffered(1))

    tok = pl.BlockSpec((tm, w), lambda i: (i, 0))
    return pl.pallas_call(
        functools.partial(_merge_body, ns=ns),
        out_shape=jax.ShapeDtypeStruct((m, d), F32),
        grid=(m // tm,),
        in_specs=[
            tok, const((ns, w)), tok, const((ns, w)),
            gate(0), gate(1), gate(2), gate(3),
            pl.BlockSpec((tm, d), lambda i: (i, 0)),
            const((w, w)), const((1, w)), const((w, d)), const((w, d)), const((d, d)), const((1, d)),
        ],
        out_specs=pl.BlockSpec((tm, d), lambda i: (i, 0)),
        compiler_params=_cparams("parallel"),
        name="merge",
    )(z, zs, hg, hgs, proj, proj, proj, proj, x, wglu, bglu, wbs, wbh, wout, post)


MERGE_ROW_TILE = 416


def kernel(x_prompt, x_sample, state_s5_re, state_s5_im, state_hgrn, ffn1_pre_norm, ffn1_w_gate, ffn1_w_up, ffn1_w_down, ffn1_post_norm, mix_pre_norm, w_in, s5_lambda_re, s5_lambda_im, s5_log_dt, s5_b_re, s5_b_im, s5_c_re, s5_c_im, s5_d, s5_w_glu, s5_b_glu, hgrn_lb_logits, hgrn_out_norm, w_branch_s5, w_branch_hgrn, w_out, mix_post_norm, ffn2_pre_norm, ffn2_w_gate, ffn2_w_up, ffn2_w_down, ffn2_post_norm):
    depth = ffn1_w_gate.shape[0]
    assert depth == 1
    batch, seq, d = x_prompt.shape
    ns = x_sample.shape[0]
    assert x_sample.shape[1] == 1
    g, p = s5_lambda_re.shape[1:]
    nh, dk = state_hgrn.shape[2], state_hgrn.shape[3]
    s5w = s5_d.shape[1]
    hgw = nh * dk
    mp = batch * seq
    m = mp + ns

    bf = lambda a: a[0].astype(BF16)
    row = lambda a: a.reshape(1, -1)

    lb = jax.nn.softmax(hgrn_lb_logits.astype(F32), axis=0)[:1]

    x, w_in_b = _ffn(
        x_prompt.reshape(mp, d), x_sample.reshape(ns, d), row(ffn1_pre_norm),
        ffn1_w_gate[0], ffn1_w_up[0], ffn1_w_down[0], row(ffn1_post_norm),
        ns=ns, split_out=False, side=(w_in[0],))
    proj = _inproj(x, row(mix_pre_norm), w_in_b)

    tw_in, tw_out, a_re, a_im = _s5_tables(
        s5_lambda_re[0], s5_lambda_im[0], s5_log_dt[0], s5_b_re[0], s5_b_im[0],
        s5_c_re[0], s5_c_im[0])
    d_skip = row(s5_d)
    z, hlast, wg2_b, wu2_b, wd2_b, wglu_b, wbs_b, wbh_b, wout_b = _s5_prompt(
        proj, tw_in.astype(BF16), tw_out.astype(BF16), a_re, a_im, d_skip,
        batch=batch, seq=seq, rows_total=mp,
        side=(ffn2_w_gate[0], ffn2_w_up[0], ffn2_w_down[0], s5_w_glu[0], w_branch_s5[0],
              w_branch_hgrn[0], w_out[0]))
    zs, s_re, s_im = _s5_sample(proj, state_s5_re[0].reshape(ns, g * p),
                                state_s5_im[0].reshape(ns, g * p),
                                tw_in, tw_out, a_re, a_im, d_skip, row0=mp)
    half = hlast.shape[-1] // 2
    p_re = hlast[:, :, SUBLANES - 1, :half].reshape(1, batch, g, p)
    p_im = hlast[:, :, SUBLANES - 1, half:].reshape(1, batch, g, p)

    gn = row(hgrn_out_norm)
    hg, st_p = _hg_prompt(proj, lb, gn, batch=batch, seq=seq, rows_total=mp, nh=nh, col0=s5w)
    hgs, st_s = _hg_sample(proj, lb, gn, state_hgrn[0], row0=mp, nh=nh, col0=s5w)

    x = _merge(z, zs, hg, hgs, proj, x, wglu_b, row(s5_b_glu), wbs_b, wbh_b, wout_b,
               row(mix_post_norm), col0=s5w + 4 * hgw, tm=MERGE_ROW_TILE)
    yp, ys = _ffn(x, None, row(ffn2_pre_norm), wg2_b, wu2_b, wd2_b, row(ffn2_post_norm),
                  ns=ns, split_out=True)

    return (yp.reshape(batch, seq, d), ys.reshape(ns, 1, d),
            p_re, p_im, jnp.swapaxes(st_p, -1, -2)[None],
            s_re.reshape(1, ns, g, p), s_im.reshape(1, ns, g, p), st_s[None])
```

```python
import functools

import jax
import jax.numpy as jnp
import numpy as np
from jax import lax
from jax.experimental import pallas as pl
from jax.experimental.pallas import tpu as pltpu

F32 = jnp.float32
BF16 = jnp.bfloat16
EPS = 1e-6
HIGHEST = lax.Precision.HIGHEST

LANES = 128
SUBLANES = 8
VMEM_LIMIT = 60 * 1024 * 1024

FFN_TILES_BF16 = (832, 512)
FFN_TILES_F32 = (1040, 256)
IN_TILE = 1536
IN_ROW_TILE = 1040
MERGE_ROW_TILE = 416
HG_CHUNK = 64
HG_CHUNKS_PER_STEP = 4
SAMPLE_TOKENS_PER_STEP = 8


def _cparams(*sem):
    return pltpu.CompilerParams(dimension_semantics=sem, vmem_limit_bytes=VMEM_LIMIT)


def _rms(x, g):
    return x * lax.rsqrt(jnp.mean(x * x, axis=-1, keepdims=True) + EPS) * g


def _silu(x):
    return x * jax.nn.sigmoid(x)


def _dot(a, b):
    return jnp.dot(a, b, preferred_element_type=F32)


def _dot_hi(a, b):
    return jnp.dot(a, b, preferred_element_type=F32, precision=HIGHEST)


def _dot_nt(a, b):
    return lax.dot_general(a, b, (((1,), (1,)), ((), ())), preferred_element_type=F32)


def _dot_tn(a, b):
    return lax.dot_general(a, b, (((0,), (0,)), ((), ())), preferred_element_type=F32)


BF16_ROWS = 16


def _side_specs(arrays, nsteps, step_of):
    in_specs, out_specs, out_shapes, plan = [], [], [], []
    for a in arrays:
        r, c = a.shape
        rows = next(t for t in range(BF16_ROWS, r + 1, BF16_ROWS) if r % t == 0 and r // t <= nsteps)
        nblk = r // rows
        every = nsteps // nblk

        def imap(*g, nblk=nblk, every=every):
            return (jnp.minimum(step_of(*g) // every, nblk - 1), 0)

        in_specs.append(pl.BlockSpec((rows, c), imap))
        out_specs.append(pl.BlockSpec((rows, c), imap))
        out_shapes.append(jax.ShapeDtypeStruct((r, c), BF16))
        plan.append((nblk, every))
    return in_specs, out_specs, out_shapes, tuple(plan)


def _side_cast(src_refs, dst_refs):
    for src, dst in zip(src_refs, dst_refs):
        dst[...] = src[...].astype(BF16)


def _ffn_body(*refs, ns, split_in, split_out, side_plan):
    refs = list(refs)
    nside = len(side_plan)
    x_ref = refs.pop(0)
    xs_ref = refs.pop(0) if split_in else None
    pre_ref, wg_ref, wu_ref, wd_ref, post_ref = refs[:5]
    side_src = refs[5:5 + nside]
    o_ref = refs[5 + nside]
    os_ref = refs[6 + nside] if split_out else None
    side_dst = refs[-1 - nside:-1]
    h_ref = refs[-1]
    acc_ref = o_ref
    i, k = pl.program_id(0), pl.program_id(1)
    last_i = pl.num_programs(0) - 1
    last_k = pl.num_programs(1) - 1
    cut = h_ref.shape[0] - ns

    def swiglu_down(h):
        _side_cast(side_src, side_dst)
        g = _dot(h, wg_ref[...].astype(BF16))
        u = _dot(h, wu_ref[...].astype(BF16))
        return _dot((_silu(g) * u).astype(BF16), wd_ref[...].astype(BF16))

    @pl.when((i < last_i) & (k == 0))
    def _():
        h = _rms(x_ref[...], pre_ref[...]).astype(BF16)
        h_ref[...] = h
        acc_ref[...] = swiglu_down(h)

    @pl.when((i < last_i) & (k == last_k))
    def _():
        acc = acc_ref[...] + swiglu_down(h_ref[...])
        o_ref[...] = x_ref[...] + 0.5 * _rms(acc, post_ref[...])

    @pl.when((i == last_i) & (k == 0))
    def _():
        h_ref[:cut, :] = _rms(x_ref[:cut, :], pre_ref[...]).astype(BF16)
        xt = xs_ref[...] if split_in else x_ref[cut:, :]
        h_ref[cut:, :] = _rms(xt, pre_ref[...]).astype(BF16)
        acc_ref[...] = jnp.zeros_like(acc_ref)

    @pl.when(((k > 0) & (k < last_k)) | (i == last_i))
    def _():
        acc_ref[...] += swiglu_down(h_ref[...])

    @pl.when((i == last_i) & (k == last_k))
    def _():
        o_ref[:cut, :] = x_ref[:cut, :] + 0.5 * _rms(acc_ref[:cut, :], post_ref[...])
        xt = xs_ref[...] if split_in else x_ref[cut:, :]
        tail = xt + 0.5 * _rms(acc_ref[cut:, :], post_ref[...])
        if split_out:
            os_ref[...] = tail
        else:
            o_ref[cut:, :] = tail


def _ffn(x, xs, pre, wg, wu, wd, post, *, ns, split_out, side=()):
    split_in = xs is not None
    d = x.shape[1]
    m = x.shape[0] + (ns if split_in else 0)
    dff = wg.shape[1]
    tm, tf = (FFN_TILES_F32 if wg.dtype == F32 else FFN_TILES_BF16)
    assert m % tm == 0 and dff % tf == 0 and 0 < ns < tm and ns % BF16_ROWS == 0
    nk = dff // tf
    side_in, side_out, side_shapes, side_plan = _side_specs(
        side, (m // tm) * nk, lambda i, k: i * nk + k)
    tok = pl.BlockSpec((tm, d), lambda i, k: (i, 0))
    smp = pl.BlockSpec((ns, d), lambda i, k: (0, 0))
    vec = pl.BlockSpec((1, d), lambda i, k: (0, 0))
    in_specs = [tok] + ([smp] if split_in else []) + [
        vec,
        pl.BlockSpec((d, tf), lambda i, k: (0, k)),
        pl.BlockSpec((d, tf), lambda i, k: (0, k)),
        pl.BlockSpec((tf, d), lambda i, k: (k, 0)),
        vec,
    ] + side_in
    if split_out:
        out_shape = [jax.ShapeDtypeStruct((m - ns, d), F32), jax.ShapeDtypeStruct((ns, d), F32)]
        out_specs = [tok, smp]
    else:
        out_shape = [jax.ShapeDtypeStruct((m, d), F32)]
        out_specs = [tok]
    args = [x] + ([xs] if split_in else []) + [pre, wg, wu, wd, post] + list(side)
    sequential_rows = split_out or bool(side)
    return pl.pallas_call(
        functools.partial(_ffn_body, ns=ns, split_in=split_in, split_out=split_out,
                          side_plan=side_plan),
        out_shape=out_shape + side_shapes,
        grid=(m // tm, nk),
        in_specs=in_specs,
        out_specs=out_specs + side_out,
        scratch_shapes=[pltpu.VMEM((tm, d), BF16)],
        compiler_params=_cparams("arbitrary" if sequential_rows else "parallel", "arbitrary"),
        name="ffn",
    )(*args)


def _inproj_body(x_ref, g_ref, w_ref, o_ref, h_ref):
    j = pl.program_id(1)

    @pl.when(j == 0)
    def _():
        h = _rms(x_ref[...], g_ref[...]).astype(BF16)
        h_ref[...] = h
        o_ref[...] = _dot(h, w_ref[...])

    @pl.when(j > 0)
    def _():
        o_ref[...] = _dot(h_ref[...], w_ref[...])


def _inproj(x, g, w):
    m, d = x.shape
    n = w.shape[1]
    tm, tn = IN_ROW_TILE, IN_TILE
    assert m % tm == 0
    return pl.pallas_call(
        _inproj_body,
        out_shape=jax.ShapeDtypeStruct((m, n), F32),
        grid=(m // tm, n // tn),
        in_specs=[
            pl.BlockSpec((tm, d), lambda i, j: (i, 0)),
            pl.BlockSpec((1, d), lambda i, j: (0, 0)),
            pl.BlockSpec((d, tn), lambda i, j: (0, j)),
        ],
        out_specs=pl.BlockSpec((tm, tn), lambda i, j: (i, j)),
        scratch_shapes=[pltpu.VMEM((tm, d), BF16)],
        compiler_params=_cparams("parallel", "arbitrary"),
        name="inproj",
    )(x, g, w)


def _s5_tables(lam_re, lam_im, log_dt, b_re, b_im, c_re, c_im):
    g, p = lam_re.shape
    n = b_re.shape[-1]
    gpt = LANES // n
    nv = g // gpt
    dt = jnp.exp(log_dt)[:, None]
    er = jnp.exp(lam_re * dt)
    th = lam_im * dt
    a_re, a_im = er * jnp.cos(th), er * jnp.sin(th)
    den = lam_re * lam_re + lam_im * lam_im
    k_re = ((a_re - 1.0) * lam_re + a_im * lam_im) / den
    k_im = (a_im * lam_re - (a_re - 1.0) * lam_im) / den
    bb_re = k_re[..., None] * b_re - k_im[..., None] * b_im
    bb_im = k_re[..., None] * b_im + k_im[..., None] * b_re
    eye = jnp.eye(gpt, dtype=F32)
    bb = jnp.stack([bb_re, bb_im]).reshape(2, nv, gpt, p, n)
    cc = jnp.stack([c_re, -c_im]).reshape(2, nv, gpt, n, p)
    w_in = jnp.einsum("ab,cvapn->vancbp", eye, bb).reshape(nv, gpt * n, 2 * gpt * p)
    w_out = jnp.einsum("ab,cvanp->vcapbn", eye, cc).reshape(nv, 2 * gpt * p, gpt * n)
    return w_in, w_out, a_re.reshape(nv, 1, gpt * p), a_im.reshape(nv, 1, gpt * p)


def _cmul(ar, ai, br, bi):
    return ar * br - ai * bi, ar * bi + ai * br


def _s5_prompt_body(*refs, seg, side_plan):
    nside = len(side_plan)
    u_ref, w_ref, cm_ref, ar_ref, ai_ref, d_ref = refs[:6]
    z_ref, hl_ref = refs[6 + nside:8 + nside]
    buh_ref, ut_ref, pad_ref = refs[-3:]
    nseg = SUBLANES
    nc = w_ref.shape[2] // LANES
    hc = nc // 2
    pitch = pad_ref.shape[0] // nseg
    nblk = nseg
    steps = seg // nseg

    def gather(b):
        for t in range(b * steps, (b + 1) * steps):
            ut_ref[t * nseg:(t + 1) * nseg, :] = pad_ref[pl.ds(t, nseg, stride=pitch), :]

    def scatter(b):
        for t in range(b * steps, (b + 1) * steps):
            pad_ref[pl.ds(t, nseg, stride=pitch), :] = ut_ref[t * nseg:(t + 1) * nseg, :]

    for j in range(nseg):
        pad_ref[j * pitch:j * pitch + seg, :] = u_ref[j * seg:(j + 1) * seg, :]

    def project_in(b):
        blk = slice(b * seg, (b + 1) * seg)
        bu = _dot(ut_ref[blk, :].astype(BF16), w_ref[0])
        for c in range(nc):
            buh_ref[c, blk, :] = bu[:, c * LANES:(c + 1) * LANES]

    def project_out(b):
        blk = slice(b * seg, (b + 1) * seg)
        y = d_ref[...] * ut_ref[blk, :]
        for c in range(nc):
            y = y + _dot(buh_ref[c, blk, :].astype(BF16), cm_ref[0, c * LANES:(c + 1) * LANES, :])
        ut_ref[blk, :] = jax.nn.gelu(y)

    ar = [jnp.broadcast_to(ar_ref[0, :, c * LANES:(c + 1) * LANES], (nseg, LANES)) for c in range(hc)]
    ai = [jnp.broadcast_to(ai_ref[0, :, c * LANES:(c + 1) * LANES], (nseg, LANES)) for c in range(hc)]

    def advance(t, hs, store):
        rows = slice(t * nseg, (t + 1) * nseg)
        out = []
        for c in range(hc):
            pr, pi = _cmul(ar[c], ai[c], hs[2 * c], hs[2 * c + 1])
            out += [pr + buh_ref[c, rows, :], pi + buh_ref[hc + c, rows, :]]
        if store:
            for c in range(hc):
                buh_ref[c, rows, :] = out[2 * c]
                buh_ref[hc + c, rows, :] = out[2 * c + 1]
        return out

    zeros = [jnp.zeros((nseg, LANES), F32) for _ in range(2 * hc)]
    hs = zeros
    gather(0)
    project_in(0)
    for b in range(nblk):
        if b + 1 < nblk:
            gather(b + 1)
            project_in(b + 1)
        for t in range(b * steps, (b + 1) * steps):
            hs = advance(t, hs, store=False)
    ends = hs

    pw = [(ar[c], ai[c]) for c in range(hc)]
    for _ in range(seg.bit_length() - 1):
        pw = [_cmul(r, i, r, i) for r, i in pw]

    row = lax.broadcasted_iota(jnp.int32, (nseg, LANES), 0)
    init = list(zeros)
    for j in range(1, nseg):
        for c in range(hc):
            pr, pi = _cmul(pw[c][0], pw[c][1], init[2 * c], init[2 * c + 1])
            nr = pltpu.roll(pr + ends[2 * c], 1, 0)
            ni = pltpu.roll(pi + ends[2 * c + 1], 1, 0)
            init[2 * c] = jnp.where(row == j, nr, init[2 * c])
            init[2 * c + 1] = jnp.where(row == j, ni, init[2 * c + 1])

    hs = init
    for b in range(nblk):
        for t in range(b * steps, (b + 1) * steps):
            hs = advance(t, hs, store=True)
        if b > 0:
            project_out(b - 1)
            scatter(b - 1)
    for c in range(hc):
        hl_ref[0, 0, :, c * LANES:(c + 1) * LANES] = hs[2 * c]
        hl_ref[0, 0, :, (hc + c) * LANES:(hc + c + 1) * LANES] = hs[2 * c + 1]
    _side_cast(refs[6:6 + nside], refs[8 + nside:-3])
    project_out(nblk - 1)
    scatter(nblk - 1)
    for j in range(nseg):
        z_ref[j * seg:(j + 1) * seg, :] = pad_ref[j * pitch:j * pitch + seg, :]


def _s5_prompt(proj, w_in, w_out, a_re, a_im, d_skip, *, batch, seq, rows_total, side=()):
    nv = w_in.shape[0]
    sw = w_in.shape[2]
    seg = seq // SUBLANES
    assert seg * SUBLANES == seq and seg & (seg - 1) == 0
    side_in, side_out, side_shapes, side_plan = _side_specs(
        side, batch * nv, lambda b, v: b * nv + v)
    body = functools.partial(_s5_prompt_body, seg=seg, side_plan=side_plan)
    return pl.pallas_call(
        body,
        out_shape=[jax.ShapeDtypeStruct((rows_total, nv * LANES), F32),
                   jax.ShapeDtypeStruct((batch, nv, SUBLANES, sw), F32)] + side_shapes,
        grid=(batch, nv),
        in_specs=[
            pl.BlockSpec((seq, LANES), lambda b, v: (b, v)),
            pl.BlockSpec((1, LANES, sw), lambda b, v: (v, 0, 0)),
            pl.BlockSpec((1, sw, LANES), lambda b, v: (v, 0, 0)),
            pl.BlockSpec((1, 1, sw // 2), lambda b, v: (v, 0, 0)),
            pl.BlockSpec((1, 1, sw // 2), lambda b, v: (v, 0, 0)),
            pl.BlockSpec((1, LANES), lambda b, v: (0, v)),
        ] + side_in,
        out_specs=[pl.BlockSpec((seq, LANES), lambda b, v: (b, v)),
                   pl.BlockSpec((1, 1, SUBLANES, sw), lambda b, v: (b, v, 0, 0))] + side_out,
        scratch_shapes=[pltpu.VMEM((sw // LANES, seq, LANES), F32),
                        pltpu.VMEM((seq, LANES), F32),
                        pltpu.VMEM((SUBLANES * (seg + SUBLANES), LANES), F32)],
        compiler_params=_cparams(*(("arbitrary",) * 2 if side else ("parallel",) * 2)),
        name="s5_prompt",
    )(proj, w_in, w_out, a_re, a_im, d_skip, *side)


def _s5_sample_body(u_ref, hre_ref, him_ref, w_ref, cm_ref, ar_ref, ai_ref, d_ref,
                    z_ref, ore_ref, oim_ref):
    half = ar_ref.shape[2]
    u = u_ref[...]
    bu = _dot_hi(u, w_ref[0])
    pr, pi = _cmul(ar_ref[0], ai_ref[0], hre_ref[...], him_ref[...])
    hr = pr + bu[:, :half]
    hi = pi + bu[:, half:]
    ore_ref[...] = hr
    oim_ref[...] = hi
    y = d_ref[...] * u + _dot_hi(hr, cm_ref[0, :half, :]) + _dot_hi(hi, cm_ref[0, half:, :])
    z_ref[...] = jax.nn.gelu(y)


def _s5_sample(proj, h_re, h_im, w_in, w_out, a_re, a_im, d_skip, *, row0):
    nv = w_in.shape[0]
    sw = w_in.shape[2]
    ns = h_re.shape[0]
    rb = row0 // ns
    assert rb * ns == row0
    st = pl.BlockSpec((ns, sw // 2), lambda v: (0, v))
    return pl.pallas_call(
        _s5_sample_body,
        out_shape=(jax.ShapeDtypeStruct((ns, nv * LANES), F32),
                   jax.ShapeDtypeStruct(h_re.shape, F32),
                   jax.ShapeDtypeStruct(h_im.shape, F32)),
        grid=(nv,),
        in_specs=[
            pl.BlockSpec((ns, LANES), lambda v: (rb, v)),
            st, st,
            pl.BlockSpec((1, LANES, sw), lambda v: (v, 0, 0)),
            pl.BlockSpec((1, sw, LANES), lambda v: (v, 0, 0)),
            pl.BlockSpec((1, 1, sw // 2), lambda v: (v, 0, 0)),
            pl.BlockSpec((1, 1, sw // 2), lambda v: (v, 0, 0)),
            pl.BlockSpec((1, LANES), lambda v: (0, v)),
        ],
        out_specs=(pl.BlockSpec((ns, LANES), lambda v: (0, v)), st, st),
        compiler_params=_cparams("parallel"),
        name="s5_sample",
    )(proj, h_re, h_im, w_in, w_out, a_re, a_im, d_skip)


def _hg_span_matrices(ch):
    t = np.arange(ch)[:, None]
    r = np.arange(ch)[None, :]
    mats = [r <= t, r > t]
    for v in range(1, ch.bit_length() - 1):
        base = (t >> v) << v
        upper = ((t >> v) & 1) == 1
        mats.append(np.where(upper, (r >= base) & (r <= t), (r > t) & (r < base + (1 << v))))
    mm = np.stack(mats).astype(np.float32)
    return jnp.asarray(np.concatenate([mm, mm], axis=-1), BF16)


def _hg_prompt_body(*refs, dk, side_plan):
    nside = len(side_plan)
    q_ref, f_ref, i_ref, og_ref, lb_ref, gn_ref, mm_ref = refs[:7]
    o_ref, sfin_ref = refs[7 + nside:9 + nside]
    st_ref = refs[-1]
    _side_cast(refs[7:7 + nside], refs[9 + nside:-1])
    c = pl.program_id(1)
    nh = st_ref.shape[0]
    ch = mm_ref.shape[1]
    nsub = q_ref.shape[0] // ch
    width = q_ref.shape[1]
    nlev = ch.bit_length() - 1

    @pl.when(c == 0)
    def _():
        st_ref[...] = jnp.zeros_like(st_ref)

    lb = lb_ref[...]
    f_all = lb + (1.0 - lb) * jax.nn.sigmoid(f_ref[...])
    kk_all = 1.0 - f_all
    qs_all = _silu(q_ref[...])
    logf_pieces = _split3(jnp.log2(f_all))[:2]
    ib_all = i_ref[...].astype(BF16)
    pairs = [slice(2 * j * dk, 2 * (j + 1) * dk) for j in range(nh // 2)]
    heads = [slice(h * dk, (h + 1) * dk) for h in range(nh)]
    r = lax.broadcasted_iota(jnp.int32, (ch, 2 * ch), 0)
    s = lax.broadcasted_iota(jnp.int32, (ch, 2 * ch), 1) & (ch - 1)
    xr = jnp.where(r > s, r ^ s, 0)
    zero_k = jnp.zeros((ch, dk), BF16)
    zero_s = jnp.zeros((dk, dk), BF16)

    def block_diag(a, b, z):
        return jnp.concatenate([jnp.concatenate([a, z], axis=1),
                                jnp.concatenate([z, b], axis=1)], axis=0)

    def pair_scores(lhs, x):
        return _dot_nt(lhs, block_diag(x[:, :dk], x[:, dk:], zero_k))

    def upper_runs(v):
        m = 1 << v
        return [(b0, b0 + m) for b0 in range(m, ch, 2 * m)]

    def mix_rows(v, qs, kk):
        m = 1 << v
        if m >= SUBLANES:
            return jnp.concatenate(
                [(qs if (b0 // m) & 1 else kk)[b0:b0 + m] for b0 in range(0, ch, m)], axis=0)
        pick = ((lax.broadcasted_iota(jnp.int32, (1, SUBLANES, width), 1) >> v) & 1) == 1
        shape3 = (ch // SUBLANES, SUBLANES, width)
        return jnp.where(pick, qs.reshape(shape3), kk.reshape(shape3)).reshape(ch, width)

    staged = []
    for sub in range(nsub):
        rows = slice(sub * ch, (sub + 1) * ch)
        kk, qs = kk_all[rows], qs_all[rows]
        logf2 = jnp.concatenate([p[rows] for p in logf_pieces], axis=0)

        def decay(idx):
            return jnp.exp2(_dot(mm_ref[idx], logf2))

        eg = decay(0)
        qg = (qs * eg).astype(BF16)
        dec = eg[ch - 1:ch, :]
        kh = (kk * decay(1)).astype(BF16)
        lev = [mix_rows(0, qs * f_all[rows], kk).astype(BF16)]
        lev += [(decay(1 + v) * mix_rows(v, qs, kk)).astype(BF16) for v in range(1, nlev)]
        qb, kb = qs.astype(BF16), kk.astype(BF16)
        atts = []
        for sl in pairs:
            att = jnp.where(r == s, pair_scores(qb[:, sl], kb[:, sl]), 0.0)
            for v in range(nlev):
                x = lev[v][:, sl]
                m = 1 << v
                if m >= BF16_ROWS:
                    runs = upper_runs(v)
                    p = pair_scores(jnp.concatenate([x[a:b] for a, b in runs], axis=0), x)
                    parts = []
                    for n in range(len(runs)):
                        parts += [jnp.zeros((m, 2 * ch), F32), p[n * m:(n + 1) * m]]
                    p = jnp.concatenate(parts, axis=0)
                else:
                    p = pair_scores(x, x)
                att = jnp.where((xr >> v) == 1, p, att)
            atts.append(att.astype(BF16))
        staged.append((rows, qg, kh, dec, atts))

    for rows, qg, kh, dec, atts in staged:
        ib = ib_all[rows]
        for j, sl in enumerate(pairs):
            st2 = block_diag(st_ref[2 * j].astype(BF16), st_ref[2 * j + 1].astype(BF16), zero_s)
            ib2 = block_diag(ib[:, sl][:, :dk], ib[:, sl][:, dk:], zero_k)
            o = _dot_nt(qg[:, sl], st2) + _dot(atts[j], ib2)
            for e in range(2):
                hs = heads[2 * j + e]
                og = og_ref[rows, hs]
                o_ref[rows, hs] = _rms(o[:, e * dk:(e + 1) * dk], gn_ref[:, hs]) * _silu(og)
        for h, hs in enumerate(heads):
            st_ref[h] = st_ref[h] * dec[:, hs] + _dot_tn(ib[:, hs], kh[:, hs])

    @pl.when(c == pl.num_programs(1) - 1)
    def _():
        sfin_ref[0] = st_ref[...]


def _hg_prompt(proj, lb, gn, *, batch, seq, rows_total, nh, col0, side=()):
    width = lb.shape[1]
    dk = width // nh
    ch = HG_CHUNK * HG_CHUNKS_PER_STEP
    nchunk = seq // ch
    cb = col0 // width
    assert cb * width == col0 and nchunk * ch == seq

    def tok(k):
        return pl.BlockSpec((ch, width), lambda b, c: (b * nchunk + c, cb + k))

    vec = pl.BlockSpec((1, width), lambda b, c: (0, 0))
    mm = _hg_span_matrices(HG_CHUNK)
    side_in, side_out, side_shapes, side_plan = _side_specs(
        side, batch * nchunk, lambda b, c: b * nchunk + c)
    return pl.pallas_call(
        functools.partial(_hg_prompt_body, dk=dk, side_plan=side_plan),
        out_shape=[jax.ShapeDtypeStruct((rows_total, width), F32),
                   jax.ShapeDtypeStruct((batch, nh, dk, dk), F32)] + side_shapes,
        grid=(batch, nchunk),
        in_specs=[tok(0), tok(1), tok(2), tok(3), vec, vec,
                  pl.BlockSpec(mm.shape, lambda b, c: (0, 0, 0))] + side_in,
        out_specs=[pl.BlockSpec((ch, width), lambda b, c: (b * nchunk + c, 0)),
                   pl.BlockSpec((1, nh, dk, dk), lambda b, c: (b, 0, 0, 0))] + side_out,
        scratch_shapes=[pltpu.VMEM((nh, dk, dk), F32)],
        compiler_params=_cparams("arbitrary" if side else "parallel", "arbitrary"),
        name="hgrn_prompt",
    )(proj, proj, proj, proj, lb, gn, mm, *side)


def _split3(x):
    p1 = x.astype(BF16)
    r1 = x - p1.astype(F32)
    p2 = r1.astype(BF16)
    p3 = (r1 - p2.astype(F32)).astype(BF16)
    return p1, p2, p3


def _hg_sample_body(q_ref, f_ref, i_ref, og_ref, lb_ref, gn_ref, s0_ref,
                    hg_ref, s_ref, fq_ref, *, dk):
    step = pl.program_id(0)
    nt = q_ref.shape[0]
    tb, nh = s0_ref.shape[0], s0_ref.shape[1]
    npiece = fq_ref.shape[1] // nt

    @pl.when(step == 0)
    def _():
        lb = lb_ref[...]
        f = lb + (1.0 - lb) * jax.nn.sigmoid(f_ref[...])
        qs = _silu(q_ref[...])
        for src, base in ((f, 0), (qs, nh)):
            for h in range(nh):
                t = src[:, h * dk:(h + 1) * dk].T
                for p, piece in enumerate(_split3(t)[:npiece]):
                    fq_ref[(base + h) * dk:(base + h + 1) * dk, p * nt:(p + 1) * nt] = piece

    tok = lax.broadcasted_iota(jnp.int32, (npiece * nt, 2 * dk), 0) & (nt - 1)
    second = lax.broadcasted_iota(jnp.int32, (npiece * nt, 2 * dk), 1) >= dk
    for j0 in range(0, tb, 2):
        n0 = step * tb + j0
        onehot = (tok == jnp.where(second, n0 + 1, n0)).astype(BF16)
        fq2 = _dot(fq_ref[...], onehot)
        for e in range(2):
            j, n = j0 + e, n0 + e
            fq = fq2[:, e * dk:(e + 1) * dk]
            irow = i_ref[pl.ds(n, 1), :]
            ogrow = og_ref[pl.ds(n, 1), :]
            for h in range(nh):
                sl = slice(h * dk, (h + 1) * dk)
                fb = fq[h * dk:(h + 1) * dk, :]
                qb = fq[(nh + h) * dk:(nh + h + 1) * dk, :]
                s1 = fb * s0_ref[j, h] + (1.0 - fb) * irow[:, sl]
                s_ref[j, h] = s1
                o = jnp.sum(qb * s1, axis=0, keepdims=True)
                hg_ref[j:j + 1, sl] = _rms(o, gn_ref[:, sl]) * _silu(ogrow[:, sl])


def _hg_sample(proj, lb, gn, s0, *, row0, nh, col0):
    width = lb.shape[1]
    dk = width // nh
    ns = s0.shape[0]
    tb = SAMPLE_TOKENS_PER_STEP
    rb, cb = row0 // ns, col0 // width
    assert rb * ns == row0 and cb * width == col0 and ns % tb == 0
    assert ns & (ns - 1) == 0 and tb % 2 == 0 and dk == LANES

    def tok(k):
        return pl.BlockSpec((ns, width), lambda t: (rb, cb + k))

    vec = pl.BlockSpec((1, width), lambda t: (0, 0))
    sspec = pl.BlockSpec((tb, nh, dk, dk), lambda t: (t, 0, 0, 0))
    return pl.pallas_call(
        functools.partial(_hg_sample_body, dk=dk),
        out_shape=(jax.ShapeDtypeStruct((ns, width), F32),
                   jax.ShapeDtypeStruct(s0.shape, F32)),
        grid=(ns // tb,),
        in_specs=[tok(0), tok(1), tok(2), tok(3), vec, vec, sspec],
        out_specs=(pl.BlockSpec((tb, width), lambda t: (t, 0)), sspec),
        scratch_shapes=[pltpu.VMEM((2 * nh * dk, 2 * ns), BF16)],
        compiler_params=_cparams("arbitrary"),
        name="hgrn_sample",
    )(proj, proj, proj, proj, lb, gn, s0)


def _merge_body(z_ref, zs_ref, hg_ref, hgs_ref, gs0_ref, gs1_ref, gh0_ref, gh1_ref, x_ref,
                wglu_ref, bglu_ref, wbs_ref, wbh_ref, wout_ref, post_ref, o_ref, *, ns):
    i = pl.program_id(0)
    last_i = pl.num_programs(0) - 1
    cut = x_ref.shape[0] - ns
    half = gs0_ref.shape[1]

    def rows(z, hg, sl):
        s5o = z * jax.nn.sigmoid(_dot(z.astype(BF16), wglu_ref[...]) + bglu_ref[...])
        a = _dot(s5o.astype(BF16), wbs_ref[...])
        b = _dot(hg.astype(BF16), wbh_ref[...])
        m0 = (jax.nn.sigmoid(gs0_ref[sl, :]) * a[:, :half]
              + jax.nn.sigmoid(gh0_ref[sl, :]) * b[:, :half])
        m1 = (jax.nn.sigmoid(gs1_ref[sl, :]) * a[:, half:]
              + jax.nn.sigmoid(gh1_ref[sl, :]) * b[:, half:])
        mix = _dot(m0.astype(BF16), wout_ref[:half, :]) + _dot(m1.astype(BF16), wout_ref[half:, :])
        o_ref[sl, :] = x_ref[sl, :] + _rms(mix, post_ref[...])

    @pl.when(i < last_i)
    def _():
        rows(z_ref[...], hg_ref[...], slice(None))

    @pl.when(i == last_i)
    def _():
        rows(z_ref[:cut, :], hg_ref[:cut, :], slice(0, cut))
        rows(zs_ref[...], hgs_ref[...], slice(cut, None))


def _merge(z, zs, hg, hgs, proj, x, wglu, bglu, wbs, wbh, wout, post, *, col0, tm):
    m, d = x.shape
    w = z.shape[1]
    ns = zs.shape[0]
    cb = col0 // w
    assert cb * w == col0 and d == 2 * w and m % tm == 0 and 0 < ns < tm

    def gate(k):
        return pl.BlockSpec((tm, w), lambda i: (i, cb + k))

    def const(shape):
        return pl.BlockSpec(shape, lambda i: (0, 0), pipeline_mode=pl.Buffered(1))

    tok = pl.BlockSpec((tm, w), lambda i: (i, 0))
    return pl.pallas_call(
        functools.partial(_merge_body, ns=ns),
        out_shape=jax.ShapeDtypeStruct((m, d), F32),
        grid=(m // tm,),
        in_specs=[
            tok, const((ns, w)), tok, const((ns, w)),
            gate(0), gate(1), gate(2), gate(3),
            pl.BlockSpec((tm, d), lambda i: (i, 0)),
            const((w, w)), const((1, w)), const((w, d)), const((w, d)), const((d, d)), const((1, d)),
        ],
        out_specs=pl.BlockSpec((tm, d), lambda i: (i, 0)),
        compiler_params=_cparams("parallel"),
        name="merge",
    )(z, zs, hg, hgs, proj, proj, proj, proj, x, wglu, bglu, wbs, wbh, wout, post)


def kernel(x_prompt, x_sample, state_s5_re, state_s5_im, state_hgrn, ffn1_pre_norm, ffn1_w_gate, ffn1_w_up, ffn1_w_down, ffn1_post_norm, mix_pre_norm, w_in, s5_lambda_re, s5_lambda_im, s5_log_dt, s5_b_re, s5_b_im, s5_c_re, s5_c_im, s5_d, s5_w_glu, s5_b_glu, hgrn_lb_logits, hgrn_out_norm, w_branch_s5, w_branch_hgrn, w_out, mix_post_norm, ffn2_pre_norm, ffn2_w_gate, ffn2_w_up, ffn2_w_down, ffn2_post_norm):
    depth = ffn1_w_gate.shape[0]
    assert depth == 1
    batch, seq, d = x_prompt.shape
    ns = x_sample.shape[0]
    assert x_sample.shape[1] == 1
    g, p = s5_lambda_re.shape[1:]
    nh, dk = state_hgrn.shape[2], state_hgrn.shape[3]
    s5w = s5_d.shape[1]
    hgw = nh * dk
    mp = batch * seq
    m = mp + ns

    lay = lambda a: a.reshape(a.shape[1:])
    row = lambda a: a.reshape(1, -1)

    lb = jax.nn.softmax(hgrn_lb_logits.astype(F32), axis=0)[:1]

    x, w_in_b = _ffn(
        x_prompt.reshape(mp, d), x_sample.reshape(ns, d), row(ffn1_pre_norm),
        lay(ffn1_w_gate), lay(ffn1_w_up), lay(ffn1_w_down), row(ffn1_post_norm),
        ns=ns, split_out=False, side=(lay(w_in),))
    proj = _inproj(x, row(mix_pre_norm), w_in_b)

    tw_in, tw_out, a_re, a_im = _s5_tables(
        lay(s5_lambda_re), lay(s5_lambda_im), lay(s5_log_dt), lay(s5_b_re), lay(s5_b_im),
        lay(s5_c_re), lay(s5_c_im))
    d_skip = row(s5_d)
    z, hlast, wg2_b, wu2_b, wd2_b, wglu_b, wbs_b, wbh_b, wout_b = _s5_prompt(
        proj, tw_in.astype(BF16), tw_out.astype(BF16), a_re, a_im, d_skip,
        batch=batch, seq=seq, rows_total=mp,
        side=(lay(ffn2_w_gate), lay(ffn2_w_up), lay(ffn2_w_down), lay(s5_w_glu),
              lay(w_branch_s5), lay(w_branch_hgrn), lay(w_out)))
    zs, s_re, s_im = _s5_sample(proj, state_s5_re.reshape(ns, g * p),
                                state_s5_im.reshape(ns, g * p),
                                tw_in, tw_out, a_re, a_im, d_skip, row0=mp)
    half = hlast.shape[-1] // 2
    p_re = hlast[:, :, SUBLANES - 1, :half].reshape(1, batch, g, p)
    p_im = hlast[:, :, SUBLANES - 1, half:].reshape(1, batch, g, p)

    gn = row(hgrn_out_norm)
    hg, st_p = _hg_prompt(proj, lb, gn, batch=batch, seq=seq, rows_total=mp, nh=nh, col0=s5w)
    hgs, st_s = _hg_sample(proj, lb, gn, lay(state_hgrn), row0=mp, nh=nh, col0=s5w)

    x = _merge(z, zs, hg, hgs, proj, x, wglu_b, row(s5_b_glu), wbs_b, wbh_b, wout_b,
               row(mix_post_norm), col0=s5w + 4 * hgw, tm=MERGE_ROW_TILE)
    yp, ys = _ffn(x, None, row(ffn2_pre_norm), wg2_b, wu2_b, wd2_b, row(ffn2_post_norm),
                  ns=ns, split_out=True)

    return (yp.reshape(batch, seq, d), ys.reshape(ns, 1, d),
            p_re, p_im, jnp.swapaxes(st_p, -1, -2)[None],
            s_re.reshape(1, ns, g, p), s_im.reshape(1, ns, g, p), st_s[None])
```

```python
import functools

import jax
import jax.numpy as jnp
import numpy as np
from jax import lax
from jax.experimental import pallas as pl
from jax.experimental.pallas import tpu as pltpu

F32 = jnp.float32
BF16 = jnp.bfloat16
EPS = 1e-6
HIGHEST = lax.Precision.HIGHEST

LANES = 128
SUBLANES = 8
VMEM_LIMIT = 60 * 1024 * 1024

FFN_TILES_BF16 = (832, 512)
FFN_TILES_F32 = (1040, 256)
IN_TILE = 1536
IN_ROW_TILE = 1040
MERGE_ROW_TILE = 416
HG_CHUNK = 64
HG_CHUNKS_PER_STEP = 4
SAMPLE_TOKENS_PER_STEP = 8


def _cparams(*sem):
    return pltpu.CompilerParams(dimension_semantics=sem, vmem_limit_bytes=VMEM_LIMIT)


def _rms(x, g):
    return x * lax.rsqrt(jnp.mean(x * x, axis=-1, keepdims=True) + EPS) * g


def _silu(x):
    return x * jax.nn.sigmoid(x)


def _dot(a, b):
    return jnp.dot(a, b, preferred_element_type=F32)


def _dot_hi(a, b):
    return jnp.dot(a, b, preferred_element_type=F32, precision=HIGHEST)


def _dot_nt(a, b):
    return lax.dot_general(a, b, (((1,), (1,)), ((), ())), preferred_element_type=F32)


def _dot_tn(a, b):
    return lax.dot_general(a, b, (((0,), (0,)), ((), ())), preferred_element_type=F32)


BF16_ROWS = 16


def _side_specs(arrays, nsteps, step_of):
    in_specs, out_specs, out_shapes, plan = [], [], [], []
    for a in arrays:
        r, c = a.shape
        rows = next(t for t in range(BF16_ROWS, r + 1, BF16_ROWS) if r % t == 0 and r // t <= nsteps)
        nblk = r // rows
        every = nsteps // nblk

        def imap(*g, nblk=nblk, every=every):
            return (jnp.minimum(step_of(*g) // every, nblk - 1), 0)

        in_specs.append(pl.BlockSpec((rows, c), imap))
        out_specs.append(pl.BlockSpec((rows, c), imap))
        out_shapes.append(jax.ShapeDtypeStruct((r, c), BF16))
        plan.append((nblk, every))
    return in_specs, out_specs, out_shapes, tuple(plan)


def _side_cast(src_refs, dst_refs):
    for src, dst in zip(src_refs, dst_refs):
        dst[...] = src[...].astype(BF16)


def _ffn_body(*refs, ns, split_in, split_out, side_plan):
    refs = list(refs)
    nside = len(side_plan)
    x_ref = refs.pop(0)
    xs_ref = refs.pop(0) if split_in else None
    pre_ref, wg_ref, wu_ref, wd_ref, post_ref = refs[:5]
    side_src = refs[5:5 + nside]
    o_ref = refs[5 + nside]
    os_ref = refs[6 + nside] if split_out else None
    side_dst = refs[-1 - nside:-1]
    h_ref = refs[-1]
    acc_ref = o_ref
    i, k = pl.program_id(0), pl.program_id(1)
    last_i = pl.num_programs(0) - 1
    last_k = pl.num_programs(1) - 1
    cut = h_ref.shape[0] - ns

    def swiglu_down(h):
        _side_cast(side_src, side_dst)
        g = _dot(h, wg_ref[...].astype(BF16))
        u = _dot(h, wu_ref[...].astype(BF16))
        return _dot((_silu(g) * u).astype(BF16), wd_ref[...].astype(BF16))

    @pl.when((i < last_i) & (k == 0))
    def _():
        h = _rms(x_ref[...], pre_ref[...]).astype(BF16)
        h_ref[...] = h
        acc_ref[...] = swiglu_down(h)

    @pl.when((i < last_i) & (k == last_k))
    def _():
        acc = acc_ref[...] + swiglu_down(h_ref[...])
        o_ref[...] = x_ref[...] + 0.5 * _rms(acc, post_ref[...])

    @pl.when((i == last_i) & (k == 0))
    def _():
        h_ref[:cut, :] = _rms(x_ref[:cut, :], pre_ref[...]).astype(BF16)
        xt = xs_ref[...] if split_in else x_ref[cut:, :]
        h_ref[cut:, :] = _rms(xt, pre_ref[...]).astype(BF16)
        acc_ref[...] = jnp.zeros_like(acc_ref)

    @pl.when(((k > 0) & (k < last_k)) | (i == last_i))
    def _():
        acc_ref[...] += swiglu_down(h_ref[...])

    @pl.when((i == last_i) & (k == last_k))
    def _():
        o_ref[:cut, :] = x_ref[:cut, :] + 0.5 * _rms(acc_ref[:cut, :], post_ref[...])
        xt = xs_ref[...] if split_in else x_ref[cut:, :]
        tail = xt + 0.5 * _rms(acc_ref[cut:, :], post_ref[...])
        if split_out:
            os_ref[...] = tail
        else:
            o_ref[cut:, :] = tail


def _ffn(x, xs, pre, wg, wu, wd, post, *, ns, split_out, side=()):
    split_in = xs is not None
    d = x.shape[1]
    m = x.shape[0] + (ns if split_in else 0)
    dff = wg.shape[1]
    tm, tf = (FFN_TILES_F32 if wg.dtype == F32 else FFN_TILES_BF16)
    assert m % tm == 0 and dff % tf == 0 and 0 < ns < tm and ns % BF16_ROWS == 0
    nk = dff // tf
    side_in, side_out, side_shapes, side_plan = _side_specs(
        side, (m // tm) * nk, lambda i, k: i * nk + k)
    tok = pl.BlockSpec((tm, d), lambda i, k: (i, 0))
    smp = pl.BlockSpec((ns, d), lambda i, k: (0, 0))
    vec = pl.BlockSpec((1, d), lambda i, k: (0, 0))
    in_specs = [tok] + ([smp] if split_in else []) + [
        vec,
        pl.BlockSpec((d, tf), lambda i, k: (0, k)),
        pl.BlockSpec((d, tf), lambda i, k: (0, k)),
        pl.BlockSpec((tf, d), lambda i, k: (k, 0)),
        vec,
    ] + side_in
    if split_out:
        out_shape = [jax.ShapeDtypeStruct((m - ns, d), F32), jax.ShapeDtypeStruct((ns, d), F32)]
        out_specs = [tok, smp]
    else:
        out_shape = [jax.ShapeDtypeStruct((m, d), F32)]
        out_specs = [tok]
    args = [x] + ([xs] if split_in else []) + [pre, wg, wu, wd, post] + list(side)
    sequential_rows = split_out or bool(side)
    return pl.pallas_call(
        functools.partial(_ffn_body, ns=ns, split_in=split_in, split_out=split_out,
                          side_plan=side_plan),
        out_shape=out_shape + side_shapes,
        grid=(m // tm, nk),
        in_specs=in_specs,
        out_specs=out_specs + side_out,
        scratch_shapes=[pltpu.VMEM((tm, d), BF16)],
        compiler_params=_cparams("arbitrary" if sequential_rows else "parallel", "arbitrary"),
        name="ffn",
    )(*args)


def _inproj_body(x_ref, g_ref, w_ref, o_ref, h_ref):
    j = pl.program_id(1)

    @pl.when(j == 0)
    def _():
        h = _rms(x_ref[...], g_ref[...]).astype(BF16)
        h_ref[...] = h
        o_ref[...] = _dot(h, w_ref[...])

    @pl.when(j > 0)
    def _():
        o_ref[...] = _dot(h_ref[...], w_ref[...])


def _inproj(x, g, w):
    m, d = x.shape
    n = w.shape[1]
    tm, tn = IN_ROW_TILE, IN_TILE
    assert m % tm == 0
    return pl.pallas_call(
        _inproj_body,
        out_shape=jax.ShapeDtypeStruct((m, n), F32),
        grid=(m // tm, n // tn),
        in_specs=[
            pl.BlockSpec((tm, d), lambda i, j: (i, 0)),
            pl.BlockSpec((1, d), lambda i, j: (0, 0)),
            pl.BlockSpec((d, tn), lambda i, j: (0, j)),
        ],
        out_specs=pl.BlockSpec((tm, tn), lambda i, j: (i, j)),
        scratch_shapes=[pltpu.VMEM((tm, d), BF16)],
        compiler_params=_cparams("parallel", "arbitrary"),
        name="inproj",
    )(x, g, w)


def _s5_tables(lam_re, lam_im, log_dt, b_re, b_im, c_re, c_im):
    g, p = lam_re.shape
    n = b_re.shape[-1]
    gpt = LANES // n
    nv = g // gpt
    dt = jnp.exp(log_dt)[:, None]
    er = jnp.exp(lam_re * dt)
    th = lam_im * dt
    a_re, a_im = er * jnp.cos(th), er * jnp.sin(th)
    den = lam_re * lam_re + lam_im * lam_im
    k_re = ((a_re - 1.0) * lam_re + a_im * lam_im) / den
    k_im = (a_im * lam_re - (a_re - 1.0) * lam_im) / den
    bb_re = k_re[..., None] * b_re - k_im[..., None] * b_im
    bb_im = k_re[..., None] * b_im + k_im[..., None] * b_re
    sw = 2 * gpt * p
    lane_group = (jnp.arange(sw) // p) % gpt
    io_group = jnp.arange(gpt * n) // n
    bb = jnp.stack([bb_re, bb_im]).reshape(2, nv, gpt, p, n)
    dense_in = bb.transpose(1, 4, 0, 2, 3).reshape(nv, n, sw)
    w_in = jnp.where(io_group[:, None] == lane_group[None, :],
                     jnp.tile(dense_in, (1, gpt, 1)), 0.0)
    cc = jnp.stack([c_re, -c_im]).reshape(2, nv, gpt, n, p)
    dense_out = cc.transpose(1, 0, 2, 4, 3).reshape(nv, sw, n)
    w_out = jnp.where(lane_group[:, None] == io_group[None, :],
                      jnp.tile(dense_out, (1, 1, gpt)), 0.0)
    return w_in, w_out, a_re.reshape(nv, 1, gpt * p), a_im.reshape(nv, 1, gpt * p)


def _cmul(ar, ai, br, bi):
    return ar * br - ai * bi, ar * bi + ai * br


def _s5_prompt_body(*refs, seg, side_plan):
    nside = len(side_plan)
    u_ref, w_ref, cm_ref, ar_ref, ai_ref, d_ref = refs[:6]
    z_ref, hl_ref = refs[6 + nside:8 + nside]
    buh_ref, ut_ref, pad_ref = refs[-3:]
    nseg = SUBLANES
    nc = w_ref.shape[2] // LANES
    hc = nc // 2
    pitch = pad_ref.shape[0] // nseg
    nblk = nseg
    steps = seg // nseg

    def gather(b):
        for t in range(b * steps, (b + 1) * steps):
            ut_ref[t * nseg:(t + 1) * nseg, :] = pad_ref[pl.ds(t, nseg, stride=pitch), :]

    def scatter(b):
        for t in range(b * steps, (b + 1) * steps):
            pad_ref[pl.ds(t, nseg, stride=pitch), :] = ut_ref[t * nseg:(t + 1) * nseg, :]

    for j in range(nseg):
        pad_ref[j * pitch:j * pitch + seg, :] = u_ref[j * seg:(j + 1) * seg, :]

    def project_in(b):
        blk = slice(b * seg, (b + 1) * seg)
        bu = _dot(ut_ref[blk, :].astype(BF16), w_ref[0])
        for c in range(nc):
            buh_ref[c, blk, :] = bu[:, c * LANES:(c + 1) * LANES]

    def project_out(b):
        blk = slice(b * seg, (b + 1) * seg)
        y = d_ref[...] * ut_ref[blk, :]
        for c in range(nc):
            y = y + _dot(buh_ref[c, blk, :].astype(BF16), cm_ref[0, c * LANES:(c + 1) * LANES, :])
        ut_ref[blk, :] = jax.nn.gelu(y)

    ar = [jnp.broadcast_to(ar_ref[0, :, c * LANES:(c + 1) * LANES], (nseg, LANES)) for c in range(hc)]
    ai = [jnp.broadcast_to(ai_ref[0, :, c * LANES:(c + 1) * LANES], (nseg, LANES)) for c in range(hc)]

    def advance(t, hs, store):
        rows = slice(t * nseg, (t + 1) * nseg)
        out = []
        for c in range(hc):
            pr, pi = _cmul(ar[c], ai[c], hs[2 * c], hs[2 * c + 1])
            out += [pr + buh_ref[c, rows, :], pi + buh_ref[hc + c, rows, :]]
        if store:
            for c in range(hc):
                buh_ref[c, rows, :] = out[2 * c]
                buh_ref[hc + c, rows, :] = out[2 * c + 1]
        return out

    zeros = [jnp.zeros((nseg, LANES), F32) for _ in range(2 * hc)]
    hs = zeros
    gather(0)
    project_in(0)
    for b in range(nblk):
        if b + 1 < nblk:
            gather(b + 1)
            project_in(b + 1)
        for t in range(b * steps, (b + 1) * steps):
            hs = advance(t, hs, store=False)
    ends = hs

    pw = [(ar[c], ai[c]) for c in range(hc)]
    for _ in range(seg.bit_length() - 1):
        pw = [_cmul(r, i, r, i) for r, i in pw]

    row = lax.broadcasted_iota(jnp.int32, (nseg, LANES), 0)
    init = list(zeros)
    for j in range(1, nseg):
        for c in range(hc):
            pr, pi = _cmul(pw[c][0], pw[c][1], init[2 * c], init[2 * c + 1])
            nr = pltpu.roll(pr + ends[2 * c], 1, 0)
            ni = pltpu.roll(pi + ends[2 * c + 1], 1, 0)
            init[2 * c] = jnp.where(row == j, nr, init[2 * c])
            init[2 * c + 1] = jnp.where(row == j, ni, init[2 * c + 1])

    hs = init
    for b in range(nblk):
        for t in range(b * steps, (b + 1) * steps):
            hs = advance(t, hs, store=True)
        if b > 0:
            project_out(b - 1)
            scatter(b - 1)
    for c in range(hc):
        hl_ref[0, 0, :, c * LANES:(c + 1) * LANES] = hs[2 * c]
        hl_ref[0, 0, :, (hc + c) * LANES:(hc + c + 1) * LANES] = hs[2 * c + 1]
    _side_cast(refs[6:6 + nside], refs[8 + nside:-3])
    project_out(nblk - 1)
    scatter(nblk - 1)
    for j in range(nseg):
        z_ref[j * seg:(j + 1) * seg, :] = pad_ref[j * pitch:j * pitch + seg, :]


def _s5_prompt(proj, w_in, w_out, a_re, a_im, d_skip, *, batch, seq, rows_total, side=()):
    nv = w_in.shape[0]
    sw = w_in.shape[2]
    seg = seq // SUBLANES
    assert seg * SUBLANES == seq and seg & (seg - 1) == 0
    side_in, side_out, side_shapes, side_plan = _side_specs(
        side, batch * nv, lambda b, v: b * nv + v)
    body = functools.partial(_s5_prompt_body, seg=seg, side_plan=side_plan)
    return pl.pallas_call(
        body,
        out_shape=[jax.ShapeDtypeStruct((rows_total, nv * LANES), F32),
                   jax.ShapeDtypeStruct((batch, nv, SUBLANES, sw), F32)] + side_shapes,
        grid=(batch, nv),
        in_specs=[
            pl.BlockSpec((seq, LANES), lambda b, v: (b, v)),
            pl.BlockSpec((1, LANES, sw), lambda b, v: (v, 0, 0)),
            pl.BlockSpec((1, sw, LANES), lambda b, v: (v, 0, 0)),
            pl.BlockSpec((1, 1, sw // 2), lambda b, v: (v, 0, 0)),
            pl.BlockSpec((1, 1, sw // 2), lambda b, v: (v, 0, 0)),
            pl.BlockSpec((1, LANES), lambda b, v: (0, v)),
        ] + side_in,
        out_specs=[pl.BlockSpec((seq, LANES), lambda b, v: (b, v)),
                   pl.BlockSpec((1, 1, SUBLANES, sw), lambda b, v: (b, v, 0, 0))] + side_out,
        scratch_shapes=[pltpu.VMEM((sw // LANES, seq, LANES), F32),
                        pltpu.VMEM((seq, LANES), F32),
                        pltpu.VMEM((SUBLANES * (seg + SUBLANES), LANES), F32)],
        compiler_params=_cparams(*(("arbitrary",) * 2 if side else ("parallel",) * 2)),
        name="s5_prompt",
    )(proj, w_in, w_out, a_re, a_im, d_skip, *side)


def _s5_sample_body(u_ref, hre_ref, him_ref, w_ref, cm_ref, ar_ref, ai_ref, d_ref,
                    z_ref, ore_ref, oim_ref):
    half = ar_ref.shape[2]
    u = u_ref[...]
    bu = _dot_hi(u, w_ref[0])
    pr, pi = _cmul(ar_ref[0], ai_ref[0], hre_ref[...], him_ref[...])
    hr = pr + bu[:, :half]
    hi = pi + bu[:, half:]
    ore_ref[...] = hr
    oim_ref[...] = hi
    y = d_ref[...] * u + _dot_hi(hr, cm_ref[0, :half, :]) + _dot_hi(hi, cm_ref[0, half:, :])
    z_ref[...] = jax.nn.gelu(y)


def _s5_sample(proj, h_re, h_im, w_in, w_out, a_re, a_im, d_skip, *, row0):
    nv = w_in.shape[0]
    sw = w_in.shape[2]
    ns = h_re.shape[0]
    rb = row0 // ns
    assert rb * ns == row0
    st = pl.BlockSpec((ns, sw // 2), lambda v: (0, v))
    return pl.pallas_call(
        _s5_sample_body,
        out_shape=(jax.ShapeDtypeStruct((ns, nv * LANES), F32),
                   jax.ShapeDtypeStruct(h_re.shape, F32),
                   jax.ShapeDtypeStruct(h_im.shape, F32)),
        grid=(nv,),
        in_specs=[
            pl.BlockSpec((ns, LANES), lambda v: (rb, v)),
            st, st,
            pl.BlockSpec((1, LANES, sw), lambda v: (v, 0, 0)),
            pl.BlockSpec((1, sw, LANES), lambda v: (v, 0, 0)),
            pl.BlockSpec((1, 1, sw // 2), lambda v: (v, 0, 0)),
            pl.BlockSpec((1, 1, sw // 2), lambda v: (v, 0, 0)),
            pl.BlockSpec((1, LANES), lambda v: (0, v)),
        ],
        out_specs=(pl.BlockSpec((ns, LANES), lambda v: (0, v)), st, st),
        compiler_params=_cparams("parallel"),
        name="s5_sample",
    )(proj, h_re, h_im, w_in, w_out, a_re, a_im, d_skip)


def _hg_span_matrices(ch):
    t = np.arange(ch)[:, None]
    r = np.arange(ch)[None, :]
    mats = [r <= t, r > t]
    for v in range(1, ch.bit_length() - 1):
        base = (t >> v) << v
        upper = ((t >> v) & 1) == 1
        mats.append(np.where(upper, (r >= base) & (r <= t), (r > t) & (r < base + (1 << v))))
    mm = np.stack(mats).astype(np.float32)
    return jnp.asarray(np.concatenate([mm, mm], axis=-1), BF16)


def _hg_prompt_body(*refs, dk, side_plan):
    nside = len(side_plan)
    q_ref, f_ref, i_ref, og_ref, lb_ref, gn_ref, mm_ref = refs[:7]
    o_ref, sfin_ref = refs[7 + nside:9 + nside]
    st_ref = refs[-1]
    _side_cast(refs[7:7 + nside], refs[9 + nside:-1])
    c = pl.program_id(1)
    nh = st_ref.shape[0]
    ch = mm_ref.shape[1]
    nsub = q_ref.shape[0] // ch
    width = q_ref.shape[1]
    nlev = ch.bit_length() - 1

    @pl.when(c == 0)
    def _():
        st_ref[...] = jnp.zeros_like(st_ref)

    lb = lb_ref[...]
    f_all = lb + (1.0 - lb) * jax.nn.sigmoid(f_ref[...])
    kk_all = 1.0 - f_all
    qs_all = _silu(q_ref[...])
    logf_pieces = _split3(jnp.log2(f_all))[:2]
    ib_all = i_ref[...].astype(BF16)
    pairs = [slice(2 * j * dk, 2 * (j + 1) * dk) for j in range(nh // 2)]
    heads = [slice(h * dk, (h + 1) * dk) for h in range(nh)]
    r = lax.broadcasted_iota(jnp.int32, (ch, 2 * ch), 0)
    s = lax.broadcasted_iota(jnp.int32, (ch, 2 * ch), 1) & (ch - 1)
    xr = jnp.where(r > s, r ^ s, 0)
    zero_k = jnp.zeros((ch, dk), BF16)
    zero_s = jnp.zeros((dk, dk), BF16)

    def block_diag(a, b, z):
        return jnp.concatenate([jnp.concatenate([a, z], axis=1),
                                jnp.concatenate([z, b], axis=1)], axis=0)

    def pair_scores(lhs, x):
        return _dot_nt(lhs, block_diag(x[:, :dk], x[:, dk:], zero_k))

    def upper_runs(v):
        m = 1 << v
        return [(b0, b0 + m) for b0 in range(m, ch, 2 * m)]

    def mix_rows(v, qs, kk):
        m = 1 << v
        if m >= SUBLANES:
            return jnp.concatenate(
                [(qs if (b0 // m) & 1 else kk)[b0:b0 + m] for b0 in range(0, ch, m)], axis=0)
        pick = ((lax.broadcasted_iota(jnp.int32, (1, SUBLANES, width), 1) >> v) & 1) == 1
        shape3 = (ch // SUBLANES, SUBLANES, width)
        return jnp.where(pick, qs.reshape(shape3), kk.reshape(shape3)).reshape(ch, width)

    staged = []
    for sub in range(nsub):
        rows = slice(sub * ch, (sub + 1) * ch)
        kk, qs = kk_all[rows], qs_all[rows]
        logf2 = jnp.concatenate([p[rows] for p in logf_pieces], axis=0)

        def decay(idx):
            return jnp.exp2(_dot(mm_ref[idx], logf2))

        eg = decay(0)
        qg = (qs * eg).astype(BF16)
        dec = eg[ch - 1:ch, :]
        kh = (kk * decay(1)).astype(BF16)
        lev = [mix_rows(0, qs * f_all[rows], kk).astype(BF16)]
        lev += [(decay(1 + v) * mix_rows(v, qs, kk)).astype(BF16) for v in range(1, nlev)]
        qb, kb = qs.astype(BF16), kk.astype(BF16)
        atts = []
        for sl in pairs:
            att = jnp.where(r == s, pair_scores(qb[:, sl], kb[:, sl]), 0.0)
            for v in range(nlev):
                x = lev[v][:, sl]
                m = 1 << v
                if m >= BF16_ROWS:
                    runs = upper_runs(v)
                    p = pair_scores(jnp.concatenate([x[a:b] for a, b in runs], axis=0), x)
                    parts = []
                    for n in range(len(runs)):
                        parts += [jnp.zeros((m, 2 * ch), F32), p[n * m:(n + 1) * m]]
                    p = jnp.concatenate(parts, axis=0)
                else:
                    p = pair_scores(x, x)
                att = jnp.where((xr >> v) == 1, p, att)
            atts.append(att.astype(BF16))
        staged.append((rows, qg, kh, dec, atts))

    for rows, qg, kh, dec, atts in staged:
        ib = ib_all[rows]
        for j, sl in enumerate(pairs):
            st2 = block_diag(st_ref[2 * j].astype(BF16), st_ref[2 * j + 1].astype(BF16), zero_s)
            ib2 = block_diag(ib[:, sl][:, :dk], ib[:, sl][:, dk:], zero_k)
            o = _dot_nt(qg[:, sl], st2) + _dot(atts[j], ib2)
            for e in range(2):
                hs = heads[2 * j + e]
                og = og_ref[rows, hs]
                o_ref[rows, hs] = _rms(o[:, e * dk:(e + 1) * dk], gn_ref[:, hs]) * _silu(og)
        for h, hs in enumerate(heads):
            st_ref[h] = st_ref[h] * dec[:, hs] + _dot_tn(ib[:, hs], kh[:, hs])

    @pl.when(c == pl.num_programs(1) - 1)
    def _():
        sfin_ref[0] = st_ref[...]


def _hg_prompt(proj, lb, gn, *, batch, seq, rows_total, nh, col0, side=()):
    width = lb.shape[1]
    dk = width // nh
    ch = HG_CHUNK * HG_CHUNKS_PER_STEP
    nchunk = seq // ch
    cb = col0 // width
    assert cb * width == col0 and nchunk * ch == seq

    def tok(k):
        return pl.BlockSpec((ch, width), lambda b, c: (b * nchunk + c, cb + k))

    vec = pl.BlockSpec((1, width), lambda b, c: (0, 0))
    mm = _hg_span_matrices(HG_CHUNK)
    side_in, side_out, side_shapes, side_plan = _side_specs(
        side, batch * nchunk, lambda b, c: b * nchunk + c)
    return pl.pallas_call(
        functools.partial(_hg_prompt_body, dk=dk, side_plan=side_plan),
        out_shape=[jax.ShapeDtypeStruct((rows_total, width), F32),
                   jax.ShapeDtypeStruct((batch, nh, dk, dk), F32)] + side_shapes,
        grid=(batch, nchunk),
        in_specs=[tok(0), tok(1), tok(2), tok(3), vec, vec,
                  pl.BlockSpec(mm.shape, lambda b, c: (0, 0, 0))] + side_in,
        out_specs=[pl.BlockSpec((ch, width), lambda b, c: (b * nchunk + c, 0)),
                   pl.BlockSpec((1, nh, dk, dk), lambda b, c: (b, 0, 0, 0))] + side_out,
        scratch_shapes=[pltpu.VMEM((nh, dk, dk), F32)],
        compiler_params=_cparams("arbitrary" if side else "parallel", "arbitrary"),
        name="hgrn_prompt",
    )(proj, proj, proj, proj, lb, gn, mm, *side)


def _split3(x):
    p1 = x.astype(BF16)
    r1 = x - p1.astype(F32)
    p2 = r1.astype(BF16)
    p3 = (r1 - p2.astype(F32)).astype(BF16)
    return p1, p2, p3


def _hg_sample_body(q_ref, f_ref, i_ref, og_ref, lb_ref, gn_ref, s0_ref,
                    hg_ref, s_ref, fq_ref, *, dk):
    step = pl.program_id(0)
    nt = q_ref.shape[0]
    tb, nh = s0_ref.shape[0], s0_ref.shape[1]
    npiece = fq_ref.shape[1] // nt

    @pl.when(step == 0)
    def _():
        lb = lb_ref[...]
        f = lb + (1.0 - lb) * jax.nn.sigmoid(f_ref[...])
        qs = _silu(q_ref[...])
        for src, base in ((f, 0), (qs, nh)):
            for h in range(nh):
                t = src[:, h * dk:(h + 1) * dk].T
                for p, piece in enumerate(_split3(t)[:npiece]):
                    fq_ref[(base + h) * dk:(base + h + 1) * dk, p * nt:(p + 1) * nt] = piece

    tok = lax.broadcasted_iota(jnp.int32, (npiece * nt, 2 * dk), 0) & (nt - 1)
    second = lax.broadcasted_iota(jnp.int32, (npiece * nt, 2 * dk), 1) >= dk
    for j0 in range(0, tb, 2):
        n0 = step * tb + j0
        onehot = (tok == jnp.where(second, n0 + 1, n0)).astype(BF16)
        fq2 = _dot(fq_ref[...], onehot)
        for e in range(2):
            j, n = j0 + e, n0 + e
            fq = fq2[:, e * dk:(e + 1) * dk]
            irow = i_ref[pl.ds(n, 1), :]
            ogrow = og_ref[pl.ds(n, 1), :]
            for h in range(nh):
                sl = slice(h * dk, (h + 1) * dk)
                fb = fq[h * dk:(h + 1) * dk, :]
                qb = fq[(nh + h) * dk:(nh + h + 1) * dk, :]
                s1 = fb * s0_ref[j, h] + (1.0 - fb) * irow[:, sl]
                s_ref[j, h] = s1
                o = jnp.sum(qb * s1, axis=0, keepdims=True)
                hg_ref[j:j + 1, sl] = _rms(o, gn_ref[:, sl]) * _silu(ogrow[:, sl])


def _hg_sample(proj, lb, gn, s0, *, row0, nh, col0):
    width = lb.shape[1]
    dk = width // nh
    ns = s0.shape[0]
    tb = SAMPLE_TOKENS_PER_STEP
    rb, cb = row0 // ns, col0 // width
    assert rb * ns == row0 and cb * width == col0 and ns % tb == 0
    assert ns & (ns - 1) == 0 and tb % 2 == 0 and dk == LANES

    def tok(k):
        return pl.BlockSpec((ns, width), lambda t: (rb, cb + k))

    vec = pl.BlockSpec((1, width), lambda t: (0, 0))
    sspec = pl.BlockSpec((tb, nh, dk, dk), lambda t: (t, 0, 0, 0))
    return pl.pallas_call(
        functools.partial(_hg_sample_body, dk=dk),
        out_shape=(jax.ShapeDtypeStruct((ns, width), F32),
                   jax.ShapeDtypeStruct(s0.shape, F32)),
        grid=(ns // tb,),
        in_specs=[tok(0), tok(1), tok(2), tok(3), vec, vec, sspec],
        out_specs=(pl.BlockSpec((tb, width), lambda t: (t, 0)), sspec),
        scratch_shapes=[pltpu.VMEM((2 * nh * dk, 2 * ns), BF16)],
        compiler_params=_cparams("arbitrary"),
        name="hgrn_sample",
    )(proj, proj, proj, proj, lb, gn, s0)


def _merge_body(z_ref, zs_ref, hg_ref, hgs_ref, gs0_ref, gs1_ref, gh0_ref, gh1_ref, x_ref,
                wglu_ref, bglu_ref, wbs_ref, wbh_ref, wout_ref, post_ref, o_ref, *, ns):
    i = pl.program_id(0)
    last_i = pl.num_programs(0) - 1
    cut = x_ref.shape[0] - ns
    half = gs0_ref.shape[1]

    def rows(z, hg, sl):
        s5o = z * jax.nn.sigmoid(_dot(z.astype(BF16), wglu_ref[...]) + bglu_ref[...])
        a = _dot(s5o.astype(BF16), wbs_ref[...])
        b = _dot(hg.astype(BF16), wbh_ref[...])
        m0 = (jax.nn.sigmoid(gs0_ref[sl, :]) * a[:, :half]
              + jax.nn.sigmoid(gh0_ref[sl, :]) * b[:, :half])
        m1 = (jax.nn.sigmoid(gs1_ref[sl, :]) * a[:, half:]
              + jax.nn.sigmoid(gh1_ref[sl, :]) * b[:, half:])
        mix = _dot(m0.astype(BF16), wout_ref[:half, :]) + _dot(m1.astype(BF16), wout_ref[half:, :])
        o_ref[sl, :] = x_ref[sl, :] + _rms(mix, post_ref[...])

    @pl.when(i < last_i)
    def _():
        rows(z_ref[...], hg_ref[...], slice(None))

    @pl.when(i == last_i)
    def _():
        rows(z_ref[:cut, :], hg_ref[:cut, :], slice(0, cut))
        rows(zs_ref[...], hgs_ref[...], slice(cut, None))


def _merge(z, zs, hg, hgs, proj, x, wglu, bglu, wbs, wbh, wout, post, *, col0, tm):
    m, d = x.shape
    w = z.shape[1]
    ns = zs.shape[0]
    cb = col0 // w
    assert cb * w == col0 and d == 2 * w and m % tm == 0 and 0 < ns < tm

    def gate(k):
        return pl.BlockSpec((tm, w), lambda i: (i, cb + k))

    def const(shape):
        return pl.BlockSpec(shape, lambda i: (0, 0), pipeline_mode=pl.Buffered(1))

    tok = pl.BlockSpec((tm, w), lambda i: (i, 0))
    return pl.pallas_call(
        functools.partial(_merge_body, ns=ns),
        out_shape=jax.ShapeDtypeStruct((m, d), F32),
        grid=(m // tm,),
        in_specs=[
            tok, const((ns, w)), tok, const((ns, w)),
            gate(0), gate(1), gate(2), gate(3),
            pl.BlockSpec((tm, d), lambda i: (i, 0)),
            const((w, w)), const((1, w)), const((w, d)), const((w, d)), const((d, d)), const((1, d)),
        ],
        out_specs=pl.BlockSpec((tm, d), lambda i: (i, 0)),
        compiler_params=_cparams("parallel"),
        name="merge",
    )(z, zs, hg, hgs, proj, proj, proj, proj, x, wglu, bglu, wbs, wbh, wout, post)


def kernel(x_prompt, x_sample, state_s5_re, state_s5_im, state_hgrn, ffn1_pre_norm, ffn1_w_gate, ffn1_w_up, ffn1_w_down, ffn1_post_norm, mix_pre_norm, w_in, s5_lambda_re, s5_lambda_im, s5_log_dt, s5_b_re, s5_b_im, s5_c_re, s5_c_im, s5_d, s5_w_glu, s5_b_glu, hgrn_lb_logits, hgrn_out_norm, w_branch_s5, w_branch_hgrn, w_out, mix_post_norm, ffn2_pre_norm, ffn2_w_gate, ffn2_w_up, ffn2_w_down, ffn2_post_norm):
    depth = ffn1_w_gate.shape[0]
    assert depth == 1
    batch, seq, d = x_prompt.shape
    ns = x_sample.shape[0]
    assert x_sample.shape[1] == 1
    g, p = s5_lambda_re.shape[1:]
    nh, dk = state_hgrn.shape[2], state_hgrn.shape[3]
    s5w = s5_d.shape[1]
    hgw = nh * dk
    mp = batch * seq
    m = mp + ns

    lay = lambda a: a.reshape(a.shape[1:])
    row = lambda a: a.reshape(1, -1)

    lb = jax.nn.softmax(hgrn_lb_logits.astype(F32), axis=0)[:1]

    x, w_in_b = _ffn(
        x_prompt.reshape(mp, d), x_sample.reshape(ns, d), row(ffn1_pre_norm),
        lay(ffn1_w_gate), lay(ffn1_w_up), lay(ffn1_w_down), row(ffn1_post_norm),
        ns=ns, split_out=False, side=(lay(w_in),))
    proj = _inproj(x, row(mix_pre_norm), w_in_b)

    tw_in, tw_out, a_re, a_im = _s5_tables(
        lay(s5_lambda_re), lay(s5_lambda_im), lay(s5_log_dt), lay(s5_b_re), lay(s5_b_im),
        lay(s5_c_re), lay(s5_c_im))
    d_skip = row(s5_d)
    z, hlast, wg2_b, wu2_b, wd2_b, wglu_b, wbs_b, wbh_b, wout_b = _s5_prompt(
        proj, tw_in.astype(BF16), tw_out.astype(BF16), a_re, a_im, d_skip,
        batch=batch, seq=seq, rows_total=mp,
        side=(lay(ffn2_w_gate), lay(ffn2_w_up), lay(ffn2_w_down), lay(s5_w_glu),
              lay(w_branch_s5), lay(w_branch_hgrn), lay(w_out)))
    zs, s_re, s_im = _s5_sample(proj, state_s5_re.reshape(ns, g * p),
                                state_s5_im.reshape(ns, g * p),
                                tw_in, tw_out, a_re, a_im, d_skip, row0=mp)
    half = hlast.shape[-1] // 2
    p_re = hlast[:, :, SUBLANES - 1, :half].reshape(1, batch, g, p)
    p_im = hlast[:, :, SUBLANES - 1, half:].reshape(1, batch, g, p)

    gn = row(hgrn_out_norm)
    hg, st_p = _hg_prompt(proj, lb, gn, batch=batch, seq=seq, rows_total=mp, nh=nh, col0=s5w)
    hgs, st_s = _hg_sample(proj, lb, gn, lay(state_hgrn), row0=mp, nh=nh, col0=s5w)

    x = _merge(z, zs, hg, hgs, proj, x, wglu_b, row(s5_b_glu), wbs_b, wbh_b, wout_b,
               row(mix_post_norm), col0=s5w + 4 * hgw, tm=MERGE_ROW_TILE)
    yp, ys = _ffn(x, None, row(ffn2_pre_norm), wg2_b, wu2_b, wd2_b, row(ffn2_post_norm),
                  ns=ns, split_out=True)

    return (yp.reshape(batch, seq, d), ys.reshape(ns, 1, d),
            p_re, p_im, jnp.swapaxes(st_p, -1, -2)[None],
            s_re.reshape(1, ns, g, p), s_im.reshape(1, ns, g, p), st_s[None])
```

```python
import functools

import jax
import jax.numpy as jnp
import numpy as np
from jax import lax
from jax.experimental import pallas as pl
from jax.experimental.pallas import tpu as pltpu

F32 = jnp.float32
BF16 = jnp.bfloat16
EPS = 1e-6
HIGHEST = lax.Precision.HIGHEST

LANES = 128
SUBLANES = 8
VMEM_LIMIT = 60 * 1024 * 1024

FFN_TILES_BF16 = (832, 512)
FFN_TILES_F32 = (1040, 256)
IN_TILE = 1536
IN_ROW_TILE = 1040
MERGE_ROW_TILE = 416
HG_CHUNK = 64
HG_CHUNKS_PER_STEP = 4
SAMPLE_TOKENS_PER_STEP = 8


def _cparams(*sem):
    return pltpu.CompilerParams(dimension_semantics=sem, vmem_limit_bytes=VMEM_LIMIT)


def _rms(x, g):
    return x * lax.rsqrt(jnp.mean(x * x, axis=-1, keepdims=True) + EPS) * g


def _silu(x):
    return x * jax.nn.sigmoid(x)


def _dot(a, b):
    return jnp.dot(a, b, preferred_element_type=F32)


def _dot_hi(a, b):
    return jnp.dot(a, b, preferred_element_type=F32, precision=HIGHEST)


def _dot_nt(a, b):
    return lax.dot_general(a, b, (((1,), (1,)), ((), ())), preferred_element_type=F32)


def _dot_tn(a, b):
    return lax.dot_general(a, b, (((0,), (0,)), ((), ())), preferred_element_type=F32)


BF16_ROWS = 16


def _side_specs(arrays, nsteps, step_of):
    in_specs, out_specs, out_shapes, plan = [], [], [], []
    for a in arrays:
        r, c = a.shape
        rows = next(t for t in range(BF16_ROWS, r + 1, BF16_ROWS) if r % t == 0 and r // t <= nsteps)
        nblk = r // rows
        every = nsteps // nblk

        def imap(*g, nblk=nblk, every=every):
            return (jnp.minimum(step_of(*g) // every, nblk - 1), 0)

        in_specs.append(pl.BlockSpec((rows, c), imap))
        out_specs.append(pl.BlockSpec((rows, c), imap))
        out_shapes.append(jax.ShapeDtypeStruct((r, c), BF16))
        plan.append((nblk, every))
    return in_specs, out_specs, out_shapes, tuple(plan)


def _side_cast(src_refs, dst_refs):
    for src, dst in zip(src_refs, dst_refs):
        dst[...] = src[...].astype(BF16)


def _ffn_body(*refs, ns, split_in, split_out, side_plan):
    refs = list(refs)
    nside = len(side_plan)
    x_ref = refs.pop(0)
    xs_ref = refs.pop(0) if split_in else None
    pre_ref, wg_ref, wu_ref, wd_ref, post_ref = refs[:5]
    side_src = refs[5:5 + nside]
    o_ref = refs[5 + nside]
    os_ref = refs[6 + nside] if split_out else None
    side_dst = refs[-1 - nside:-1]
    h_ref = refs[-1]
    acc_ref = o_ref
    i, k = pl.program_id(0), pl.program_id(1)
    last_i = pl.num_programs(0) - 1
    last_k = pl.num_programs(1) - 1
    cut = h_ref.shape[0] - ns

    def swiglu_down(h):
        _side_cast(side_src, side_dst)
        g = _dot(h, wg_ref[...].astype(BF16))
        u = _dot(h, wu_ref[...].astype(BF16))
        return _dot((_silu(g) * u).astype(BF16), wd_ref[...].astype(BF16))

    @pl.when((i < last_i) & (k == 0))
    def _():
        h = _rms(x_ref[...], pre_ref[...]).astype(BF16)
        h_ref[...] = h
        acc_ref[...] = swiglu_down(h)

    @pl.when((i < last_i) & (k == last_k))
    def _():
        acc = acc_ref[...] + swiglu_down(h_ref[...])
        o_ref[...] = x_ref[...] + 0.5 * _rms(acc, post_ref[...])

    @pl.when((i == last_i) & (k == 0))
    def _():
        h_ref[:cut, :] = _rms(x_ref[:cut, :], pre_ref[...]).astype(BF16)
        xt = xs_ref[...] if split_in else x_ref[cut:, :]
        h_ref[cut:, :] = _rms(xt, pre_ref[...]).astype(BF16)
        acc_ref[...] = jnp.zeros_like(acc_ref)

    @pl.when(((k > 0) & (k < last_k)) | (i == last_i))
    def _():
        acc_ref[...] += swiglu_down(h_ref[...])

    @pl.when((i == last_i) & (k == last_k))
    def _():
        o_ref[:cut, :] = x_ref[:cut, :] + 0.5 * _rms(acc_ref[:cut, :], post_ref[...])
        xt = xs_ref[...] if split_in else x_ref[cut:, :]
        tail = xt + 0.5 * _rms(acc_ref[cut:, :], post_ref[...])
        if split_out:
            os_ref[...] = tail
        else:
            o_ref[cut:, :] = tail


def _ffn(x, xs, pre, wg, wu, wd, post, *, ns, split_out, side=()):
    split_in = xs is not None
    d = x.shape[1]
    m = x.shape[0] + (ns if split_in else 0)
    dff = wg.shape[1]
    tm, tf = (FFN_TILES_F32 if wg.dtype == F32 else FFN_TILES_BF16)
    assert m % tm == 0 and dff % tf == 0 and 0 < ns < tm and ns % BF16_ROWS == 0
    nk = dff // tf
    side_in, side_out, side_shapes, side_plan = _side_specs(
        side, (m // tm) * nk, lambda i, k: i * nk + k)
    tok = pl.BlockSpec((tm, d), lambda i, k: (i, 0))
    smp = pl.BlockSpec((ns, None, d), lambda i, k: (0, 0, 0))
    vec = pl.BlockSpec((1, d), lambda i, k: (0, 0))
    in_specs = [tok] + ([smp] if split_in else []) + [
        vec,
        pl.BlockSpec((d, tf), lambda i, k: (0, k)),
        pl.BlockSpec((d, tf), lambda i, k: (0, k)),
        pl.BlockSpec((tf, d), lambda i, k: (k, 0)),
        vec,
    ] + side_in
    if split_out:
        out_shape = [jax.ShapeDtypeStruct((m - ns, d), F32), jax.ShapeDtypeStruct((ns, 1, d), F32)]
        out_specs = [tok, smp]
    else:
        out_shape = [jax.ShapeDtypeStruct((m, d), F32)]
        out_specs = [tok]
    args = [x] + ([xs] if split_in else []) + [pre, wg, wu, wd, post] + list(side)
    sequential_rows = split_out or bool(side)
    return pl.pallas_call(
        functools.partial(_ffn_body, ns=ns, split_in=split_in, split_out=split_out,
                          side_plan=side_plan),
        out_shape=out_shape + side_shapes,
        grid=(m // tm, nk),
        in_specs=in_specs,
        out_specs=out_specs + side_out,
        scratch_shapes=[pltpu.VMEM((tm, d), BF16)],
        compiler_params=_cparams("arbitrary" if sequential_rows else "parallel", "arbitrary"),
        name="ffn",
    )(*args)


def _inproj_body(x_ref, g_ref, w_ref, o_ref, h_ref):
    j = pl.program_id(1)

    @pl.when(j == 0)
    def _():
        h = _rms(x_ref[...], g_ref[...]).astype(BF16)
        h_ref[...] = h
        o_ref[...] = _dot(h, w_ref[...])

    @pl.when(j > 0)
    def _():
        o_ref[...] = _dot(h_ref[...], w_ref[...])


def _inproj(x, g, w):
    m, d = x.shape
    n = w.shape[1]
    tm, tn = IN_ROW_TILE, IN_TILE
    assert m % tm == 0
    return pl.pallas_call(
        _inproj_body,
        out_shape=jax.ShapeDtypeStruct((m, n), F32),
        grid=(m // tm, n // tn),
        in_specs=[
            pl.BlockSpec((tm, d), lambda i, j: (i, 0)),
            pl.BlockSpec((1, d), lambda i, j: (0, 0)),
            pl.BlockSpec((d, tn), lambda i, j: (0, j)),
        ],
        out_specs=pl.BlockSpec((tm, tn), lambda i, j: (i, j)),
        scratch_shapes=[pltpu.VMEM((tm, d), BF16)],
        compiler_params=_cparams("parallel", "arbitrary"),
        name="inproj",
    )(x, g, w)


def _s5_tables(lam_re, lam_im, log_dt, b_re, b_im, c_re, c_im):
    g, p = lam_re.shape
    n = b_re.shape[-1]
    gpt = LANES // n
    nv = g // gpt
    dt = jnp.exp(log_dt)[:, None]
    er = jnp.exp(lam_re * dt)
    th = lam_im * dt
    a_re, a_im = er * jnp.cos(th), er * jnp.sin(th)
    den = lam_re * lam_re + lam_im * lam_im
    k_re = ((a_re - 1.0) * lam_re + a_im * lam_im) / den
    k_im = (a_im * lam_re - (a_re - 1.0) * lam_im) / den
    bb_re = k_re[..., None] * b_re - k_im[..., None] * b_im
    bb_im = k_re[..., None] * b_im + k_im[..., None] * b_re
    sw = 2 * gpt * p
    lane_group = (jnp.arange(sw) // p) % gpt
    io_group = jnp.arange(gpt * n) // n
    bb = jnp.stack([bb_re, bb_im]).reshape(2, nv, gpt, p, n)
    dense_in = bb.transpose(1, 4, 0, 2, 3).reshape(nv, n, sw)
    w_in = jnp.where(io_group[:, None] == lane_group[None, :],
                     jnp.tile(dense_in, (1, gpt, 1)), 0.0)
    cc = jnp.stack([c_re, -c_im]).reshape(2, nv, gpt, n, p)
    dense_out = cc.transpose(1, 0, 2, 4, 3).reshape(nv, sw, n)
    w_out = jnp.where(lane_group[:, None] == io_group[None, :],
                      jnp.tile(dense_out, (1, 1, gpt)), 0.0)
    return w_in, w_out, a_re.reshape(nv, 1, gpt * p), a_im.reshape(nv, 1, gpt * p)


def _cmul(ar, ai, br, bi):
    return ar * br - ai * bi, ar * bi + ai * br


def _s5_prompt_body(*refs, seg, side_plan):
    nside = len(side_plan)
    u_ref, w_ref, cm_ref, ar_ref, ai_ref, d_ref = refs[:6]
    z_ref, hl_ref = refs[6 + nside:8 + nside]
    buh_ref, ut_ref, pad_ref = refs[-3:]
    nseg = SUBLANES
    nc = w_ref.shape[2] // LANES
    hc = nc // 2
    pitch = pad_ref.shape[0] // nseg
    nblk = nseg
    steps = seg // nseg

    def gather(b):
        for t in range(b * steps, (b + 1) * steps):
            ut_ref[t * nseg:(t + 1) * nseg, :] = pad_ref[pl.ds(t, nseg, stride=pitch), :]

    def scatter(b):
        for t in range(b * steps, (b + 1) * steps):
            pad_ref[pl.ds(t, nseg, stride=pitch), :] = ut_ref[t * nseg:(t + 1) * nseg, :]

    for j in range(nseg):
        pad_ref[j * pitch:j * pitch + seg, :] = u_ref[j * seg:(j + 1) * seg, :]

    def project_in(b):
        blk = slice(b * seg, (b + 1) * seg)
        bu = _dot(ut_ref[blk, :].astype(BF16), w_ref[0])
        for c in range(nc):
            buh_ref[c, blk, :] = bu[:, c * LANES:(c + 1) * LANES]

    def project_out(b):
        blk = slice(b * seg, (b + 1) * seg)
        y = d_ref[...] * ut_ref[blk, :]
        for c in range(nc):
            y = y + _dot(buh_ref[c, blk, :].astype(BF16), cm_ref[0, c * LANES:(c + 1) * LANES, :])
        ut_ref[blk, :] = jax.nn.gelu(y)

    ar = [jnp.broadcast_to(ar_ref[0, :, c * LANES:(c + 1) * LANES], (nseg, LANES)) for c in range(hc)]
    ai = [jnp.broadcast_to(ai_ref[0, :, c * LANES:(c + 1) * LANES], (nseg, LANES)) for c in range(hc)]

    def advance(t, hs, store):
        rows = slice(t * nseg, (t + 1) * nseg)
        out = []
        for c in range(hc):
            pr, pi = _cmul(ar[c], ai[c], hs[2 * c], hs[2 * c + 1])
            out += [pr + buh_ref[c, rows, :], pi + buh_ref[hc + c, rows, :]]
        if store:
            for c in range(hc):
                buh_ref[c, rows, :] = out[2 * c]
                buh_ref[hc + c, rows, :] = out[2 * c + 1]
        return out

    zeros = [jnp.zeros((nseg, LANES), F32) for _ in range(2 * hc)]
    hs = zeros
    gather(0)
    project_in(0)
    for b in range(nblk):
        if b + 1 < nblk:
            gather(b + 1)
            project_in(b + 1)
        for t in range(b * steps, (b + 1) * steps):
            hs = advance(t, hs, store=False)
    ends = hs

    pw = [(ar[c], ai[c]) for c in range(hc)]
    for _ in range(seg.bit_length() - 1):
        pw = [_cmul(r, i, r, i) for r, i in pw]

    row = lax.broadcasted_iota(jnp.int32, (nseg, LANES), 0)
    init = list(zeros)
    for j in range(1, nseg):
        for c in range(hc):
            pr, pi = _cmul(pw[c][0], pw[c][1], init[2 * c], init[2 * c + 1])
            nr = pltpu.roll(pr + ends[2 * c], 1, 0)
            ni = pltpu.roll(pi + ends[2 * c + 1], 1, 0)
            init[2 * c] = jnp.where(row == j, nr, init[2 * c])
            init[2 * c + 1] = jnp.where(row == j, ni, init[2 * c + 1])

    hs = init
    for b in range(nblk):
        for t in range(b * steps, (b + 1) * steps):
            hs = advance(t, hs, store=True)
        if b > 0:
            project_out(b - 1)
            scatter(b - 1)
    for c in range(hc):
        hl_ref[0, 0, :, c * LANES:(c + 1) * LANES] = hs[2 * c]
        hl_ref[0, 0, :, (hc + c) * LANES:(hc + c + 1) * LANES] = hs[2 * c + 1]
    _side_cast(refs[6:6 + nside], refs[8 + nside:-3])
    project_out(nblk - 1)
    scatter(nblk - 1)
    for j in range(nseg):
        z_ref[j * seg:(j + 1) * seg, :] = pad_ref[j * pitch:j * pitch + seg, :]


def _s5_prompt(proj, w_in, w_out, a_re, a_im, d_skip, *, batch, seq, rows_total, side=()):
    nv = w_in.shape[0]
    sw = w_in.shape[2]
    seg = seq // SUBLANES
    assert seg * SUBLANES == seq and seg & (seg - 1) == 0
    side_in, side_out, side_shapes, side_plan = _side_specs(
        side, batch * nv, lambda b, v: b * nv + v)
    body = functools.partial(_s5_prompt_body, seg=seg, side_plan=side_plan)
    return pl.pallas_call(
        body,
        out_shape=[jax.ShapeDtypeStruct((rows_total, nv * LANES), F32),
                   jax.ShapeDtypeStruct((batch, nv, SUBLANES, sw), F32)] + side_shapes,
        grid=(batch, nv),
        in_specs=[
            pl.BlockSpec((seq, LANES), lambda b, v: (b, v)),
            pl.BlockSpec((1, LANES, sw), lambda b, v: (v, 0, 0)),
            pl.BlockSpec((1, sw, LANES), lambda b, v: (v, 0, 0)),
            pl.BlockSpec((1, 1, sw // 2), lambda b, v: (v, 0, 0)),
            pl.BlockSpec((1, 1, sw // 2), lambda b, v: (v, 0, 0)),
            pl.BlockSpec((1, LANES), lambda b, v: (0, v)),
        ] + side_in,
        out_specs=[pl.BlockSpec((seq, LANES), lambda b, v: (b, v)),
                   pl.BlockSpec((1, 1, SUBLANES, sw), lambda b, v: (b, v, 0, 0))] + side_out,
        scratch_shapes=[pltpu.VMEM((sw // LANES, seq, LANES), F32),
                        pltpu.VMEM((seq, LANES), F32),
                        pltpu.VMEM((SUBLANES * (seg + SUBLANES), LANES), F32)],
        compiler_params=_cparams(*(("arbitrary",) * 2 if side else ("parallel",) * 2)),
        name="s5_prompt",
    )(proj, w_in, w_out, a_re, a_im, d_skip, *side)


def _s5_sample_body(u_ref, hre_ref, him_ref, w_ref, cm_ref, ar_ref, ai_ref, d_ref,
                    z_ref, ore_ref, oim_ref):
    half = ar_ref.shape[2]
    u = u_ref[...]
    bu = _dot_hi(u, w_ref[0])
    pr, pi = _cmul(ar_ref[0], ai_ref[0], hre_ref[...], him_ref[...])
    hr = pr + bu[:, :half]
    hi = pi + bu[:, half:]
    ore_ref[...] = hr
    oim_ref[...] = hi
    y = d_ref[...] * u + _dot_hi(hr, cm_ref[0, :half, :]) + _dot_hi(hi, cm_ref[0, half:, :])
    z_ref[...] = jax.nn.gelu(y)


def _s5_sample(proj, h_re, h_im, w_in, w_out, a_re, a_im, d_skip, *, row0):
    nv = w_in.shape[0]
    sw = w_in.shape[2]
    ns = h_re.shape[0]
    rb = row0 // ns
    assert rb * ns == row0
    st = pl.BlockSpec((ns, sw // 2), lambda v: (0, v))
    return pl.pallas_call(
        _s5_sample_body,
        out_shape=(jax.ShapeDtypeStruct((ns, nv * LANES), F32),
                   jax.ShapeDtypeStruct(h_re.shape, F32),
                   jax.ShapeDtypeStruct(h_im.shape, F32)),
        grid=(nv,),
        in_specs=[
            pl.BlockSpec((ns, LANES), lambda v: (rb, v)),
            st, st,
            pl.BlockSpec((1, LANES, sw), lambda v: (v, 0, 0)),
            pl.BlockSpec((1, sw, LANES), lambda v: (v, 0, 0)),
            pl.BlockSpec((1, 1, sw // 2), lambda v: (v, 0, 0)),
            pl.BlockSpec((1, 1, sw // 2), lambda v: (v, 0, 0)),
            pl.BlockSpec((1, LANES), lambda v: (0, v)),
        ],
        out_specs=(pl.BlockSpec((ns, LANES), lambda v: (0, v)), st, st),
        compiler_params=_cparams("parallel"),
        name="s5_sample",
    )(proj, h_re, h_im, w_in, w_out, a_re, a_im, d_skip)


def _hg_span_matrices(ch):
    t = np.arange(ch)[:, None]
    r = np.arange(ch)[None, :]
    mats = [r <= t, r > t]
    for v in range(1, ch.bit_length() - 1):
        base = (t >> v) << v
        upper = ((t >> v) & 1) == 1
        mats.append(np.where(upper, (r >= base) & (r <= t), (r > t) & (r < base + (1 << v))))
    mm = np.stack(mats).astype(np.float32)
    return jnp.asarray(np.concatenate([mm, mm], axis=-1), BF16)


def _hg_prompt_body(*refs, dk, side_plan):
    nside = len(side_plan)
    q_ref, f_ref, i_ref, og_ref, lb_ref, gn_ref, mm_ref = refs[:7]
    o_ref, sfin_ref = refs[7 + nside:9 + nside]
    st_ref = refs[-1]
    _side_cast(refs[7:7 + nside], refs[9 + nside:-1])
    c = pl.program_id(1)
    nh = st_ref.shape[0]
    ch = mm_ref.shape[1]
    nsub = q_ref.shape[0] // ch
    width = q_ref.shape[1]
    nlev = ch.bit_length() - 1

    @pl.when(c == 0)
    def _():
        st_ref[...] = jnp.zeros_like(st_ref)

    lb = lb_ref[...]
    f_all = lb + (1.0 - lb) * jax.nn.sigmoid(f_ref[...])
    kk_all = 1.0 - f_all
    qs_all = _silu(q_ref[...])
    logf_pieces = _split3(jnp.log2(f_all))[:2]
    ib_all = i_ref[...].astype(BF16)
    pairs = [slice(2 * j * dk, 2 * (j + 1) * dk) for j in range(nh // 2)]
    heads = [slice(h * dk, (h + 1) * dk) for h in range(nh)]
    r = lax.broadcasted_iota(jnp.int32, (ch, 2 * ch), 0)
    s = lax.broadcasted_iota(jnp.int32, (ch, 2 * ch), 1) & (ch - 1)
    xr = jnp.where(r > s, r ^ s, 0)
    zero_k = jnp.zeros((ch, dk), BF16)
    zero_s = jnp.zeros((dk, dk), BF16)

    def block_diag(a, b, z):
        return jnp.concatenate([jnp.concatenate([a, z], axis=1),
                                jnp.concatenate([z, b], axis=1)], axis=0)

    def pair_scores(lhs, x):
        return _dot_nt(lhs, block_diag(x[:, :dk], x[:, dk:], zero_k))

    def upper_runs(v):
        m = 1 << v
        return [(b0, b0 + m) for b0 in range(m, ch, 2 * m)]

    def mix_rows(v, qs, kk):
        m = 1 << v
        if m >= SUBLANES:
            return jnp.concatenate(
                [(qs if (b0 // m) & 1 else kk)[b0:b0 + m] for b0 in range(0, ch, m)], axis=0)
        pick = ((lax.broadcasted_iota(jnp.int32, (1, SUBLANES, width), 1) >> v) & 1) == 1
        shape3 = (ch // SUBLANES, SUBLANES, width)
        return jnp.where(pick, qs.reshape(shape3), kk.reshape(shape3)).reshape(ch, width)

    staged = []
    for sub in range(nsub):
        rows = slice(sub * ch, (sub + 1) * ch)
        kk, qs = kk_all[rows], qs_all[rows]
        logf2 = jnp.concatenate([p[rows] for p in logf_pieces], axis=0)

        def decay(idx):
            return jnp.exp2(_dot(mm_ref[idx], logf2))

        eg = decay(0)
        qg = (qs * eg).astype(BF16)
        dec = eg[ch - 1:ch, :]
        kh = (kk * decay(1)).astype(BF16)
        lev = [mix_rows(0, qs * f_all[rows], kk).astype(BF16)]
        lev += [(decay(1 + v) * mix_rows(v, qs, kk)).astype(BF16) for v in range(1, nlev)]
        qb, kb = qs.astype(BF16), kk.astype(BF16)
        atts = []
        for sl in pairs:
            att = jnp.where(r == s, pair_scores(qb[:, sl], kb[:, sl]), 0.0)
            for v in range(nlev):
                x = lev[v][:, sl]
                m = 1 << v
                if m >= BF16_ROWS:
                    runs = upper_runs(v)
                    p = pair_scores(jnp.concatenate([x[a:b] for a, b in runs], axis=0), x)
                    parts = []
                    for n in range(len(runs)):
                        parts += [jnp.zeros((m, 2 * ch), F32), p[n * m:(n + 1) * m]]
                    p = jnp.concatenate(parts, axis=0)
                else:
                    p = pair_scores(x, x)
                att = jnp.where((xr >> v) == 1, p, att)
            atts.append(att.astype(BF16))
        staged.append((rows, qg, kh, dec, atts))

    for rows, qg, kh, dec, atts in staged:
        ib = ib_all[rows]
        for j, sl in enumerate(pairs):
            st2 = block_diag(st_ref[2 * j].astype(BF16), st_ref[2 * j + 1].astype(BF16), zero_s)
            ib2 = block_diag(ib[:, sl][:, :dk], ib[:, sl][:, dk:], zero_k)
            o = _dot_nt(qg[:, sl], st2) + _dot(atts[j], ib2)
            for e in range(2):
                hs = heads[2 * j + e]
                og = og_ref[rows, hs]
                o_ref[rows, hs] = _rms(o[:, e * dk:(e + 1) * dk], gn_ref[:, hs]) * _silu(og)
        for h, hs in enumerate(heads):
            st_ref[h] = st_ref[h] * dec[:, hs] + _dot_tn(ib[:, hs], kh[:, hs])

    @pl.when(c == pl.num_programs(1) - 1)
    def _():
        for h in range(nh):
            sfin_ref[0, h] = st_ref[h].T


def _hg_prompt(proj, lb, gn, *, batch, seq, rows_total, nh, col0, side=()):
    width = lb.shape[1]
    dk = width // nh
    ch = HG_CHUNK * HG_CHUNKS_PER_STEP
    nchunk = seq // ch
    cb = col0 // width
    assert cb * width == col0 and nchunk * ch == seq

    def tok(k):
        return pl.BlockSpec((ch, width), lambda b, c: (b * nchunk + c, cb + k))

    vec = pl.BlockSpec((1, width), lambda b, c: (0, 0))
    mm = _hg_span_matrices(HG_CHUNK)
    side_in, side_out, side_shapes, side_plan = _side_specs(
        side, batch * nchunk, lambda b, c: b * nchunk + c)
    return pl.pallas_call(
        functools.partial(_hg_prompt_body, dk=dk, side_plan=side_plan),
        out_shape=[jax.ShapeDtypeStruct((rows_total, width), F32),
                   jax.ShapeDtypeStruct((batch, nh, dk, dk), F32)] + side_shapes,
        grid=(batch, nchunk),
        in_specs=[tok(0), tok(1), tok(2), tok(3), vec, vec,
                  pl.BlockSpec(mm.shape, lambda b, c: (0, 0, 0))] + side_in,
        out_specs=[pl.BlockSpec((ch, width), lambda b, c: (b * nchunk + c, 0)),
                   pl.BlockSpec((1, nh, dk, dk), lambda b, c: (b, 0, 0, 0))] + side_out,
        scratch_shapes=[pltpu.VMEM((nh, dk, dk), F32)],
        compiler_params=_cparams("arbitrary" if side else "parallel", "arbitrary"),
        name="hgrn_prompt",
    )(proj, proj, proj, proj, lb, gn, mm, *side)


def _split3(x):
    p1 = x.astype(BF16)
    r1 = x - p1.astype(F32)
    p2 = r1.astype(BF16)
    p3 = (r1 - p2.astype(F32)).astype(BF16)
    return p1, p2, p3


def _hg_sample_body(q_ref, f_ref, i_ref, og_ref, lb_ref, gn_ref, s0_ref,
                    hg_ref, s_ref, fq_ref, *, dk):
    step = pl.program_id(0)
    nt = q_ref.shape[0]
    tb, nh = s0_ref.shape[0], s0_ref.shape[1]
    npiece = fq_ref.shape[1] // nt

    @pl.when(step == 0)
    def _():
        lb = lb_ref[...]
        f = lb + (1.0 - lb) * jax.nn.sigmoid(f_ref[...])
        qs = _silu(q_ref[...])
        for src, base in ((f, 0), (qs, nh)):
            for h in range(nh):
                t = src[:, h * dk:(h + 1) * dk].T
                for p, piece in enumerate(_split3(t)[:npiece]):
                    fq_ref[(base + h) * dk:(base + h + 1) * dk, p * nt:(p + 1) * nt] = piece

    tok = lax.broadcasted_iota(jnp.int32, (npiece * nt, 2 * dk), 0) & (nt - 1)
    second = lax.broadcasted_iota(jnp.int32, (npiece * nt, 2 * dk), 1) >= dk
    for j0 in range(0, tb, 2):
        n0 = step * tb + j0
        onehot = (tok == jnp.where(second, n0 + 1, n0)).astype(BF16)
        fq2 = _dot(fq_ref[...], onehot)
        for e in range(2):
            j, n = j0 + e, n0 + e
            fq = fq2[:, e * dk:(e + 1) * dk]
            irow = i_ref[pl.ds(n, 1), :]
            ogrow = og_ref[pl.ds(n, 1), :]
            for h in range(nh):
                sl = slice(h * dk, (h + 1) * dk)
                fb = fq[h * dk:(h + 1) * dk, :]
                qb = fq[(nh + h) * dk:(nh + h + 1) * dk, :]
                s1 = fb * s0_ref[j, h] + (1.0 - fb) * irow[:, sl]
                s_ref[j, h] = s1
                o = jnp.sum(qb * s1, axis=0, keepdims=True)
                hg_ref[j:j + 1, sl] = _rms(o, gn_ref[:, sl]) * _silu(ogrow[:, sl])


def _hg_sample(proj, lb, gn, s0, *, row0, nh, col0):
    width = lb.shape[1]
    dk = width // nh
    ns = s0.shape[0]
    tb = SAMPLE_TOKENS_PER_STEP
    rb, cb = row0 // ns, col0 // width
    assert rb * ns == row0 and cb * width == col0 and ns % tb == 0
    assert ns & (ns - 1) == 0 and tb % 2 == 0 and dk == LANES

    def tok(k):
        return pl.BlockSpec((ns, width), lambda t: (rb, cb + k))

    vec = pl.BlockSpec((1, width), lambda t: (0, 0))
    sspec = pl.BlockSpec((tb, nh, dk, dk), lambda t: (t, 0, 0, 0))
    return pl.pallas_call(
        functools.partial(_hg_sample_body, dk=dk),
        out_shape=(jax.ShapeDtypeStruct((ns, width), F32),
                   jax.ShapeDtypeStruct(s0.shape, F32)),
        grid=(ns // tb,),
        in_specs=[tok(0), tok(1), tok(2), tok(3), vec, vec, sspec],
        out_specs=(pl.BlockSpec((tb, width), lambda t: (t, 0)), sspec),
        scratch_shapes=[pltpu.VMEM((2 * nh * dk, 2 * ns), BF16)],
        compiler_params=_cparams("arbitrary"),
        name="hgrn_sample",
    )(proj, proj, proj, proj, lb, gn, s0)


def _merge_body(z_ref, zs_ref, hg_ref, hgs_ref, gs0_ref, gs1_ref, gh0_ref, gh1_ref, x_ref,
                wglu_ref, bglu_ref, wbs_ref, wbh_ref, wout_ref, post_ref, o_ref, *, ns):
    i = pl.program_id(0)
    last_i = pl.num_programs(0) - 1
    cut = x_ref.shape[0] - ns
    half = gs0_ref.shape[1]

    def rows(z, hg, sl):
        s5o = z * jax.nn.sigmoid(_dot(z.astype(BF16), wglu_ref[...]) + bglu_ref[...])
        a = _dot(s5o.astype(BF16), wbs_ref[...])
        b = _dot(hg.astype(BF16), wbh_ref[...])
        m0 = (jax.nn.sigmoid(gs0_ref[sl, :]) * a[:, :half]
              + jax.nn.sigmoid(gh0_ref[sl, :]) * b[:, :half])
        m1 = (jax.nn.sigmoid(gs1_ref[sl, :]) * a[:, half:]
              + jax.nn.sigmoid(gh1_ref[sl, :]) * b[:, half:])
        mix = _dot(m0.astype(BF16), wout_ref[:half, :]) + _dot(m1.astype(BF16), wout_ref[half:, :])
        o_ref[sl, :] = x_ref[sl, :] + _rms(mix, post_ref[...])

    @pl.when(i < last_i)
    def _():
        rows(z_ref[...], hg_ref[...], slice(None))

    @pl.when(i == last_i)
    def _():
        rows(z_ref[:cut, :], hg_ref[:cut, :], slice(0, cut))
        rows(zs_ref[...], hgs_ref[...], slice(cut, None))


def _merge(z, zs, hg, hgs, proj, x, wglu, bglu, wbs, wbh, wout, post, *, col0, tm):
    m, d = x.shape
    w = z.shape[1]
    ns = zs.shape[0]
    cb = col0 // w
    assert cb * w == col0 and d == 2 * w and m % tm == 0 and 0 < ns < tm

    def gate(k):
        return pl.BlockSpec((tm, w), lambda i: (i, cb + k))

    def const(shape):
        return pl.BlockSpec(shape, lambda i: (0, 0), pipeline_mode=pl.Buffered(1))

    tok = pl.BlockSpec((tm, w), lambda i: (i, 0))
    return pl.pallas_call(
        functools.partial(_merge_body, ns=ns),
        out_shape=jax.ShapeDtypeStruct((m, d), F32),
        grid=(m // tm,),
        in_specs=[
            tok, const((ns, w)), tok, const((ns, w)),
            gate(0), gate(1), gate(2), gate(3),
            pl.BlockSpec((tm, d), lambda i: (i, 0)),
            const((w, w)), const((1, w)), const((w, d)), const((w, d)), const((d, d)), const((1, d)),
        ],
        out_specs=pl.BlockSpec((tm, d), lambda i: (i, 0)),
        compiler_params=_cparams("parallel"),
        name="merge",
    )(z, zs, hg, hgs, proj, proj, proj, proj, x, wglu, bglu, wbs, wbh, wout, post)


def kernel(x_prompt, x_sample, state_s5_re, state_s5_im, state_hgrn, ffn1_pre_norm, ffn1_w_gate, ffn1_w_up, ffn1_w_down, ffn1_post_norm, mix_pre_norm, w_in, s5_lambda_re, s5_lambda_im, s5_log_dt, s5_b_re, s5_b_im, s5_c_re, s5_c_im, s5_d, s5_w_glu, s5_b_glu, hgrn_lb_logits, hgrn_out_norm, w_branch_s5, w_branch_hgrn, w_out, mix_post_norm, ffn2_pre_norm, ffn2_w_gate, ffn2_w_up, ffn2_w_down, ffn2_post_norm):
    depth = ffn1_w_gate.shape[0]
    assert depth == 1
    batch, seq, d = x_prompt.shape
    ns = x_sample.shape[0]
    assert x_sample.shape[1] == 1
    g, p = s5_lambda_re.shape[1:]
    nh, dk = state_hgrn.shape[2], state_hgrn.shape[3]
    s5w = s5_d.shape[1]
    hgw = nh * dk
    mp = batch * seq
    m = mp + ns

    lay = lambda a: a.reshape(a.shape[1:])
    row = lambda a: a.reshape(1, -1)

    lb = jax.nn.softmax(hgrn_lb_logits.astype(F32), axis=0)[:1]

    x, w_in_b = _ffn(
        x_prompt.reshape(mp, d), x_sample, row(ffn1_pre_norm),
        lay(ffn1_w_gate), lay(ffn1_w_up), lay(ffn1_w_down), row(ffn1_post_norm),
        ns=ns, split_out=False, side=(lay(w_in),))
    proj = _inproj(x, row(mix_pre_norm), w_in_b)

    tw_in, tw_out, a_re, a_im = _s5_tables(
        lay(s5_lambda_re), lay(s5_lambda_im), lay(s5_log_dt), lay(s5_b_re), lay(s5_b_im),
        lay(s5_c_re), lay(s5_c_im))
    d_skip = row(s5_d)
    z, hlast, wg2_b, wu2_b, wd2_b, wglu_b, wbs_b, wbh_b, wout_b = _s5_prompt(
        proj, tw_in.astype(BF16), tw_out.astype(BF16), a_re, a_im, d_skip,
        batch=batch, seq=seq, rows_total=mp,
        side=(lay(ffn2_w_gate), lay(ffn2_w_up), lay(ffn2_w_down), lay(s5_w_glu),
              lay(w_branch_s5), lay(w_branch_hgrn), lay(w_out)))
    zs, s_re, s_im = _s5_sample(proj, state_s5_re.reshape(ns, g * p),
                                state_s5_im.reshape(ns, g * p),
                                tw_in, tw_out, a_re, a_im, d_skip, row0=mp)
    half = hlast.shape[-1] // 2
    p_re = hlast[:, :, SUBLANES - 1, :half].reshape(1, batch, g, p)
    p_im = hlast[:, :, SUBLANES - 1, half:].reshape(1, batch, g, p)

    gn = row(hgrn_out_norm)
    hg, st_p = _hg_prompt(proj, lb, gn, batch=batch, seq=seq, rows_total=mp, nh=nh, col0=s5w)
    hgs, st_s = _hg_sample(proj, lb, gn, lay(state_hgrn), row0=mp, nh=nh, col0=s5w)

    x = _merge(z, zs, hg, hgs, proj, x, wglu_b, row(s5_b_glu), wbs_b, wbh_b, wout_b,
               row(mix_post_norm), col0=s5w + 4 * hgw, tm=MERGE_ROW_TILE)
    yp, ys = _ffn(x, None, row(ffn2_pre_norm), wg2_b, wu2_b, wd2_b, row(ffn2_post_norm),
                  ns=ns, split_out=True)

    return (yp.reshape(batch, seq, d), ys,
            p_re, p_im, st_p[None],
            s_re.reshape(1, ns, g, p), s_im.reshape(1, ns, g, p), st_s[None])
```

```python
import functools

import jax
import jax.numpy as jnp
import numpy as np
from jax import lax
from jax.experimental import pallas as pl
from jax.experimental.pallas import tpu as pltpu

F32 = jnp.float32
BF16 = jnp.bfloat16
EPS = 1e-6
HIGHEST = lax.Precision.HIGHEST

LANES = 128
SUBLANES = 8
VMEM_LIMIT = 60 * 1024 * 1024

FFN_TILES_BF16 = (832, 512)
FFN_TILES_F32 = (1040, 256)
IN_TILE = 1536
IN_ROW_TILE = 1040
MERGE_ROW_TILE = 416
HG_CHUNK = 64
HG_CHUNKS_PER_STEP = 4
SAMPLE_TOKENS_PER_STEP = 8


def _cparams(*sem):
    return pltpu.CompilerParams(dimension_semantics=sem, vmem_limit_bytes=VMEM_LIMIT)


def _rms(x, g):
    return x * lax.rsqrt(jnp.mean(x * x, axis=-1, keepdims=True) + EPS) * g


def _silu(x):
    return x * jax.nn.sigmoid(x)


def _dot(a, b):
    return jnp.dot(a, b, preferred_element_type=F32)


def _dot_hi(a, b):
    return jnp.dot(a, b, preferred_element_type=F32, precision=HIGHEST)


def _dot_nt(a, b):
    return lax.dot_general(a, b, (((1,), (1,)), ((), ())), preferred_element_type=F32)


def _dot_tn(a, b):
    return lax.dot_general(a, b, (((0,), (0,)), ((), ())), preferred_element_type=F32)


BF16_ROWS = 16


def _side_specs(arrays, nsteps, step_of):
    in_specs, out_specs, out_shapes, plan = [], [], [], []
    for a in arrays:
        r, c = a.shape
        rows = next(t for t in range(BF16_ROWS, r + 1, BF16_ROWS) if r % t == 0 and r // t <= nsteps)
        nblk = r // rows
        every = nsteps // nblk

        def imap(*g, nblk=nblk, every=every):
            return (jnp.minimum(step_of(*g) // every, nblk - 1), 0)

        in_specs.append(pl.BlockSpec((rows, c), imap))
        out_specs.append(pl.BlockSpec((rows, c), imap))
        out_shapes.append(jax.ShapeDtypeStruct((r, c), BF16))
        plan.append((nblk, every))
    return in_specs, out_specs, out_shapes, tuple(plan)


def _side_cast(src_refs, dst_refs):
    for src, dst in zip(src_refs, dst_refs):
        dst[...] = src[...].astype(BF16)


def _ffn_body(*refs, ns, split_in, split_out, side_plan):
    refs = list(refs)
    nside = len(side_plan)
    x_ref = refs.pop(0)
    xs_ref = refs.pop(0) if split_in else None
    pre_ref, wg_ref, wu_ref, wd_ref, post_ref = refs[:5]
    side_src = refs[5:5 + nside]
    o_ref = refs[5 + nside]
    os_ref = refs[6 + nside] if split_out else None
    side_dst = refs[-1 - nside:-1]
    h_ref = refs[-1]
    acc_ref = o_ref
    i, k = pl.program_id(0), pl.program_id(1)
    last_i = pl.num_programs(0) - 1
    last_k = pl.num_programs(1) - 1
    cut = h_ref.shape[0] - ns

    def swiglu_down(h):
        _side_cast(side_src, side_dst)
        g = _dot(h, wg_ref[...].astype(BF16))
        u = _dot(h, wu_ref[...].astype(BF16))
        return _dot((_silu(g) * u).astype(BF16), wd_ref[...].astype(BF16))

    @pl.when((i < last_i) & (k == 0))
    def _():
        h = _rms(x_ref[...], pre_ref[...]).astype(BF16)
        h_ref[...] = h
        acc_ref[...] = swiglu_down(h)

    @pl.when((i < last_i) & (k == last_k))
    def _():
        acc = acc_ref[...] + swiglu_down(h_ref[...])
        o_ref[...] = x_ref[...] + 0.5 * _rms(acc, post_ref[...])

    @pl.when((i == last_i) & (k == 0))
    def _():
        h_ref[:cut, :] = _rms(x_ref[:cut, :], pre_ref[...]).astype(BF16)
        xt = xs_ref[...] if split_in else x_ref[cut:, :]
        h_ref[cut:, :] = _rms(xt, pre_ref[...]).astype(BF16)
        acc_ref[...] = jnp.zeros_like(acc_ref)

    @pl.when(((k > 0) & (k < last_k)) | (i == last_i))
    def _():
        acc_ref[...] += swiglu_down(h_ref[...])

    @pl.when((i == last_i) & (k == last_k))
    def _():
        o_ref[:cut, :] = x_ref[:cut, :] + 0.5 * _rms(acc_ref[:cut, :], post_ref[...])
        xt = xs_ref[...] if split_in else x_ref[cut:, :]
        tail = xt + 0.5 * _rms(acc_ref[cut:, :], post_ref[...])
        if split_out:
            os_ref[...] = tail
        else:
            o_ref[cut:, :] = tail


def _ffn(x, xs, pre, wg, wu, wd, post, *, ns, split_out, side=()):
    split_in = xs is not None
    d = x.shape[1]
    m = x.shape[0] + (ns if split_in else 0)
    dff = wg.shape[1]
    tm, tf = (FFN_TILES_F32 if wg.dtype == F32 else FFN_TILES_BF16)
    assert m % tm == 0 and dff % tf == 0 and 0 < ns < tm and ns % BF16_ROWS == 0
    nk = dff // tf
    side_in, side_out, side_shapes, side_plan = _side_specs(
        side, (m // tm) * nk, lambda i, k: i * nk + k)
    tok = pl.BlockSpec((tm, d), lambda i, k: (i, 0))
    smp_in = pl.BlockSpec((ns, d), lambda i, k: (0, 0))
    smp = pl.BlockSpec((ns, None, d), lambda i, k: (0, 0, 0))
    vec = pl.BlockSpec((1, d), lambda i, k: (0, 0))
    in_specs = [tok] + ([smp_in] if split_in else []) + [
        vec,
        pl.BlockSpec((d, tf), lambda i, k: (0, k)),
        pl.BlockSpec((d, tf), lambda i, k: (0, k)),
        pl.BlockSpec((tf, d), lambda i, k: (k, 0)),
        vec,
    ] + side_in
    if split_out:
        out_shape = [jax.ShapeDtypeStruct((m - ns, d), F32), jax.ShapeDtypeStruct((ns, 1, d), F32)]
        out_specs = [tok, smp]
    else:
        out_shape = [jax.ShapeDtypeStruct((m, d), F32)]
        out_specs = [tok]
    args = [x] + ([xs] if split_in else []) + [pre, wg, wu, wd, post] + list(side)
    sequential_rows = split_out or bool(side)
    return pl.pallas_call(
        functools.partial(_ffn_body, ns=ns, split_in=split_in, split_out=split_out,
                          side_plan=side_plan),
        out_shape=out_shape + side_shapes,
        grid=(m // tm, nk),
        in_specs=in_specs,
        out_specs=out_specs + side_out,
        scratch_shapes=[pltpu.VMEM((tm, d), BF16)],
        compiler_params=_cparams("arbitrary" if sequential_rows else "parallel", "arbitrary"),
        name="ffn",
    )(*args)


def _inproj_body(x_ref, g_ref, w_ref, o_ref, h_ref):
    j = pl.program_id(1)

    @pl.when(j == 0)
    def _():
        h = _rms(x_ref[...], g_ref[...]).astype(BF16)
        h_ref[...] = h
        o_ref[...] = _dot(h, w_ref[...])

    @pl.when(j > 0)
    def _():
        o_ref[...] = _dot(h_ref[...], w_ref[...])


def _inproj(x, g, w):
    m, d = x.shape
    n = w.shape[1]
    tm, tn = IN_ROW_TILE, IN_TILE
    assert m % tm == 0
    return pl.pallas_call(
        _inproj_body,
        out_shape=jax.ShapeDtypeStruct((m, n), F32),
        grid=(m // tm, n // tn),
        in_specs=[
            pl.BlockSpec((tm, d), lambda i, j: (i, 0)),
            pl.BlockSpec((1, d), lambda i, j: (0, 0)),
            pl.BlockSpec((d, tn), lambda i, j: (0, j)),
        ],
        out_specs=pl.BlockSpec((tm, tn), lambda i, j: (i, j)),
        scratch_shapes=[pltpu.VMEM((tm, d), BF16)],
        compiler_params=_cparams("parallel", "arbitrary"),
        name="inproj",
    )(x, g, w)


def _s5_tables(lam_re, lam_im, log_dt, b_re, b_im, c_re, c_im):
    g, p = lam_re.shape
    n = b_re.shape[-1]
    gpt = LANES // n
    nv = g // gpt
    dt = jnp.exp(log_dt)[:, None]
    er = jnp.exp(lam_re * dt)
    th = lam_im * dt
    a_re, a_im = er * jnp.cos(th), er * jnp.sin(th)
    den = lam_re * lam_re + lam_im * lam_im
    k_re = ((a_re - 1.0) * lam_re + a_im * lam_im) / den
    k_im = (a_im * lam_re - (a_re - 1.0) * lam_im) / den
    bb_re = k_re[..., None] * b_re - k_im[..., None] * b_im
    bb_im = k_re[..., None] * b_im + k_im[..., None] * b_re
    sw = 2 * gpt * p
    lane_group = (jnp.arange(sw) // p) % gpt
    io_group = jnp.arange(gpt * n) // n
    bb = jnp.stack([bb_re, bb_im]).reshape(2, nv, gpt, p, n)
    dense_in = bb.transpose(1, 4, 0, 2, 3).reshape(nv, n, sw)
    w_in = jnp.where(io_group[:, None] == lane_group[None, :],
                     jnp.tile(dense_in, (1, gpt, 1)), 0.0)
    cc = jnp.stack([c_re, -c_im]).reshape(2, nv, gpt, n, p)
    dense_out = cc.transpose(1, 0, 2, 4, 3).reshape(nv, sw, n)
    w_out = jnp.where(lane_group[:, None] == io_group[None, :],
                      jnp.tile(dense_out, (1, 1, gpt)), 0.0)
    return w_in, w_out, a_re.reshape(nv, 1, gpt * p), a_im.reshape(nv, 1, gpt * p)


def _cmul(ar, ai, br, bi):
    return ar * br - ai * bi, ar * bi + ai * br


def _s5_prompt_body(*refs, seg, side_plan):
    nside = len(side_plan)
    u_ref, w_ref, cm_ref, ar_ref, ai_ref, d_ref = refs[:6]
    z_ref, hl_ref = refs[6 + nside:8 + nside]
    buh_ref, ut_ref, pad_ref = refs[-3:]
    nseg = SUBLANES
    nc = w_ref.shape[2] // LANES
    hc = nc // 2
    pitch = pad_ref.shape[0] // nseg
    nblk = nseg
    steps = seg // nseg

    def gather(b):
        for t in range(b * steps, (b + 1) * steps):
            ut_ref[t * nseg:(t + 1) * nseg, :] = pad_ref[pl.ds(t, nseg, stride=pitch), :]

    def scatter(b):
        for t in range(b * steps, (b + 1) * steps):
            pad_ref[pl.ds(t, nseg, stride=pitch), :] = ut_ref[t * nseg:(t + 1) * nseg, :]

    for j in range(nseg):
        pad_ref[j * pitch:j * pitch + seg, :] = u_ref[j * seg:(j + 1) * seg, :]

    def project_in(b):
        blk = slice(b * seg, (b + 1) * seg)
        bu = _dot(ut_ref[blk, :].astype(BF16), w_ref[0])
        for c in range(nc):
            buh_ref[c, blk, :] = bu[:, c * LANES:(c + 1) * LANES]

    def project_out(b):
        blk = slice(b * seg, (b + 1) * seg)
        y = d_ref[...] * ut_ref[blk, :]
        for c in range(nc):
            y = y + _dot(buh_ref[c, blk, :].astype(BF16), cm_ref[0, c * LANES:(c + 1) * LANES, :])
        ut_ref[blk, :] = jax.nn.gelu(y)

    ar = [jnp.broadcast_to(ar_ref[0, :, c * LANES:(c + 1) * LANES], (nseg, LANES)) for c in range(hc)]
    ai = [jnp.broadcast_to(ai_ref[0, :, c * LANES:(c + 1) * LANES], (nseg, LANES)) for c in range(hc)]

    def advance(t, hs, store):
        rows = slice(t * nseg, (t + 1) * nseg)
        out = []
        for c in range(hc):
            pr, pi = _cmul(ar[c], ai[c], hs[2 * c], hs[2 * c + 1])
            out += [pr + buh_ref[c, rows, :], pi + buh_ref[hc + c, rows, :]]
        if store:
            for c in range(hc):
                buh_ref[c, rows, :] = out[2 * c]
                buh_ref[hc + c, rows, :] = out[2 * c + 1]
        return out

    zeros = [jnp.zeros((nseg, LANES), F32) for _ in range(2 * hc)]
    hs = zeros
    gather(0)
    project_in(0)
    for b in range(nblk):
        if b + 1 < nblk:
            gather(b + 1)
            project_in(b + 1)
        for t in range(b * steps, (b + 1) * steps):
            hs = advance(t, hs, store=False)
    ends = hs

    pw = [(ar[c], ai[c]) for c in range(hc)]
    for _ in range(seg.bit_length() - 1):
        pw = [_cmul(r, i, r, i) for r, i in pw]

    row = lax.broadcasted_iota(jnp.int32, (nseg, LANES), 0)
    init = list(zeros)
    for j in range(1, nseg):
        for c in range(hc):
            pr, pi = _cmul(pw[c][0], pw[c][1], init[2 * c], init[2 * c + 1])
            nr = pltpu.roll(pr + ends[2 * c], 1, 0)
            ni = pltpu.roll(pi + ends[2 * c + 1], 1, 0)
            init[2 * c] = jnp.where(row == j, nr, init[2 * c])
            init[2 * c + 1] = jnp.where(row == j, ni, init[2 * c + 1])

    hs = init
    for b in range(nblk):
        for t in range(b * steps, (b + 1) * steps):
            hs = advance(t, hs, store=True)
        if b > 0:
            project_out(b - 1)
            scatter(b - 1)
    for c in range(hc):
        hl_ref[0, 0, :, c * LANES:(c + 1) * LANES] = hs[2 * c]
        hl_ref[0, 0, :, (hc + c) * LANES:(hc + c + 1) * LANES] = hs[2 * c + 1]
    _side_cast(refs[6:6 + nside], refs[8 + nside:-3])
    project_out(nblk - 1)
    scatter(nblk - 1)
    for j in range(nseg):
        z_ref[j * seg:(j + 1) * seg, :] = pad_ref[j * pitch:j * pitch + seg, :]


def _s5_prompt(proj, w_in, w_out, a_re, a_im, d_skip, *, batch, seq, rows_total, side=()):
    nv = w_in.shape[0]
    sw = w_in.shape[2]
    seg = seq // SUBLANES
    assert seg * SUBLANES == seq and seg & (seg - 1) == 0
    side_in, side_out, side_shapes, side_plan = _side_specs(
        side, batch * nv, lambda b, v: b * nv + v)
    body = functools.partial(_s5_prompt_body, seg=seg, side_plan=side_plan)
    return pl.pallas_call(
        body,
        out_shape=[jax.ShapeDtypeStruct((rows_total, nv * LANES), F32),
                   jax.ShapeDtypeStruct((batch, nv, SUBLANES, sw), F32)] + side_shapes,
        grid=(batch, nv),
        in_specs=[
            pl.BlockSpec((seq, LANES), lambda b, v: (b, v)),
            pl.BlockSpec((1, LANES, sw), lambda b, v: (v, 0, 0)),
            pl.BlockSpec((1, sw, LANES), lambda b, v: (v, 0, 0)),
            pl.BlockSpec((1, 1, sw // 2), lambda b, v: (v, 0, 0)),
            pl.BlockSpec((1, 1, sw // 2), lambda b, v: (v, 0, 0)),
            pl.BlockSpec((1, LANES), lambda b, v: (0, v)),
        ] + side_in,
        out_specs=[pl.BlockSpec((seq, LANES), lambda b, v: (b, v)),
                   pl.BlockSpec((1, 1, SUBLANES, sw), lambda b, v: (b, v, 0, 0))] + side_out,
        scratch_shapes=[pltpu.VMEM((sw // LANES, seq, LANES), F32),
                        pltpu.VMEM((seq, LANES), F32),
                        pltpu.VMEM((SUBLANES * (seg + SUBLANES), LANES), F32)],
        compiler_params=_cparams(*(("arbitrary",) * 2 if side else ("parallel",) * 2)),
        name="s5_prompt",
    )(proj, w_in, w_out, a_re, a_im, d_skip, *side)


def _s5_sample_body(u_ref, hre_ref, him_ref, w_ref, cm_ref, ar_ref, ai_ref, d_ref,
                    z_ref, ore_ref, oim_ref):
    half = ar_ref.shape[2]
    u = u_ref[...]
    bu = _dot_hi(u, w_ref[0])
    pr, pi = _cmul(ar_ref[0], ai_ref[0], hre_ref[...], him_ref[...])
    hr = pr + bu[:, :half]
    hi = pi + bu[:, half:]
    ore_ref[...] = hr
    oim_ref[...] = hi
    y = d_ref[...] * u + _dot_hi(hr, cm_ref[0, :half, :]) + _dot_hi(hi, cm_ref[0, half:, :])
    z_ref[...] = jax.nn.gelu(y)


def _s5_sample(proj, h_re, h_im, w_in, w_out, a_re, a_im, d_skip, *, row0):
    nv = w_in.shape[0]
    sw = w_in.shape[2]
    ns = h_re.shape[0]
    rb = row0 // ns
    assert rb * ns == row0
    st = pl.BlockSpec((ns, sw // 2), lambda v: (0, v))
    return pl.pallas_call(
        _s5_sample_body,
        out_shape=(jax.ShapeDtypeStruct((ns, nv * LANES), F32),
                   jax.ShapeDtypeStruct(h_re.shape, F32),
                   jax.ShapeDtypeStruct(h_im.shape, F32)),
        grid=(nv,),
        in_specs=[
            pl.BlockSpec((ns, LANES), lambda v: (rb, v)),
            st, st,
            pl.BlockSpec((1, LANES, sw), lambda v: (v, 0, 0)),
            pl.BlockSpec((1, sw, LANES), lambda v: (v, 0, 0)),
            pl.BlockSpec((1, 1, sw // 2), lambda v: (v, 0, 0)),
            pl.BlockSpec((1, 1, sw // 2), lambda v: (v, 0, 0)),
            pl.BlockSpec((1, LANES), lambda v: (0, v)),
        ],
        out_specs=(pl.BlockSpec((ns, LANES), lambda v: (0, v)), st, st),
        compiler_params=_cparams("parallel"),
        name="s5_sample",
    )(proj, h_re, h_im, w_in, w_out, a_re, a_im, d_skip)


def _hg_span_matrices(ch):
    t = np.arange(ch)[:, None]
    r = np.arange(ch)[None, :]
    mats = [r <= t, r > t]
    for v in range(1, ch.bit_length() - 1):
        base = (t >> v) << v
        upper = ((t >> v) & 1) == 1
        mats.append(np.where(upper, (r >= base) & (r <= t), (r > t) & (r < base + (1 << v))))
    mm = np.stack(mats).astype(np.float32)
    return jnp.asarray(np.concatenate([mm, mm], axis=-1), BF16)


def _hg_prompt_body(*refs, dk, side_plan):
    nside = len(side_plan)
    q_ref, f_ref, i_ref, og_ref, lb_ref, gn_ref, mm_ref = refs[:7]
    o_ref, sfin_ref = refs[7 + nside:9 + nside]
    st_ref = refs[-1]
    _side_cast(refs[7:7 + nside], refs[9 + nside:-1])
    c = pl.program_id(1)
    nh = st_ref.shape[0]
    ch = mm_ref.shape[1]
    nsub = q_ref.shape[0] // ch
    width = q_ref.shape[1]
    nlev = ch.bit_length() - 1

    @pl.when(c == 0)
    def _():
        st_ref[...] = jnp.zeros_like(st_ref)

    lb = lb_ref[...]
    f_all = lb + (1.0 - lb) * jax.nn.sigmoid(f_ref[...])
    kk_all = 1.0 - f_all
    qs_all = _silu(q_ref[...])
    logf_pieces = _split3(jnp.log2(f_all))[:2]
    ib_all = i_ref[...].astype(BF16)
    pairs = [slice(2 * j * dk, 2 * (j + 1) * dk) for j in range(nh // 2)]
    heads = [slice(h * dk, (h + 1) * dk) for h in range(nh)]
    r = lax.broadcasted_iota(jnp.int32, (ch, 2 * ch), 0)
    s = lax.broadcasted_iota(jnp.int32, (ch, 2 * ch), 1) & (ch - 1)
    xr = jnp.where(r > s, r ^ s, 0)
    zero_k = jnp.zeros((ch, dk), BF16)
    zero_s = jnp.zeros((dk, dk), BF16)

    def block_diag(a, b, z):
        return jnp.concatenate([jnp.concatenate([a, z], axis=1),
                                jnp.concatenate([z, b], axis=1)], axis=0)

    def pair_scores(lhs, x):
        return _dot_nt(lhs, block_diag(x[:, :dk], x[:, dk:], zero_k))

    def upper_runs(v):
        m = 1 << v
        return [(b0, b0 + m) for b0 in range(m, ch, 2 * m)]

    def mix_rows(v, qs, kk):
        m = 1 << v
        if m >= SUBLANES:
            return jnp.concatenate(
                [(qs if (b0 // m) & 1 else kk)[b0:b0 + m] for b0 in range(0, ch, m)], axis=0)
        pick = ((lax.broadcasted_iota(jnp.int32, (1, SUBLANES, width), 1) >> v) & 1) == 1
        shape3 = (ch // SUBLANES, SUBLANES, width)
        return jnp.where(pick, qs.reshape(shape3), kk.reshape(shape3)).reshape(ch, width)

    staged = []
    for sub in range(nsub):
        rows = slice(sub * ch, (sub + 1) * ch)
        kk, qs = kk_all[rows], qs_all[rows]
        logf2 = jnp.concatenate([p[rows] for p in logf_pieces], axis=0)

        def decay(idx):
            return jnp.exp2(_dot(mm_ref[idx], logf2))

        eg = decay(0)
        qg = (qs * eg).astype(BF16)
        dec = eg[ch - 1:ch, :]
        kh = (kk * decay(1)).astype(BF16)
        lev = [mix_rows(0, qs * f_all[rows], kk).astype(BF16)]
        lev += [(decay(1 + v) * mix_rows(v, qs, kk)).astype(BF16) for v in range(1, nlev)]
        qb, kb = qs.astype(BF16), kk.astype(BF16)
        atts = []
        for sl in pairs:
            att = jnp.where(r == s, pair_scores(qb[:, sl], kb[:, sl]), 0.0)
            for v in range(nlev):
                x = lev[v][:, sl]
                m = 1 << v
                if m >= BF16_ROWS:
                    runs = upper_runs(v)
                    p = pair_scores(jnp.concatenate([x[a:b] for a, b in runs], axis=0), x)
                    parts = []
                    for n in range(len(runs)):
                        parts += [jnp.zeros((m, 2 * ch), F32), p[n * m:(n + 1) * m]]
                    p = jnp.concatenate(parts, axis=0)
                else:
                    p = pair_scores(x, x)
                att = jnp.where((xr >> v) == 1, p, att)
            atts.append(att.astype(BF16))
        staged.append((rows, qg, kh, dec, atts))

    for rows, qg, kh, dec, atts in staged:
        ib = ib_all[rows]
        for j, sl in enumerate(pairs):
            st2 = block_diag(st_ref[2 * j].astype(BF16), st_ref[2 * j + 1].astype(BF16), zero_s)
            ib2 = block_diag(ib[:, sl][:, :dk], ib[:, sl][:, dk:], zero_k)
            o = _dot_nt(qg[:, sl], st2) + _dot(atts[j], ib2)
            for e in range(2):
                hs = heads[2 * j + e]
                og = og_ref[rows, hs]
                o_ref[rows, hs] = _rms(o[:, e * dk:(e + 1) * dk], gn_ref[:, hs]) * _silu(og)
        for h, hs in enumerate(heads):
            st_ref[h] = st_ref[h] * dec[:, hs] + _dot_tn(ib[:, hs], kh[:, hs])

    @pl.when(c == pl.num_programs(1) - 1)
    def _():
        for h in range(nh):
            sfin_ref[0, h] = st_ref[h].T


def _hg_prompt(proj, lb, gn, *, batch, seq, rows_total, nh, col0, side=()):
    width = lb.shape[1]
    dk = width // nh
    ch = HG_CHUNK * HG_CHUNKS_PER_STEP
    nchunk = seq // ch
    cb = col0 // width
    assert cb * width == col0 and nchunk * ch == seq

    def tok(k):
        return pl.BlockSpec((ch, width), lambda b, c: (b * nchunk + c, cb + k))

    vec = pl.BlockSpec((1, width), lambda b, c: (0, 0))
    mm = _hg_span_matrices(HG_CHUNK)
    side_in, side_out, side_shapes, side_plan = _side_specs(
        side, batch * nchunk, lambda b, c: b * nchunk + c)
    return pl.pallas_call(
        functools.partial(_hg_prompt_body, dk=dk, side_plan=side_plan),
        out_shape=[jax.ShapeDtypeStruct((rows_total, width), F32),
                   jax.ShapeDtypeStruct((batch, nh, dk, dk), F32)] + side_shapes,
        grid=(batch, nchunk),
        in_specs=[tok(0), tok(1), tok(2), tok(3), vec, vec,
                  pl.BlockSpec(mm.shape, lambda b, c: (0, 0, 0))] + side_in,
        out_specs=[pl.BlockSpec((ch, width), lambda b, c: (b * nchunk + c, 0)),
                   pl.BlockSpec((1, nh, dk, dk), lambda b, c: (b, 0, 0, 0))] + side_out,
        scratch_shapes=[pltpu.VMEM((nh, dk, dk), F32)],
        compiler_params=_cparams("arbitrary" if side else "parallel", "arbitrary"),
        name="hgrn_prompt",
    )(proj, proj, proj, proj, lb, gn, mm, *side)


def _split3(x):
    p1 = x.astype(BF16)
    r1 = x - p1.astype(F32)
    p2 = r1.astype(BF16)
    p3 = (r1 - p2.astype(F32)).astype(BF16)
    return p1, p2, p3


def _hg_sample_body(q_ref, f_ref, i_ref, og_ref, lb_ref, gn_ref, s0_ref,
                    hg_ref, s_ref, fq_ref, *, dk):
    step = pl.program_id(0)
    nt = q_ref.shape[0]
    tb, nh = s0_ref.shape[0], s0_ref.shape[1]
    npiece = fq_ref.shape[1] // nt

    @pl.when(step == 0)
    def _():
        lb = lb_ref[...]
        f = lb + (1.0 - lb) * jax.nn.sigmoid(f_ref[...])
        qs = _silu(q_ref[...])
        for src, base in ((f, 0), (qs, nh)):
            for h in range(nh):
                t = src[:, h * dk:(h + 1) * dk].T
                for p, piece in enumerate(_split3(t)[:npiece]):
                    fq_ref[(base + h) * dk:(base + h + 1) * dk, p * nt:(p + 1) * nt] = piece

    tok = lax.broadcasted_iota(jnp.int32, (npiece * nt, 2 * dk), 0) & (nt - 1)
    second = lax.broadcasted_iota(jnp.int32, (npiece * nt, 2 * dk), 1) >= dk
    for j0 in range(0, tb, 2):
        n0 = step * tb + j0
        onehot = (tok == jnp.where(second, n0 + 1, n0)).astype(BF16)
        fq2 = _dot(fq_ref[...], onehot)
        for e in range(2):
            j, n = j0 + e, n0 + e
            fq = fq2[:, e * dk:(e + 1) * dk]
            irow = i_ref[pl.ds(n, 1), :]
            ogrow = og_ref[pl.ds(n, 1), :]
            for h in range(nh):
                sl = slice(h * dk, (h + 1) * dk)
                fb = fq[h * dk:(h + 1) * dk, :]
                qb = fq[(nh + h) * dk:(nh + h + 1) * dk, :]
                s1 = fb * s0_ref[j, h] + (1.0 - fb) * irow[:, sl]
                s_ref[j, h] = s1
                o = jnp.sum(qb * s1, axis=0, keepdims=True)
                hg_ref[j:j + 1, sl] = _rms(o, gn_ref[:, sl]) * _silu(ogrow[:, sl])


def _hg_sample(proj, lb, gn, s0, *, row0, nh, col0):
    width = lb.shape[1]
    dk = width // nh
    ns = s0.shape[0]
    tb = SAMPLE_TOKENS_PER_STEP
    rb, cb = row0 // ns, col0 // width
    assert rb * ns == row0 and cb * width == col0 and ns % tb == 0
    assert ns & (ns - 1) == 0 and tb % 2 == 0 and dk == LANES

    def tok(k):
        return pl.BlockSpec((ns, width), lambda t: (rb, cb + k))

    vec = pl.BlockSpec((1, width), lambda t: (0, 0))
    sspec = pl.BlockSpec((tb, nh, dk, dk), lambda t: (t, 0, 0, 0))
    return pl.pallas_call(
        functools.partial(_hg_sample_body, dk=dk),
        out_shape=(jax.ShapeDtypeStruct((ns, width), F32),
                   jax.ShapeDtypeStruct(s0.shape, F32)),
        grid=(ns // tb,),
        in_specs=[tok(0), tok(1), tok(2), tok(3), vec, vec, sspec],
        out_specs=(pl.BlockSpec((tb, width), lambda t: (t, 0)), sspec),
        scratch_shapes=[pltpu.VMEM((2 * nh * dk, 2 * ns), BF16)],
        compiler_params=_cparams("arbitrary"),
        name="hgrn_sample",
    )(proj, proj, proj, proj, lb, gn, s0)


def _merge_body(z_ref, zs_ref, hg_ref, hgs_ref, gs0_ref, gs1_ref, gh0_ref, gh1_ref, x_ref,
                wglu_ref, bglu_ref, wbs_ref, wbh_ref, wout_ref, post_ref, o_ref, *, ns):
    i = pl.program_id(0)
    last_i = pl.num_programs(0) - 1
    cut = x_ref.shape[0] - ns
    half = gs0_ref.shape[1]

    def rows(z, hg, sl):
        s5o = z * jax.nn.sigmoid(_dot(z.astype(BF16), wglu_ref[...]) + bglu_ref[...])
        a = _dot(s5o.astype(BF16), wbs_ref[...])
        b = _dot(hg.astype(BF16), wbh_ref[...])
        m0 = (jax.nn.sigmoid(gs0_ref[sl, :]) * a[:, :half]
              + jax.nn.sigmoid(gh0_ref[sl, :]) * b[:, :half])
        m1 = (jax.nn.sigmoid(gs1_ref[sl, :]) * a[:, half:]
              + jax.nn.sigmoid(gh1_ref[sl, :]) * b[:, half:])
        mix = _dot(m0.astype(BF16), wout_ref[:half, :]) + _dot(m1.astype(BF16), wout_ref[half:, :])
        o_ref[sl, :] = x_ref[sl, :] + _rms(mix, post_ref[...])

    @pl.when(i < last_i)
    def _():
        rows(z_ref[...], hg_ref[...], slice(None))

    @pl.when(i == last_i)
    def _():
        rows(z_ref[:cut, :], hg_ref[:cut, :], slice(0, cut))
        rows(zs_ref[...], hgs_ref[...], slice(cut, None))


def _merge(z, zs, hg, hgs, proj, x, wglu, bglu, wbs, wbh, wout, post, *, col0, tm):
    m, d = x.shape
    w = z.shape[1]
    ns = zs.shape[0]
    cb = col0 // w
    assert cb * w == col0 and d == 2 * w and m % tm == 0 and 0 < ns < tm

    def gate(k):
        return pl.BlockSpec((tm, w), lambda i: (i, cb + k))

    def const(shape):
        return pl.BlockSpec(shape, lambda i: (0, 0), pipeline_mode=pl.Buffered(1))

    tok = pl.BlockSpec((tm, w), lambda i: (i, 0))
    return pl.pallas_call(
        functools.partial(_merge_body, ns=ns),
        out_shape=jax.ShapeDtypeStruct((m, d), F32),
        grid=(m // tm,),
        in_specs=[
            tok, const((ns, w)), tok, const((ns, w)),
            gate(0), gate(1), gate(2), gate(3),
            pl.BlockSpec((tm, d), lambda i: (i, 0)),
            const((w, w)), const((1, w)), const((w, d)), const((w, d)), const((d, d)), const((1, d)),
        ],
        out_specs=pl.BlockSpec((tm, d), lambda i: (i, 0)),
        compiler_params=_cparams("parallel"),
        name="merge",
    )(z, zs, hg, hgs, proj, proj, proj, proj, x, wglu, bglu, wbs, wbh, wout, post)


def kernel(x_prompt, x_sample, state_s5_re, state_s5_im, state_hgrn, ffn1_pre_norm, ffn1_w_gate, ffn1_w_up, ffn1_w_down, ffn1_post_norm, mix_pre_norm, w_in, s5_lambda_re, s5_lambda_im, s5_log_dt, s5_b_re, s5_b_im, s5_c_re, s5_c_im, s5_d, s5_w_glu, s5_b_glu, hgrn_lb_logits, hgrn_out_norm, w_branch_s5, w_branch_hgrn, w_out, mix_post_norm, ffn2_pre_norm, ffn2_w_gate, ffn2_w_up, ffn2_w_down, ffn2_post_norm):
    depth = ffn1_w_gate.shape[0]
    assert depth == 1
    batch, seq, d = x_prompt.shape
    ns = x_sample.shape[0]
    assert x_sample.shape[1] == 1
    g, p = s5_lambda_re.shape[1:]
    nh, dk = state_hgrn.shape[2], state_hgrn.shape[3]
    s5w = s5_d.shape[1]
    hgw = nh * dk
    mp = batch * seq
    m = mp + ns

    lay = lambda a: a.reshape(a.shape[1:])
    row = lambda a: a.reshape(1, -1)

    lb = jax.nn.softmax(hgrn_lb_logits.astype(F32), axis=0)[:1]

    x, w_in_b = _ffn(
        x_prompt.reshape(mp, d), x_sample.reshape(ns, d), row(ffn1_pre_norm),
        lay(ffn1_w_gate), lay(ffn1_w_up), lay(ffn1_w_down), row(ffn1_post_norm),
        ns=ns, split_out=False, side=(lay(w_in),))
    proj = _inproj(x, row(mix_pre_norm), w_in_b)

    tw_in, tw_out, a_re, a_im = _s5_tables(
        lay(s5_lambda_re), lay(s5_lambda_im), lay(s5_log_dt), lay(s5_b_re), lay(s5_b_im),
        lay(s5_c_re), lay(s5_c_im))
    d_skip = row(s5_d)
    z, hlast, wg2_b, wu2_b, wd2_b, wglu_b, wbs_b, wbh_b, wout_b = _s5_prompt(
        proj, tw_in.astype(BF16), tw_out.astype(BF16), a_re, a_im, d_skip,
        batch=batch, seq=seq, rows_total=mp,
        side=(lay(ffn2_w_gate), lay(ffn2_w_up), lay(ffn2_w_down), lay(s5_w_glu),
              lay(w_branch_s5), lay(w_branch_hgrn), lay(w_out)))
    zs, s_re, s_im = _s5_sample(proj, state_s5_re.reshape(ns, g * p),
                                state_s5_im.reshape(ns, g * p),
                                tw_in, tw_out, a_re, a_im, d_skip, row0=mp)
    half = hlast.shape[-1] // 2
    p_re = hlast[:, :, SUBLANES - 1, :half].reshape(1, batch, g, p)
    p_im = hlast[:, :, SUBLANES - 1, half:].reshape(1, batch, g, p)

    gn = row(hgrn_out_norm)
    hg, st_p = _hg_prompt(proj, lb, gn, batch=batch, seq=seq, rows_total=mp, nh=nh, col0=s5w)
    hgs, st_s = _hg_sample(proj, lb, gn, lay(state_hgrn), row0=mp, nh=nh, col0=s5w)

    x = _merge(z, zs, hg, hgs, proj, x, wglu_b, row(s5_b_glu), wbs_b, wbh_b, wout_b,
               row(mix_post_norm), col0=s5w + 4 * hgw, tm=MERGE_ROW_TILE)
    yp, ys = _ffn(x, None, row(ffn2_pre_norm), wg2_b, wu2_b, wd2_b, row(ffn2_post_norm),
                  ns=ns, split_out=True)

    return (yp.reshape(batch, seq, d), ys,
            p_re, p_im, st_p[None],
            s_re.reshape(1, ns, g, p), s_im.reshape(1, ns, g, p), st_s[None])
```

```python
import functools

import jax
import jax.numpy as jnp
import numpy as np
from jax import lax
from jax.experimental import pallas as pl
from jax.experimental.pallas import tpu as pltpu

F32 = jnp.float32
BF16 = jnp.bfloat16
EPS = 1e-6
HIGHEST = lax.Precision.HIGHEST

LANES = 128
SUBLANES = 8
VMEM_LIMIT = 60 * 1024 * 1024

FFN_TILES_BF16 = (832, 512)
FFN_TILES_F32 = (1040, 256)
IN_TILE = 2304
IN_ROW_TILE = 832
MERGE_ROW_TILE = 416
HG_CHUNK = 64
HG_CHUNKS_PER_STEP = 4
SAMPLE_TOKENS_PER_STEP = 16


def _cparams(*sem):
    return pltpu.CompilerParams(dimension_semantics=sem, vmem_limit_bytes=VMEM_LIMIT)


def _rms(x, g):
    return x * lax.rsqrt(jnp.mean(x * x, axis=-1, keepdims=True) + EPS) * g


def _silu(x):
    return x * jax.nn.sigmoid(x)


def _dot(a, b):
    return jnp.dot(a, b, preferred_element_type=F32)


def _dot_hi(a, b):
    return jnp.dot(a, b, preferred_element_type=F32, precision=HIGHEST)


def _dot_nt(a, b):
    return lax.dot_general(a, b, (((1,), (1,)), ((), ())), preferred_element_type=F32)


def _dot_tn(a, b):
    return lax.dot_general(a, b, (((0,), (0,)), ((), ())), preferred_element_type=F32)


BF16_ROWS = 16


def _side_specs(arrays, nsteps, step_of):
    in_specs, out_specs, out_shapes, plan = [], [], [], []
    for a in arrays:
        r, c = a.shape
        rows = next(t for t in range(BF16_ROWS, r + 1, BF16_ROWS) if r % t == 0 and r // t <= nsteps)
        nblk = r // rows
        every = nsteps // nblk

        def imap(*g, nblk=nblk, every=every):
            return (jnp.minimum(step_of(*g) // every, nblk - 1), 0)

        in_specs.append(pl.BlockSpec((rows, c), imap))
        out_specs.append(pl.BlockSpec((rows, c), imap))
        out_shapes.append(jax.ShapeDtypeStruct((r, c), BF16))
        plan.append((nblk, every))
    return in_specs, out_specs, out_shapes, tuple(plan)


def _side_cast(src_refs, dst_refs):
    for src, dst in zip(src_refs, dst_refs):
        dst[...] = src[...].astype(BF16)


def _ffn_body(*refs, ns, split_in, split_out, side_plan):
    refs = list(refs)
    nside = len(side_plan)
    x_ref = refs.pop(0)
    xs_ref = refs.pop(0) if split_in else None
    pre_ref, wg_ref, wu_ref, wd_ref, post_ref = refs[:5]
    side_src = refs[5:5 + nside]
    o_ref = refs[5 + nside]
    os_ref = refs[6 + nside] if split_out else None
    side_dst = refs[-1 - nside:-1]
    h_ref = refs[-1]
    acc_ref = o_ref
    i, k = pl.program_id(0), pl.program_id(1)
    last_i = pl.num_programs(0) - 1
    last_k = pl.num_programs(1) - 1
    cut = h_ref.shape[0] - ns

    def swiglu_down(h):
        _side_cast(side_src, side_dst)
        g = _dot(h, wg_ref[...].astype(BF16))
        u = _dot(h, wu_ref[...].astype(BF16))
        return _dot((_silu(g) * u).astype(BF16), wd_ref[...].astype(BF16))

    @pl.when((i < last_i) & (k == 0))
    def _():
        h = _rms(x_ref[...], pre_ref[...]).astype(BF16)
        h_ref[...] = h
        acc_ref[...] = swiglu_down(h)

    @pl.when((i < last_i) & (k == last_k))
    def _():
        acc = acc_ref[...] + swiglu_down(h_ref[...])
        o_ref[...] = x_ref[...] + 0.5 * _rms(acc, post_ref[...])

    @pl.when((i == last_i) & (k == 0))
    def _():
        h_ref[:cut, :] = _rms(x_ref[:cut, :], pre_ref[...]).astype(BF16)
        xt = xs_ref[...] if split_in else x_ref[cut:, :]
        h_ref[cut:, :] = _rms(xt, pre_ref[...]).astype(BF16)
        acc_ref[...] = jnp.zeros_like(acc_ref)

    @pl.when(((k > 0) & (k < last_k)) | (i == last_i))
    def _():
        acc_ref[...] += swiglu_down(h_ref[...])

    @pl.when((i == last_i) & (k == last_k))
    def _():
        o_ref[:cut, :] = x_ref[:cut, :] + 0.5 * _rms(acc_ref[:cut, :], post_ref[...])
        xt = xs_ref[...] if split_in else x_ref[cut:, :]
        tail = xt + 0.5 * _rms(acc_ref[cut:, :], post_ref[...])
        if split_out:
            os_ref[...] = tail
        else:
            o_ref[cut:, :] = tail


def _ffn(x, xs, pre, wg, wu, wd, post, *, ns, split_out, side=()):
    split_in = xs is not None
    d = x.shape[1]
    m = x.shape[0] + (ns if split_in else 0)
    dff = wg.shape[1]
    tm, tf = (FFN_TILES_F32 if wg.dtype == F32 else FFN_TILES_BF16)
    assert m % tm == 0 and dff % tf == 0 and 0 < ns < tm and ns % BF16_ROWS == 0
    nk = dff // tf
    side_in, side_out, side_shapes, side_plan = _side_specs(
        side, (m // tm) * nk, lambda i, k: i * nk + k)
    tok = pl.BlockSpec((tm, d), lambda i, k: (i, 0))
    smp_in = pl.BlockSpec((ns, d), lambda i, k: (0, 0))
    smp = pl.BlockSpec((ns, None, d), lambda i, k: (0, 0, 0))
    vec = pl.BlockSpec((1, d), lambda i, k: (0, 0))
    in_specs = [tok] + ([smp_in] if split_in else []) + [
        vec,
        pl.BlockSpec((d, tf), lambda i, k: (0, k)),
        pl.BlockSpec((d, tf), lambda i, k: (0, k)),
        pl.BlockSpec((tf, d), lambda i, k: (k, 0)),
        vec,
    ] + side_in
    if split_out:
        out_shape = [jax.ShapeDtypeStruct((m - ns, d), F32), jax.ShapeDtypeStruct((ns, 1, d), F32)]
        out_specs = [tok, smp]
    else:
        out_shape = [jax.ShapeDtypeStruct((m, d), F32)]
        out_specs = [tok]
    args = [x] + ([xs] if split_in else []) + [pre, wg, wu, wd, post] + list(side)
    sequential_rows = split_out or bool(side)
    return pl.pallas_call(
        functools.partial(_ffn_body, ns=ns, split_in=split_in, split_out=split_out,
                          side_plan=side_plan),
        out_shape=out_shape + side_shapes,
        grid=(m // tm, nk),
        in_specs=in_specs,
        out_specs=out_specs + side_out,
        scratch_shapes=[pltpu.VMEM((tm, d), BF16)],
        compiler_params=_cparams("arbitrary" if sequential_rows else "parallel", "arbitrary"),
        name="ffn",
    )(*args)


def _inproj_body(x_ref, g_ref, w_ref, o_ref, h_ref):
    j = pl.program_id(1)

    @pl.when(j == 0)
    def _():
        h = _rms(x_ref[...], g_ref[...]).astype(BF16)
        h_ref[...] = h
        o_ref[...] = _dot(h, w_ref[...])

    @pl.when(j > 0)
    def _():
        o_ref[...] = _dot(h_ref[...], w_ref[...])


def _inproj(x, g, w):
    m, d = x.shape
    n = w.shape[1]
    tm, tn = IN_ROW_TILE, IN_TILE
    assert m % tm == 0
    return pl.pallas_call(
        _inproj_body,
        out_shape=jax.ShapeDtypeStruct((m, n), F32),
        grid=(m // tm, n // tn),
        in_specs=[
            pl.BlockSpec((tm, d), lambda i, j: (i, 0)),
            pl.BlockSpec((1, d), lambda i, j: (0, 0)),
            pl.BlockSpec((d, tn), lambda i, j: (0, j)),
        ],
        out_specs=pl.BlockSpec((tm, tn), lambda i, j: (i, j)),
        scratch_shapes=[pltpu.VMEM((tm, d), BF16)],
        compiler_params=_cparams("parallel", "arbitrary"),
        name="inproj",
    )(x, g, w)


def _s5_tables(lam_re, lam_im, log_dt, b_re, b_im, c_re, c_im):
    g, p = lam_re.shape
    n = b_re.shape[-1]
    gpt = LANES // n
    nv = g // gpt
    dt = jnp.exp(log_dt)[:, None]
    er = jnp.exp(lam_re * dt)
    th = lam_im * dt
    a_re, a_im = er * jnp.cos(th), er * jnp.sin(th)
    den = lam_re * lam_re + lam_im * lam_im
    k_re = ((a_re - 1.0) * lam_re + a_im * lam_im) / den
    k_im = (a_im * lam_re - (a_re - 1.0) * lam_im) / den
    bb_re = k_re[..., None] * b_re - k_im[..., None] * b_im
    bb_im = k_re[..., None] * b_im + k_im[..., None] * b_re
    sw = 2 * gpt * p
    lane_group = (jnp.arange(sw) // p) % gpt
    io_group = jnp.arange(gpt * n) // n
    bb = jnp.stack([bb_re, bb_im]).reshape(2, nv, gpt, p, n)
    dense_in = bb.transpose(1, 4, 0, 2, 3).reshape(nv, n, sw)
    w_in = jnp.where(io_group[:, None] == lane_group[None, :],
                     jnp.tile(dense_in, (1, gpt, 1)), 0.0)
    cc = jnp.stack([c_re, -c_im]).reshape(2, nv, gpt, n, p)
    dense_out = cc.transpose(1, 0, 2, 4, 3).reshape(nv, sw, n)
    w_out = jnp.where(lane_group[:, None] == io_group[None, :],
                      jnp.tile(dense_out, (1, 1, gpt)), 0.0)
    return w_in, w_out, a_re.reshape(nv, 1, gpt * p), a_im.reshape(nv, 1, gpt * p)


def _cmul(ar, ai, br, bi):
    return ar * br - ai * bi, ar * bi + ai * br


def _s5_prompt_body(*refs, seg, side_plan):
    nside = len(side_plan)
    u_ref, w_ref, cm_ref, ar_ref, ai_ref, d_ref = refs[:6]
    z_ref, hl_ref = refs[6 + nside:8 + nside]
    buh_ref, ut_ref, pad_ref = refs[-3:]
    nseg = SUBLANES
    nc = w_ref.shape[2] // LANES
    hc = nc // 2
    pitch = pad_ref.shape[0] // nseg
    nblk = nseg
    steps = seg // nseg

    def gather(b):
        for t in range(b * steps, (b + 1) * steps):
            ut_ref[t * nseg:(t + 1) * nseg, :] = pad_ref[pl.ds(t, nseg, stride=pitch), :]

    def scatter(b):
        for t in range(b * steps, (b + 1) * steps):
            pad_ref[pl.ds(t, nseg, stride=pitch), :] = ut_ref[t * nseg:(t + 1) * nseg, :]

    for j in range(nseg):
        pad_ref[j * pitch:j * pitch + seg, :] = u_ref[j * seg:(j + 1) * seg, :]

    def project_in(b):
        blk = slice(b * seg, (b + 1) * seg)
        bu = _dot(ut_ref[blk, :].astype(BF16), w_ref[0])
        for c in range(nc):
            buh_ref[c, blk, :] = bu[:, c * LANES:(c + 1) * LANES]

    def project_out(b):
        blk = slice(b * seg, (b + 1) * seg)
        y = d_ref[...] * ut_ref[blk, :]
        for c in range(nc):
            y = y + _dot(buh_ref[c, blk, :].astype(BF16), cm_ref[0, c * LANES:(c + 1) * LANES, :])
        ut_ref[blk, :] = jax.nn.gelu(y)

    ar = [jnp.broadcast_to(ar_ref[0, :, c * LANES:(c + 1) * LANES], (nseg, LANES)) for c in range(hc)]
    ai = [jnp.broadcast_to(ai_ref[0, :, c * LANES:(c + 1) * LANES], (nseg, LANES)) for c in range(hc)]

    def advance(t, hs, store):
        rows = slice(t * nseg, (t + 1) * nseg)
        out = []
        for c in range(hc):
            pr, pi = _cmul(ar[c], ai[c], hs[2 * c], hs[2 * c + 1])
            out += [pr + buh_ref[c, rows, :], pi + buh_ref[hc + c, rows, :]]
        if store:
            for c in range(hc):
                buh_ref[c, rows, :] = out[2 * c]
                buh_ref[hc + c, rows, :] = out[2 * c + 1]
        return out

    zeros = [jnp.zeros((nseg, LANES), F32) for _ in range(2 * hc)]
    hs = zeros
    gather(0)
    project_in(0)
    for b in range(nblk):
        if b + 1 < nblk:
            gather(b + 1)
            project_in(b + 1)
        for t in range(b * steps, (b + 1) * steps):
            hs = advance(t, hs, store=False)
    ends = hs

    pw = [(ar[c], ai[c]) for c in range(hc)]
    for _ in range(seg.bit_length() - 1):
        pw = [_cmul(r, i, r, i) for r, i in pw]

    row = lax.broadcasted_iota(jnp.int32, (nseg, LANES), 0)
    init = list(zeros)
    for j in range(1, nseg):
        for c in range(hc):
            pr, pi = _cmul(pw[c][0], pw[c][1], init[2 * c], init[2 * c + 1])
            nr = pltpu.roll(pr + ends[2 * c], 1, 0)
            ni = pltpu.roll(pi + ends[2 * c + 1], 1, 0)
            init[2 * c] = jnp.where(row == j, nr, init[2 * c])
            init[2 * c + 1] = jnp.where(row == j, ni, init[2 * c + 1])

    hs = init
    for b in range(nblk):
        for t in range(b * steps, (b + 1) * steps):
            hs = advance(t, hs, store=True)
        if b > 0:
            project_out(b - 1)
            scatter(b - 1)
    for c in range(hc):
        hl_ref[0, 0, :, c * LANES:(c + 1) * LANES] = hs[2 * c]
        hl_ref[0, 0, :, (hc + c) * LANES:(hc + c + 1) * LANES] = hs[2 * c + 1]
    _side_cast(refs[6:6 + nside], refs[8 + nside:-3])
    project_out(nblk - 1)
    scatter(nblk - 1)
    for j in range(nseg):
        z_ref[j * seg:(j + 1) * seg, :] = pad_ref[j * pitch:j * pitch + seg, :]


def _s5_prompt(proj, w_in, w_out, a_re, a_im, d_skip, *, batch, seq, rows_total, side=()):
    nv = w_in.shape[0]
    sw = w_in.shape[2]
    seg = seq // SUBLANES
    assert seg * SUBLANES == seq and seg & (seg - 1) == 0
    side_in, side_out, side_shapes, side_plan = _side_specs(
        side, batch * nv, lambda b, v: b * nv + v)
    body = functools.partial(_s5_prompt_body, seg=seg, side_plan=side_plan)
    return pl.pallas_call(
        body,
        out_shape=[jax.ShapeDtypeStruct((rows_total, nv * LANES), F32),
                   jax.ShapeDtypeStruct((batch, nv, SUBLANES, sw), F32)] + side_shapes,
        grid=(batch, nv),
        in_specs=[
            pl.BlockSpec((seq, LANES), lambda b, v: (b, v)),
            pl.BlockSpec((1, LANES, sw), lambda b, v: (v, 0, 0)),
            pl.BlockSpec((1, sw, LANES), lambda b, v: (v, 0, 0)),
            pl.BlockSpec((1, 1, sw // 2), lambda b, v: (v, 0, 0)),
            pl.BlockSpec((1, 1, sw // 2), lambda b, v: (v, 0, 0)),
            pl.BlockSpec((1, LANES), lambda b, v: (0, v)),
        ] + side_in,
        out_specs=[pl.BlockSpec((seq, LANES), lambda b, v: (b, v)),
                   pl.BlockSpec((1, 1, SUBLANES, sw), lambda b, v: (b, v, 0, 0))] + side_out,
        scratch_shapes=[pltpu.VMEM((sw // LANES, seq, LANES), F32),
                        pltpu.VMEM((seq, LANES), F32),
                        pltpu.VMEM((SUBLANES * (seg + SUBLANES), LANES), F32)],
        compiler_params=_cparams(*(("arbitrary",) * 2 if side else ("parallel",) * 2)),
        name="s5_prompt",
    )(proj, w_in, w_out, a_re, a_im, d_skip, *side)


def _s5_sample_body(u_ref, hre_ref, him_ref, w_ref, cm_ref, ar_ref, ai_ref, d_ref,
                    z_ref, ore_ref, oim_ref):
    half = ar_ref.shape[2]
    u = u_ref[...]
    bu = _dot_hi(u, w_ref[0])
    pr, pi = _cmul(ar_ref[0], ai_ref[0], hre_ref[...], him_ref[...])
    hr = pr + bu[:, :half]
    hi = pi + bu[:, half:]
    ore_ref[...] = hr
    oim_ref[...] = hi
    y = d_ref[...] * u + _dot_hi(hr, cm_ref[0, :half, :]) + _dot_hi(hi, cm_ref[0, half:, :])
    z_ref[...] = jax.nn.gelu(y)


def _s5_sample(proj, h_re, h_im, w_in, w_out, a_re, a_im, d_skip, *, row0):
    nv = w_in.shape[0]
    sw = w_in.shape[2]
    ns = h_re.shape[0]
    rb = row0 // ns
    assert rb * ns == row0
    st = pl.BlockSpec((ns, sw // 2), lambda v: (0, v))
    return pl.pallas_call(
        _s5_sample_body,
        out_shape=(jax.ShapeDtypeStruct((ns, nv * LANES), F32),
                   jax.ShapeDtypeStruct(h_re.shape, F32),
                   jax.ShapeDtypeStruct(h_im.shape, F32)),
        grid=(nv,),
        in_specs=[
            pl.BlockSpec((ns, LANES), lambda v: (rb, v)),
            st, st,
            pl.BlockSpec((1, LANES, sw), lambda v: (v, 0, 0)),
            pl.BlockSpec((1, sw, LANES), lambda v: (v, 0, 0)),
            pl.BlockSpec((1, 1, sw // 2), lambda v: (v, 0, 0)),
            pl.BlockSpec((1, 1, sw // 2), lambda v: (v, 0, 0)),
            pl.BlockSpec((1, LANES), lambda v: (0, v)),
        ],
        out_specs=(pl.BlockSpec((ns, LANES), lambda v: (0, v)), st, st),
        compiler_params=_cparams("parallel"),
        name="s5_sample",
    )(proj, h_re, h_im, w_in, w_out, a_re, a_im, d_skip)


def _hg_span_matrices(ch):
    t = np.arange(ch)[:, None]
    r = np.arange(ch)[None, :]
    mats = [r <= t, r > t]
    for v in range(1, ch.bit_length() - 1):
        base = (t >> v) << v
        upper = ((t >> v) & 1) == 1
        mats.append(np.where(upper, (r >= base) & (r <= t), (r > t) & (r < base + (1 << v))))
    mm = np.stack(mats).astype(np.float32)
    return jnp.asarray(np.concatenate([mm, mm], axis=-1), BF16)


def _hg_prompt_body(*refs, dk, side_plan):
    nside = len(side_plan)
    q_ref, f_ref, i_ref, og_ref, lb_ref, gn_ref, mm_ref = refs[:7]
    o_ref, sfin_ref = refs[7 + nside:9 + nside]
    st_ref = refs[-1]
    _side_cast(refs[7:7 + nside], refs[9 + nside:-1])
    c = pl.program_id(1)
    nh = st_ref.shape[0]
    ch = mm_ref.shape[1]
    nsub = q_ref.shape[0] // ch
    width = q_ref.shape[1]
    nlev = ch.bit_length() - 1

    @pl.when(c == 0)
    def _():
        st_ref[...] = jnp.zeros_like(st_ref)

    lb = lb_ref[...]
    f_all = lb + (1.0 - lb) * jax.nn.sigmoid(f_ref[...])
    kk_all = 1.0 - f_all
    qs_all = _silu(q_ref[...])
    logf_pieces = _split3(jnp.log2(f_all))[:2]
    ib_all = i_ref[...].astype(BF16)
    pairs = [slice(2 * j * dk, 2 * (j + 1) * dk) for j in range(nh // 2)]
    heads = [slice(h * dk, (h + 1) * dk) for h in range(nh)]
    r = lax.broadcasted_iota(jnp.int32, (ch, 2 * ch), 0)
    s = lax.broadcasted_iota(jnp.int32, (ch, 2 * ch), 1) & (ch - 1)
    xr = jnp.where(r > s, r ^ s, 0)
    zero_k = jnp.zeros((ch, dk), BF16)
    zero_s = jnp.zeros((dk, dk), BF16)

    def block_diag(a, b, z):
        return jnp.concatenate([jnp.concatenate([a, z], axis=1),
                                jnp.concatenate([z, b], axis=1)], axis=0)

    def pair_scores(lhs, x):
        return _dot_nt(lhs, block_diag(x[:, :dk], x[:, dk:], zero_k))

    def upper_runs(v):
        m = 1 << v
        return [(b0, b0 + m) for b0 in range(m, ch, 2 * m)]

    def mix_rows(v, qs, kk):
        m = 1 << v
        if m >= SUBLANES:
            return jnp.concatenate(
                [(qs if (b0 // m) & 1 else kk)[b0:b0 + m] for b0 in range(0, ch, m)], axis=0)
        pick = ((lax.broadcasted_iota(jnp.int32, (1, SUBLANES, width), 1) >> v) & 1) == 1
        shape3 = (ch // SUBLANES, SUBLANES, width)
        return jnp.where(pick, qs.reshape(shape3), kk.reshape(shape3)).reshape(ch, width)

    staged = []
    for sub in range(nsub):
        rows = slice(sub * ch, (sub + 1) * ch)
        kk, qs = kk_all[rows], qs_all[rows]
        logf2 = jnp.concatenate([p[rows] for p in logf_pieces], axis=0)

        def decay(idx):
            return jnp.exp2(_dot(mm_ref[idx], logf2))

        eg = decay(0)
        qg = (qs * eg).astype(BF16)
        dec = eg[ch - 1:ch, :]
        kh = (kk * decay(1)).astype(BF16)
        lev = [mix_rows(0, qs * f_all[rows], kk).astype(BF16)]
        lev += [(decay(1 + v) * mix_rows(v, qs, kk)).astype(BF16) for v in range(1, nlev)]
        qb, kb = qs.astype(BF16), kk.astype(BF16)
        atts = []
        for sl in pairs:
            att = jnp.where(r == s, pair_scores(qb[:, sl], kb[:, sl]), 0.0)
            for v in range(nlev):
                x = lev[v][:, sl]
                m = 1 << v
                if m >= BF16_ROWS:
                    runs = upper_runs(v)
                    p = pair_scores(jnp.concatenate([x[a:b] for a, b in runs], axis=0), x)
                    parts = []
                    for n in range(len(runs)):
                        parts += [jnp.zeros((m, 2 * ch), F32), p[n * m:(n + 1) * m]]
                    p = jnp.concatenate(parts, axis=0)
                else:
                    p = pair_scores(x, x)
                att = jnp.where((xr >> v) == 1, p, att)
            atts.append(att.astype(BF16))
        staged.append((rows, qg, kh, dec, atts))

    for rows, qg, kh, dec, atts in staged:
        ib = ib_all[rows]
        for j, sl in enumerate(pairs):
            st2 = block_diag(st_ref[2 * j].astype(BF16), st_ref[2 * j + 1].astype(BF16), zero_s)
            ib2 = block_diag(ib[:, sl][:, :dk], ib[:, sl][:, dk:], zero_k)
            o = _dot_nt(qg[:, sl], st2) + _dot(atts[j], ib2)
            for e in range(2):
                hs = heads[2 * j + e]
                og = og_ref[rows, hs]
                o_ref[rows, hs] = _rms(o[:, e * dk:(e + 1) * dk], gn_ref[:, hs]) * _silu(og)
        for h, hs in enumerate(heads):
            st_ref[h] = st_ref[h] * dec[:, hs] + _dot_tn(ib[:, hs], kh[:, hs])

    @pl.when(c == pl.num_programs(1) - 1)
    def _():
        for h in range(nh):
            sfin_ref[0, h] = st_ref[h].T


def _hg_prompt(proj, lb, gn, *, batch, seq, rows_total, nh, col0, side=()):
    width = lb.shape[1]
    dk = width // nh
    ch = HG_CHUNK * HG_CHUNKS_PER_STEP
    nchunk = seq // ch
    cb = col0 // width
    assert cb * width == col0 and nchunk * ch == seq

    def tok(k):
        return pl.BlockSpec((ch, width), lambda b, c: (b * nchunk + c, cb + k))

    vec = pl.BlockSpec((1, width), lambda b, c: (0, 0))
    mm = _hg_span_matrices(HG_CHUNK)
    side_in, side_out, side_shapes, side_plan = _side_specs(
        side, batch * nchunk, lambda b, c: b * nchunk + c)
    return pl.pallas_call(
        functools.partial(_hg_prompt_body, dk=dk, side_plan=side_plan),
        out_shape=[jax.ShapeDtypeStruct((rows_total, width), F32),
                   jax.ShapeDtypeStruct((batch, nh, dk, dk), F32)] + side_shapes,
        grid=(batch, nchunk),
        in_specs=[tok(0), tok(1), tok(2), tok(3), vec, vec,
                  pl.BlockSpec(mm.shape, lambda b, c: (0, 0, 0))] + side_in,
        out_specs=[pl.BlockSpec((ch, width), lambda b, c: (b * nchunk + c, 0)),
                   pl.BlockSpec((1, nh, dk, dk), lambda b, c: (b, 0, 0, 0))] + side_out,
        scratch_shapes=[pltpu.VMEM((nh, dk, dk), F32)],
        compiler_params=_cparams("arbitrary" if side else "parallel", "arbitrary"),
        name="hgrn_prompt",
    )(proj, proj, proj, proj, lb, gn, mm, *side)


def _split3(x):
    p1 = x.astype(BF16)
    r1 = x - p1.astype(F32)
    p2 = r1.astype(BF16)
    p3 = (r1 - p2.astype(F32)).astype(BF16)
    return p1, p2, p3


def _hg_sample_body(q_ref, f_ref, i_ref, og_ref, lb_ref, gn_ref, s0_ref,
                    hg_ref, s_ref, fq_ref, *, dk):
    step = pl.program_id(0)
    nt = q_ref.shape[0]
    tb, nh = s0_ref.shape[0], s0_ref.shape[1]
    npiece = fq_ref.shape[1] // nt

    @pl.when(step == 0)
    def _():
        lb = lb_ref[...]
        f = lb + (1.0 - lb) * jax.nn.sigmoid(f_ref[...])
        qs = _silu(q_ref[...])
        for src, base in ((f, 0), (qs, nh)):
            for h in range(nh):
                t = src[:, h * dk:(h + 1) * dk].T
                for p, piece in enumerate(_split3(t)[:npiece]):
                    fq_ref[(base + h) * dk:(base + h + 1) * dk, p * nt:(p + 1) * nt] = piece

    tok = lax.broadcasted_iota(jnp.int32, (npiece * nt, 2 * dk), 0) & (nt - 1)
    second = lax.broadcasted_iota(jnp.int32, (npiece * nt, 2 * dk), 1) >= dk
    for j0 in range(0, tb, 2):
        n0 = step * tb + j0
        onehot = (tok == jnp.where(second, n0 + 1, n0)).astype(BF16)
        fq2 = _dot(fq_ref[...], onehot)
        for e in range(2):
            j, n = j0 + e, n0 + e
            fq = fq2[:, e * dk:(e + 1) * dk]
            irow = i_ref[pl.ds(n, 1), :]
            ogrow = og_ref[pl.ds(n, 1), :]
            for h in range(nh):
                sl = slice(h * dk, (h + 1) * dk)
                fb = fq[h * dk:(h + 1) * dk, :]
                qb = fq[(nh + h) * dk:(nh + h + 1) * dk, :]
                s1 = fb * s0_ref[j, h] + (1.0 - fb) * irow[:, sl]
                s_ref[j, h] = s1
                o = jnp.sum(qb * s1, axis=0, keepdims=True)
                hg_ref[j:j + 1, sl] = _rms(o, gn_ref[:, sl]) * _silu(ogrow[:, sl])


def _hg_sample(proj, lb, gn, s0, *, row0, nh, col0):
    width = lb.shape[1]
    dk = width // nh
    ns = s0.shape[0]
    tb = SAMPLE_TOKENS_PER_STEP
    rb, cb = row0 // ns, col0 // width
    assert rb * ns == row0 and cb * width == col0 and ns % tb == 0
    assert ns & (ns - 1) == 0 and tb % 2 == 0 and dk == LANES

    def tok(k):
        return pl.BlockSpec((ns, width), lambda t: (rb, cb + k))

    vec = pl.BlockSpec((1, width), lambda t: (0, 0))
    sspec = pl.BlockSpec((tb, nh, dk, dk), lambda t: (t, 0, 0, 0))
    return pl.pallas_call(
        functools.partial(_hg_sample_body, dk=dk),
        out_shape=(jax.ShapeDtypeStruct((ns, width), F32),
                   jax.ShapeDtypeStruct(s0.shape, F32)),
        grid=(ns // tb,),
        in_specs=[tok(0), tok(1), tok(2), tok(3), vec, vec, sspec],
        out_specs=(pl.BlockSpec((tb, width), lambda t: (t, 0)), sspec),
        scratch_shapes=[pltpu.VMEM((2 * nh * dk, 2 * ns), BF16)],
        compiler_params=_cparams("arbitrary"),
        name="hgrn_sample",
    )(proj, proj, proj, proj, lb, gn, s0)


def _merge_body(z_ref, zs_ref, hg_ref, hgs_ref, gs0_ref, gs1_ref, gh0_ref, gh1_ref, x_ref,
                wglu_ref, bglu_ref, wbs_ref, wbh_ref, wout_ref, post_ref, o_ref, *, ns):
    i = pl.program_id(0)
    last_i = pl.num_programs(0) - 1
    cut = x_ref.shape[0] - ns
    half = gs0_ref.shape[1]

    def rows(z, hg, sl):
        s5o = z * jax.nn.sigmoid(_dot(z.astype(BF16), wglu_ref[...]) + bglu_ref[...])
        a = _dot(s5o.astype(BF16), wbs_ref[...])
        b = _dot(hg.astype(BF16), wbh_ref[...])
        m0 = (jax.nn.sigmoid(gs0_ref[sl, :]) * a[:, :half]
              + jax.nn.sigmoid(gh0_ref[sl, :]) * b[:, :half])
        m1 = (jax.nn.sigmoid(gs1_ref[sl, :]) * a[:, half:]
              + jax.nn.sigmoid(gh1_ref[sl, :]) * b[:, half:])
        mix = _dot(m0.astype(BF16), wout_ref[:half, :]) + _dot(m1.astype(BF16), wout_ref[half:, :])
        o_ref[sl, :] = x_ref[sl, :] + _rms(mix, post_ref[...])

    @pl.when(i < last_i)
    def _():
        rows(z_ref[...], hg_ref[...], slice(None))

    @pl.when(i == last_i)
    def _():
        rows(z_ref[:cut, :], hg_ref[:cut, :], slice(0, cut))
        rows(zs_ref[...], hgs_ref[...], slice(cut, None))


def _merge(z, zs, hg, hgs, proj, x, wglu, bglu, wbs, wbh, wout, post, *, col0, tm):
    m, d = x.shape
    w = z.shape[1]
    ns = zs.shape[0]
    cb = col0 // w
    assert cb * w == col0 and d == 2 * w and m % tm == 0 and 0 < ns < tm

    def gate(k):
        return pl.BlockSpec((tm, w), lambda i: (i, cb + k))

    def const(shape):
        return pl.BlockSpec(shape, lambda i: (0, 0), pipeline_mode=pl.Buffered(1))

    tok = pl.BlockSpec((tm, w), lambda i: (i, 0))
    return pl.pallas_call(
        functools.partial(_merge_body, ns=ns),
        out_shape=jax.ShapeDtypeStruct((m, d), F32),
        grid=(m // tm,),
        in_specs=[
            tok, const((ns, w)), tok, const((ns, w)),
            gate(0), gate(1), gate(2), gate(3),
            pl.BlockSpec((tm, d), lambda i: (i, 0)),
            const((w, w)), const((1, w)), const((w, d)), const((w, d)), const((d, d)), const((1, d)),
        ],
        out_specs=pl.BlockSpec((tm, d), lambda i: (i, 0)),
        compiler_params=_cparams("parallel"),
        name="merge",
    )(z, zs, hg, hgs, proj, proj, proj, proj, x, wglu, bglu, wbs, wbh, wout, post)


def kernel(x_prompt, x_sample, state_s5_re, state_s5_im, state_hgrn, ffn1_pre_norm, ffn1_w_gate, ffn1_w_up, ffn1_w_down, ffn1_post_norm, mix_pre_norm, w_in, s5_lambda_re, s5_lambda_im, s5_log_dt, s5_b_re, s5_b_im, s5_c_re, s5_c_im, s5_d, s5_w_glu, s5_b_glu, hgrn_lb_logits, hgrn_out_norm, w_branch_s5, w_branch_hgrn, w_out, mix_post_norm, ffn2_pre_norm, ffn2_w_gate, ffn2_w_up, ffn2_w_down, ffn2_post_norm):
    depth = ffn1_w_gate.shape[0]
    assert depth == 1
    batch, seq, d = x_prompt.shape
    ns = x_sample.shape[0]
    assert x_sample.shape[1] == 1
    g, p = s5_lambda_re.shape[1:]
    nh, dk = state_hgrn.shape[2], state_hgrn.shape[3]
    s5w = s5_d.shape[1]
    hgw = nh * dk
    mp = batch * seq
    m = mp + ns

    lay = lambda a: a.reshape(a.shape[1:])
    row = lambda a: a.reshape(1, -1)

    lb = jax.nn.softmax(hgrn_lb_logits.astype(F32), axis=0)[:1]

    x, w_in_b = _ffn(
        x_prompt.reshape(mp, d), x_sample.reshape(ns, d), row(ffn1_pre_norm),
        lay(ffn1_w_gate), lay(ffn1_w_up), lay(ffn1_w_down), row(ffn1_post_norm),
        ns=ns, split_out=False, side=(lay(w_in),))
    proj = _inproj(x, row(mix_pre_norm), w_in_b)

    tw_in, tw_out, a_re, a_im = _s5_tables(
        lay(s5_lambda_re), lay(s5_lambda_im), lay(s5_log_dt), lay(s5_b_re), lay(s5_b_im),
        lay(s5_c_re), lay(s5_c_im))
    d_skip = row(s5_d)
    z, hlast, wg2_b, wu2_b, wd2_b, wglu_b, wbs_b, wbh_b, wout_b = _s5_prompt(
        proj, tw_in.astype(BF16), tw_out.astype(BF16), a_re, a_im, d_skip,
        batch=batch, seq=seq, rows_total=mp,
        side=(lay(ffn2_w_gate), lay(ffn2_w_up), lay(ffn2_w_down), lay(s5_w_glu),
              lay(w_branch_s5), lay(w_branch_hgrn), lay(w_out)))
    zs, s_re, s_im = _s5_sample(proj, state_s5_re.reshape(ns, g * p),
                                state_s5_im.reshape(ns, g * p),
                                tw_in, tw_out, a_re, a_im, d_skip, row0=mp)
    half = hlast.shape[-1] // 2
    p_re = hlast[:, :, SUBLANES - 1, :half].reshape(1, batch, g, p)
    p_im = hlast[:, :, SUBLANES - 1, half:].reshape(1, batch, g, p)

    gn = row(hgrn_out_norm)
    hg, st_p = _hg_prompt(proj, lb, gn, batch=batch, seq=seq, rows_total=mp, nh=nh, col0=s5w)
    hgs, st_s = _hg_sample(proj, lb, gn, lay(state_hgrn), row0=mp, nh=nh, col0=s5w)

    x = _merge(z, zs, hg, hgs, proj, x, wglu_b, row(s5_b_glu), wbs_b, wbh_b, wout_b,
               row(mix_post_norm), col0=s5w + 4 * hgw, tm=MERGE_ROW_TILE)
    yp, ys = _ffn(x, None, row(ffn2_pre_norm), wg2_b, wu2_b, wd2_b, row(ffn2_post_norm),
                  ns=ns, split_out=True)

    return (yp.reshape(batch, seq, d), ys,
            p_re, p_im, st_p[None],
            s_re.reshape(1, ns, g, p), s_im.reshape(1, ns, g, p), st_s[None])
```

```python
import functools

import jax
import jax.numpy as jnp
import numpy as np
from jax import lax
from jax.experimental import pallas as pl
from jax.experimental.pallas import tpu as pltpu

F32 = jnp.float32
BF16 = jnp.bfloat16
EPS = 1e-6
HIGHEST = lax.Precision.HIGHEST

LANES = 128
SUBLANES = 8
VMEM_LIMIT = 60 * 1024 * 1024

FFN_TILES_BF16 = (832, 512)
FFN_TILES_F32 = (1040, 256)
IN_TILE = 2304
IN_ROW_TILE = 832
MERGE_ROW_TILE = 416
HG_CHUNK = 64
HG_CHUNKS_PER_STEP = 8
SAMPLE_TOKENS_PER_STEP = 16


def _cparams(*sem):
    return pltpu.CompilerParams(dimension_semantics=sem, vmem_limit_bytes=VMEM_LIMIT)


def _rms(x, g):
    return x * lax.rsqrt(jnp.mean(x * x, axis=-1, keepdims=True) + EPS) * g


def _silu(x):
    return x * jax.nn.sigmoid(x)


def _dot(a, b):
    return jnp.dot(a, b, preferred_element_type=F32)


def _dot_hi(a, b):
    return jnp.dot(a, b, preferred_element_type=F32, precision=HIGHEST)


def _dot_nt(a, b):
    return lax.dot_general(a, b, (((1,), (1,)), ((), ())), preferred_element_type=F32)


def _dot_tn(a, b):
    return lax.dot_general(a, b, (((0,), (0,)), ((), ())), preferred_element_type=F32)


BF16_ROWS = 16


def _side_specs(arrays, nsteps, step_of):
    in_specs, out_specs, out_shapes, plan = [], [], [], []
    for a in arrays:
        r, c = a.shape
        rows = next(t for t in range(BF16_ROWS, r + 1, BF16_ROWS) if r % t == 0 and r // t <= nsteps)
        nblk = r // rows
        every = nsteps // nblk

        def imap(*g, nblk=nblk, every=every):
            return (jnp.minimum(step_of(*g) // every, nblk - 1), 0)

        in_specs.append(pl.BlockSpec((rows, c), imap))
        out_specs.append(pl.BlockSpec((rows, c), imap))
        out_shapes.append(jax.ShapeDtypeStruct((r, c), BF16))
        plan.append((nblk, every))
    return in_specs, out_specs, out_shapes, tuple(plan)


def _side_cast(src_refs, dst_refs):
    for src, dst in zip(src_refs, dst_refs):
        dst[...] = src[...].astype(BF16)


def _ffn_body(*refs, ns, split_in, split_out, side_plan):
    refs = list(refs)
    nside = len(side_plan)
    x_ref = refs.pop(0)
    xs_ref = refs.pop(0) if split_in else None
    pre_ref, wg_ref, wu_ref, wd_ref, post_ref = refs[:5]
    side_src = refs[5:5 + nside]
    o_ref = refs[5 + nside]
    os_ref = refs[6 + nside] if split_out else None
    side_dst = refs[-1 - nside:-1]
    h_ref = refs[-1]
    acc_ref = o_ref
    i, k = pl.program_id(0), pl.program_id(1)
    last_i = pl.num_programs(0) - 1
    last_k = pl.num_programs(1) - 1
    cut = h_ref.shape[0] - ns

    def swiglu_down(h):
        _side_cast(side_src, side_dst)
        g = _dot(h, wg_ref[...].astype(BF16))
        u = _dot(h, wu_ref[...].astype(BF16))
        return _dot((_silu(g) * u).astype(BF16), wd_ref[...].astype(BF16))

    @pl.when((i < last_i) & (k == 0))
    def _():
        h = _rms(x_ref[...], pre_ref[...]).astype(BF16)
        h_ref[...] = h
        acc_ref[...] = swiglu_down(h)

    @pl.when((i < last_i) & (k == last_k))
    def _():
        acc = acc_ref[...] + swiglu_down(h_ref[...])
        o_ref[...] = x_ref[...] + 0.5 * _rms(acc, post_ref[...])

    @pl.when((i == last_i) & (k == 0))
    def _():
        h_ref[:cut, :] = _rms(x_ref[:cut, :], pre_ref[...]).astype(BF16)
        xt = xs_ref[...] if split_in else x_ref[cut:, :]
        h_ref[cut:, :] = _rms(xt, pre_ref[...]).astype(BF16)
        acc_ref[...] = jnp.zeros_like(acc_ref)

    @pl.when(((k > 0) & (k < last_k)) | (i == last_i))
    def _():
        acc_ref[...] += swiglu_down(h_ref[...])

    @pl.when((i == last_i) & (k == last_k))
    def _():
        o_ref[:cut, :] = x_ref[:cut, :] + 0.5 * _rms(acc_ref[:cut, :], post_ref[...])
        xt = xs_ref[...] if split_in else x_ref[cut:, :]
        tail = xt + 0.5 * _rms(acc_ref[cut:, :], post_ref[...])
        if split_out:
            os_ref[...] = tail
        else:
            o_ref[cut:, :] = tail


def _ffn(x, xs, pre, wg, wu, wd, post, *, ns, split_out, side=()):
    split_in = xs is not None
    d = x.shape[1]
    m = x.shape[0] + (ns if split_in else 0)
    dff = wg.shape[1]
    tm, tf = (FFN_TILES_F32 if wg.dtype == F32 else FFN_TILES_BF16)
    assert m % tm == 0 and dff % tf == 0 and 0 < ns < tm and ns % BF16_ROWS == 0
    nk = dff // tf
    side_in, side_out, side_shapes, side_plan = _side_specs(
        side, (m // tm) * nk, lambda i, k: i * nk + k)
    tok = pl.BlockSpec((tm, d), lambda i, k: (i, 0))
    smp_in = pl.BlockSpec((ns, d), lambda i, k: (0, 0))
    smp = pl.BlockSpec((ns, None, d), lambda i, k: (0, 0, 0))
    vec = pl.BlockSpec((1, d), lambda i, k: (0, 0))
    in_specs = [tok] + ([smp_in] if split_in else []) + [
        vec,
        pl.BlockSpec((d, tf), lambda i, k: (0, k)),
        pl.BlockSpec((d, tf), lambda i, k: (0, k)),
        pl.BlockSpec((tf, d), lambda i, k: (k, 0)),
        vec,
    ] + side_in
    if split_out:
        out_shape = [jax.ShapeDtypeStruct((m - ns, d), F32), jax.ShapeDtypeStruct((ns, 1, d), F32)]
        out_specs = [tok, smp]
    else:
        out_shape = [jax.ShapeDtypeStruct((m, d), F32)]
        out_specs = [tok]
    args = [x] + ([xs] if split_in else []) + [pre, wg, wu, wd, post] + list(side)
    sequential_rows = split_out or bool(side)
    return pl.pallas_call(
        functools.partial(_ffn_body, ns=ns, split_in=split_in, split_out=split_out,
                          side_plan=side_plan),
        out_shape=out_shape + side_shapes,
        grid=(m // tm, nk),
        in_specs=in_specs,
        out_specs=out_specs + side_out,
        scratch_shapes=[pltpu.VMEM((tm, d), BF16)],
        compiler_params=_cparams("arbitrary" if sequential_rows else "parallel", "arbitrary"),
        name="ffn",
    )(*args)


def _inproj_body(x_ref, g_ref, w_ref, o_ref, h_ref):
    j = pl.program_id(1)

    @pl.when(j == 0)
    def _():
        h = _rms(x_ref[...], g_ref[...]).astype(BF16)
        h_ref[...] = h
        o_ref[...] = _dot(h, w_ref[...])

    @pl.when(j > 0)
    def _():
        o_ref[...] = _dot(h_ref[...], w_ref[...])


def _inproj(x, g, w):
    m, d = x.shape
    n = w.shape[1]
    tm, tn = IN_ROW_TILE, IN_TILE
    assert m % tm == 0
    return pl.pallas_call(
        _inproj_body,
        out_shape=jax.ShapeDtypeStruct((m, n), F32),
        grid=(m // tm, n // tn),
        in_specs=[
            pl.BlockSpec((tm, d), lambda i, j: (i, 0)),
            pl.BlockSpec((1, d), lambda i, j: (0, 0)),
            pl.BlockSpec((d, tn), lambda i, j: (0, j)),
        ],
        out_specs=pl.BlockSpec((tm, tn), lambda i, j: (i, j)),
        scratch_shapes=[pltpu.VMEM((tm, d), BF16)],
        compiler_params=_cparams("parallel", "arbitrary"),
        name="inproj",
    )(x, g, w)


def _s5_tables(lam_re, lam_im, log_dt, b_re, b_im, c_re, c_im):
    g, p = lam_re.shape
    n = b_re.shape[-1]
    gpt = LANES // n
    nv = g // gpt
    dt = jnp.exp(log_dt)[:, None]
    er = jnp.exp(lam_re * dt)
    th = lam_im * dt
    a_re, a_im = er * jnp.cos(th), er * jnp.sin(th)
    den = lam_re * lam_re + lam_im * lam_im
    k_re = ((a_re - 1.0) * lam_re + a_im * lam_im) / den
    k_im = (a_im * lam_re - (a_re - 1.0) * lam_im) / den
    bb_re = k_re[..., None] * b_re - k_im[..., None] * b_im
    bb_im = k_re[..., None] * b_im + k_im[..., None] * b_re
    sw = 2 * gpt * p
    lane_group = (jnp.arange(sw) // p) % gpt
    io_group = jnp.arange(gpt * n) // n
    bb = jnp.stack([bb_re, bb_im]).reshape(2, nv, gpt, p, n)
    dense_in = bb.transpose(1, 4, 0, 2, 3).reshape(nv, n, sw)
    w_in = jnp.where(io_group[:, None] == lane_group[None, :],
                     jnp.tile(dense_in, (1, gpt, 1)), 0.0)
    cc = jnp.stack([c_re, -c_im]).reshape(2, nv, gpt, n, p)
    dense_out = cc.transpose(1, 0, 2, 4, 3).reshape(nv, sw, n)
    w_out = jnp.where(lane_group[:, None] == io_group[None, :],
                      jnp.tile(dense_out, (1, 1, gpt)), 0.0)
    return w_in, w_out, a_re.reshape(nv, 1, gpt * p), a_im.reshape(nv, 1, gpt * p)


def _cmul(ar, ai, br, bi):
    return ar * br - ai * bi, ar * bi + ai * br


def _s5_prompt_body(*refs, seg, side_plan):
    nside = len(side_plan)
    u_ref, w_ref, cm_ref, ar_ref, ai_ref, d_ref = refs[:6]
    z_ref, hl_ref = refs[6 + nside:8 + nside]
    buh_ref, ut_ref, pad_ref = refs[-3:]
    nseg = SUBLANES
    nc = w_ref.shape[2] // LANES
    hc = nc // 2
    pitch = pad_ref.shape[0] // nseg
    nblk = nseg
    steps = seg // nseg

    def gather(b):
        for t in range(b * steps, (b + 1) * steps):
            ut_ref[t * nseg:(t + 1) * nseg, :] = pad_ref[pl.ds(t, nseg, stride=pitch), :]

    def scatter(b):
        for t in range(b * steps, (b + 1) * steps):
            pad_ref[pl.ds(t, nseg, stride=pitch), :] = ut_ref[t * nseg:(t + 1) * nseg, :]

    for j in range(nseg):
        pad_ref[j * pitch:j * pitch + seg, :] = u_ref[j * seg:(j + 1) * seg, :]

    def project_in(b):
        blk = slice(b * seg, (b + 1) * seg)
        bu = _dot(ut_ref[blk, :].astype(BF16), w_ref[0])
        for c in range(nc):
            buh_ref[c, blk, :] = bu[:, c * LANES:(c + 1) * LANES]

    def project_out(b):
        blk = slice(b * seg, (b + 1) * seg)
        y = d_ref[...] * ut_ref[blk, :]
        for c in range(nc):
            y = y + _dot(buh_ref[c, blk, :].astype(BF16), cm_ref[0, c * LANES:(c + 1) * LANES, :])
        ut_ref[blk, :] = jax.nn.gelu(y)

    ar = [jnp.broadcast_to(ar_ref[0, :, c * LANES:(c + 1) * LANES], (nseg, LANES)) for c in range(hc)]
    ai = [jnp.broadcast_to(ai_ref[0, :, c * LANES:(c + 1) * LANES], (nseg, LANES)) for c in range(hc)]

    def advance(t, hs, store):
        rows = slice(t * nseg, (t + 1) * nseg)
        out = []
        for c in range(hc):
            pr, pi = _cmul(ar[c], ai[c], hs[2 * c], hs[2 * c + 1])
            out += [pr + buh_ref[c, rows, :], pi + buh_ref[hc + c, rows, :]]
        if store:
            for c in range(hc):
                buh_ref[c, rows, :] = out[2 * c]
                buh_ref[hc + c, rows, :] = out[2 * c + 1]
        return out

    zeros = [jnp.zeros((nseg, LANES), F32) for _ in range(2 * hc)]
    hs = zeros
    gather(0)
    project_in(0)
    for b in range(nblk):
        if b + 1 < nblk:
            gather(b + 1)
            project_in(b + 1)
        for t in range(b * steps, (b + 1) * steps):
            hs = advance(t, hs, store=False)
    ends = hs

    pw = [(ar[c], ai[c]) for c in range(hc)]
    for _ in range(seg.bit_length() - 1):
        pw = [_cmul(r, i, r, i) for r, i in pw]

    row = lax.broadcasted_iota(jnp.int32, (nseg, LANES), 0)
    init = list(zeros)
    for j in range(1, nseg):
        for c in range(hc):
            pr, pi = _cmul(pw[c][0], pw[c][1], init[2 * c], init[2 * c + 1])
            nr = pltpu.roll(pr + ends[2 * c], 1, 0)
            ni = pltpu.roll(pi + ends[2 * c + 1], 1, 0)
            init[2 * c] = jnp.where(row == j, nr, init[2 * c])
            init[2 * c + 1] = jnp.where(row == j, ni, init[2 * c + 1])

    hs = init
    for b in range(nblk):
        for t in range(b * steps, (b + 1) * steps):
            hs = advance(t, hs, store=True)
        if b > 0:
            project_out(b - 1)
            scatter(b - 1)
    for c in range(hc):
        hl_ref[0, 0, :, c * LANES:(c + 1) * LANES] = hs[2 * c]
        hl_ref[0, 0, :, (hc + c) * LANES:(hc + c + 1) * LANES] = hs[2 * c + 1]
    _side_cast(refs[6:6 + nside], refs[8 + nside:-3])
    project_out(nblk - 1)
    scatter(nblk - 1)
    for j in range(nseg):
        z_ref[j * seg:(j + 1) * seg, :] = pad_ref[j * pitch:j * pitch + seg, :]


def _s5_prompt(proj, w_in, w_out, a_re, a_im, d_skip, *, batch, seq, rows_total, side=()):
    nv = w_in.shape[0]
    sw = w_in.shape[2]
    seg = seq // SUBLANES
    assert seg * SUBLANES == seq and seg & (seg - 1) == 0
    side_in, side_out, side_shapes, side_plan = _side_specs(
        side, batch * nv, lambda b, v: b * nv + v)
    body = functools.partial(_s5_prompt_body, seg=seg, side_plan=side_plan)
    return pl.pallas_call(
        body,
        out_shape=[jax.ShapeDtypeStruct((rows_total, nv * LANES), F32),
                   jax.ShapeDtypeStruct((batch, nv, SUBLANES, sw), F32)] + side_shapes,
        grid=(batch, nv),
        in_specs=[
            pl.BlockSpec((seq, LANES), lambda b, v: (b, v)),
            pl.BlockSpec((1, LANES, sw), lambda b, v: (v, 0, 0)),
            pl.BlockSpec((1, sw, LANES), lambda b, v: (v, 0, 0)),
            pl.BlockSpec((1, 1, sw // 2), lambda b, v: (v, 0, 0)),
            pl.BlockSpec((1, 1, sw // 2), lambda b, v: (v, 0, 0)),
            pl.BlockSpec((1, LANES), lambda b, v: (0, v)),
        ] + side_in,
        out_specs=[pl.BlockSpec((seq, LANES), lambda b, v: (b, v)),
                   pl.BlockSpec((1, 1, SUBLANES, sw), lambda b, v: (b, v, 0, 0))] + side_out,
        scratch_shapes=[pltpu.VMEM((sw // LANES, seq, LANES), F32),
                        pltpu.VMEM((seq, LANES), F32),
                        pltpu.VMEM((SUBLANES * (seg + SUBLANES), LANES), F32)],
        compiler_params=_cparams(*(("arbitrary",) * 2 if side else ("parallel",) * 2)),
        name="s5_prompt",
    )(proj, w_in, w_out, a_re, a_im, d_skip, *side)


def _s5_sample_body(u_ref, hre_ref, him_ref, w_ref, cm_ref, ar_ref, ai_ref, d_ref,
                    z_ref, ore_ref, oim_ref):
    half = ar_ref.shape[2]
    u = u_ref[...]
    bu = _dot_hi(u, w_ref[0])
    pr, pi = _cmul(ar_ref[0], ai_ref[0], hre_ref[...], him_ref[...])
    hr = pr + bu[:, :half]
    hi = pi + bu[:, half:]
    ore_ref[...] = hr
    oim_ref[...] = hi
    y = d_ref[...] * u + _dot_hi(hr, cm_ref[0, :half, :]) + _dot_hi(hi, cm_ref[0, half:, :])
    z_ref[...] = jax.nn.gelu(y)


def _s5_sample(proj, h_re, h_im, w_in, w_out, a_re, a_im, d_skip, *, row0):
    nv = w_in.shape[0]
    sw = w_in.shape[2]
    ns = h_re.shape[0]
    rb = row0 // ns
    assert rb * ns == row0
    st = pl.BlockSpec((ns, sw // 2), lambda v: (0, v))
    return pl.pallas_call(
        _s5_sample_body,
        out_shape=(jax.ShapeDtypeStruct((ns, nv * LANES), F32),
                   jax.ShapeDtypeStruct(h_re.shape, F32),
                   jax.ShapeDtypeStruct(h_im.shape, F32)),
        grid=(nv,),
        in_specs=[
            pl.BlockSpec((ns, LANES), lambda v: (rb, v)),
            st, st,
            pl.BlockSpec((1, LANES, sw), lambda v: (v, 0, 0)),
            pl.BlockSpec((1, sw, LANES), lambda v: (v, 0, 0)),
            pl.BlockSpec((1, 1, sw // 2), lambda v: (v, 0, 0)),
            pl.BlockSpec((1, 1, sw // 2), lambda v: (v, 0, 0)),
            pl.BlockSpec((1, LANES), lambda v: (0, v)),
        ],
        out_specs=(pl.BlockSpec((ns, LANES), lambda v: (0, v)), st, st),
        compiler_params=_cparams("parallel"),
        name="s5_sample",
    )(proj, h_re, h_im, w_in, w_out, a_re, a_im, d_skip)


def _hg_span_matrices(ch):
    t = np.arange(ch)[:, None]
    r = np.arange(ch)[None, :]
    mats = [r <= t, r > t]
    for v in range(1, ch.bit_length() - 1):
        base = (t >> v) << v
        upper = ((t >> v) & 1) == 1
        mats.append(np.where(upper, (r >= base) & (r <= t), (r > t) & (r < base + (1 << v))))
    mm = np.stack(mats).astype(np.float32)
    return jnp.asarray(np.concatenate([mm, mm], axis=-1), BF16)


def _hg_prompt_body(*refs, dk, side_plan):
    nside = len(side_plan)
    q_ref, f_ref, i_ref, og_ref, lb_ref, gn_ref, mm_ref = refs[:7]
    o_ref, sfin_ref = refs[7 + nside:9 + nside]
    st_ref = refs[-1]
    _side_cast(refs[7:7 + nside], refs[9 + nside:-1])
    c = pl.program_id(1)
    nh = st_ref.shape[0]
    ch = mm_ref.shape[1]
    nsub = q_ref.shape[0] // ch
    width = q_ref.shape[1]
    nlev = ch.bit_length() - 1

    @pl.when(c == 0)
    def _():
        st_ref[...] = jnp.zeros_like(st_ref)

    lb = lb_ref[...]
    f_all = lb + (1.0 - lb) * jax.nn.sigmoid(f_ref[...])
    kk_all = 1.0 - f_all
    qs_all = _silu(q_ref[...])
    logf_pieces = _split3(jnp.log2(f_all))[:2]
    ib_all = i_ref[...].astype(BF16)
    pairs = [slice(2 * j * dk, 2 * (j + 1) * dk) for j in range(nh // 2)]
    heads = [slice(h * dk, (h + 1) * dk) for h in range(nh)]
    r = lax.broadcasted_iota(jnp.int32, (ch, 2 * ch), 0)
    s = lax.broadcasted_iota(jnp.int32, (ch, 2 * ch), 1) & (ch - 1)
    xr = jnp.where(r > s, r ^ s, 0)
    zero_k = jnp.zeros((ch, dk), BF16)
    zero_s = jnp.zeros((dk, dk), BF16)

    def block_diag(a, b, z):
        return jnp.concatenate([jnp.concatenate([a, z], axis=1),
                                jnp.concatenate([z, b], axis=1)], axis=0)

    def pair_scores(lhs, x):
        return _dot_nt(lhs, block_diag(x[:, :dk], x[:, dk:], zero_k))

    def upper_runs(v):
        m = 1 << v
        return [(b0, b0 + m) for b0 in range(m, ch, 2 * m)]

    def mix_rows(v, qs, kk):
        m = 1 << v
        if m >= SUBLANES:
            return jnp.concatenate(
                [(qs if (b0 // m) & 1 else kk)[b0:b0 + m] for b0 in range(0, ch, m)], axis=0)
        pick = ((lax.broadcasted_iota(jnp.int32, (1, SUBLANES, width), 1) >> v) & 1) == 1
        shape3 = (ch // SUBLANES, SUBLANES, width)
        return jnp.where(pick, qs.reshape(shape3), kk.reshape(shape3)).reshape(ch, width)

    staged = []
    for sub in range(nsub):
        rows = slice(sub * ch, (sub + 1) * ch)
        kk, qs = kk_all[rows], qs_all[rows]
        logf2 = jnp.concatenate([p[rows] for p in logf_pieces], axis=0)

        def decay(idx):
            return jnp.exp2(_dot(mm_ref[idx], logf2))

        eg = decay(0)
        qg = (qs * eg).astype(BF16)
        dec = eg[ch - 1:ch, :]
        kh = (kk * decay(1)).astype(BF16)
        lev = [mix_rows(0, qs * f_all[rows], kk).astype(BF16)]
        lev += [(decay(1 + v) * mix_rows(v, qs, kk)).astype(BF16) for v in range(1, nlev)]
        qb, kb = qs.astype(BF16), kk.astype(BF16)
        atts = []
        for sl in pairs:
            att = jnp.where(r == s, pair_scores(qb[:, sl], kb[:, sl]), 0.0)
            for v in range(nlev):
                x = lev[v][:, sl]
                m = 1 << v
                if m >= BF16_ROWS:
                    runs = upper_runs(v)
                    p = pair_scores(jnp.concatenate([x[a:b] for a, b in runs], axis=0), x)
                    parts = []
                    for n in range(len(runs)):
                        parts += [jnp.zeros((m, 2 * ch), F32), p[n * m:(n + 1) * m]]
                    p = jnp.concatenate(parts, axis=0)
                else:
                    p = pair_scores(x, x)
                att = jnp.where((xr >> v) == 1, p, att)
            atts.append(att.astype(BF16))
        staged.append((rows, qg, kh, dec, atts))

    for rows, qg, kh, dec, atts in staged:
        ib = ib_all[rows]
        for j, sl in enumerate(pairs):
            st2 = block_diag(st_ref[2 * j].astype(BF16), st_ref[2 * j + 1].astype(BF16), zero_s)
            ib2 = block_diag(ib[:, sl][:, :dk], ib[:, sl][:, dk:], zero_k)
            o = _dot_nt(qg[:, sl], st2) + _dot(atts[j], ib2)
            for e in range(2):
                hs = heads[2 * j + e]
                og = og_ref[rows, hs]
                o_ref[rows, hs] = _rms(o[:, e * dk:(e + 1) * dk], gn_ref[:, hs]) * _silu(og)
        for h, hs in enumerate(heads):
            st_ref[h] = st_ref[h] * dec[:, hs] + _dot_tn(ib[:, hs], kh[:, hs])

    @pl.when(c == pl.num_programs(1) - 1)
    def _():
        for h in range(nh):
            sfin_ref[0, h] = st_ref[h].T


def _hg_prompt(proj, lb, gn, *, batch, seq, rows_total, nh, col0, side=()):
    width = lb.shape[1]
    dk = width // nh
    ch = HG_CHUNK * HG_CHUNKS_PER_STEP
    nchunk = seq // ch
    cb = col0 // width
    assert cb * width == col0 and nchunk * ch == seq

    def tok(k):
        return pl.BlockSpec((ch, width), lambda b, c: (b * nchunk + c, cb + k))

    vec = pl.BlockSpec((1, width), lambda b, c: (0, 0))
    mm = _hg_span_matrices(HG_CHUNK)
    side_in, side_out, side_shapes, side_plan = _side_specs(
        side, batch * nchunk, lambda b, c: b * nchunk + c)
    return pl.pallas_call(
        functools.partial(_hg_prompt_body, dk=dk, side_plan=side_plan),
        out_shape=[jax.ShapeDtypeStruct((rows_total, width), F32),
                   jax.ShapeDtypeStruct((batch, nh, dk, dk), F32)] + side_shapes,
        grid=(batch, nchunk),
        in_specs=[tok(0), tok(1), tok(2), tok(3), vec, vec,
                  pl.BlockSpec(mm.shape, lambda b, c: (0, 0, 0))] + side_in,
        out_specs=[pl.BlockSpec((ch, width), lambda b, c: (b * nchunk + c, 0)),
                   pl.BlockSpec((1, nh, dk, dk), lambda b, c: (b, 0, 0, 0))] + side_out,
        scratch_shapes=[pltpu.VMEM((nh, dk, dk), F32)],
        compiler_params=_cparams("arbitrary" if side else "parallel", "arbitrary"),
        name="hgrn_prompt",
    )(proj, proj, proj, proj, lb, gn, mm, *side)


def _split3(x):
    p1 = x.astype(BF16)
    r1 = x - p1.astype(F32)
    p2 = r1.astype(BF16)
    p3 = (r1 - p2.astype(F32)).astype(BF16)
    return p1, p2, p3


def _hg_sample_body(q_ref, f_ref, i_ref, og_ref, lb_ref, gn_ref, s0_ref,
                    hg_ref, s_ref, fq_ref, *, dk):
    step = pl.program_id(0)
    nt = q_ref.shape[0]
    tb, nh = s0_ref.shape[0], s0_ref.shape[1]
    npiece = fq_ref.shape[1] // nt

    @pl.when(step == 0)
    def _():
        lb = lb_ref[...]
        f = lb + (1.0 - lb) * jax.nn.sigmoid(f_ref[...])
        qs = _silu(q_ref[...])
        for src, base in ((f, 0), (qs, nh)):
            for h in range(nh):
                t = src[:, h * dk:(h + 1) * dk].T
                for p, piece in enumerate(_split3(t)[:npiece]):
                    fq_ref[(base + h) * dk:(base + h + 1) * dk, p * nt:(p + 1) * nt] = piece

    tok = lax.broadcasted_iota(jnp.int32, (npiece * nt, 2 * dk), 0) & (nt - 1)
    second = lax.broadcasted_iota(jnp.int32, (npiece * nt, 2 * dk), 1) >= dk
    for j0 in range(0, tb, 2):
        n0 = step * tb + j0
        onehot = (tok == jnp.where(second, n0 + 1, n0)).astype(BF16)
        fq2 = _dot(fq_ref[...], onehot)
        for e in range(2):
            j, n = j0 + e, n0 + e
            fq = fq2[:, e * dk:(e + 1) * dk]
            irow = i_ref[pl.ds(n, 1), :]
            ogrow = og_ref[pl.ds(n, 1), :]
            for h in range(nh):
                sl = slice(h * dk, (h + 1) * dk)
                fb = fq[h * dk:(h + 1) * dk, :]
                qb = fq[(nh + h) * dk:(nh + h + 1) * dk, :]
                s1 = fb * s0_ref[j, h] + (1.0 - fb) * irow[:, sl]
                s_ref[j, h] = s1
                o = jnp.sum(qb * s1, axis=0, keepdims=True)
                hg_ref[j:j + 1, sl] = _rms(o, gn_ref[:, sl]) * _silu(ogrow[:, sl])


def _hg_sample(proj, lb, gn, s0, *, row0, nh, col0):
    width = lb.shape[1]
    dk = width // nh
    ns = s0.shape[0]
    tb = SAMPLE_TOKENS_PER_STEP
    rb, cb = row0 // ns, col0 // width
    assert rb * ns == row0 and cb * width == col0 and ns % tb == 0
    assert ns & (ns - 1) == 0 and tb % 2 == 0 and dk == LANES

    def tok(k):
        return pl.BlockSpec((ns, width), lambda t: (rb, cb + k))

    vec = pl.BlockSpec((1, width), lambda t: (0, 0))
    sspec = pl.BlockSpec((tb, nh, dk, dk), lambda t: (t, 0, 0, 0))
    return pl.pallas_call(
        functools.partial(_hg_sample_body, dk=dk),
        out_shape=(jax.ShapeDtypeStruct((ns, width), F32),
                   jax.ShapeDtypeStruct(s0.shape, F32)),
        grid=(ns // tb,),
        in_specs=[tok(0), tok(1), tok(2), tok(3), vec, vec, sspec],
        out_specs=(pl.BlockSpec((tb, width), lambda t: (t, 0)), sspec),
        scratch_shapes=[pltpu.VMEM((2 * nh * dk, 2 * ns), BF16)],
        compiler_params=_cparams("arbitrary"),
        name="hgrn_sample",
    )(proj, proj, proj, proj, lb, gn, s0)


def _merge_body(z_ref, zs_ref, hg_ref, hgs_ref, gs0_ref, gs1_ref, gh0_ref, gh1_ref, x_ref,
                wglu_ref, bglu_ref, wbs_ref, wbh_ref, wout_ref, post_ref, o_ref, *, ns):
    i = pl.program_id(0)
    last_i = pl.num_programs(0) - 1
    cut = x_ref.shape[0] - ns
    half = gs0_ref.shape[1]

    def rows(z, hg, sl):
        s5o = z * jax.nn.sigmoid(_dot(z.astype(BF16), wglu_ref[...]) + bglu_ref[...])
        a = _dot(s5o.astype(BF16), wbs_ref[...])
        b = _dot(hg.astype(BF16), wbh_ref[...])
        m0 = (jax.nn.sigmoid(gs0_ref[sl, :]) * a[:, :half]
              + jax.nn.sigmoid(gh0_ref[sl, :]) * b[:, :half])
        m1 = (jax.nn.sigmoid(gs1_ref[sl, :]) * a[:, half:]
              + jax.nn.sigmoid(gh1_ref[sl, :]) * b[:, half:])
        mix = _dot(m0.astype(BF16), wout_ref[:half, :]) + _dot(m1.astype(BF16), wout_ref[half:, :])
        o_ref[sl, :] = x_ref[sl, :] + _rms(mix, post_ref[...])

    @pl.when(i < last_i)
    def _():
        rows(z_ref[...], hg_ref[...], slice(None))

    @pl.when(i == last_i)
    def _():
        rows(z_ref[:cut, :], hg_ref[:cut, :], slice(0, cut))
        rows(zs_ref[...], hgs_ref[...], slice(cut, None))


def _merge(z, zs, hg, hgs, proj, x, wglu, bglu, wbs, wbh, wout, post, *, col0, tm):
    m, d = x.shape
    w = z.shape[1]
    ns = zs.shape[0]
    cb = col0 // w
    assert cb * w == col0 and d == 2 * w and m % tm == 0 and 0 < ns < tm

    def gate(k):
        return pl.BlockSpec((tm, w), lambda i: (i, cb + k))

    def const(shape):
        return pl.BlockSpec(shape, lambda i: (0, 0), pipeline_mode=pl.Buffered(1))

    tok = pl.BlockSpec((tm, w), lambda i: (i, 0))
    return pl.pallas_call(
        functools.partial(_merge_body, ns=ns),
        out_shape=jax.ShapeDtypeStruct((m, d), F32),
        grid=(m // tm,),
        in_specs=[
            tok, const((ns, w)), tok, const((ns, w)),
            gate(0), gate(1), gate(2), gate(3),
            pl.BlockSpec((tm, d), lambda i: (i, 0)),
            const((w, w)), const((1, w)), const((w, d)), const((w, d)), const((d, d)), const((1, d)),
        ],
        out_specs=pl.BlockSpec((tm, d), lambda i: (i, 0)),
        compiler_params=_cparams("parallel"),
        name="merge",
    )(z, zs, hg, hgs, proj, proj, proj, proj, x, wglu, bglu, wbs, wbh, wout, post)


def kernel(x_prompt, x_sample, state_s5_re, state_s5_im, state_hgrn, ffn1_pre_norm, ffn1_w_gate, ffn1_w_up, ffn1_w_down, ffn1_post_norm, mix_pre_norm, w_in, s5_lambda_re, s5_lambda_im, s5_log_dt, s5_b_re, s5_b_im, s5_c_re, s5_c_im, s5_d, s5_w_glu, s5_b_glu, hgrn_lb_logits, hgrn_out_norm, w_branch_s5, w_branch_hgrn, w_out, mix_post_norm, ffn2_pre_norm, ffn2_w_gate, ffn2_w_up, ffn2_w_down, ffn2_post_norm):
    depth = ffn1_w_gate.shape[0]
    assert depth == 1
    batch, seq, d = x_prompt.shape
    ns = x_sample.shape[0]
    assert x_sample.shape[1] == 1
    g, p = s5_lambda_re.shape[1:]
    nh, dk = state_hgrn.shape[2], state_hgrn.shape[3]
    s5w = s5_d.shape[1]
    hgw = nh * dk
    mp = batch * seq
    m = mp + ns

    lay = lambda a: a.reshape(a.shape[1:])
    row = lambda a: a.reshape(1, -1)

    lb = jax.nn.softmax(hgrn_lb_logits.astype(F32), axis=0)[:1]

    x, w_in_b = _ffn(
        x_prompt.reshape(mp, d), x_sample.reshape(ns, d), row(ffn1_pre_norm),
        lay(ffn1_w_gate), lay(ffn1_w_up), lay(ffn1_w_down), row(ffn1_post_norm),
        ns=ns, split_out=False, side=(lay(w_in),))
    proj = _inproj(x, row(mix_pre_norm), w_in_b)

    tw_in, tw_out, a_re, a_im = _s5_tables(
        lay(s5_lambda_re), lay(s5_lambda_im), lay(s5_log_dt), lay(s5_b_re), lay(s5_b_im),
        lay(s5_c_re), lay(s5_c_im))
    d_skip = row(s5_d)
    z, hlast, wg2_b, wu2_b, wd2_b, wglu_b, wbs_b, wbh_b, wout_b = _s5_prompt(
        proj, tw_in.astype(BF16), tw_out.astype(BF16), a_re, a_im, d_skip,
        batch=batch, seq=seq, rows_total=mp,
        side=(lay(ffn2_w_gate), lay(ffn2_w_up), lay(ffn2_w_down), lay(s5_w_glu),
              lay(w_branch_s5), lay(w_branch_hgrn), lay(w_out)))
    zs, s_re, s_im = _s5_sample(proj, state_s5_re.reshape(ns, g * p),
                                state_s5_im.reshape(ns, g * p),
                                tw_in, tw_out, a_re, a_im, d_skip, row0=mp)
    half = hlast.shape[-1] // 2
    p_re = hlast[:, :, SUBLANES - 1, :half].reshape(1, batch, g, p)
    p_im = hlast[:, :, SUBLANES - 1, half:].reshape(1, batch, g, p)

    gn = row(hgrn_out_norm)
    hg, st_p = _hg_prompt(proj, lb, gn, batch=batch, seq=seq, rows_total=mp, nh=nh, col0=s5w)
    hgs, st_s = _hg_sample(proj, lb, gn, lay(state_hgrn), row0=mp, nh=nh, col0=s5w)

    x = _merge(z, zs, hg, hgs, proj, x, wglu_b, row(s5_b_glu), wbs_b, wbh_b, wout_b,
               row(mix_post_norm), col0=s5w + 4 * hgw, tm=MERGE_ROW_TILE)
    yp, ys = _ffn(x, None, row(ffn2_pre_norm), wg2_b, wu2_b, wd2_b, row(ffn2_post_norm),
                  ns=ns, split_out=True)

    return (yp.reshape(batch, seq, d), ys,
            p_re, p_im, st_p[None],
            s_re.reshape(1, ns, g, p), s_im.reshape(1, ns, g, p), st_s[None])
```
